```python
import jax, jax.numpy as jnp
from jax import lax
import numpy as np

D_MODEL = 2048
BATCH = 1
SEQ = 8192
DEPTH = 1

GRID_W = 64
CTX_LEN = 256
HG_HEADS = 8
HG_DK = 128
HG_DV = 128
HG_QK = HG_HEADS * HG_DK
HG_WIDTH = HG_HEADS * HG_DV
RET_HEADS = 4
RET_DK = 128
RET_DV = 256
RET_QK = RET_HEADS * RET_DK
RET_WIDTH = RET_HEADS * RET_DV
MIX_WIDTH = HG_WIDTH + RET_WIDTH
CHUNK = 64
PROJ_SIZES = (HG_QK, HG_QK, HG_QK, HG_WIDTH, HG_WIDTH, RET_QK, RET_QK, RET_WIDTH, RET_WIDTH)
PROJ_WIDTH = 3 * HG_QK + 2 * HG_WIDTH + 2 * RET_QK + 2 * RET_WIDTH
N_GROUPS = 4
EXPERTS_PER_GROUP = 8
N_EXPERTS = N_GROUPS * EXPERTS_PER_GROUP
EXPERT_TOPK = 2
D_EXPERT = 1024
MOE_BLOCK = 128
RMS_EPS = 1e-6
ROPE_BASE = 10000.0

kernel_name = "hybrid_hgrn2_retention_hmoe_dit"


def rms_norm(x, w):
    xf = x.astype(jnp.float32)
    y = xf * lax.rsqrt(jnp.mean(xf * xf, axis=-1, keepdims=True) + RMS_EPS)
    return (y * w.astype(jnp.float32)).astype(x.dtype)


def head_rms_norm(o, w, n_heads):
    B, N, _ = o.shape
    of = o.reshape(B, N, n_heads, -1)
    of = of * lax.rsqrt(jnp.mean(of * of, axis=-1, keepdims=True) + RMS_EPS)
    return of.reshape(B, N, -1) * w.astype(jnp.float32)


def axial_rope(n_tokens):
    rows = n_tokens // GRID_W
    r = jnp.broadcast_to(jnp.arange(rows)[:, None], (rows, GRID_W)).reshape(-1).astype(jnp.float32)
    cidx = jnp.broadcast_to(jnp.arange(GRID_W)[None, :], (rows, GRID_W)).reshape(-1).astype(jnp.float32)
    n_freq = RET_DK // 4
    inv = ROPE_BASE ** (-jnp.arange(n_freq, dtype=jnp.float32) / n_freq)
    ang = jnp.concatenate([r[:, None] * inv, cidx[:, None] * inv], axis=-1)
    return jnp.cos(ang), jnp.sin(ang)


def apply_rope(x, cos, sin):
    half = x.shape[-1] // 2
    x1, x2 = x[..., :half], x[..., half:]
    return jnp.concatenate([x1 * cos - x2 * sin, x1 * sin + x2 * cos], axis=-1)


def _to_chunks(a):
    B, H, N, d = a.shape
    return a.reshape(B, H, N // CHUNK, CHUNK, d).transpose(2, 0, 1, 3, 4)


def _from_chunks(a):
    n, B, H, C, d = a.shape
    return a.transpose(1, 2, 0, 3, 4).reshape(B, H, n * C, d)


def gated_state_scan(q, k, v, log_f, s0):
    causal = jnp.tril(jnp.ones((CHUNK, CHUNK), dtype=bool))[:, :, None]

    def step(s, blk):
        qc, kc, vc, gc = blk
        b = jnp.cumsum(gc, axis=2)
        rel = jnp.where(causal, b[:, :, :, None, :] - b[:, :, None, :, :], -jnp.inf)
        att = jnp.einsum('bhtk,bhtsk,bhsk->bhts', qc, jnp.exp(rel), kc)
        o = jnp.einsum('bhts,bhsv->bhtv', att, vc) + jnp.einsum('bhtk,bhkv->bhtv', qc * jnp.exp(b), s)
        b_end = b[:, :, -1:, :]
        s_new = jnp.exp(b_end[:, :, 0, :])[..., None] * s + jnp.einsum('bhsk,bhsv->bhkv', kc * jnp.exp(b_end - b), vc)
        return s_new, o

    s_fin, o = lax.scan(step, s0, (_to_chunks(q), _to_chunks(k), _to_chunks(v), _to_chunks(log_f)))
    return _from_chunks(o), s_fin


def retention_scan(q, k, v, log_gamma, s0):
    idx = jnp.arange(CHUNK, dtype=jnp.float32)
    diff = idx[:, None] - idx[None, :]
    lg = log_gamma[:, None, None]
    decay_mask = jnp.where(diff >= 0, jnp.exp(lg * jnp.maximum(diff, 0.0)), 0.0)
    q_decay = jnp.exp(log_gamma[:, None] * (idx + 1.0))[:, :, None]
    k_decay = jnp.exp(log_gamma[:, None] * (CHUNK - 1.0 - idx))[:, :, None]
    chunk_decay = jnp.exp(log_gamma * CHUNK)[:, None, None]

    def step(s, blk):
        qc, kc, vc = blk
        inner = jnp.einsum('bhts,bhsv->bhtv', jnp.einsum('bhtk,bhsk->bhts', qc, kc) * decay_mask, vc)
        cross = jnp.einsum('bhtk,bhkv->bhtv', qc * q_decay, s)
        s_new = chunk_decay * s + jnp.einsum('bhsk,bhsv->bhkv', kc * k_decay, vc)
        return s_new, inner + cross

    s_fin, o = lax.scan(step, s0, (_to_chunks(q), _to_chunks(k), _to_chunks(v)))
    return _from_chunks(o), s_fin


def mixer_stream(h, w_in, w_out, lb, ret_log_gamma, hg_norm_w, ret_norm_w, init_states, rope):
    B, N, _ = h.shape
    split_at = np.cumsum(PROJ_SIZES)[:-1].tolist()
    hq, hf_fwd, hf_bwd, hi, hgate, rq, rk, rv, rgate = jnp.split(h @ w_in, split_at, axis=-1)

    def heads(a, n):
        return a.reshape(B, N, n, -1).transpose(0, 2, 1, 3).astype(jnp.float32)

    def merge(o):
        return o.transpose(0, 2, 1, 3).reshape(B, N, -1)

    def flip(a):
        return jnp.flip(a, axis=2)

    q = heads(jax.nn.silu(hq), HG_HEADS)
    v = heads(hi, HG_HEADS)
    f_fwd = lb[0] + (1.0 - lb[0]) * jax.nn.sigmoid(heads(hf_fwd, HG_HEADS))
    f_bwd = lb[1] + (1.0 - lb[1]) * jax.nn.sigmoid(heads(hf_bwd, HG_HEADS))
    o_f, s_hf = gated_state_scan(q, 1.0 - f_fwd, v, jnp.log(f_fwd), init_states[0])
    o_b, s_hb = gated_state_scan(flip(q), flip(1.0 - f_bwd), flip(v), flip(jnp.log(f_bwd)), init_states[1])
    o_hg = head_rms_norm(merge(o_f + flip(o_b)), hg_norm_w, HG_HEADS) * jax.nn.silu(hgate.astype(jnp.float32))

    q_r = heads(rq, RET_HEADS)
    k_r = heads(rk, RET_HEADS) * (RET_DK ** -0.5)
    if rope is not None:
        q_r = apply_rope(q_r, rope[0], rope[1])
        k_r = apply_rope(k_r, rope[0], rope[1])
    v_r = heads(rv, RET_HEADS)
    r_f, s_rf = retention_scan(q_r, k_r, v_r, ret_log_gamma[0], init_states[2])
    r_b, s_rb = retention_scan(flip(q_r), flip(k_r), flip(v_r), ret_log_gamma[1], init_states[3])
    o_ret = head_rms_norm(merge(r_f + flip(r_b)), ret_norm_w, RET_HEADS) * jax.nn.silu(rgate.astype(jnp.float32))

    out = jnp.concatenate([o_hg, o_ret], axis=-1).astype(h.dtype) @ w_out
    return out, (s_hf, s_hb, s_rf, s_rb)


def moe_ffn(h, wg, bg, we, be, w_gate, w_up, w_down):
    T, D = h.shape
    logits_g = (h @ wg).astype(jnp.float32) + bg.astype(jnp.float32)
    p_g = jax.nn.softmax(logits_g, axis=-1)
    grp = jnp.argmax(logits_g, axis=-1)
    p_sel = jnp.take_along_axis(p_g, grp[:, None], axis=-1)
    logits_e = ((h @ we).astype(jnp.float32) + be.astype(jnp.float32)).reshape(T, N_GROUPS, EXPERTS_PER_GROUP)
    logits_in = jnp.take_along_axis(logits_e, grp[:, None, None], axis=1)[:, 0]
    top_v, top_i = lax.top_k(logits_in, EXPERT_TOPK)
    gates = p_sel * jax.nn.softmax(top_v, axis=-1)
    expert_id = grp[:, None] * EXPERTS_PER_GROUP + top_i

    n_assign = T * EXPERT_TOPK
    flat_e = expert_id.reshape(-1).astype(jnp.int32)
    flat_tok = jnp.repeat(jnp.arange(T, dtype=jnp.int32), EXPERT_TOPK)
    flat_gate = gates.reshape(-1)
    order = jnp.argsort(flat_e)
    sorted_e = flat_e[order]
    counts = jnp.bincount(flat_e, length=N_EXPERTS)
    padded = (counts + MOE_BLOCK - 1) // MOE_BLOCK * MOE_BLOCK
    padded_end = jnp.cumsum(padded)
    start = jnp.cumsum(counts) - counts
    dest = (padded_end - padded)[sorted_e] + jnp.arange(n_assign, dtype=jnp.int32) - start[sorted_e]
    n_blocks = (n_assign + N_EXPERTS * (MOE_BLOCK - 1) + MOE_BLOCK - 1) // MOE_BLOCK
    slots = n_blocks * MOE_BLOCK
    slot_tok = jnp.full((slots,), T, dtype=jnp.int32).at[dest].set(flat_tok[order])
    slot_gate = jnp.zeros((slots,), jnp.float32).at[dest].set(flat_gate[order])
    block_e = jnp.minimum(jnp.searchsorted(padded_end, jnp.arange(n_blocks, dtype=jnp.int32) * MOE_BLOCK, side='right'), N_EXPERTS - 1)

    h_pad = jnp.concatenate([h, jnp.zeros((1, D), h.dtype)], axis=0)
    xb = h_pad[slot_tok].reshape(n_blocks, MOE_BLOCK, D)

    def expert_block(args):
        xblk, e = args
        return (jax.nn.silu(xblk @ w_gate[e]) * (xblk @ w_up[e])) @ w_down[e]

    yb = lax.map(expert_block, (xb, block_e)).reshape(slots, D)
    y = jax.ops.segment_sum(yb * slot_gate[:, None].astype(yb.dtype), slot_tok, num_segments=T + 1)
    return y[:T]


def setup_inputs(seed: int = 0) -> dict:
    key = jax.random.key(seed)
    ks = jax.random.split(key, 24)
    f32 = jnp.float32
    nrm = lambda k, shape, s: jax.random.normal(k, shape, f32) * s
    h_idx = jnp.arange(RET_HEADS, dtype=f32)
    ret_init = jnp.log(2.0 ** (5.0 + h_idx) - 1.0)
    return {
        "x": nrm(ks[0], (BATCH, SEQ, D_MODEL), 1.0),
        "c": nrm(ks[1], (BATCH, D_MODEL), 1.0),
        "ctx": nrm(ks[2], (BATCH, CTX_LEN, D_MODEL), 1.0),
        "c_ctx": nrm(ks[3], (D_MODEL,), 1.0),
        "w_mod": nrm(ks[4], (DEPTH, D_MODEL, 6 * D_MODEL), 0.01),
        "b_mod": nrm(ks[5], (DEPTH, 6 * D_MODEL), 0.01),
        "norm_w": 1.0 + nrm(ks[6], (DEPTH, 4, D_MODEL), 0.02),
        "w_in": nrm(ks[7], (DEPTH, D_MODEL, PROJ_WIDTH), D_MODEL ** -0.5),
        "hg_lb_logits": nrm(ks[8], (DEPTH + 1, 2, HG_QK), 0.1),
        "hg_norm_w": 1.0 + nrm(ks[9], (DEPTH, HG_WIDTH), 0.02),
        "ret_decay_logits": ret_init + nrm(ks[10], (DEPTH, 2, RET_HEADS), 0.01),
        "ret_norm_w": 1.0 + nrm(ks[11], (DEPTH, RET_WIDTH), 0.02),
        "w_out": nrm(ks[12], (DEPTH, MIX_WIDTH, D_MODEL), MIX_WIDTH ** -0.5),
        "router_group_w": nrm(ks[13], (DEPTH, D_MODEL, N_GROUPS), D_MODEL ** -0.5),
        "router_group_b": nrm(ks[14], (DEPTH, N_GROUPS), 0.01),
        "router_expert_w": nrm(ks[15], (DEPTH, D_MODEL, N_EXPERTS), D_MODEL ** -0.5),
        "router_expert_b": nrm(ks[16], (DEPTH, N_EXPERTS), 0.01),
        "w_gate": nrm(ks[17], (DEPTH, N_EXPERTS, D_MODEL, D_EXPERT), D_MODEL ** -0.5),
        "w_up": nrm(ks[18], (DEPTH, N_EXPERTS, D_MODEL, D_EXPERT), D_MODEL ** -0.5),
        "w_down": nrm(ks[19], (DEPTH, N_EXPERTS, D_EXPERT, D_MODEL), D_EXPERT ** -0.5),
    }


def reference(x, c, ctx, c_ctx, w_mod, b_mod, norm_w, w_in, hg_lb_logits, hg_norm_w, ret_decay_logits, ret_norm_w, w_out, router_group_w, router_group_b, router_expert_w, router_expert_b, w_gate, w_up, w_down):
    B, N, D = x.shape
    Bc, L, _ = ctx.shape
    rope = axial_rope(N)
    lb_all = jnp.cumsum(jax.nn.softmax(hg_lb_logits.astype(jnp.float32), axis=0), axis=0)
    zero_states = (
        jnp.zeros((B, HG_HEADS, HG_DK, HG_DV), jnp.float32),
        jnp.zeros((B, HG_HEADS, HG_DK, HG_DV), jnp.float32),
        jnp.zeros((B, RET_HEADS, RET_DK, RET_DV), jnp.float32),
        jnp.zeros((B, RET_HEADS, RET_DK, RET_DV), jnp.float32),
    )
    for l in range(DEPTH):
        mod_x = jax.nn.silu(c) @ w_mod[l] + b_mod[l]
        mod_c = jax.nn.silu(c_ctx) @ w_mod[l] + b_mod[l]
        sh1, sc1, g1, sh2, sc2, g2 = jnp.split(mod_x[:, None, :], 6, axis=-1)
        csh1, csc1, cg1, csh2, csc2, cg2 = jnp.split(mod_c, 6, axis=-1)
        lb = lb_all[l].reshape(2, HG_HEADS, 1, HG_DK)
        ret_lg = jax.nn.log_sigmoid(ret_decay_logits[l].astype(jnp.float32))

        hc = rms_norm(ctx, norm_w[l, 0]) * (1.0 + csc1) + csh1
        hx = rms_norm(x, norm_w[l, 0]) * (1.0 + sc1) + sh1
        out_c, ctx_states = mixer_stream(hc, w_in[l], w_out[l], lb, ret_lg, hg_norm_w[l], ret_norm_w[l], zero_states, None)
        out_x, _ = mixer_stream(hx, w_in[l], w_out[l], lb, ret_lg, hg_norm_w[l], ret_norm_w[l], ctx_states, rope)
        x = x + g1 * rms_norm(out_x, norm_w[l, 1])

        hx = rms_norm(x, norm_w[l, 2]) * (1.0 + sc2) + sh2
        ffn_x = moe_ffn(hx.reshape(B * N, D), router_group_w[l], router_group_b[l], router_expert_w[l], router_expert_b[l], w_gate[l], w_up[l], w_down[l]).reshape(B, N, D)
        x = x + g2 * rms_norm(ffn_x, norm_w[l, 3])

        if l < DEPTH - 1:
            ctx = ctx + cg1 * rms_norm(out_c, norm_w[l, 1])
            hc = rms_norm(ctx, norm_w[l, 2]) * (1.0 + csc2) + csh2
            ffn_c = moe_ffn(hc.reshape(Bc * L, D), router_group_w[l], router_group_b[l], router_expert_w[l], router_expert_b[l], w_gate[l], w_up[l], w_down[l]).reshape(Bc, L, D)
            ctx = ctx + cg2 * rms_norm(ffn_c, norm_w[l, 3])
    return x
```

```python
import functools

import jax
import jax.numpy as jnp
from jax import lax
from jax.experimental import pallas as pl
from jax.experimental.pallas import tpu as pltpu

F32 = jnp.float32
BF16 = jnp.bfloat16

D_MODEL = 2048
GRID_W = 64
HG_HEADS = 8
HG_DK = 128
HG_WIDTH = 1024
RET_HEADS = 4
RET_DK = 128
RET_DV = 256
RET_WIDTH = 1024
PROJ_WIDTH = 8192
N_GROUPS = 4
EXPERTS_PER_GROUP = 8
N_EXPERTS = 32
EXPERT_TOPK = 2
D_EXPERT = 1024
RMS_EPS = 1e-6
ROPE_BASE = 10000.0

COL_HQ, COL_HF_FWD, COL_HF_BWD, COL_HI, COL_HGATE = 0, 1, 2, 3, 4
COL_RQ_512, COL_RK_512 = 10, 11
COL_RV, COL_RGATE = 6, 7

HG_CHUNK = 64
HG_DIAG = 16
HG_LOG_CLAMP = 10.0
RET_CHUNK = 128

MOE_PAD = 256
MOE_ITEM_ROWS = 1024
ROUTER_LANES = 128


def _cparams(n_axes, vmem_mb):
    return pltpu.CompilerParams(
        dimension_semantics=("arbitrary",) * n_axes,
        vmem_limit_bytes=vmem_mb * 1024 * 1024,
    )


def _sigmoid(x):
    return 1.0 / (1.0 + jnp.exp(-x))


def _dot(a, b):
    return jnp.dot(a, b, preferred_element_type=F32)


def _dot_nt(a, b):
    return lax.dot_general(a, b, (((1,), (1,)), ((), ())), preferred_element_type=F32)


def _mod_kernel(cc_ref, w_ref, b_ref, o_ref):
    s = cc_ref[...]
    s = s * _sigmoid(s)
    s16 = jnp.concatenate([s, s], axis=0)
    hi = s16.astype(BF16).astype(F32)
    row = lax.broadcasted_iota(jnp.int32, s16.shape, 0)
    lhs = jnp.where(row < 8, hi, s16 - hi).astype(BF16)
    r = _dot(lhs, w_ref[...].astype(BF16))
    o_ref[...] = r[:8] + r[8:] + b_ref[...]


def _modulation(cc, w_mod, b_mod):
    d, n = w_mod.shape
    tn = 1024
    return pl.pallas_call(
        _mod_kernel,
        grid=(n // tn,),
        in_specs=[
            pl.BlockSpec((8, d), lambda j: (0, 0)),
            pl.BlockSpec((d, tn), lambda j: (0, j)),
            pl.BlockSpec((1, tn), lambda j: (0, j)),
        ],
        out_specs=pl.BlockSpec((8, tn), lambda j: (0, j)),
        out_shape=jax.ShapeDtypeStruct((8, n), F32),
        compiler_params=_cparams(1, 40),
        name="mod",
    )(cc, w_mod, b_mod.reshape(1, n))


def _proj_kernel(x_ref, nw_ref, sh_ref, sc_ref, w_ref, *rest, mod_row):
    o_ref, h_scr = rest[-2], rest[-1]

    @pl.when(pl.program_id(1) == 0)
    def _():
        x = x_ref[...]
        y = x * lax.rsqrt(jnp.mean(x * x, axis=-1, keepdims=True) + RMS_EPS)
        y = y * nw_ref[0:1, :]
        sc = sc_ref[mod_row:mod_row + 1, :]
        sh = sh_ref[mod_row:mod_row + 1, :]
        h_scr[...] = (y * (1.0 + sc) + sh).astype(BF16)

    o_ref[...] = _dot(h_scr[...], w_ref[...].astype(BF16))


def _projection(x2d, ctx2d, norm_w, mod, w_in):
    n, d = x2d.shape
    l = ctx2d.shape[0]
    t_rows = n + l
    tm, tn = 1024, 512
    mod_specs = [
        pl.BlockSpec((8, d), lambda i, j: (0, 0)),
        pl.BlockSpec((8, d), lambda i, j: (0, 1)),
    ]
    p = pl.pallas_call(
        functools.partial(_proj_kernel, mod_row=0),
        grid=(n // tm, PROJ_WIDTH // tn),
        in_specs=[
            pl.BlockSpec((tm, d), lambda i, j: (i, 0)),
            pl.BlockSpec((4, d), lambda i, j: (0, 0)),
            *mod_specs,
            pl.BlockSpec((d, tn), lambda i, j: (0, j)),
        ],
        out_specs=pl.BlockSpec((tm, tn), lambda i, j: (i, j)),
        out_shape=jax.ShapeDtypeStruct((t_rows, PROJ_WIDTH), F32),
        scratch_shapes=[pltpu.VMEM((tm, d), BF16)],
        compiler_params=_cparams(2, 48),
        name="proj_x",
    )(x2d, norm_w, mod, mod, w_in)
    return pl.pallas_call(
        functools.partial(_proj_kernel, mod_row=1),
        grid=(1, PROJ_WIDTH // tn),
        in_specs=[
            pl.BlockSpec((l, d), lambda i, j: (0, 0)),
            pl.BlockSpec((4, d), lambda i, j: (0, 0)),
            *mod_specs,
            pl.BlockSpec((d, tn), lambda i, j: (0, j)),
            pl.BlockSpec(memory_space=pl.ANY),
        ],
        out_specs=pl.BlockSpec((l, tn), lambda i, j: (n // l, j)),
        out_shape=jax.ShapeDtypeStruct((t_rows, PROJ_WIDTH), F32),
        scratch_shapes=[pltpu.VMEM((l, d), BF16)],
        input_output_aliases={5: 0},
        compiler_params=_cparams(2, 32),
        name="proj_ctx",
    )(ctx2d, norm_w, mod, mod, w_in, p)


def _stream_chunks(n_x, n_ctx):
    def fwd(p):
        return jnp.where(p < n_ctx, n_x + p, p - n_ctx)

    def bwd(p):
        return jnp.where(p < n_ctx, n_x + n_ctx - 1 - p, n_x + n_ctx - 1 - p)

    def out_fwd(p):
        return jnp.maximum(p - n_ctx, 0)

    def out_bwd(p):
        return jnp.minimum(n_x + n_ctx - 1 - p, n_x - 1)

    return fwd, bwd, out_fwd, out_bwd


def _cumsum_rows(g, rev):
    c = g.shape[0]
    row = lax.broadcasted_iota(jnp.int32, g.shape, 0)
    b = g
    d = 1
    while d < c:
        if rev:
            b = b + jnp.where(row < c - d, pltpu.roll(b, c - d, axis=0), 0.0)
        else:
            b = b + jnp.where(row >= d, pltpu.roll(b, d, axis=0), 0.0)
        d *= 2
    return b


def _row_refs(b, idxs, blk):
    parts = [jnp.broadcast_to(b[i:i + 1, :], (blk, b.shape[1])) for i in idxs]
    return parts[0] if len(parts) == 1 else jnp.concatenate(parts, axis=0)


def _hgrn_levels(c, rev):
    t = lax.broadcasted_iota(jnp.int32, (c, c), 0)
    s = lax.broadcasted_iota(jnp.int32, (c, c), 1)
    levels = []
    m = c // 2
    while m >= HG_DIAG:
        blk = 2 * m
        same = ((t ^ s) & ~(blk - 1)) == 0
        if rev:
            idxs = [b0 + m for b0 in range(0, c, blk)]
            cross = jnp.logical_and((t & m) == 0, (s & m) != 0)
        else:
            idxs = [b0 + m - 1 for b0 in range(0, c, blk)]
            cross = jnp.logical_and((t & m) != 0, (s & m) == 0)
        levels.append((idxs, blk, jnp.logical_and(same, cross), True))
        m //= 2
    blk = HG_DIAG
    same = ((t ^ s) & ~(blk - 1)) == 0
    if rev:
        idxs = [b0 + blk // 2 for b0 in range(0, c, blk)]
        mask = jnp.logical_and(same, s >= t)
    else:
        idxs = [b0 + blk // 2 - 1 for b0 in range(0, c, blk)]
        mask = jnp.logical_and(same, s <= t)
    levels.append((idxs, blk, mask, False))
    return levels


def _hgrn_head(q_raw, z, v, lb, st_ref, sidx, rev, levels):
    c = q_raw.shape[0]
    f = lb + (1.0 - lb) * _sigmoid(z)
    g = jnp.maximum(jnp.log(f), -HG_LOG_CLAMP)
    kk = 1.0 - f
    q = q_raw * _sigmoid(q_raw)
    b = _cumsum_rows(g, rev)

    att = jnp.zeros((c, c), F32)
    for idxs, blk, mask, clamp in levels:
        ref = _row_refs(b, idxs, blk)
        dq = b - ref
        dk = ref - b
        if clamp:
            dq = jnp.minimum(dq, 0.0)
            dk = jnp.minimum(dk, 0.0)
        a_l = _dot_nt((q * jnp.exp(dq)).astype(BF16), (kk * jnp.exp(dk)).astype(BF16))
        att = att + jnp.where(mask, a_l, 0.0)

    st = st_ref[sidx]
    vb = v.astype(BF16)
    o = _dot(att.astype(BF16), vb) + _dot_nt((q * jnp.exp(b)).astype(BF16), st.astype(BF16))
    b_end = b[0:1, :] if rev else b[c - 1:c, :]
    k_end = (kk * jnp.exp(b_end - b)).astype(BF16)
    st_ref[sidx] = st * jnp.exp(b_end) + _dot(v.T.astype(BF16), k_end)
    return o


def _hgrn_kernel(qf_ref, zf_ref, vf_ref, qb_ref, zb_ref, vb_ref, lb_ref, of_ref, ob_ref, st_ref):
    @pl.when(pl.program_id(0) == 0)
    def _():
        st_ref[...] = jnp.zeros_like(st_ref)

    c = qf_ref.shape[0]
    lev_f = _hgrn_levels(c, False)
    lev_b = _hgrn_levels(c, True)
    for h in range(HG_HEADS):
        sl = slice(h * HG_DK, (h + 1) * HG_DK)
        of_ref[:, sl] = _hgrn_head(qf_ref[:, sl], zf_ref[:, sl], vf_ref[:, sl], lb_ref[0:1, sl],
                                   st_ref, h, False, lev_f)
        ob_ref[:, sl] = _hgrn_head(qb_ref[:, sl], zb_ref[:, sl], vb_ref[:, sl], lb_ref[1:2, sl],
                                   st_ref, HG_HEADS + h, True, lev_b)


def _hgrn_scan(p, lb, n, l):
    c = HG_CHUNK
    n_x, n_ctx = n // c, l // c
    fwd, bwd, out_fwd, out_bwd = _stream_chunks(n_x, n_ctx)
    w = HG_WIDTH

    def spec(chunk_fn, col):
        return pl.BlockSpec((c, w), lambda s: (chunk_fn(s), col))

    return pl.pallas_call(
        _hgrn_kernel,
        grid=(n_x + n_ctx,),
        in_specs=[
            spec(fwd, COL_HQ), spec(fwd, COL_HF_FWD), spec(fwd, COL_HI),
            spec(bwd, COL_HQ), spec(bwd, COL_HF_BWD), spec(bwd, COL_HI),
            pl.BlockSpec((2, w), lambda s: (0, 0)),
        ],
        out_specs=[
            pl.BlockSpec((c, w), lambda s: (out_fwd(s), 0)),
            pl.BlockSpec((c, w), lambda s: (out_bwd(s), 0)),
        ],
        out_shape=[jax.ShapeDtypeStruct((n, w), F32)] * 2,
        scratch_shapes=[pltpu.VMEM((2 * HG_HEADS, HG_DK, HG_DK), F32)],
        compiler_params=_cparams(1, 32),
        name="hgrn",
    )(p, p, p, p, p, p, lb)


def _ret_head(q_raw, k_raw, v, cs, sn, lg, st_ref, sidx, rev):
    c = q_raw.shape[0]
    q = q_raw * cs + pltpu.roll(q_raw, RET_DK // 2, axis=1) * sn
    k = (k_raw * cs + pltpu.roll(k_raw, RET_DK // 2, axis=1) * sn) * (RET_DK ** -0.5)

    t = lax.broadcasted_iota(jnp.int32, (c, c), 0)
    s = lax.broadcasted_iota(jnp.int32, (c, c), 1)
    diff = (s - t) if rev else (t - s)
    decay = jnp.where(diff >= 0, jnp.exp(lg * jnp.maximum(diff, 0).astype(F32)), 0.0)
    pos = lax.broadcasted_iota(jnp.int32, (c, RET_DK), 0).astype(F32)
    if rev:
        q_dec = jnp.exp(lg * (c - pos))
        k_dec = jnp.exp(lg * pos)
    else:
        q_dec = jnp.exp(lg * (pos + 1.0))
        k_dec = jnp.exp(lg * (c - 1.0 - pos))

    att = _dot_nt(q.astype(BF16), k.astype(BF16)) * decay
    st = st_ref[sidx]
    o = _dot(att.astype(BF16), v.astype(BF16)) + _dot_nt((q * q_dec).astype(BF16), st.astype(BF16))
    chunk_decay = jnp.exp(jnp.full((1, RET_DK), lg * c, F32))
    st_ref[sidx] = st * chunk_decay + _dot(v.T.astype(BF16), (k * k_dec).astype(BF16))
    return o


def _ret_kernel(lg_ref, qf_ref, kf_ref, vf_ref, csf_ref, snf_ref,
                qb_ref, kb_ref, vb_ref, csb_ref, snb_ref, of_ref, ob_ref, st_ref):
    @pl.when(pl.program_id(0) == 0)
    def _():
        st_ref[...] = jnp.zeros_like(st_ref)

    csf, snf = csf_ref[...], snf_ref[...]
    csb, snb = csb_ref[...], snb_ref[...]
    for h in range(RET_HEADS):
        sk = slice(h * RET_DK, (h + 1) * RET_DK)
        sv = slice(h * RET_DV, (h + 1) * RET_DV)
        of_ref[:, sv] = _ret_head(qf_ref[:, sk], kf_ref[:, sk], vf_ref[:, sv], csf, snf,
                                  lg_ref[0, h], st_ref, h, False)
        ob_ref[:, sv] = _ret_head(qb_ref[:, sk], kb_ref[:, sk], vb_ref[:, sv], csb, snb,
                                  lg_ref[1, h], st_ref, RET_HEADS + h, True)


def _ret_scan(p, ret_lg, cs_tab, sn_tab, n, l):
    c = RET_CHUNK
    n_x, n_ctx = n // c, l // c
    fwd, bwd, out_fwd, out_bwd = _stream_chunks(n_x, n_ctx)
    qk_w = RET_HEADS * RET_DK

    def specs(chunk_fn):
        return [
            pl.BlockSpec((c, qk_w), lambda s: (chunk_fn(s), COL_RQ_512)),
            pl.BlockSpec((c, qk_w), lambda s: (chunk_fn(s), COL_RK_512)),
            pl.BlockSpec((c, RET_WIDTH), lambda s: (chunk_fn(s), COL_RV)),
            pl.BlockSpec((c, RET_DK), lambda s: (chunk_fn(s), 0)),
            pl.BlockSpec((c, RET_DK), lambda s: (chunk_fn(s), 0)),
        ]

    return pl.pallas_call(
        _ret_kernel,
        grid=(n_x + n_ctx,),
        in_specs=[pl.BlockSpec(memory_space=pltpu.SMEM), *specs(fwd), *specs(bwd)],
        out_specs=[
            pl.BlockSpec((c, RET_WIDTH), lambda s: (out_fwd(s), 0)),
            pl.BlockSpec((c, RET_WIDTH), lambda s: (out_bwd(s), 0)),
        ],
        out_shape=[jax.ShapeDtypeStruct((n, RET_WIDTH), F32)] * 2,
        scratch_shapes=[pltpu.VMEM((2 * RET_HEADS, RET_DV, RET_DK), F32)],
        compiler_params=_cparams(1, 32),
        name="ret",
    )(ret_lg, p, p, p, cs_tab, sn_tab, p, p, p, cs_tab, sn_tab)


def _rms(y):
    return y * lax.rsqrt(jnp.mean(y * y, axis=-1, keepdims=True) + RMS_EPS)


def _outproj_kernel(of_ref, ob_ref, rf_ref, rb_ref, hgate_ref, rgate_ref, x_ref, wout_ref,
                    hgw_ref, rnw_ref, nw_ref, g1_ref, sh2_ref, sc2_ref, wrh_ref, wrl_ref, br_ref,
                    x1_ref, h2_ref, lg_ref, cat_scr):
    o = of_ref[...] + ob_ref[...]
    gate = hgate_ref[...]
    for h in range(HG_HEADS):
        sl = slice(h * HG_DK, (h + 1) * HG_DK)
        gh = gate[:, sl]
        cat_scr[:, sl] = (_rms(o[:, sl]) * hgw_ref[0:1, sl] * (gh * _sigmoid(gh))).astype(BF16)
    r = rf_ref[...] + rb_ref[...]
    gate = rgate_ref[...]
    for h in range(RET_HEADS):
        sl = slice(h * RET_DV, (h + 1) * RET_DV)
        gh = gate[:, sl]
        cat_scr[:, HG_WIDTH + h * RET_DV:HG_WIDTH + (h + 1) * RET_DV] = (
            _rms(r[:, sl]) * rnw_ref[0:1, sl] * (gh * _sigmoid(gh))).astype(BF16)

    y = _dot(cat_scr[...], wout_ref[...])
    x1 = x_ref[...] + g1_ref[0:1, :] * (_rms(y) * nw_ref[1:2, :])
    x1_ref[...] = x1
    h2 = _rms(x1) * nw_ref[2:3, :] * (1.0 + sc2_ref[0:1, :]) + sh2_ref[0:1, :]
    h2_ref[...] = h2
    hi = h2.astype(BF16)
    lo = (h2 - hi.astype(F32)).astype(BF16)
    lg_ref[...] = (_dot(hi, wrh_ref[...]) + _dot(lo, wrh_ref[...]) + _dot(hi, wrl_ref[...])
                   + br_ref[...])


def _out_projection(o_f, o_b, r_f, r_b, p, x2d, w_out_bf, hg_norm_w, ret_norm_w, norm_w, mod,
                    wr_hi, wr_lo, br):
    n, d = x2d.shape
    tm = 256
    row = lambda w: pl.BlockSpec((tm, w), lambda i: (i, 0))
    full = lambda a: pl.BlockSpec(a.shape, lambda i: (0,) * a.ndim)
    modcol = lambda k: pl.BlockSpec((8, d), lambda i: (0, k))
    return pl.pallas_call(
        _outproj_kernel,
        grid=(n // tm,),
        in_specs=[
            row(HG_WIDTH), row(HG_WIDTH), row(RET_WIDTH), row(RET_WIDTH),
            pl.BlockSpec((tm, HG_WIDTH), lambda i: (i, COL_HGATE)),
            pl.BlockSpec((tm, RET_WIDTH), lambda i: (i, COL_RGATE)),
            row(d), full(w_out_bf), full(hg_norm_w), full(ret_norm_w), full(norm_w),
            modcol(2), modcol(3), modcol(4),
            full(wr_hi), full(wr_lo), full(br),
        ],
        out_specs=[row(d), row(d), row(ROUTER_LANES)],
        out_shape=[
            jax.ShapeDtypeStruct((n, d), F32),
            jax.ShapeDtypeStruct((n, d), F32),
            jax.ShapeDtypeStruct((n, ROUTER_LANES), F32),
        ],
        scratch_shapes=[pltpu.VMEM((tm, d), BF16)],
        compiler_params=_cparams(1, 48),
        name="outproj",
    )(o_f, o_b, r_f, r_b, p, p, x2d, w_out_bf, hg_norm_w, ret_norm_w, norm_w, mod, mod, mod,
      wr_hi, wr_lo, br)


def _routing_plan(logits, bg, be, n_slots, n_items):
    t = logits.shape[0]
    lg = logits[:, :N_GROUPS] + bg
    le = logits[:, N_GROUPS:N_GROUPS + N_EXPERTS] + be
    grp = jnp.argmax(lg, axis=-1)
    p_sel = jnp.take_along_axis(jax.nn.softmax(lg, axis=-1), grp[:, None], axis=-1)
    le3 = le.reshape(t, N_GROUPS, EXPERTS_PER_GROUP)
    lin = jnp.take_along_axis(le3, grp[:, None, None], axis=1)[:, 0]
    top_v, top_i = lax.top_k(lin, EXPERT_TOPK)
    gates = p_sel * jax.nn.softmax(top_v, axis=-1)
    flat_e = (grp[:, None] * EXPERTS_PER_GROUP + top_i).reshape(-1).astype(jnp.int32)

    onehot = (flat_e[:, None] == jnp.arange(N_EXPERTS, dtype=jnp.int32)[None, :]).astype(jnp.int32)
    csum = jnp.cumsum(onehot, axis=0)
    rank = jnp.take_along_axis(csum, flat_e[:, None], axis=1)[:, 0] - 1
    counts = csum[-1]
    padded = (counts + MOE_PAD - 1) // MOE_PAD * MOE_PAD
    pstart = jnp.cumsum(padded) - padded
    slot_of = (pstart[flat_e] + rank).astype(jnp.int32)
    tok = jnp.arange(t * EXPERT_TOPK, dtype=jnp.int32) // EXPERT_TOPK
    slot_tok = jnp.zeros((n_slots,), jnp.int32).at[slot_of].set(tok)

    r = MOE_ITEM_ROWS
    ipe = (padded + r - 1) // r
    iend = jnp.cumsum(ipe)
    ioff = iend - ipe
    total = iend[-1]
    a = jnp.arange(n_items, dtype=jnp.int32)
    e_a = jnp.minimum(jnp.searchsorted(iend, a, side="right"), N_EXPERTS - 1).astype(jnp.int32)
    valid = a < total
    sub = a - ioff[e_a]
    irows = jnp.clip(padded[e_a] - sub * r, 0, r)
    istart = pstart[e_a] + sub * r
    e_last = e_a[jnp.maximum(total - 1, 0)]
    item_e = jnp.where(valid, e_a, e_last).astype(jnp.int32)
    item_rows = jnp.where(valid, irows, 0).astype(jnp.int32)
    item_start = jnp.where(valid, istart, 0).astype(jnp.int32)
    return gates, slot_of, slot_tok, item_e, item_start, item_rows


def _moe_kernel(item_e, item_start, item_rows, slot_tok,
                h2_hbm, wg_ref, wu_ref, wd_ref, yb_hbm,
                xbuf, act, wgb, wub, wdb, ystage, gsem, osem):
    a = pl.program_id(0)
    t = pl.program_id(1)
    n = item_rows[a]
    start = item_start[a]
    rb_rows = MOE_PAD
    n_rb = n // rb_rows
    half_w = wgb.shape[1]
    out_w = wdb.shape[1]

    @pl.when(jnp.logical_and(a == 0, t == 0))
    def _():
        xbuf[...] = jnp.zeros_like(xbuf)

    @pl.when(t == 0)
    def _():
        def issue(i, carry):
            tok = slot_tok[start + i]
            pltpu.make_async_copy(h2_hbm.at[pl.ds(tok, 1), :], xbuf.at[pl.ds(i, 1), :], gsem).start()
            return carry
        lax.fori_loop(0, n, issue, 0)

        def drain(bk, carry):
            r0 = pl.multiple_of(bk * rb_rows, rb_rows)
            pltpu.make_async_copy(h2_hbm.at[pl.ds(0, rb_rows), :], xbuf.at[pl.ds(r0, rb_rows), :],
                                  gsem).wait()
            return carry
        lax.fori_loop(0, n_rb, drain, 0)

    def gate_up(half):
        wgb[...] = wg_ref[0].astype(BF16)
        wub[...] = wu_ref[0].astype(BF16)

        def body(rb, carry):
            r0 = pl.multiple_of(rb * rb_rows, rb_rows)
            xs = xbuf[pl.ds(r0, rb_rows), :].astype(BF16)
            g = _dot(xs, wgb[...])
            u = _dot(xs, wub[...])
            act[pl.ds(r0, rb_rows), half * half_w:(half + 1) * half_w] = (
                g * _sigmoid(g) * u).astype(BF16)
            return carry
        lax.fori_loop(0, n_rb, body, 0)

    def out_copy(slot, r0, half):
        return pltpu.make_async_copy(
            ystage.at[slot],
            yb_hbm.at[pl.ds(pl.multiple_of(start + r0, rb_rows), rb_rows),
                      half * out_w:(half + 1) * out_w],
            osem.at[slot])

    def down(half):
        wdb[...] = wd_ref[0].astype(BF16)

        def body(rb, carry):
            r0 = pl.multiple_of(rb * rb_rows, rb_rows)
            slot = rb % 2
            y = _dot(act[pl.ds(r0, rb_rows), :], wdb[...])

            @pl.when(rb >= 2)
            def _():
                out_copy(slot, r0, half).wait()

            ystage[slot] = y
            out_copy(slot, r0, half).start()
            return carry
        lax.fori_loop(0, n_rb, body, 0)

        @pl.when(n_rb >= 1)
        def _():
            out_copy((n_rb - 1) % 2, 0, half).wait()

        @pl.when(n_rb >= 2)
        def _():
            out_copy(n_rb % 2, 0, half).wait()

    for half in range(2):
        pl.when(t == half)(functools.partial(gate_up, half))
        pl.when(t == 2 + half)(functools.partial(down, half))


def _moe_ffn(h2, w_gate, w_up, w_down, item_e, item_start, item_rows, slot_tok, n_slots):
    t_rows, d = h2.shape
    n_items = item_e.shape[0]
    half_w = D_EXPERT // 2
    out_w = d // 2

    def wgu_map(a, t, ie, ist, ir, st):
        return (ie[a], 0, jnp.where(ir[a] > 0, jnp.minimum(t, 1), 1))

    def wd_map(a, t, ie, ist, ir, st):
        return (ie[a], 0, jnp.where(ir[a] > 0, jnp.maximum(t - 2, 0), 1))

    grid_spec = pltpu.PrefetchScalarGridSpec(
        num_scalar_prefetch=4,
        grid=(n_items, 4),
        in_specs=[
            pl.BlockSpec(memory_space=pl.ANY),
            pl.BlockSpec((1, d, half_w), wgu_map),
            pl.BlockSpec((1, d, half_w), wgu_map),
            pl.BlockSpec((1, D_EXPERT, out_w), wd_map),
        ],
        out_specs=pl.BlockSpec(memory_space=pl.ANY),
        scratch_shapes=[
            pltpu.VMEM((MOE_ITEM_ROWS, d), F32),
            pltpu.VMEM((MOE_ITEM_ROWS, D_EXPERT), BF16),
            pltpu.VMEM((d, half_w), BF16),
            pltpu.VMEM((d, half_w), BF16),
            pltpu.VMEM((D_EXPERT, out_w), BF16),
            pltpu.VMEM((2, MOE_PAD, out_w), F32),
            pltpu.SemaphoreType.DMA(()),
            pltpu.SemaphoreType.DMA((2,)),
        ],
    )
    return pl.pallas_call(
        _moe_kernel,
        grid_spec=grid_spec,
        out_shape=jax.ShapeDtypeStruct((n_slots, d), F32),
        compiler_params=_cparams(2, 56),
        name="moe",
    )(item_e, item_start, item_rows, slot_tok, h2, w_gate, w_up, w_down)


def _combine_kernel(slot_of, yb_hbm, gates_ref, x1_ref, nw_ref, g2_ref, out_ref, rows, sem):
    i = pl.program_id(0)
    tm = x1_ref.shape[0]

    def issue(r, carry):
        base = (i * tm + r) * EXPERT_TOPK
        for k in range(EXPERT_TOPK):
            s = slot_of[base + k]
            pltpu.make_async_copy(yb_hbm.at[pl.ds(s, 1), :], rows.at[k, pl.ds(r, 1), :], sem).start()
        return carry
    lax.fori_loop(0, tm, issue, 0)
    for k in range(EXPERT_TOPK):
        pltpu.make_async_copy(yb_hbm.at[pl.ds(0, tm), :], rows.at[k], sem).wait()

    g = gates_ref[...]
    y = rows[0] * g[:, 0:1] + rows[1] * g[:, 1:2]
    out_ref[...] = x1_ref[...] + g2_ref[0:1, :] * (_rms(y) * nw_ref[3:4, :])


def _combine(slot_of, yb, gates, x1, norm_w, mod):
    n, d = x1.shape
    tm = 256
    grid_spec = pltpu.PrefetchScalarGridSpec(
        num_scalar_prefetch=1,
        grid=(n // tm,),
        in_specs=[
            pl.BlockSpec(memory_space=pl.ANY),
            pl.BlockSpec((tm, EXPERT_TOPK), lambda i, so: (i, 0)),
            pl.BlockSpec((tm, d), lambda i, so: (i, 0)),
            pl.BlockSpec((4, d), lambda i, so: (0, 0)),
            pl.BlockSpec((8, d), lambda i, so: (0, 5)),
        ],
        out_specs=pl.BlockSpec((tm, d), lambda i, so: (i, 0)),
        scratch_shapes=[
            pltpu.VMEM((EXPERT_TOPK, tm, d), F32),
            pltpu.SemaphoreType.DMA(()),
        ],
    )
    return pl.pallas_call(
        _combine_kernel,
        grid_spec=grid_spec,
        out_shape=jax.ShapeDtypeStruct((n, d), F32),
        compiler_params=_cparams(1, 32),
        name="combine",
    )(slot_of, yb, gates, x1, norm_w, mod)


def _rope_tables(n, l):
    rows = n // GRID_W
    r = jnp.broadcast_to(jnp.arange(rows)[:, None], (rows, GRID_W)).reshape(-1).astype(F32)
    cidx = jnp.broadcast_to(jnp.arange(GRID_W)[None, :], (rows, GRID_W)).reshape(-1).astype(F32)
    n_freq = RET_DK // 4
    inv = ROPE_BASE ** (-jnp.arange(n_freq, dtype=F32) / n_freq)
    ang = jnp.concatenate([r[:, None] * inv, cidx[:, None] * inv], axis=-1)
    cos, sin = jnp.cos(ang), jnp.sin(ang)
    cs = jnp.concatenate([cos, cos], axis=-1)
    sn = jnp.concatenate([-sin, sin], axis=-1)
    cs = jnp.concatenate([cs, jnp.ones((l, RET_DK), F32)], axis=0)
    sn = jnp.concatenate([sn, jnp.zeros((l, RET_DK), F32)], axis=0)
    return cs, sn


def kernel(x, c, ctx, c_ctx, w_mod, b_mod, norm_w, w_in, hg_lb_logits, hg_norm_w, ret_decay_logits,
           ret_norm_w, w_out, router_group_w, router_group_b, router_expert_w, router_expert_b,
           w_gate, w_up, w_down):
    bsz, n, d = x.shape
    l = ctx.shape[1]
    assert bsz == 1 and w_mod.shape[0] == 1, "single sample, single layer"
    x2d, ctx2d = x[0], ctx[0]

    cc = jnp.zeros((8, d), F32).at[0].set(c[0]).at[1].set(c_ctx)
    mod = _modulation(cc, w_mod[0], b_mod[0])

    p = _projection(x2d, ctx2d, norm_w[0], mod, w_in[0])

    lb = jnp.cumsum(jax.nn.softmax(hg_lb_logits.astype(F32), axis=0), axis=0)[0]
    ret_lg = jax.nn.log_sigmoid(ret_decay_logits[0].astype(F32))
    cs_tab, sn_tab = _rope_tables(n, l)

    o_f, o_b = _hgrn_scan(p, lb, n, l)
    r_f, r_b = _ret_scan(p, ret_lg, cs_tab, sn_tab, n, l)

    wr = jnp.concatenate([router_group_w[0], router_expert_w[0]], axis=1)
    wr = jnp.pad(wr, ((0, 0), (0, ROUTER_LANES - wr.shape[1])))
    wr_hi = wr.astype(BF16)
    wr_lo = (wr - wr_hi.astype(F32)).astype(BF16)
    br = jnp.zeros((1, ROUTER_LANES), F32)
    x1, h2, logits = _out_projection(
        o_f, o_b, r_f, r_b, p, x2d, w_out[0].astype(BF16), hg_norm_w, ret_norm_w, norm_w[0], mod,
        wr_hi, wr_lo, br)

    n_assign = n * EXPERT_TOPK
    n_slots = n_assign + N_EXPERTS * MOE_PAD
    n_items = N_EXPERTS + n_slots // MOE_ITEM_ROWS
    gates, slot_of, slot_tok, item_e, item_start, item_rows = _routing_plan(
        logits, router_group_b[0], router_expert_b[0], n_slots, n_items)

    yb = _moe_ffn(h2, w_gate[0], w_up[0], w_down[0], item_e, item_start, item_rows, slot_tok, n_slots)
    out = _combine(slot_of, yb, gates, x1, norm_w[0], mod)
    return out[None]
```

```python
import functools

import jax
import jax.numpy as jnp
from jax import lax
from jax.experimental import pallas as pl
from jax.experimental.pallas import tpu as pltpu

F32 = jnp.float32
BF16 = jnp.bfloat16

D_MODEL = 2048
GRID_W = 64
HG_HEADS = 8
HG_DK = 128
HG_WIDTH = 1024
RET_HEADS = 4
RET_DK = 128
RET_DV = 256
RET_WIDTH = 1024
PROJ_WIDTH = 8192
N_GROUPS = 4
EXPERTS_PER_GROUP = 8
N_EXPERTS = 32
EXPERT_TOPK = 2
D_EXPERT = 1024
RMS_EPS = 1e-6
ROPE_BASE = 10000.0

COL_HQ, COL_HF_FWD, COL_HF_BWD, COL_HI, COL_HGATE = 0, 1, 2, 3, 4
COL_RQ_512, COL_RK_512 = 10, 11
COL_RV, COL_RGATE = 6, 7

HG_CHUNK = 64
HG_DIAG = 16
HG_LOG_CLAMP = 10.0
RET_CHUNK = 128

MOE_PAD = 256
MOE_ITEM_ROWS = 1024
ROUTER_LANES = 128


def _cparams(n_axes, vmem_mb):
    return pltpu.CompilerParams(
        dimension_semantics=("arbitrary",) * n_axes,
        vmem_limit_bytes=vmem_mb * 1024 * 1024,
    )


def _sigmoid(x):
    return 1.0 / (1.0 + jnp.exp(-x))


def _dot(a, b):
    return jnp.dot(a, b, preferred_element_type=F32)


def _dot_nt(a, b):
    return lax.dot_general(a, b, (((1,), (1,)), ((), ())), preferred_element_type=F32)


def _mod_kernel(cc_ref, w_ref, b_ref, o_ref):
    s = cc_ref[...]
    s = s * _sigmoid(s)
    s16 = jnp.concatenate([s, s], axis=0)
    hi = s16.astype(BF16).astype(F32)
    row = lax.broadcasted_iota(jnp.int32, s16.shape, 0)
    lhs = jnp.where(row < 8, hi, s16 - hi).astype(BF16)
    r = _dot(lhs, w_ref[...].astype(BF16))
    o_ref[...] = r[:8] + r[8:] + b_ref[...]


def _modulation(cc, w_mod, b_mod):
    d, n = w_mod.shape
    tn = 1024
    return pl.pallas_call(
        _mod_kernel,
        grid=(n // tn,),
        in_specs=[
            pl.BlockSpec((8, d), lambda j: (0, 0)),
            pl.BlockSpec((d, tn), lambda j: (0, j)),
            pl.BlockSpec((1, tn), lambda j: (0, j)),
        ],
        out_specs=pl.BlockSpec((8, tn), lambda j: (0, j)),
        out_shape=jax.ShapeDtypeStruct((8, n), F32),
        compiler_params=_cparams(1, 40),
        name="mod",
    )(cc, w_mod, b_mod.reshape(1, n))


PROJ_TN = 512
PROLOGUE_ROWS = 32


def _proj_kernel(x_ref, nw_ref, sh_ref, sc_ref, w_ref, lb_ref, o_ref, g_ref, h_scr, *, mod_row):
    j = pl.program_id(1)

    @pl.when(j == 0)
    def _():
        scale = nw_ref[0:1, :] * (1.0 + sc_ref[mod_row:mod_row + 1, :])
        shift = sh_ref[mod_row:mod_row + 1, :]

        def body(r, carry):
            r0 = pl.multiple_of(r * PROLOGUE_ROWS, PROLOGUE_ROWS)
            x = x_ref[pl.ds(r0, PROLOGUE_ROWS), :]
            y = x * lax.rsqrt(jnp.mean(x * x, axis=-1, keepdims=True) + RMS_EPS)
            h_scr[pl.ds(r0, PROLOGUE_ROWS), :] = (y * scale + shift).astype(BF16)
            return carry
        lax.fori_loop(0, x_ref.shape[0] // PROLOGUE_ROWS, body, 0)

    is_gate = jnp.logical_and(j >= 2, j < 6)
    is_silu = jnp.logical_or(j < 2, jnp.logical_or(jnp.logical_and(j >= 8, j < 10), j >= 14))
    is_lin = jnp.logical_not(jnp.logical_or(is_gate, is_silu))

    @pl.when(is_lin)
    def _():
        lin_scale = jnp.where(j == 11, RET_DK ** -0.5, 1.0).astype(F32)
        o_ref[...] = (_dot(h_scr[...], w_ref[...]) * lin_scale).astype(BF16)

    @pl.when(is_silu)
    def _():
        acc = _dot(h_scr[...], w_ref[...])
        o_ref[...] = (acc * _sigmoid(acc)).astype(BF16)

    @pl.when(is_gate)
    def _():
        acc = _dot(h_scr[...], w_ref[...])
        lb = lb_ref[...]
        g = jnp.maximum(jnp.log(lb + (1.0 - lb) * _sigmoid(acc)), -HG_LOG_CLAMP)
        g_ref[...] = g
        o_ref[...] = g.astype(BF16)


def _projection(x2d, norm_w, mod, w_in, lb_cols, mod_row, tm, name):
    n, d = x2d.shape
    tn = PROJ_TN
    return pl.pallas_call(
        functools.partial(_proj_kernel, mod_row=mod_row),
        grid=(n // tm, PROJ_WIDTH // tn),
        in_specs=[
            pl.BlockSpec((tm, d), lambda i, j: (i, 0)),
            pl.BlockSpec((4, d), lambda i, j: (0, 0)),
            pl.BlockSpec((8, d), lambda i, j: (0, 0)),
            pl.BlockSpec((8, d), lambda i, j: (0, 1)),
            pl.BlockSpec((d, tn), lambda i, j: (0, j)),
            pl.BlockSpec((1, tn), lambda i, j: (0, j)),
        ],
        out_specs=[
            pl.BlockSpec((tm, tn), lambda i, j: (i, j)),
            pl.BlockSpec((tm, tn), lambda i, j: (i, jnp.clip(j - 2, 0, 3))),
        ],
        out_shape=[
            jax.ShapeDtypeStruct((n, PROJ_WIDTH), BF16),
            jax.ShapeDtypeStruct((n, 2 * HG_WIDTH), F32),
        ],
        scratch_shapes=[pltpu.VMEM((tm, d), BF16)],
        compiler_params=_cparams(2, 48),
        name=name,
    )(x2d, norm_w, mod, mod, w_in, lb_cols)


def _cumsum_rows(g, rev):
    c = g.shape[0]
    row = lax.broadcasted_iota(jnp.int32, g.shape, 0)
    b = g
    d = 1
    while d < c:
        if rev:
            b = b + jnp.where(row < c - d, pltpu.roll(b, c - d, axis=0), 0.0)
        else:
            b = b + jnp.where(row >= d, pltpu.roll(b, d, axis=0), 0.0)
        d *= 2
    return b


def _row_refs(b, idxs, blk):
    parts = [jnp.broadcast_to(b[i:i + 1, :], (blk, b.shape[1])) for i in idxs]
    return parts[0] if len(parts) == 1 else jnp.concatenate(parts, axis=0)


def _hgrn_levels(c, rev):
    t = lax.broadcasted_iota(jnp.int32, (c, c), 0)
    s = lax.broadcasted_iota(jnp.int32, (c, c), 1)
    levels = []
    m = c // 2
    while m >= HG_DIAG:
        blk = 2 * m
        same = ((t ^ s) & ~(blk - 1)) == 0
        if rev:
            idxs = [b0 + m for b0 in range(0, c, blk)]
            cross = jnp.logical_and((t & m) == 0, (s & m) != 0)
        else:
            idxs = [b0 + m - 1 for b0 in range(0, c, blk)]
            cross = jnp.logical_and((t & m) != 0, (s & m) == 0)
        levels.append((idxs, blk, jnp.logical_and(same, cross), True))
        m //= 2
    blk = HG_DIAG
    same = ((t ^ s) & ~(blk - 1)) == 0
    if rev:
        idxs = [b0 + blk // 2 for b0 in range(0, c, blk)]
        mask = jnp.logical_and(same, s >= t)
    else:
        idxs = [b0 + blk // 2 - 1 for b0 in range(0, c, blk)]
        mask = jnp.logical_and(same, s <= t)
    levels.append((idxs, blk, mask, False))
    return levels


def _hgrn_head(q_bf, g, v_bf, st, rev, levels):
    c = g.shape[0]
    kk = 1.0 - jnp.exp(g)
    b = _cumsum_rows(g, rev)

    o = None
    if levels is not None:
        q = q_bf.astype(F32)
        att = jnp.zeros((c, c), F32)
        for idxs, blk, mask, clamp in levels:
            ref = _row_refs(b, idxs, blk)
            dq = b - ref
            dk = ref - b
            if clamp:
                dq = jnp.minimum(dq, 0.0)
                dk = jnp.minimum(dk, 0.0)
            a_l = _dot_nt((q * jnp.exp(dq)).astype(BF16), (kk * jnp.exp(dk)).astype(BF16))
            att = att + jnp.where(mask, a_l, 0.0)
        o = _dot(att.astype(BF16), v_bf) + _dot_nt((q * jnp.exp(b)).astype(BF16), st.astype(BF16))

    b_end = b[0:1, :] if rev else b[c - 1:c, :]
    k_end = (kk * jnp.exp(b_end - b)).astype(BF16)
    st_new = st * jnp.exp(b_end) + _dot(v_bf.astype(F32).T.astype(BF16), k_end)
    return o, st_new


def _hgrn_ctx_kernel(gf_ref, vf_ref, gb_ref, vb_ref, st_ref):
    @pl.when(pl.program_id(0) == 0)
    def _():
        st_ref[...] = jnp.zeros_like(st_ref)

    for h in range(HG_HEADS):
        sl = slice(h * HG_DK, (h + 1) * HG_DK)
        _, st_ref[h] = _hgrn_head(None, gf_ref[:, sl], vf_ref[:, sl], st_ref[h], False, None)
        hb = HG_HEADS + h
        _, st_ref[hb] = _hgrn_head(None, gb_ref[:, sl], vb_ref[:, sl], st_ref[hb], True, None)


def _hgrn_kernel(qf_ref, gf_ref, vf_ref, qb_ref, gb_ref, vb_ref, st0_ref, of_ref, ob_ref, st_ref):
    @pl.when(pl.program_id(0) == 0)
    def _():
        st_ref[...] = st0_ref[...]

    c = gf_ref.shape[0]
    lev_f = _hgrn_levels(c, False)
    lev_b = _hgrn_levels(c, True)
    for h in range(HG_HEADS):
        sl = slice(h * HG_DK, (h + 1) * HG_DK)
        of_ref[:, sl], st_ref[h] = _hgrn_head(qf_ref[:, sl], gf_ref[:, sl], vf_ref[:, sl],
                                              st_ref[h], False, lev_f)
        hb = HG_HEADS + h
        ob_ref[:, sl], st_ref[hb] = _hgrn_head(qb_ref[:, sl], gb_ref[:, sl], vb_ref[:, sl],
                                               st_ref[hb], True, lev_b)


def _hgrn_scan(p, g, pc, gc):
    c = HG_CHUNK
    w = HG_WIDTH
    n, l = p.shape[0], pc.shape[0]
    n_x, n_ctx = n // c, l // c
    st_shape = (2 * HG_HEADS, HG_DK, HG_DK)

    def spec(chunk_fn, col):
        return pl.BlockSpec((c, w), lambda s: (chunk_fn(s), col))

    st0 = pl.pallas_call(
        _hgrn_ctx_kernel,
        grid=(n_ctx,),
        in_specs=[
            spec(lambda s: s, 0), spec(lambda s: s, COL_HI),
            spec(lambda s: n_ctx - 1 - s, 1), spec(lambda s: n_ctx - 1 - s, COL_HI),
        ],
        out_specs=pl.BlockSpec(st_shape, lambda s: (0, 0, 0)),
        out_shape=jax.ShapeDtypeStruct(st_shape, F32),
        compiler_params=_cparams(1, 32),
        name="hgrn_ctx",
    )(gc, pc, gc, pc)

    fwd = lambda s: s
    bwd = lambda s: n_x - 1 - s
    return pl.pallas_call(
        _hgrn_kernel,
        grid=(n_x,),
        in_specs=[
            spec(fwd, COL_HQ), spec(fwd, 0), spec(fwd, COL_HI),
            spec(bwd, COL_HQ), spec(bwd, 1), spec(bwd, COL_HI),
            pl.BlockSpec(st_shape, lambda s: (0, 0, 0)),
        ],
        out_specs=[spec(fwd, 0), spec(bwd, 0)],
        out_shape=[jax.ShapeDtypeStruct((n, w), F32)] * 2,
        scratch_shapes=[pltpu.VMEM(st_shape, F32)],
        compiler_params=_cparams(1, 32),
        name="hgrn",
    )(p, g, p, p, g, p, st0)


def _rope(x_bf, cs, sn):
    x = x_bf.astype(F32)
    return x * cs + pltpu.roll(x, RET_DK // 2, axis=1) * sn


def _ret_head(q, k, v_bf, lg, st, rev):
    c = k.shape[0]
    pos = lax.broadcasted_iota(jnp.int32, (c, RET_DK), 0).astype(F32)
    o = None
    if q is not None:
        t = lax.broadcasted_iota(jnp.int32, (c, c), 0)
        s = lax.broadcasted_iota(jnp.int32, (c, c), 1)
        diff = (s - t) if rev else (t - s)
        decay = jnp.where(diff >= 0, jnp.exp(lg * jnp.maximum(diff, 0).astype(F32)), 0.0)
        q_dec = jnp.exp(lg * ((c - pos) if rev else (pos + 1.0)))
        att = _dot_nt(q.astype(BF16), k.astype(BF16)) * decay
        o = _dot(att.astype(BF16), v_bf) + _dot_nt((q * q_dec).astype(BF16), st.astype(BF16))
    k_dec = jnp.exp(lg * (pos if rev else (c - 1.0 - pos)))
    chunk_decay = jnp.exp(jnp.full((1, RET_DK), lg * c, F32))
    st_new = st * chunk_decay + _dot(v_bf.astype(F32).T.astype(BF16), (k * k_dec).astype(BF16))
    return o, st_new


def _ret_ctx_kernel(lg_ref, kf_ref, vf_ref, kb_ref, vb_ref, st_ref):
    @pl.when(pl.program_id(0) == 0)
    def _():
        st_ref[...] = jnp.zeros_like(st_ref)

    for h in range(RET_HEADS):
        sk = slice(h * RET_DK, (h + 1) * RET_DK)
        sv = slice(h * RET_DV, (h + 1) * RET_DV)
        _, st_ref[h] = _ret_head(None, kf_ref[:, sk].astype(F32), vf_ref[:, sv], lg_ref[0, h],
                                 st_ref[h], False)
        hb = RET_HEADS + h
        _, st_ref[hb] = _ret_head(None, kb_ref[:, sk].astype(F32), vb_ref[:, sv], lg_ref[1, h],
                                  st_ref[hb], True)


def _ret_kernel(lg_ref, qf_ref, kf_ref, vf_ref, csf_ref, snf_ref,
                qb_ref, kb_ref, vb_ref, csb_ref, snb_ref, st0_ref, of_ref, ob_ref, st_ref):
    @pl.when(pl.program_id(0) == 0)
    def _():
        st_ref[...] = st0_ref[...]

    csf, snf = csf_ref[...], snf_ref[...]
    csb, snb = csb_ref[...], snb_ref[...]
    for h in range(RET_HEADS):
        sk = slice(h * RET_DK, (h + 1) * RET_DK)
        sv = slice(h * RET_DV, (h + 1) * RET_DV)
        of_ref[:, sv], st_ref[h] = _ret_head(
            _rope(qf_ref[:, sk], csf, snf), _rope(kf_ref[:, sk], csf, snf), vf_ref[:, sv],
            lg_ref[0, h], st_ref[h], False)
        hb = RET_HEADS + h
        ob_ref[:, sv], st_ref[hb] = _ret_head(
            _rope(qb_ref[:, sk], csb, snb), _rope(kb_ref[:, sk], csb, snb), vb_ref[:, sv],
            lg_ref[1, h], st_ref[hb], True)


def _ret_scan(p, pc, ret_lg, cs_tab, sn_tab):
    c = RET_CHUNK
    n, l = p.shape[0], pc.shape[0]
    n_x, n_ctx = n // c, l // c
    qk_w = RET_HEADS * RET_DK
    st_shape = (2 * RET_HEADS, RET_DV, RET_DK)
    smem = pl.BlockSpec(memory_space=pltpu.SMEM)

    def kv_specs(chunk_fn):
        return [
            pl.BlockSpec((c, qk_w), lambda s: (chunk_fn(s), COL_RK_512)),
            pl.BlockSpec((c, RET_WIDTH), lambda s: (chunk_fn(s), COL_RV)),
        ]

    st0 = pl.pallas_call(
        _ret_ctx_kernel,
        grid=(n_ctx,),
        in_specs=[smem, *kv_specs(lambda s: s), *kv_specs(lambda s: n_ctx - 1 - s)],
        out_specs=pl.BlockSpec(st_shape, lambda s: (0, 0, 0)),
        out_shape=jax.ShapeDtypeStruct(st_shape, F32),
        compiler_params=_cparams(1, 32),
        name="ret_ctx",
    )(ret_lg, pc, pc, pc, pc)

    def specs(chunk_fn):
        return [
            pl.BlockSpec((c, qk_w), lambda s: (chunk_fn(s), COL_RQ_512)),
            *kv_specs(chunk_fn),
            pl.BlockSpec((c, RET_DK), lambda s: (chunk_fn(s), 0)),
            pl.BlockSpec((c, RET_DK), lambda s: (chunk_fn(s), 0)),
        ]

    fwd = lambda s: s
    bwd = lambda s: n_x - 1 - s
    return pl.pallas_call(
        _ret_kernel,
        grid=(n_x,),
        in_specs=[smem, *specs(fwd), *specs(bwd), pl.BlockSpec(st_shape, lambda s: (0, 0, 0))],
        out_specs=[
            pl.BlockSpec((c, RET_WIDTH), lambda s: (fwd(s), 0)),
            pl.BlockSpec((c, RET_WIDTH), lambda s: (bwd(s), 0)),
        ],
        out_shape=[jax.ShapeDtypeStruct((n, RET_WIDTH), F32)] * 2,
        scratch_shapes=[pltpu.VMEM(st_shape, F32)],
        compiler_params=_cparams(1, 32),
        name="ret",
    )(ret_lg, p, p, p, cs_tab, sn_tab, p, p, p, cs_tab, sn_tab, st0)


def _rms(y):
    return y * lax.rsqrt(jnp.mean(y * y, axis=-1, keepdims=True) + RMS_EPS)


def _outproj_kernel(of_ref, ob_ref, rf_ref, rb_ref, hgate_ref, rgate_ref, x_ref, wout_ref,
                    hgw_ref, rnw_ref, nw_ref, g1_ref, sh2_ref, sc2_ref, wrh_ref, wrl_ref, br_ref,
                    x1_ref, h2_ref, lg_ref, cat_scr):
    o = of_ref[...] + ob_ref[...]
    for h in range(HG_HEADS):
        sl = slice(h * HG_DK, (h + 1) * HG_DK)
        cat_scr[:, sl] = (_rms(o[:, sl]) * hgw_ref[0:1, sl]
                          * hgate_ref[:, sl].astype(F32)).astype(BF16)
    r = rf_ref[...] + rb_ref[...]
    for h in range(RET_HEADS):
        sl = slice(h * RET_DV, (h + 1) * RET_DV)
        cat_scr[:, HG_WIDTH + h * RET_DV:HG_WIDTH + (h + 1) * RET_DV] = (
            _rms(r[:, sl]) * rnw_ref[0:1, sl] * rgate_ref[:, sl].astype(F32)).astype(BF16)

    y = _dot(cat_scr[...], wout_ref[...])
    x1 = x_ref[...] + g1_ref[0:1, :] * (_rms(y) * nw_ref[1:2, :])
    x1_ref[...] = x1
    h2 = _rms(x1) * nw_ref[2:3, :] * (1.0 + sc2_ref[0:1, :]) + sh2_ref[0:1, :]
    h2_ref[...] = h2
    hi = h2.astype(BF16)
    lo = (h2 - hi.astype(F32)).astype(BF16)
    lg_ref[...] = (_dot(hi, wrh_ref[...]) + _dot(lo, wrh_ref[...]) + _dot(hi, wrl_ref[...])
                   + br_ref[...])


def _out_projection(o_f, o_b, r_f, r_b, p, x2d, w_out_bf, hg_norm_w, ret_norm_w, norm_w, mod,
                    wr_hi, wr_lo, br):
    n, d = x2d.shape
    tm = 256
    row = lambda w: pl.BlockSpec((tm, w), lambda i: (i, 0))
    full = lambda a: pl.BlockSpec(a.shape, lambda i: (0,) * a.ndim)
    modcol = lambda k: pl.BlockSpec((8, d), lambda i: (0, k))
    return pl.pallas_call(
        _outproj_kernel,
        grid=(n // tm,),
        in_specs=[
            row(HG_WIDTH), row(HG_WIDTH), row(RET_WIDTH), row(RET_WIDTH),
            pl.BlockSpec((tm, HG_WIDTH), lambda i: (i, COL_HGATE)),
            pl.BlockSpec((tm, RET_WIDTH), lambda i: (i, COL_RGATE)),
            row(d), full(w_out_bf), full(hg_norm_w), full(ret_norm_w), full(norm_w),
            modcol(2), modcol(3), modcol(4),
            full(wr_hi), full(wr_lo), full(br),
        ],
        out_specs=[row(d), row(d), row(ROUTER_LANES)],
        out_shape=[
            jax.ShapeDtypeStruct((n, d), F32),
            jax.ShapeDtypeStruct((n, d), F32),
            jax.ShapeDtypeStruct((n, ROUTER_LANES), F32),
        ],
        scratch_shapes=[pltpu.VMEM((tm, d), BF16)],
        compiler_params=_cparams(1, 48),
        name="outproj",
    )(o_f, o_b, r_f, r_b, p, p, x2d, w_out_bf, hg_norm_w, ret_norm_w, norm_w, mod, mod, mod,
      wr_hi, wr_lo, br)


def _routing_plan(logits, bg, be, n_slots, n_items):
    t = logits.shape[0]
    lg = logits[:, :N_GROUPS] + bg
    le = logits[:, N_GROUPS:N_GROUPS + N_EXPERTS] + be
    grp = jnp.argmax(lg, axis=-1)
    p_sel = jnp.take_along_axis(jax.nn.softmax(lg, axis=-1), grp[:, None], axis=-1)
    le3 = le.reshape(t, N_GROUPS, EXPERTS_PER_GROUP)
    lin = jnp.take_along_axis(le3, grp[:, None, None], axis=1)[:, 0]
    top_v, top_i = lax.top_k(lin, EXPERT_TOPK)
    gates = p_sel * jax.nn.softmax(top_v, axis=-1)
    flat_e = (grp[:, None] * EXPERTS_PER_GROUP + top_i).reshape(-1).astype(jnp.int32)

    onehot = (flat_e[:, None] == jnp.arange(N_EXPERTS, dtype=jnp.int32)[None, :]).astype(jnp.int32)
    csum = jnp.cumsum(onehot, axis=0)
    rank = jnp.take_along_axis(csum, flat_e[:, None], axis=1)[:, 0] - 1
    counts = csum[-1]
    padded = (counts + MOE_PAD - 1) // MOE_PAD * MOE_PAD
    pstart = jnp.cumsum(padded) - padded
    slot_of = (pstart[flat_e] + rank).astype(jnp.int32)
    tok = jnp.arange(t * EXPERT_TOPK, dtype=jnp.int32) // EXPERT_TOPK
    slot_tok = jnp.zeros((n_slots,), jnp.int32).at[slot_of].set(tok)

    r = MOE_ITEM_ROWS
    ipe = (padded + r - 1) // r
    iend = jnp.cumsum(ipe)
    ioff = iend - ipe
    total = iend[-1]
    a = jnp.arange(n_items, dtype=jnp.int32)
    e_a = jnp.minimum(jnp.searchsorted(iend, a, side="right"), N_EXPERTS - 1).astype(jnp.int32)
    valid = a < total
    sub = a - ioff[e_a]
    irows = jnp.clip(padded[e_a] - sub * r, 0, r)
    istart = pstart[e_a] + sub * r
    e_last = e_a[jnp.maximum(total - 1, 0)]
    item_e = jnp.where(valid, e_a, e_last).astype(jnp.int32)
    item_rows = jnp.where(valid, irows, 0).astype(jnp.int32)
    item_start = jnp.where(valid, istart, 0).astype(jnp.int32)
    return gates, slot_of, slot_tok, item_e, item_start, item_rows


def _moe_kernel(item_e, item_start, item_rows, slot_tok,
                h2_hbm, wg_ref, wu_ref, wd_ref, yb_hbm,
                xbuf, act, wgb, wub, wdb, ystage, gsem, osem):
    a = pl.program_id(0)
    t = pl.program_id(1)
    n = item_rows[a]
    start = item_start[a]
    rb_rows = MOE_PAD
    n_rb = n // rb_rows
    half_w = wgb.shape[1]
    out_w = wdb.shape[1]

    @pl.when(jnp.logical_and(a == 0, t == 0))
    def _():
        xbuf[...] = jnp.zeros_like(xbuf)

    @pl.when(t == 0)
    def _():
        def issue(i, carry):
            tok = slot_tok[start + i]
            pltpu.make_async_copy(h2_hbm.at[pl.ds(tok, 1), :], xbuf.at[pl.ds(i, 1), :], gsem).start()
            return carry
        lax.fori_loop(0, n, issue, 0)

        def drain(bk, carry):
            r0 = pl.multiple_of(bk * rb_rows, rb_rows)
            pltpu.make_async_copy(h2_hbm.at[pl.ds(0, rb_rows), :], xbuf.at[pl.ds(r0, rb_rows), :],
                                  gsem).wait()
            return carry
        lax.fori_loop(0, n_rb, drain, 0)

    def gate_up(half):
        wgb[...] = wg_ref[0].astype(BF16)
        wub[...] = wu_ref[0].astype(BF16)

        def body(rb, carry):
            r0 = pl.multiple_of(rb * rb_rows, rb_rows)
            xs = xbuf[pl.ds(r0, rb_rows), :].astype(BF16)
            g = _dot(xs, wgb[...])
            u = _dot(xs, wub[...])
            act[pl.ds(r0, rb_rows), half * half_w:(half + 1) * half_w] = (
                g * _sigmoid(g) * u).astype(BF16)
            return carry
        lax.fori_loop(0, n_rb, body, 0)

    def out_copy(slot, r0, half):
        return pltpu.make_async_copy(
            ystage.at[slot],
            yb_hbm.at[pl.ds(pl.multiple_of(start + r0, rb_rows), rb_rows),
                      half * out_w:(half + 1) * out_w],
            osem.at[slot])

    def down(half):
        wdb[...] = wd_ref[0].astype(BF16)

        def body(rb, carry):
            r0 = pl.multiple_of(rb * rb_rows, rb_rows)
            slot = rb % 2
            y = _dot(act[pl.ds(r0, rb_rows), :], wdb[...])

            @pl.when(rb >= 2)
            def _():
                out_copy(slot, r0, half).wait()

            ystage[slot] = y
            out_copy(slot, r0, half).start()
            return carry
        lax.fori_loop(0, n_rb, body, 0)

        @pl.when(n_rb >= 1)
        def _():
            out_copy((n_rb - 1) % 2, 0, half).wait()

        @pl.when(n_rb >= 2)
        def _():
            out_copy(n_rb % 2, 0, half).wait()

    for half in range(2):
        pl.when(t == half)(functools.partial(gate_up, half))
        pl.when(t == 2 + half)(functools.partial(down, half))


def _moe_ffn(h2, w_gate, w_up, w_down, item_e, item_start, item_rows, slot_tok, n_slots):
    t_rows, d = h2.shape
    n_items = item_e.shape[0]
    half_w = D_EXPERT // 2
    out_w = d // 2

    def wgu_map(a, t, ie, ist, ir, st):
        return (ie[a], 0, jnp.where(ir[a] > 0, jnp.minimum(t, 1), 1))

    def wd_map(a, t, ie, ist, ir, st):
        return (ie[a], 0, jnp.where(ir[a] > 0, jnp.maximum(t - 2, 0), 1))

    grid_spec = pltpu.PrefetchScalarGridSpec(
        num_scalar_prefetch=4,
        grid=(n_items, 4),
        in_specs=[
            pl.BlockSpec(memory_space=pl.ANY),
            pl.BlockSpec((1, d, half_w), wgu_map),
            pl.BlockSpec((1, d, half_w), wgu_map),
            pl.BlockSpec((1, D_EXPERT, out_w), wd_map),
        ],
        out_specs=pl.BlockSpec(memory_space=pl.ANY),
        scratch_shapes=[
            pltpu.VMEM((MOE_ITEM_ROWS, d), F32),
            pltpu.VMEM((MOE_ITEM_ROWS, D_EXPERT), BF16),
            pltpu.VMEM((d, half_w), BF16),
            pltpu.VMEM((d, half_w), BF16),
            pltpu.VMEM((D_EXPERT, out_w), BF16),
            pltpu.VMEM((2, MOE_PAD, out_w), F32),
            pltpu.SemaphoreType.DMA(()),
            pltpu.SemaphoreType.DMA((2,)),
        ],
    )
    return pl.pallas_call(
        _moe_kernel,
        grid_spec=grid_spec,
        out_shape=jax.ShapeDtypeStruct((n_slots, d), F32),
        compiler_params=_cparams(2, 56),
        name="moe",
    )(item_e, item_start, item_rows, slot_tok, h2, w_gate, w_up, w_down)


def _combine_kernel(slot_of, yb_hbm, gates_ref, x1_ref, nw_ref, g2_ref, out_ref, rows, sem):
    i = pl.program_id(0)
    tm = x1_ref.shape[0]

    def issue(r, carry):
        base = (i * tm + r) * EXPERT_TOPK
        for k in range(EXPERT_TOPK):
            s = slot_of[base + k]
            pltpu.make_async_copy(yb_hbm.at[pl.ds(s, 1), :], rows.at[k, pl.ds(r, 1), :], sem).start()
        return carry
    lax.fori_loop(0, tm, issue, 0)
    for k in range(EXPERT_TOPK):
        pltpu.make_async_copy(yb_hbm.at[pl.ds(0, tm), :], rows.at[k], sem).wait()

    g = gates_ref[...]
    y = rows[0] * g[:, 0:1] + rows[1] * g[:, 1:2]
    out_ref[...] = x1_ref[...] + g2_ref[0:1, :] * (_rms(y) * nw_ref[3:4, :])


def _combine(slot_of, yb, gates, x1, norm_w, mod):
    n, d = x1.shape
    tm = 256
    grid_spec = pltpu.PrefetchScalarGridSpec(
        num_scalar_prefetch=1,
        grid=(n // tm,),
        in_specs=[
            pl.BlockSpec(memory_space=pl.ANY),
            pl.BlockSpec((tm, EXPERT_TOPK), lambda i, so: (i, 0)),
            pl.BlockSpec((tm, d), lambda i, so: (i, 0)),
            pl.BlockSpec((4, d), lambda i, so: (0, 0)),
            pl.BlockSpec((8, d), lambda i, so: (0, 5)),
        ],
        out_specs=pl.BlockSpec((tm, d), lambda i, so: (i, 0)),
        scratch_shapes=[
            pltpu.VMEM((EXPERT_TOPK, tm, d), F32),
            pltpu.SemaphoreType.DMA(()),
        ],
    )
    return pl.pallas_call(
        _combine_kernel,
        grid_spec=grid_spec,
        out_shape=jax.ShapeDtypeStruct((n, d), F32),
        compiler_params=_cparams(1, 32),
        name="combine",
    )(slot_of, yb, gates, x1, norm_w, mod)


def _rope_tables(n):
    rows = n // GRID_W
    r = jnp.broadcast_to(jnp.arange(rows)[:, None], (rows, GRID_W)).reshape(-1).astype(F32)
    cidx = jnp.broadcast_to(jnp.arange(GRID_W)[None, :], (rows, GRID_W)).reshape(-1).astype(F32)
    n_freq = RET_DK // 4
    inv = ROPE_BASE ** (-jnp.arange(n_freq, dtype=F32) / n_freq)
    ang = jnp.concatenate([r[:, None] * inv, cidx[:, None] * inv], axis=-1)
    cos, sin = jnp.cos(ang), jnp.sin(ang)
    cs = jnp.concatenate([cos, cos], axis=-1)
    sn = jnp.concatenate([-sin, sin], axis=-1)
    return cs, sn


def kernel(x, c, ctx, c_ctx, w_mod, b_mod, norm_w, w_in, hg_lb_logits, hg_norm_w, ret_decay_logits,
           ret_norm_w, w_out, router_group_w, router_group_b, router_expert_w, router_expert_b,
           w_gate, w_up, w_down):
    bsz, n, d = x.shape
    l = ctx.shape[1]
    assert bsz == 1 and w_mod.shape[0] == 1, "single sample, single layer"
    x2d, ctx2d = x[0], ctx[0]

    cc = jnp.zeros((8, d), F32).at[0].set(c[0]).at[1].set(c_ctx)
    mod = _modulation(cc, w_mod[0], b_mod[0])

    lb = jnp.cumsum(jax.nn.softmax(hg_lb_logits.astype(F32), axis=0), axis=0)[0]
    lb_cols = jnp.zeros((1, PROJ_WIDTH), F32).at[0, HG_WIDTH:3 * HG_WIDTH].set(lb.reshape(-1))
    w_in_bf = w_in[0].astype(BF16)
    p, g = _projection(x2d, norm_w[0], mod, w_in_bf, lb_cols, 0, 1024, "proj_x")
    pc, gc = _projection(ctx2d, norm_w[0], mod, w_in_bf, lb_cols, 1, l, "proj_ctx")

    ret_lg = jax.nn.log_sigmoid(ret_decay_logits[0].astype(F32))
    cs_tab, sn_tab = _rope_tables(n)

    o_f, o_b = _hgrn_scan(p, g, pc, gc)
    r_f, r_b = _ret_scan(p, pc, ret_lg, cs_tab, sn_tab)

    wr = jnp.concatenate([router_group_w[0], router_expert_w[0]], axis=1)
    wr = jnp.pad(wr, ((0, 0), (0, ROUTER_LANES - wr.shape[1])))
    wr_hi = wr.astype(BF16)
    wr_lo = (wr - wr_hi.astype(F32)).astype(BF16)
    br = jnp.zeros((1, ROUTER_LANES), F32)
    x1, h2, logits = _out_projection(
        o_f, o_b, r_f, r_b, p, x2d, w_out[0].astype(BF16), hg_norm_w, ret_norm_w, norm_w[0], mod,
        wr_hi, wr_lo, br)

    n_assign = n * EXPERT_TOPK
    n_slots = n_assign + N_EXPERTS * MOE_PAD
    n_items = N_EXPERTS + n_slots // MOE_ITEM_ROWS
    gates, slot_of, slot_tok, item_e, item_start, item_rows = _routing_plan(
        logits, router_group_b[0], router_expert_b[0], n_slots, n_items)

    yb = _moe_ffn(h2, w_gate[0], w_up[0], w_down[0], item_e, item_start, item_rows, slot_tok, n_slots)
    out = _combine(slot_of, yb, gates, x1, norm_w[0], mod)
    return out[None]
```

```python
import functools

import jax
import jax.numpy as jnp
from jax import lax
from jax.experimental import pallas as pl
from jax.experimental.pallas import tpu as pltpu

F32 = jnp.float32
BF16 = jnp.bfloat16

D_MODEL = 2048
GRID_W = 64
HG_HEADS = 8
HG_DK = 128
HG_WIDTH = 1024
RET_HEADS = 4
RET_DK = 128
RET_DV = 256
RET_WIDTH = 1024
PROJ_WIDTH = 8192
N_GROUPS = 4
EXPERTS_PER_GROUP = 8
N_EXPERTS = 32
EXPERT_TOPK = 2
D_EXPERT = 1024
RMS_EPS = 1e-6
ROPE_BASE = 10000.0

COL_HQ, COL_HF_FWD, COL_HF_BWD, COL_HI, COL_HGATE = 0, 1, 2, 3, 4
COL_RQ_512, COL_RK_512 = 10, 11
COL_RV, COL_RGATE = 6, 7

HG_CHUNK = 64
HG_DIAG = 16
HG_LOG_CLAMP = 10.0
RET_CHUNK = 128

MOE_PAD = 256
MOE_ITEM_ROWS = 1024
MOE_GATHER_GROUP = 128
ROUTER_LANES = 128


def _cparams(n_axes, vmem_mb):
    return pltpu.CompilerParams(
        dimension_semantics=("arbitrary",) * n_axes,
        vmem_limit_bytes=vmem_mb * 1024 * 1024,
    )


def _sigmoid(x):
    return 1.0 / (1.0 + jnp.exp(-x))


def _dot(a, b):
    return jnp.dot(a, b, preferred_element_type=F32)


def _dot_nt(a, b):
    return lax.dot_general(a, b, (((1,), (1,)), ((), ())), preferred_element_type=F32)


def _mod_kernel(cc_ref, w_ref, b_ref, o_ref):
    s = cc_ref[...]
    s = s * _sigmoid(s)
    s16 = jnp.concatenate([s, s], axis=0)
    hi = s16.astype(BF16).astype(F32)
    row = lax.broadcasted_iota(jnp.int32, s16.shape, 0)
    lhs = jnp.where(row < 8, hi, s16 - hi).astype(BF16)
    r = _dot(lhs, w_ref[...].astype(BF16))
    o_ref[...] = r[:8] + r[8:] + b_ref[...]


def _modulation(cc, w_mod, b_mod):
    d, n = w_mod.shape
    tn = 1024
    return pl.pallas_call(
        _mod_kernel,
        grid=(n // tn,),
        in_specs=[
            pl.BlockSpec((8, d), lambda j: (0, 0)),
            pl.BlockSpec((d, tn), lambda j: (0, j)),
            pl.BlockSpec((1, tn), lambda j: (0, j)),
        ],
        out_specs=pl.BlockSpec((8, tn), lambda j: (0, j)),
        out_shape=jax.ShapeDtypeStruct((8, n), F32),
        compiler_params=_cparams(1, 40),
        name="mod",
    )(cc, w_mod, b_mod.reshape(1, n))


PROJ_TN = 512
PROLOGUE_ROWS = 32


def _proj_kernel(x_ref, nw_ref, sh_ref, sc_ref, w_ref, lb_ref, o_ref, g_ref, h_scr, *, mod_row):
    j = pl.program_id(1)

    @pl.when(j == 0)
    def _():
        scale = nw_ref[0:1, :] * (1.0 + sc_ref[mod_row:mod_row + 1, :])
        shift = sh_ref[mod_row:mod_row + 1, :]

        def body(r, carry):
            r0 = pl.multiple_of(r * PROLOGUE_ROWS, PROLOGUE_ROWS)
            x = x_ref[pl.ds(r0, PROLOGUE_ROWS), :]
            y = x * lax.rsqrt(jnp.mean(x * x, axis=-1, keepdims=True) + RMS_EPS)
            h_scr[pl.ds(r0, PROLOGUE_ROWS), :] = (y * scale + shift).astype(BF16)
            return carry
        lax.fori_loop(0, x_ref.shape[0] // PROLOGUE_ROWS, body, 0)

    is_gate = jnp.logical_and(j >= 2, j < 6)
    is_silu = jnp.logical_or(j < 2, jnp.logical_or(jnp.logical_and(j >= 8, j < 10), j >= 14))
    is_lin = jnp.logical_not(jnp.logical_or(is_gate, is_silu))

    @pl.when(is_lin)
    def _():
        lin_scale = jnp.where(j == 11, RET_DK ** -0.5, 1.0).astype(F32)
        o_ref[...] = (_dot(h_scr[...], w_ref[...]) * lin_scale).astype(BF16)

    @pl.when(is_silu)
    def _():
        acc = _dot(h_scr[...], w_ref[...])
        o_ref[...] = (acc * _sigmoid(acc)).astype(BF16)

    @pl.when(is_gate)
    def _():
        acc = _dot(h_scr[...], w_ref[...])
        lb = lb_ref[...]
        g = jnp.maximum(jnp.log(lb + (1.0 - lb) * _sigmoid(acc)), -HG_LOG_CLAMP)
        g_ref[...] = g
        o_ref[...] = g.astype(BF16)


def _projection(x2d, norm_w, mod, w_in, lb_cols, mod_row, tm, name):
    n, d = x2d.shape
    tn = PROJ_TN
    return pl.pallas_call(
        functools.partial(_proj_kernel, mod_row=mod_row),
        grid=(n // tm, PROJ_WIDTH // tn),
        in_specs=[
            pl.BlockSpec((tm, d), lambda i, j: (i, 0)),
            pl.BlockSpec((4, d), lambda i, j: (0, 0)),
            pl.BlockSpec((8, d), lambda i, j: (0, 0)),
            pl.BlockSpec((8, d), lambda i, j: (0, 1)),
            pl.BlockSpec((d, tn), lambda i, j: (0, j)),
            pl.BlockSpec((1, tn), lambda i, j: (0, j)),
        ],
        out_specs=[
            pl.BlockSpec((tm, tn), lambda i, j: (i, j)),
            pl.BlockSpec((tm, tn), lambda i, j: (i, jnp.clip(j - 2, 0, 3))),
        ],
        out_shape=[
            jax.ShapeDtypeStruct((n, PROJ_WIDTH), BF16),
            jax.ShapeDtypeStruct((n, 2 * HG_WIDTH), F32),
        ],
        scratch_shapes=[pltpu.VMEM((tm, d), BF16)],
        compiler_params=_cparams(2, 48),
        name=name,
    )(x2d, norm_w, mod, mod, w_in, lb_cols)


def _cumsum_rows(g, rev):
    c = g.shape[0]
    row = lax.broadcasted_iota(jnp.int32, g.shape, 0)
    b = g
    d = 1
    while d < c:
        if rev:
            b = b + jnp.where(row < c - d, pltpu.roll(b, c - d, axis=0), 0.0)
        else:
            b = b + jnp.where(row >= d, pltpu.roll(b, d, axis=0), 0.0)
        d *= 2
    return b


def _row_refs(b, idxs, blk):
    parts = [jnp.broadcast_to(b[i:i + 1, :], (blk, b.shape[1])) for i in idxs]
    return parts[0] if len(parts) == 1 else jnp.concatenate(parts, axis=0)


def _hgrn_levels(c, rev):
    t = lax.broadcasted_iota(jnp.int32, (c, c), 0)
    s = lax.broadcasted_iota(jnp.int32, (c, c), 1)
    levels = []
    m = c // 2
    while m >= HG_DIAG:
        blk = 2 * m
        same = ((t ^ s) & ~(blk - 1)) == 0
        if rev:
            idxs = [b0 + m for b0 in range(0, c, blk)]
            cross = jnp.logical_and((t & m) == 0, (s & m) != 0)
        else:
            idxs = [b0 + m - 1 for b0 in range(0, c, blk)]
            cross = jnp.logical_and((t & m) != 0, (s & m) == 0)
        levels.append((idxs, blk, jnp.logical_and(same, cross), True))
        m //= 2
    blk = HG_DIAG
    same = ((t ^ s) & ~(blk - 1)) == 0
    if rev:
        idxs = [b0 + blk // 2 for b0 in range(0, c, blk)]
        mask = jnp.logical_and(same, s >= t)
    else:
        idxs = [b0 + blk // 2 - 1 for b0 in range(0, c, blk)]
        mask = jnp.logical_and(same, s <= t)
    levels.append((idxs, blk, mask, False))
    return levels


def _hgrn_head(q_bf, g, v_bf, st, rev, levels):
    c = g.shape[0]
    kk = 1.0 - jnp.exp(g)
    b = _cumsum_rows(g, rev)

    o = None
    if levels is not None:
        q = q_bf.astype(F32)
        att = jnp.zeros((c, c), F32)
        for idxs, blk, mask, clamp in levels:
            ref = _row_refs(b, idxs, blk)
            dq = b - ref
            dk = ref - b
            if clamp:
                dq = jnp.minimum(dq, 0.0)
                dk = jnp.minimum(dk, 0.0)
            a_l = _dot_nt((q * jnp.exp(dq)).astype(BF16), (kk * jnp.exp(dk)).astype(BF16))
            att = att + jnp.where(mask, a_l, 0.0)
        o = _dot(att.astype(BF16), v_bf) + _dot_nt((q * jnp.exp(b)).astype(BF16), st.astype(BF16))

    b_end = b[0:1, :] if rev else b[c - 1:c, :]
    k_end = (kk * jnp.exp(b_end - b)).astype(BF16)
    st_new = st * jnp.exp(b_end) + _dot(v_bf.astype(F32).T.astype(BF16), k_end)
    return o, st_new


def _hgrn_ctx_kernel(gf_ref, vf_ref, gb_ref, vb_ref, st_ref):
    @pl.when(pl.program_id(0) == 0)
    def _():
        st_ref[...] = jnp.zeros_like(st_ref)

    for h in range(HG_HEADS):
        sl = slice(h * HG_DK, (h + 1) * HG_DK)
        _, st_ref[h] = _hgrn_head(None, gf_ref[:, sl], vf_ref[:, sl], st_ref[h], False, None)
        hb = HG_HEADS + h
        _, st_ref[hb] = _hgrn_head(None, gb_ref[:, sl], vb_ref[:, sl], st_ref[hb], True, None)


def _hgrn_kernel(qf_ref, gf_ref, vf_ref, qb_ref, gb_ref, vb_ref, st0_ref, of_ref, ob_ref, st_ref):
    @pl.when(pl.program_id(0) == 0)
    def _():
        st_ref[...] = st0_ref[...]

    c = gf_ref.shape[0]
    lev_f = _hgrn_levels(c, False)
    lev_b = _hgrn_levels(c, True)
    for h in range(HG_HEADS):
        sl = slice(h * HG_DK, (h + 1) * HG_DK)
        of_ref[:, sl], st_ref[h] = _hgrn_head(qf_ref[:, sl], gf_ref[:, sl], vf_ref[:, sl],
                                              st_ref[h], False, lev_f)
        hb = HG_HEADS + h
        ob_ref[:, sl], st_ref[hb] = _hgrn_head(qb_ref[:, sl], gb_ref[:, sl], vb_ref[:, sl],
                                               st_ref[hb], True, lev_b)


def _hgrn_scan(p, g, pc, gc):
    c = HG_CHUNK
    w = HG_WIDTH
    n, l = p.shape[0], pc.shape[0]
    n_x, n_ctx = n // c, l // c
    st_shape = (2 * HG_HEADS, HG_DK, HG_DK)

    def spec(chunk_fn, col):
        return pl.BlockSpec((c, w), lambda s: (chunk_fn(s), col))

    st0 = pl.pallas_call(
        _hgrn_ctx_kernel,
        grid=(n_ctx,),
        in_specs=[
            spec(lambda s: s, 0), spec(lambda s: s, COL_HI),
            spec(lambda s: n_ctx - 1 - s, 1), spec(lambda s: n_ctx - 1 - s, COL_HI),
        ],
        out_specs=pl.BlockSpec(st_shape, lambda s: (0, 0, 0)),
        out_shape=jax.ShapeDtypeStruct(st_shape, F32),
        compiler_params=_cparams(1, 32),
        name="hgrn_ctx",
    )(gc, pc, gc, pc)

    fwd = lambda s: s
    bwd = lambda s: n_x - 1 - s
    return pl.pallas_call(
        _hgrn_kernel,
        grid=(n_x,),
        in_specs=[
            spec(fwd, COL_HQ), spec(fwd, 0), spec(fwd, COL_HI),
            spec(bwd, COL_HQ), spec(bwd, 1), spec(bwd, COL_HI),
            pl.BlockSpec(st_shape, lambda s: (0, 0, 0)),
        ],
        out_specs=[spec(fwd, 0), spec(bwd, 0)],
        out_shape=[jax.ShapeDtypeStruct((n, w), F32)] * 2,
        scratch_shapes=[pltpu.VMEM(st_shape, F32)],
        compiler_params=_cparams(1, 32),
        name="hgrn",
    )(p, g, p, p, g, p, st0)


def _rope(x_bf, cs, sn):
    x = x_bf.astype(F32)
    return x * cs + pltpu.roll(x, RET_DK // 2, axis=1) * sn


def _ret_head(q, k, v_bf, lg, st, rev):
    c = k.shape[0]
    pos = lax.broadcasted_iota(jnp.int32, (c, RET_DK), 0).astype(F32)
    o = None
    if q is not None:
        t = lax.broadcasted_iota(jnp.int32, (c, c), 0)
        s = lax.broadcasted_iota(jnp.int32, (c, c), 1)
        diff = (s - t) if rev else (t - s)
        decay = jnp.where(diff >= 0, jnp.exp(lg * jnp.maximum(diff, 0).astype(F32)), 0.0)
        q_dec = jnp.exp(lg * ((c - pos) if rev else (pos + 1.0)))
        att = _dot_nt(q.astype(BF16), k.astype(BF16)) * decay
        o = _dot(att.astype(BF16), v_bf) + _dot_nt((q * q_dec).astype(BF16), st.astype(BF16))
    k_dec = jnp.exp(lg * (pos if rev else (c - 1.0 - pos)))
    chunk_decay = jnp.exp(jnp.full((1, RET_DK), lg * c, F32))
    st_new = st * chunk_decay + _dot(v_bf.astype(F32).T.astype(BF16), (k * k_dec).astype(BF16))
    return o, st_new


def _ret_ctx_kernel(lg_ref, kf_ref, vf_ref, kb_ref, vb_ref, st_ref):
    @pl.when(pl.program_id(0) == 0)
    def _():
        st_ref[...] = jnp.zeros_like(st_ref)

    for h in range(RET_HEADS):
        sk = slice(h * RET_DK, (h + 1) * RET_DK)
        sv = slice(h * RET_DV, (h + 1) * RET_DV)
        _, st_ref[h] = _ret_head(None, kf_ref[:, sk].astype(F32), vf_ref[:, sv], lg_ref[0, h],
                                 st_ref[h], False)
        hb = RET_HEADS + h
        _, st_ref[hb] = _ret_head(None, kb_ref[:, sk].astype(F32), vb_ref[:, sv], lg_ref[1, h],
                                  st_ref[hb], True)


def _ret_kernel(lg_ref, qf_ref, kf_ref, vf_ref, csf_ref, snf_ref,
                qb_ref, kb_ref, vb_ref, csb_ref, snb_ref, st0_ref, of_ref, ob_ref, st_ref):
    @pl.when(pl.program_id(0) == 0)
    def _():
        st_ref[...] = st0_ref[...]

    csf, snf = csf_ref[...], snf_ref[...]
    csb, snb = csb_ref[...], snb_ref[...]
    for h in range(RET_HEADS):
        sk = slice(h * RET_DK, (h + 1) * RET_DK)
        sv = slice(h * RET_DV, (h + 1) * RET_DV)
        of_ref[:, sv], st_ref[h] = _ret_head(
            _rope(qf_ref[:, sk], csf, snf), _rope(kf_ref[:, sk], csf, snf), vf_ref[:, sv],
            lg_ref[0, h], st_ref[h], False)
        hb = RET_HEADS + h
        ob_ref[:, sv], st_ref[hb] = _ret_head(
            _rope(qb_ref[:, sk], csb, snb), _rope(kb_ref[:, sk], csb, snb), vb_ref[:, sv],
            lg_ref[1, h], st_ref[hb], True)


def _ret_scan(p, pc, ret_lg, cs_tab, sn_tab):
    c = RET_CHUNK
    n, l = p.shape[0], pc.shape[0]
    n_x, n_ctx = n // c, l // c
    qk_w = RET_HEADS * RET_DK
    st_shape = (2 * RET_HEADS, RET_DV, RET_DK)
    smem = pl.BlockSpec(memory_space=pltpu.SMEM)

    def kv_specs(chunk_fn):
        return [
            pl.BlockSpec((c, qk_w), lambda s: (chunk_fn(s), COL_RK_512)),
            pl.BlockSpec((c, RET_WIDTH), lambda s: (chunk_fn(s), COL_RV)),
        ]

    st0 = pl.pallas_call(
        _ret_ctx_kernel,
        grid=(n_ctx,),
        in_specs=[smem, *kv_specs(lambda s: s), *kv_specs(lambda s: n_ctx - 1 - s)],
        out_specs=pl.BlockSpec(st_shape, lambda s: (0, 0, 0)),
        out_shape=jax.ShapeDtypeStruct(st_shape, F32),
        compiler_params=_cparams(1, 32),
        name="ret_ctx",
    )(ret_lg, pc, pc, pc, pc)

    def specs(chunk_fn):
        return [
            pl.BlockSpec((c, qk_w), lambda s: (chunk_fn(s), COL_RQ_512)),
            *kv_specs(chunk_fn),
            pl.BlockSpec((c, RET_DK), lambda s: (chunk_fn(s), 0)),
            pl.BlockSpec((c, RET_DK), lambda s: (chunk_fn(s), 0)),
        ]

    fwd = lambda s: s
    bwd = lambda s: n_x - 1 - s
    return pl.pallas_call(
        _ret_kernel,
        grid=(n_x,),
        in_specs=[smem, *specs(fwd), *specs(bwd), pl.BlockSpec(st_shape, lambda s: (0, 0, 0))],
        out_specs=[
            pl.BlockSpec((c, RET_WIDTH), lambda s: (fwd(s), 0)),
            pl.BlockSpec((c, RET_WIDTH), lambda s: (bwd(s), 0)),
        ],
        out_shape=[jax.ShapeDtypeStruct((n, RET_WIDTH), F32)] * 2,
        scratch_shapes=[pltpu.VMEM(st_shape, F32)],
        compiler_params=_cparams(1, 32),
        name="ret",
    )(ret_lg, p, p, p, cs_tab, sn_tab, p, p, p, cs_tab, sn_tab, st0)


def _rms(y):
    return y * lax.rsqrt(jnp.mean(y * y, axis=-1, keepdims=True) + RMS_EPS)


def _outproj_kernel(of_ref, ob_ref, rf_ref, rb_ref, hgate_ref, rgate_ref, x_ref, wout_ref,
                    hgw_ref, rnw_ref, nw_ref, g1_ref, sh2_ref, sc2_ref, wrh_ref, wrl_ref, br_ref,
                    x1_ref, h2_ref, lg_ref, cat_scr):
    o = of_ref[...] + ob_ref[...]
    for h in range(HG_HEADS):
        sl = slice(h * HG_DK, (h + 1) * HG_DK)
        cat_scr[:, sl] = (_rms(o[:, sl]) * hgw_ref[0:1, sl]
                          * hgate_ref[:, sl].astype(F32)).astype(BF16)
    r = rf_ref[...] + rb_ref[...]
    for h in range(RET_HEADS):
        sl = slice(h * RET_DV, (h + 1) * RET_DV)
        cat_scr[:, HG_WIDTH + h * RET_DV:HG_WIDTH + (h + 1) * RET_DV] = (
            _rms(r[:, sl]) * rnw_ref[0:1, sl] * rgate_ref[:, sl].astype(F32)).astype(BF16)

    y = _dot(cat_scr[...], wout_ref[...])
    x1 = x_ref[...] + g1_ref[0:1, :] * (_rms(y) * nw_ref[1:2, :])
    x1_ref[...] = x1
    h2 = _rms(x1) * nw_ref[2:3, :] * (1.0 + sc2_ref[0:1, :]) + sh2_ref[0:1, :]
    h2_ref[...] = _pack_pairs(h2)
    hi = h2.astype(BF16)
    lo = (h2 - hi.astype(F32)).astype(BF16)
    lg_ref[...] = (_dot(hi, wrh_ref[...]) + _dot(lo, wrh_ref[...]) + _dot(hi, wrl_ref[...])
                   + br_ref[...])


def _out_projection(o_f, o_b, r_f, r_b, p, x2d, w_out_bf, hg_norm_w, ret_norm_w, norm_w, mod,
                    wr_hi, wr_lo, br):
    n, d = x2d.shape
    tm = 256
    row = lambda w: pl.BlockSpec((tm, w), lambda i: (i, 0))
    full = lambda a: pl.BlockSpec(a.shape, lambda i: (0,) * a.ndim)
    modcol = lambda k: pl.BlockSpec((8, d), lambda i: (0, k))
    return pl.pallas_call(
        _outproj_kernel,
        grid=(n // tm,),
        in_specs=[
            row(HG_WIDTH), row(HG_WIDTH), row(RET_WIDTH), row(RET_WIDTH),
            pl.BlockSpec((tm, HG_WIDTH), lambda i: (i, COL_HGATE)),
            pl.BlockSpec((tm, RET_WIDTH), lambda i: (i, COL_RGATE)),
            row(d), full(w_out_bf), full(hg_norm_w), full(ret_norm_w), full(norm_w),
            modcol(2), modcol(3), modcol(4),
            full(wr_hi), full(wr_lo), full(br),
        ],
        out_specs=[row(d), row(d // 2), row(ROUTER_LANES)],
        out_shape=[
            jax.ShapeDtypeStruct((n, d), F32),
            jax.ShapeDtypeStruct((n, d // 2), jnp.uint32),
            jax.ShapeDtypeStruct((n, ROUTER_LANES), F32),
        ],
        scratch_shapes=[pltpu.VMEM((tm, d), BF16)],
        compiler_params=_cparams(1, 48),
        name="outproj",
    )(o_f, o_b, r_f, r_b, p, p, x2d, w_out_bf, hg_norm_w, ret_norm_w, norm_w, mod, mod, mod,
      wr_hi, wr_lo, br)


def _routing_plan(logits, bg, be, n_slots, n_items):
    t = logits.shape[0]
    lg = logits[:, :N_GROUPS] + bg
    le = logits[:, N_GROUPS:N_GROUPS + N_EXPERTS] + be
    grp = jnp.argmax(lg, axis=-1)
    p_sel = jnp.take_along_axis(jax.nn.softmax(lg, axis=-1), grp[:, None], axis=-1)
    le3 = le.reshape(t, N_GROUPS, EXPERTS_PER_GROUP)
    lin = jnp.take_along_axis(le3, grp[:, None, None], axis=1)[:, 0]
    top_v, top_i = lax.top_k(lin, EXPERT_TOPK)
    gates = p_sel * jax.nn.softmax(top_v, axis=-1)
    flat_e = (grp[:, None] * EXPERTS_PER_GROUP + top_i).reshape(-1).astype(jnp.int32)

    onehot = (flat_e[:, None] == jnp.arange(N_EXPERTS, dtype=jnp.int32)[None, :]).astype(jnp.int32)
    csum = jnp.cumsum(onehot, axis=0)
    rank = jnp.take_along_axis(csum, flat_e[:, None], axis=1)[:, 0] - 1
    counts = csum[-1]
    padded = (counts + MOE_PAD - 1) // MOE_PAD * MOE_PAD
    pstart = jnp.cumsum(padded) - padded
    slot_of = (pstart[flat_e] + rank).astype(jnp.int32)
    tok = jnp.arange(t * EXPERT_TOPK, dtype=jnp.int32) // EXPERT_TOPK
    slot_tok = jnp.zeros((n_slots,), jnp.int32).at[slot_of].set(tok)

    r = MOE_ITEM_ROWS
    ipe = (padded + r - 1) // r
    iend = jnp.cumsum(ipe)
    ioff = iend - ipe
    total = iend[-1]
    a = jnp.arange(n_items, dtype=jnp.int32)
    e_a = jnp.minimum(jnp.searchsorted(iend, a, side="right"), N_EXPERTS - 1).astype(jnp.int32)
    valid = a < total
    sub = a - ioff[e_a]
    irows = jnp.clip(padded[e_a] - sub * r, 0, r)
    istart = pstart[e_a] + sub * r
    e_last = e_a[jnp.maximum(total - 1, 0)]
    used = jnp.sum(padded)
    fill_start = used + (a - total) * r
    fill_rows = jnp.clip(n_slots - fill_start, 0, r)
    item_e = jnp.where(valid, e_a, e_last).astype(jnp.int32)
    item_rows = jnp.where(valid, irows, 0).astype(jnp.int32)
    item_fill = jnp.where(valid, 0, fill_rows).astype(jnp.int32)
    item_start = jnp.where(valid, istart, jnp.where(fill_rows > 0, fill_start, 0)).astype(jnp.int32)
    return gates, slot_of, slot_tok, item_e, item_start, item_rows, item_fill


def _unpack_pairs(u):
    lo = lax.bitcast_convert_type(u << 16, F32)
    hi = lax.bitcast_convert_type(u & jnp.uint32(0xFFFF0000), F32)
    return jnp.concatenate([lo, hi], axis=1)


def _pack_pairs(y):
    w = y.shape[1] // 2
    bits = lax.bitcast_convert_type(y.astype(BF16).astype(F32), jnp.uint32)
    return (bits[:, :w] >> 16) | (bits[:, w:] & jnp.uint32(0xFFFF0000))


def _moe_kernel(item_e, item_start, item_rows, item_fill, slot_tok,
                h2p_hbm, wg_ref, wu_ref, wd_ref, yb_hbm,
                xbuf, xs_bf, act, ystage, pf, gsem, osem):
    a = pl.program_id(0)
    t = pl.program_id(1)
    n_items = pl.num_programs(0)
    n = item_rows[a]
    start = item_start[a]
    rb_rows = MOE_PAD
    n_rb = n // rb_rows
    half_w = wg_ref.shape[2]
    cur = a % 2
    nxt = 1 - cur
    a1 = jnp.minimum(a + 1, n_items - 1)
    n_next = jnp.where(a + 1 < n_items, item_rows[a1], 0)
    start_next = item_start[a1]

    def row_copy(tok, slot, i):
        return pltpu.make_async_copy(h2p_hbm.at[pl.ds(tok, 1), :], xbuf.at[slot, pl.ds(i, 1), :],
                                     gsem.at[slot])

    @pl.when(jnp.logical_and(a == 0, t == 0))
    def _():
        def issue(i, carry):
            row_copy(slot_tok[start + i], 0, i).start()
            return carry
        lax.fori_loop(0, n, issue, 0)

    @pl.when(t == 0)
    def _():
        pf[0] = 0

        def drain(bk, carry):
            r0 = pl.multiple_of(bk * rb_rows, rb_rows)
            pltpu.make_async_copy(h2p_hbm.at[pl.ds(0, rb_rows), :],
                                  xbuf.at[cur, pl.ds(r0, rb_rows), :], gsem.at[cur]).wait()
            return carry
        lax.fori_loop(0, n_rb, drain, 0)

    def prefetch_group():
        base = pf[0]
        for q in range(MOE_GATHER_GROUP):
            row_copy(slot_tok[start_next + base + q], nxt, base + q).start()
        pf[0] = base + MOE_GATHER_GROUP

    def run_blocks(compute):
        def body(rb, carry):
            more = pf[0] < n_next

            @pl.when(more)
            def _():
                compute(rb)
                prefetch_group()

            @pl.when(jnp.logical_not(more))
            def _():
                compute(rb)
            return carry
        lax.fori_loop(0, n_rb, body, 0)

    def gate_up(half):
        def compute(rb):
            r0 = pl.multiple_of(rb * rb_rows, rb_rows)
            if half == 0:
                xs = _unpack_pairs(xbuf[cur, pl.ds(r0, rb_rows), :]).astype(BF16)
                xs_bf[pl.ds(r0, rb_rows), :] = xs
            else:
                xs = xs_bf[pl.ds(r0, rb_rows), :]
            g = _dot(xs, wg_ref[0].astype(BF16))
            u = _dot(xs, wu_ref[0].astype(BF16))
            act[pl.ds(r0, rb_rows), half * half_w:(half + 1) * half_w] = (
                g * _sigmoid(g) * u).astype(BF16)
        run_blocks(compute)

    def out_copy(slot, r0):
        return pltpu.make_async_copy(
            ystage.at[slot],
            yb_hbm.at[pl.ds(pl.multiple_of(start + r0, rb_rows), rb_rows), :],
            osem.at[slot])

    def down():
        def compute(rb):
            r0 = pl.multiple_of(rb * rb_rows, rb_rows)
            slot = rb % 2
            y = _dot(act[pl.ds(r0, rb_rows), :], wd_ref[0].astype(BF16))

            @pl.when(rb >= 2)
            def _():
                out_copy(slot, r0).wait()

            ystage[slot] = _pack_pairs(y)
            out_copy(slot, r0).start()
        run_blocks(compute)

        @pl.when(n_rb >= 1)
        def _():
            out_copy((n_rb - 1) % 2, 0).wait()

        @pl.when(n_rb >= 2)
        def _():
            out_copy(n_rb % 2, 0).wait()

        def tail(i, carry):
            row_copy(slot_tok[start_next + i], nxt, i).start()
            return carry
        lax.fori_loop(pf[0], n_next, tail, 0)

    pl.when(t == 0)(functools.partial(gate_up, 0))
    pl.when(t == 1)(functools.partial(gate_up, 1))
    pl.when(t == 2)(down)

    fill = item_fill[a]

    @pl.when(jnp.logical_and(t == 0, fill > 0))
    def _():
        ystage[0] = jnp.zeros(ystage.shape[1:], ystage.dtype)

        def zero_block(bk, carry):
            cp = out_copy(0, bk * rb_rows)
            cp.start()
            cp.wait()
            return carry
        lax.fori_loop(0, fill // rb_rows, zero_block, 0)


def _moe_ffn(h2p, w_gate, w_up, w_down, item_e, item_start, item_rows, item_fill, slot_tok,
             n_slots):
    pw = h2p.shape[1]
    d = 2 * pw
    n_items = item_e.shape[0]
    half_w = D_EXPERT // 2

    def wgu_map(a, t, ie, ist, ir, fl, st):
        return (ie[a], 0, jnp.where(ir[a] > 0, jnp.minimum(t, 1), 1))

    def wd_map(a, t, ie, ist, ir, fl, st):
        return (jnp.where(t >= 2, ie[a], ie[jnp.maximum(a - 1, 0)]), 0, 0)

    grid_spec = pltpu.PrefetchScalarGridSpec(
        num_scalar_prefetch=5,
        grid=(n_items, 3),
        in_specs=[
            pl.BlockSpec(memory_space=pl.ANY),
            pl.BlockSpec((1, d, half_w), wgu_map),
            pl.BlockSpec((1, d, half_w), wgu_map),
            pl.BlockSpec((1, D_EXPERT, d), wd_map),
        ],
        out_specs=pl.BlockSpec(memory_space=pl.ANY),
        scratch_shapes=[
            pltpu.VMEM((2, MOE_ITEM_ROWS, pw), jnp.uint32),
            pltpu.VMEM((MOE_ITEM_ROWS, d), BF16),
            pltpu.VMEM((MOE_ITEM_ROWS, D_EXPERT), BF16),
            pltpu.VMEM((2, MOE_PAD, pw), jnp.uint32),
            pltpu.SMEM((1,), jnp.int32),
            pltpu.SemaphoreType.DMA((2,)),
            pltpu.SemaphoreType.DMA((2,)),
        ],
    )
    return pl.pallas_call(
        _moe_kernel,
        grid_spec=grid_spec,
        out_shape=jax.ShapeDtypeStruct((n_slots, pw), jnp.uint32),
        compiler_params=_cparams(2, 56),
        name="moe",
    )(item_e, item_start, item_rows, item_fill, slot_tok, h2p, w_gate, w_up, w_down)


def _combine_kernel(slot_of, yb_hbm, gates_ref, x1_ref, nw_ref, g2_ref, out_ref, rows, sem):
    i = pl.program_id(0)
    n_tiles = pl.num_programs(0)
    tm = x1_ref.shape[0]

    def gather(tile, buf):
        def issue(r8, carry):
            for dr in range(8):
                r = r8 * 8 + dr
                base = (tile * tm + r) * EXPERT_TOPK
                for k in range(EXPERT_TOPK):
                    pltpu.make_async_copy(yb_hbm.at[pl.ds(slot_of[base + k], 1), :],
                                          rows.at[buf, k, pl.ds(r, 1), :], sem.at[buf]).start()
            return carry
        lax.fori_loop(0, tm // 8, issue, 0)

    @pl.when(i == 0)
    def _():
        gather(0, 0)

    @pl.when(i + 1 < n_tiles)
    def _():
        gather(i + 1, (i + 1) % 2)

    buf = i % 2
    for k in range(EXPERT_TOPK):
        pltpu.make_async_copy(yb_hbm.at[pl.ds(0, tm), :], rows.at[buf, k], sem.at[buf]).wait()

    g = gates_ref[...]
    y = (_unpack_pairs(rows[buf, 0]) * g[:, 0:1] + _unpack_pairs(rows[buf, 1]) * g[:, 1:2])
    out_ref[...] = x1_ref[...] + g2_ref[0:1, :] * (_rms(y) * nw_ref[3:4, :])


def _combine(slot_of, yb, gates, x1, norm_w, mod):
    n, d = x1.shape
    tm = 256
    grid_spec = pltpu.PrefetchScalarGridSpec(
        num_scalar_prefetch=1,
        grid=(n // tm,),
        in_specs=[
            pl.BlockSpec(memory_space=pl.ANY),
            pl.BlockSpec((tm, EXPERT_TOPK), lambda i, so: (i, 0)),
            pl.BlockSpec((tm, d), lambda i, so: (i, 0)),
            pl.BlockSpec((4, d), lambda i, so: (0, 0)),
            pl.BlockSpec((8, d), lambda i, so: (0, 5)),
        ],
        out_specs=pl.BlockSpec((tm, d), lambda i, so: (i, 0)),
        scratch_shapes=[
            pltpu.VMEM((2, EXPERT_TOPK, tm, d // 2), jnp.uint32),
            pltpu.SemaphoreType.DMA((2,)),
        ],
    )
    return pl.pallas_call(
        _combine_kernel,
        grid_spec=grid_spec,
        out_shape=jax.ShapeDtypeStruct((n, d), F32),
        compiler_params=_cparams(1, 32),
        name="combine",
    )(slot_of, yb, gates, x1, norm_w, mod)


def _rope_tables(n):
    rows = n // GRID_W
    r = jnp.broadcast_to(jnp.arange(rows)[:, None], (rows, GRID_W)).reshape(-1).astype(F32)
    cidx = jnp.broadcast_to(jnp.arange(GRID_W)[None, :], (rows, GRID_W)).reshape(-1).astype(F32)
    n_freq = RET_DK // 4
    inv = ROPE_BASE ** (-jnp.arange(n_freq, dtype=F32) / n_freq)
    ang = jnp.concatenate([r[:, None] * inv, cidx[:, None] * inv], axis=-1)
    cos, sin = jnp.cos(ang), jnp.sin(ang)
    cs = jnp.concatenate([cos, cos], axis=-1)
    sn = jnp.concatenate([-sin, sin], axis=-1)
    return cs, sn


def kernel(x, c, ctx, c_ctx, w_mod, b_mod, norm_w, w_in, hg_lb_logits, hg_norm_w, ret_decay_logits,
           ret_norm_w, w_out, router_group_w, router_group_b, router_expert_w, router_expert_b,
           w_gate, w_up, w_down):
    bsz, n, d = x.shape
    l = ctx.shape[1]
    assert bsz == 1 and w_mod.shape[0] == 1, "single sample, single layer"
    x2d, ctx2d = x[0], ctx[0]

    cc = jnp.zeros((8, d), F32).at[0].set(c[0]).at[1].set(c_ctx)
    mod = _modulation(cc, w_mod[0], b_mod[0])

    lb = jnp.cumsum(jax.nn.softmax(hg_lb_logits.astype(F32), axis=0), axis=0)[0]
    lb_cols = jnp.zeros((1, PROJ_WIDTH), F32).at[0, HG_WIDTH:3 * HG_WIDTH].set(lb.reshape(-1))
    w_in_bf = w_in[0].astype(BF16)
    p, g = _projection(x2d, norm_w[0], mod, w_in_bf, lb_cols, 0, 1024, "proj_x")
    pc, gc = _projection(ctx2d, norm_w[0], mod, w_in_bf, lb_cols, 1, l, "proj_ctx")

    ret_lg = jax.nn.log_sigmoid(ret_decay_logits[0].astype(F32))
    cs_tab, sn_tab = _rope_tables(n)

    o_f, o_b = _hgrn_scan(p, g, pc, gc)
    r_f, r_b = _ret_scan(p, pc, ret_lg, cs_tab, sn_tab)

    wr = jnp.concatenate([router_group_w[0], router_expert_w[0]], axis=1)
    wr = jnp.pad(wr, ((0, 0), (0, ROUTER_LANES - wr.shape[1])))
    wr_hi = wr.astype(BF16)
    wr_lo = (wr - wr_hi.astype(F32)).astype(BF16)
    br = jnp.zeros((1, ROUTER_LANES), F32)
    x1, h2, logits = _out_projection(
        o_f, o_b, r_f, r_b, p, x2d, w_out[0].astype(BF16), hg_norm_w, ret_norm_w, norm_w[0], mod,
        wr_hi, wr_lo, br)

    n_assign = n * EXPERT_TOPK
    n_slots = n_assign + N_EXPERTS * MOE_PAD
    n_items = N_EXPERTS + n_slots // MOE_ITEM_ROWS
    gates, slot_of, slot_tok, item_e, item_start, item_rows, item_fill = _routing_plan(
        logits, router_group_b[0], router_expert_b[0], n_slots, n_items)

    yb = _moe_ffn(h2, w_gate[0], w_up[0], w_down[0], item_e, item_start, item_rows, item_fill,
                  slot_tok, n_slots)
    out = _combine(slot_of, yb, gates, x1, norm_w[0], mod)
    return out[None]
```

```python
import functools

import jax
import jax.numpy as jnp
from jax import lax
from jax.experimental import pallas as pl
from jax.experimental.pallas import tpu as pltpu

F32 = jnp.float32
BF16 = jnp.bfloat16

D_MODEL = 2048
GRID_W = 64
HG_HEADS = 8
HG_DK = 128
HG_WIDTH = 1024
RET_HEADS = 4
RET_DK = 128
RET_DV = 256
RET_WIDTH = 1024
PROJ_WIDTH = 8192
N_GROUPS = 4
EXPERTS_PER_GROUP = 8
N_EXPERTS = 32
EXPERT_TOPK = 2
D_EXPERT = 1024
RMS_EPS = 1e-6
ROPE_BASE = 10000.0

COL_HQ, COL_HF_FWD, COL_HF_BWD, COL_HI, COL_HGATE = 0, 1, 2, 3, 4
COL_RQ_512, COL_RK_512 = 10, 11
COL_RV, COL_RGATE = 6, 7

HG_CHUNK = 128
HG_DIAG = 16
HG_LOG_CLAMP = 10.0
RET_CHUNK = 128

MOE_PAD = 256
MOE_ITEM_ROWS = 1024
MOE_GATHER_GROUP = 128
ROUTER_LANES = 128


def _cparams(n_axes, vmem_mb):
    return pltpu.CompilerParams(
        dimension_semantics=("arbitrary",) * n_axes,
        vmem_limit_bytes=vmem_mb * 1024 * 1024,
    )


def _sigmoid(x):
    return 1.0 / (1.0 + jnp.exp(-x))


def _dot(a, b):
    return jnp.dot(a, b, preferred_element_type=F32)


def _dot_nt(a, b):
    return lax.dot_general(a, b, (((1,), (1,)), ((), ())), preferred_element_type=F32)


def _dot_tn(a, b):
    return lax.dot_general(a, b, (((0,), (0,)), ((), ())), preferred_element_type=F32)


def _mod_kernel(cc_ref, w_ref, b_ref, o_ref):
    s = cc_ref[...]
    s = s * _sigmoid(s)
    s16 = jnp.concatenate([s, s], axis=0)
    hi = s16.astype(BF16).astype(F32)
    row = lax.broadcasted_iota(jnp.int32, s16.shape, 0)
    lhs = jnp.where(row < 8, hi, s16 - hi).astype(BF16)
    r = _dot(lhs, w_ref[...].astype(BF16))
    o_ref[...] = r[:8] + r[8:] + b_ref[...]


def _modulation(cc, w_mod, b_mod):
    d, n = w_mod.shape
    tn = 1024
    return pl.pallas_call(
        _mod_kernel,
        grid=(n // tn,),
        in_specs=[
            pl.BlockSpec((8, d), lambda j: (0, 0)),
            pl.BlockSpec((d, tn), lambda j: (0, j)),
            pl.BlockSpec((1, tn), lambda j: (0, j)),
        ],
        out_specs=pl.BlockSpec((8, tn), lambda j: (0, j)),
        out_shape=jax.ShapeDtypeStruct((8, n), F32),
        compiler_params=_cparams(1, 40),
        name="mod",
    )(cc, w_mod, b_mod.reshape(1, n))


PROJ_TN = 1024
PROLOGUE_ROWS = 32


def _proj_kernel(x_ref, nw_ref, sh_ref, sc_ref, w_ref, cols_ref, o_ref, g_ref, h_scr, *, mod_row):
    j = pl.program_id(1)

    @pl.when(j == 0)
    def _():
        scale = nw_ref[0:1, :] * (1.0 + sc_ref[mod_row:mod_row + 1, :])
        shift = sh_ref[mod_row:mod_row + 1, :]

        def body(r, carry):
            r0 = pl.multiple_of(r * PROLOGUE_ROWS, PROLOGUE_ROWS)
            x = x_ref[pl.ds(r0, PROLOGUE_ROWS), :]
            y = x * lax.rsqrt(jnp.mean(x * x, axis=-1, keepdims=True) + RMS_EPS)
            h_scr[pl.ds(r0, PROLOGUE_ROWS), :] = (y * scale + shift).astype(BF16)
            return carry
        lax.fori_loop(0, x_ref.shape[0] // PROLOGUE_ROWS, body, 0)

    is_gate = jnp.logical_or(j == 1, j == 2)
    is_silu = jnp.logical_or(j == 0, jnp.logical_or(j == 4, j == 7))
    is_lin = jnp.logical_not(jnp.logical_or(is_gate, is_silu))

    @pl.when(is_lin)
    def _():
        o_ref[...] = (_dot(h_scr[...], w_ref[...]) * cols_ref[1:2, :]).astype(BF16)

    @pl.when(is_silu)
    def _():
        acc = _dot(h_scr[...], w_ref[...])
        o_ref[...] = (acc * _sigmoid(acc)).astype(BF16)

    @pl.when(is_gate)
    def _():
        acc = _dot(h_scr[...], w_ref[...])
        lb = cols_ref[0:1, :]
        g = jnp.maximum(jnp.log(lb + (1.0 - lb) * _sigmoid(acc)), -HG_LOG_CLAMP)
        g_ref[...] = g
        o_ref[...] = g.astype(BF16)


def _projection(x2d, norm_w, mod, w_in, cols, mod_row, tm, name):
    n, d = x2d.shape
    tn = PROJ_TN
    return pl.pallas_call(
        functools.partial(_proj_kernel, mod_row=mod_row),
        grid=(n // tm, PROJ_WIDTH // tn),
        in_specs=[
            pl.BlockSpec((tm, d), lambda i, j: (i, 0)),
            pl.BlockSpec((4, d), lambda i, j: (0, 0)),
            pl.BlockSpec((8, d), lambda i, j: (0, 0)),
            pl.BlockSpec((8, d), lambda i, j: (0, 1)),
            pl.BlockSpec((d, tn), lambda i, j: (0, j)),
            pl.BlockSpec((2, tn), lambda i, j: (0, j)),
        ],
        out_specs=[
            pl.BlockSpec((tm, tn), lambda i, j: (i, j)),
            pl.BlockSpec((tm, tn), lambda i, j: (i, jnp.clip(j - 1, 0, 1))),
        ],
        out_shape=[
            jax.ShapeDtypeStruct((n, PROJ_WIDTH), BF16),
            jax.ShapeDtypeStruct((n, 2 * HG_WIDTH), F32),
        ],
        scratch_shapes=[pltpu.VMEM((tm, d), BF16)],
        compiler_params=_cparams(2, 52),
        name=name,
    )(x2d, norm_w, mod, mod, w_in, cols)


def _tri_ones(c, rev):
    t = lax.broadcasted_iota(jnp.int32, (c, c), 0)
    s = lax.broadcasted_iota(jnp.int32, (c, c), 1)
    return jnp.where((s >= t) if rev else (s <= t), 1.0, 0.0).astype(BF16)


def _cumsum_rows(g, tri):
    hi = g.astype(BF16)
    lo = (g - hi.astype(F32)).astype(BF16)
    return _dot(tri, hi) + _dot(tri, lo)


def _row_refs(b, idxs, blk):
    parts = [jnp.broadcast_to(b[i:i + 1, :], (blk, b.shape[1])) for i in idxs]
    return parts[0] if len(parts) == 1 else jnp.concatenate(parts, axis=0)


def _hgrn_levels(c, rev):
    t = lax.broadcasted_iota(jnp.int32, (c, c), 0)
    s = lax.broadcasted_iota(jnp.int32, (c, c), 1)
    levels = []
    m = c // 2
    while m >= HG_DIAG:
        blk = 2 * m
        same = ((t ^ s) & ~(blk - 1)) == 0
        if rev:
            idxs = [b0 + m for b0 in range(0, c, blk)]
            cross = jnp.logical_and((t & m) == 0, (s & m) != 0)
        else:
            idxs = [b0 + m - 1 for b0 in range(0, c, blk)]
            cross = jnp.logical_and((t & m) != 0, (s & m) == 0)
        levels.append((idxs, blk, jnp.logical_and(same, cross)))
        m //= 2
    blk = HG_DIAG
    same = ((t ^ s) & ~(blk - 1)) == 0
    if rev:
        idxs = [b0 + blk // 2 for b0 in range(0, c, blk)]
        mask = jnp.logical_and(same, s >= t)
    else:
        idxs = [b0 + blk // 2 - 1 for b0 in range(0, c, blk)]
        mask = jnp.logical_and(same, s <= t)
    levels.append((idxs, blk, mask))
    return levels


def _hgrn_head(q_bf, g, b, v_bf, st, rev, levels):
    c = g.shape[0]
    kk = 1.0 - jnp.exp(g)

    o = None
    if levels is not None:
        q = q_bf.astype(F32)
        att = jnp.zeros((c, c), F32)
        for idxs, blk, mask in levels:
            ref = _row_refs(b, idxs, blk)
            a_l = _dot_nt((q * jnp.exp(b - ref)).astype(BF16),
                          (kk * jnp.exp(ref - b)).astype(BF16))
            att = att + jnp.where(mask, a_l, 0.0)
        o = _dot(att.astype(BF16), v_bf) + _dot_nt((q * jnp.exp(b)).astype(BF16), st.astype(BF16))

    b_end = b[0:1, :] if rev else b[c - 1:c, :]
    k_end = (kk * jnp.exp(b_end - b)).astype(BF16)
    st_new = st * jnp.exp(b_end) + _dot_tn(v_bf, k_end)
    return o, st_new


def _hgrn_ctx_kernel(gf_ref, vf_ref, gb_ref, vb_ref, st_ref):
    @pl.when(pl.program_id(0) == 0)
    def _():
        st_ref[...] = jnp.zeros_like(st_ref)

    c = gf_ref.shape[0]
    bf = _cumsum_rows(gf_ref[...], _tri_ones(c, False))
    bb = _cumsum_rows(gb_ref[...], _tri_ones(c, True))
    for h in range(HG_HEADS):
        sl = slice(h * HG_DK, (h + 1) * HG_DK)
        _, st_ref[h] = _hgrn_head(None, gf_ref[:, sl], bf[:, sl], vf_ref[:, sl], st_ref[h],
                                  False, None)
        hb = HG_HEADS + h
        _, st_ref[hb] = _hgrn_head(None, gb_ref[:, sl], bb[:, sl], vb_ref[:, sl], st_ref[hb],
                                   True, None)


def _hgrn_kernel(qf_ref, gf_ref, vf_ref, qb_ref, gb_ref, vb_ref, st0_ref, of_ref, ob_ref, st_ref):
    @pl.when(pl.program_id(0) == 0)
    def _():
        st_ref[...] = st0_ref[...]

    c = gf_ref.shape[0]
    lev_f = _hgrn_levels(c, False)
    lev_b = _hgrn_levels(c, True)
    bf = _cumsum_rows(gf_ref[...], _tri_ones(c, False))
    bb = _cumsum_rows(gb_ref[...], _tri_ones(c, True))
    for h in range(HG_HEADS):
        sl = slice(h * HG_DK, (h + 1) * HG_DK)
        of_ref[:, sl], st_ref[h] = _hgrn_head(qf_ref[:, sl], gf_ref[:, sl], bf[:, sl],
                                              vf_ref[:, sl], st_ref[h], False, lev_f)
        hb = HG_HEADS + h
        ob_ref[:, sl], st_ref[hb] = _hgrn_head(qb_ref[:, sl], gb_ref[:, sl], bb[:, sl],
                                               vb_ref[:, sl], st_ref[hb], True, lev_b)


def _hgrn_scan(p, g, pc, gc):
    c = HG_CHUNK
    w = HG_WIDTH
    n, l = p.shape[0], pc.shape[0]
    n_x, n_ctx = n // c, l // c
    st_shape = (2 * HG_HEADS, HG_DK, HG_DK)

    def spec(chunk_fn, col):
        return pl.BlockSpec((c, w), lambda s: (chunk_fn(s), col))

    st0 = pl.pallas_call(
        _hgrn_ctx_kernel,
        grid=(n_ctx,),
        in_specs=[
            spec(lambda s: s, 0), spec(lambda s: s, COL_HI),
            spec(lambda s: n_ctx - 1 - s, 1), spec(lambda s: n_ctx - 1 - s, COL_HI),
        ],
        out_specs=pl.BlockSpec(st_shape, lambda s: (0, 0, 0)),
        out_shape=jax.ShapeDtypeStruct(st_shape, F32),
        compiler_params=_cparams(1, 32),
        name="hgrn_ctx",
    )(gc, pc, gc, pc)

    fwd = lambda s: s
    bwd = lambda s: n_x - 1 - s
    return pl.pallas_call(
        _hgrn_kernel,
        grid=(n_x,),
        in_specs=[
            spec(fwd, COL_HQ), spec(fwd, 0), spec(fwd, COL_HI),
            spec(bwd, COL_HQ), spec(bwd, 1), spec(bwd, COL_HI),
            pl.BlockSpec(st_shape, lambda s: (0, 0, 0)),
        ],
        out_specs=[spec(fwd, 0), spec(bwd, 0)],
        out_shape=[jax.ShapeDtypeStruct((n, w), F32)] * 2,
        scratch_shapes=[pltpu.VMEM(st_shape, F32)],
        compiler_params=_cparams(1, 32),
        name="hgrn",
    )(p, g, p, p, g, p, st0)


def _rope(x_bf, cs, sn):
    x = x_bf.astype(F32)
    return x * cs + pltpu.roll(x, RET_DK // 2, axis=1) * sn


def _ret_decays(lg, c, rev):
    t = lax.broadcasted_iota(jnp.int32, (c, c), 0)
    s = lax.broadcasted_iota(jnp.int32, (c, c), 1)
    diff = (s - t) if rev else (t - s)
    pair = jnp.where(diff >= 0, jnp.exp(lg * jnp.maximum(diff, 0).astype(F32)), 0.0)
    pos = lax.broadcasted_iota(jnp.int32, (c, RET_DK), 0).astype(F32)
    q_dec = jnp.exp(lg * ((c - pos) if rev else (pos + 1.0)))
    k_dec = jnp.exp(lg * (pos if rev else (c - 1.0 - pos)))
    chunk = jnp.exp(jnp.full((8, RET_DK), lg * c, F32))
    return pair, q_dec, k_dec, chunk


def _ret_head(q, k, v_bf, decays, st):
    pair, q_dec, k_dec, chunk = decays
    o = None
    if q is not None:
        att = _dot_nt(q.astype(BF16), k.astype(BF16)) * pair
        o = _dot(att.astype(BF16), v_bf) + _dot_nt((q * q_dec).astype(BF16), st.astype(BF16))
    st_new = st * chunk[0:1, :] + _dot_tn(v_bf, (k * k_dec).astype(BF16))
    return o, st_new


def _ret_ctx_kernel(lg_ref, kf_ref, vf_ref, kb_ref, vb_ref, st_ref):
    @pl.when(pl.program_id(0) == 0)
    def _():
        st_ref[...] = jnp.zeros_like(st_ref)

    c = kf_ref.shape[0]
    for h in range(RET_HEADS):
        sk = slice(h * RET_DK, (h + 1) * RET_DK)
        sv = slice(h * RET_DV, (h + 1) * RET_DV)
        _, st_ref[h] = _ret_head(None, kf_ref[:, sk].astype(F32), vf_ref[:, sv],
                                 _ret_decays(lg_ref[0, h], c, False), st_ref[h])
        hb = RET_HEADS + h
        _, st_ref[hb] = _ret_head(None, kb_ref[:, sk].astype(F32), vb_ref[:, sv],
                                  _ret_decays(lg_ref[1, h], c, True), st_ref[hb])


def _ret_kernel(lg_ref, qf_ref, kf_ref, vf_ref, csf_ref, snf_ref,
                qb_ref, kb_ref, vb_ref, csb_ref, snb_ref, st0_ref, of_ref, ob_ref,
                st_ref, pair_ref, qd_ref, kd_ref, cd_ref):
    c = qf_ref.shape[0]

    @pl.when(pl.program_id(0) == 0)
    def _():
        st_ref[...] = st0_ref[...]
        for d in range(2):
            for h in range(RET_HEADS):
                i = d * RET_HEADS + h
                pair_ref[i], qd_ref[i], kd_ref[i], cd_ref[i] = _ret_decays(lg_ref[d, h], c, d == 1)

    csf, snf = csf_ref[...], snf_ref[...]
    csb, snb = csb_ref[...], snb_ref[...]
    for h in range(RET_HEADS):
        sk = slice(h * RET_DK, (h + 1) * RET_DK)
        sv = slice(h * RET_DV, (h + 1) * RET_DV)
        of_ref[:, sv], st_ref[h] = _ret_head(
            _rope(qf_ref[:, sk], csf, snf), _rope(kf_ref[:, sk], csf, snf), vf_ref[:, sv],
            (pair_ref[h], qd_ref[h], kd_ref[h], cd_ref[h]), st_ref[h])
        hb = RET_HEADS + h
        ob_ref[:, sv], st_ref[hb] = _ret_head(
            _rope(qb_ref[:, sk], csb, snb), _rope(kb_ref[:, sk], csb, snb), vb_ref[:, sv],
            (pair_ref[hb], qd_ref[hb], kd_ref[hb], cd_ref[hb]), st_ref[hb])


def _ret_scan(p, pc, ret_lg, cs_tab, sn_tab):
    c = RET_CHUNK
    n, l = p.shape[0], pc.shape[0]
    n_x, n_ctx = n // c, l // c
    qk_w = RET_HEADS * RET_DK
    st_shape = (2 * RET_HEADS, RET_DV, RET_DK)
    smem = pl.BlockSpec(memory_space=pltpu.SMEM)

    def kv_specs(chunk_fn):
        return [
            pl.BlockSpec((c, qk_w), lambda s: (chunk_fn(s), COL_RK_512)),
            pl.BlockSpec((c, RET_WIDTH), lambda s: (chunk_fn(s), COL_RV)),
        ]

    st0 = pl.pallas_call(
        _ret_ctx_kernel,
        grid=(n_ctx,),
        in_specs=[smem, *kv_specs(lambda s: s), *kv_specs(lambda s: n_ctx - 1 - s)],
        out_specs=pl.BlockSpec(st_shape, lambda s: (0, 0, 0)),
        out_shape=jax.ShapeDtypeStruct(st_shape, F32),
        compiler_params=_cparams(1, 32),
        name="ret_ctx",
    )(ret_lg, pc, pc, pc, pc)

    def specs(chunk_fn):
        return [
            pl.BlockSpec((c, qk_w), lambda s: (chunk_fn(s), COL_RQ_512)),
            *kv_specs(chunk_fn),
            pl.BlockSpec((c, RET_DK), lambda s: (chunk_fn(s), 0)),
            pl.BlockSpec((c, RET_DK), lambda s: (chunk_fn(s), 0)),
        ]

    fwd = lambda s: s
    bwd = lambda s: n_x - 1 - s
    return pl.pallas_call(
        _ret_kernel,
        grid=(n_x,),
        in_specs=[smem, *specs(fwd), *specs(bwd), pl.BlockSpec(st_shape, lambda s: (0, 0, 0))],
        out_specs=[
            pl.BlockSpec((c, RET_WIDTH), lambda s: (fwd(s), 0)),
            pl.BlockSpec((c, RET_WIDTH), lambda s: (bwd(s), 0)),
        ],
        out_shape=[jax.ShapeDtypeStruct((n, RET_WIDTH), F32)] * 2,
        scratch_shapes=[
            pltpu.VMEM(st_shape, F32),
            pltpu.VMEM((2 * RET_HEADS, c, c), F32),
            pltpu.VMEM((2 * RET_HEADS, c, RET_DK), F32),
            pltpu.VMEM((2 * RET_HEADS, c, RET_DK), F32),
            pltpu.VMEM((2 * RET_HEADS, 8, RET_DK), F32),
        ],
        compiler_params=_cparams(1, 32),
        name="ret",
    )(ret_lg, p, p, p, cs_tab, sn_tab, p, p, p, cs_tab, sn_tab, st0)


def _rms(y):
    return y * lax.rsqrt(jnp.mean(y * y, axis=-1, keepdims=True) + RMS_EPS)


def _outproj_kernel(of_ref, ob_ref, rf_ref, rb_ref, hgate_ref, rgate_ref, x_ref, wout_ref,
                    hgw_ref, rnw_ref, nw_ref, g1_ref, sh2_ref, sc2_ref, wrh_ref, wrl_ref, br_ref,
                    x1_ref, h2_ref, lg_ref, cat_scr):
    o = of_ref[...] + ob_ref[...]
    for h in range(HG_HEADS):
        sl = slice(h * HG_DK, (h + 1) * HG_DK)
        cat_scr[:, sl] = (_rms(o[:, sl]) * hgw_ref[0:1, sl]
                          * hgate_ref[:, sl].astype(F32)).astype(BF16)
    r = rf_ref[...] + rb_ref[...]
    for h in range(RET_HEADS):
        sl = slice(h * RET_DV, (h + 1) * RET_DV)
        cat_scr[:, HG_WIDTH + h * RET_DV:HG_WIDTH + (h + 1) * RET_DV] = (
            _rms(r[:, sl]) * rnw_ref[0:1, sl] * rgate_ref[:, sl].astype(F32)).astype(BF16)

    y = _dot(cat_scr[...], wout_ref[...])
    x1 = x_ref[...] + g1_ref[0:1, :] * (_rms(y) * nw_ref[1:2, :])
    x1_ref[...] = x1
    h2 = _rms(x1) * nw_ref[2:3, :] * (1.0 + sc2_ref[0:1, :]) + sh2_ref[0:1, :]
    h2_ref[...] = _pack_pairs(h2)
    hi = h2.astype(BF16)
    lo = (h2 - hi.astype(F32)).astype(BF16)
    lg_ref[...] = (_dot(hi, wrh_ref[...]) + _dot(lo, wrh_ref[...]) + _dot(hi, wrl_ref[...])
                   + br_ref[...])


def _out_projection(o_f, o_b, r_f, r_b, p, x2d, w_out_bf, hg_norm_w, ret_norm_w, norm_w, mod,
                    wr_hi, wr_lo, br):
    n, d = x2d.shape
    tm = 256
    row = lambda w: pl.BlockSpec((tm, w), lambda i: (i, 0))
    full = lambda a: pl.BlockSpec(a.shape, lambda i: (0,) * a.ndim)
    modcol = lambda k: pl.BlockSpec((8, d), lambda i: (0, k))
    return pl.pallas_call(
        _outproj_kernel,
        grid=(n // tm,),
        in_specs=[
            row(HG_WIDTH), row(HG_WIDTH), row(RET_WIDTH), row(RET_WIDTH),
            pl.BlockSpec((tm, HG_WIDTH), lambda i: (i, COL_HGATE)),
            pl.BlockSpec((tm, RET_WIDTH), lambda i: (i, COL_RGATE)),
            row(d), full(w_out_bf), full(hg_norm_w), full(ret_norm_w), full(norm_w),
            modcol(2), modcol(3), modcol(4),
            full(wr_hi), full(wr_lo), full(br),
        ],
        out_specs=[row(d), row(d // 2), row(ROUTER_LANES)],
        out_shape=[
            jax.ShapeDtypeStruct((n, d), F32),
            jax.ShapeDtypeStruct((n, d // 2), jnp.uint32),
            jax.ShapeDtypeStruct((n, ROUTER_LANES), F32),
        ],
        scratch_shapes=[pltpu.VMEM((tm, d), BF16)],
        compiler_params=_cparams(1, 48),
        name="outproj",
    )(o_f, o_b, r_f, r_b, p, p, x2d, w_out_bf, hg_norm_w, ret_norm_w, norm_w, mod, mod, mod,
      wr_hi, wr_lo, br)


def _routing_plan(logits, bg, be, n_slots, n_items):
    t = logits.shape[0]
    lg = logits[:, :N_GROUPS] + bg
    le = logits[:, N_GROUPS:N_GROUPS + N_EXPERTS] + be
    grp = jnp.argmax(lg, axis=-1)
    p_sel = jnp.take_along_axis(jax.nn.softmax(lg, axis=-1), grp[:, None], axis=-1)
    le3 = le.reshape(t, N_GROUPS, EXPERTS_PER_GROUP)
    lin = jnp.take_along_axis(le3, grp[:, None, None], axis=1)[:, 0]
    top_v, top_i = lax.top_k(lin, EXPERT_TOPK)
    gates = p_sel * jax.nn.softmax(top_v, axis=-1)
    flat_e = (grp[:, None] * EXPERTS_PER_GROUP + top_i).reshape(-1).astype(jnp.int32)

    onehot = (flat_e[:, None] == jnp.arange(N_EXPERTS, dtype=jnp.int32)[None, :]).astype(jnp.int32)
    csum = jnp.cumsum(onehot, axis=0)
    rank = jnp.take_along_axis(csum, flat_e[:, None], axis=1)[:, 0] - 1
    counts = csum[-1]
    padded = (counts + MOE_PAD - 1) // MOE_PAD * MOE_PAD
    pstart = jnp.cumsum(padded) - padded
    slot_of = (pstart[flat_e] + rank).astype(jnp.int32)
    tok = jnp.arange(t * EXPERT_TOPK, dtype=jnp.int32) // EXPERT_TOPK
    slot_tok = jnp.zeros((n_slots,), jnp.int32).at[slot_of].set(tok)

    r = MOE_ITEM_ROWS
    ipe = (padded + r - 1) // r
    iend = jnp.cumsum(ipe)
    ioff = iend - ipe
    total = iend[-1]
    a = jnp.arange(n_items, dtype=jnp.int32)
    e_a = jnp.minimum(jnp.searchsorted(iend, a, side="right"), N_EXPERTS - 1).astype(jnp.int32)
    valid = a < total
    sub = a - ioff[e_a]
    irows = jnp.clip(padded[e_a] - sub * r, 0, r)
    istart = pstart[e_a] + sub * r
    e_last = e_a[jnp.maximum(total - 1, 0)]
    used = jnp.sum(padded)
    fill_start = used + (a - total) * r
    fill_rows = jnp.clip(n_slots - fill_start, 0, r)
    item_e = jnp.where(valid, e_a, e_last).astype(jnp.int32)
    item_rows = jnp.where(valid, irows, 0).astype(jnp.int32)
    item_fill = jnp.where(valid, 0, fill_rows).astype(jnp.int32)
    item_start = jnp.where(valid, istart, jnp.where(fill_rows > 0, fill_start, 0)).astype(jnp.int32)
    return gates, slot_of, slot_tok, item_e, item_start, item_rows, item_fill


def _unpack_pairs(u):
    lo = lax.bitcast_convert_type(u << 16, F32)
    hi = lax.bitcast_convert_type(u & jnp.uint32(0xFFFF0000), F32)
    return jnp.concatenate([lo, hi], axis=1)


def _pack_pairs(y):
    w = y.shape[1] // 2
    bits = lax.bitcast_convert_type(y.astype(BF16).astype(F32), jnp.uint32)
    return (bits[:, :w] >> 16) | (bits[:, w:] & jnp.uint32(0xFFFF0000))


def _moe_kernel(item_e, item_start, item_rows, item_fill, slot_tok,
                h2p_hbm, wg_ref, wu_ref, wd_ref, yb_hbm,
                xbuf, xs_bf, act, ystage, pf, gsem, osem):
    a = pl.program_id(0)
    t = pl.program_id(1)
    n_items = pl.num_programs(0)
    n = item_rows[a]
    start = item_start[a]
    rb_rows = MOE_PAD
    n_rb = n // rb_rows
    half_w = wg_ref.shape[2]
    cur = a % 2
    nxt = 1 - cur
    a1 = jnp.minimum(a + 1, n_items - 1)
    n_next = jnp.where(a + 1 < n_items, item_rows[a1], 0)
    start_next = item_start[a1]

    def row_copy(tok, slot, i):
        return pltpu.make_async_copy(h2p_hbm.at[pl.ds(tok, 1), :], xbuf.at[slot, pl.ds(i, 1), :],
                                     gsem.at[slot])

    @pl.when(jnp.logical_and(a == 0, t == 0))
    def _():
        def issue(i, carry):
            row_copy(slot_tok[start + i], 0, i).start()
            return carry
        lax.fori_loop(0, n, issue, 0)

    @pl.when(t == 0)
    def _():
        pf[0] = 0

        def drain(bk, carry):
            r0 = pl.multiple_of(bk * rb_rows, rb_rows)
            pltpu.make_async_copy(h2p_hbm.at[pl.ds(0, rb_rows), :],
                                  xbuf.at[cur, pl.ds(r0, rb_rows), :], gsem.at[cur]).wait()
            return carry
        lax.fori_loop(0, n_rb, drain, 0)

    def prefetch_group():
        base = pf[0]
        for q in range(MOE_GATHER_GROUP):
            row_copy(slot_tok[start_next + base + q], nxt, base + q).start()
        pf[0] = base + MOE_GATHER_GROUP

    def run_blocks(compute):
        def body(rb, carry):
            more = pf[0] < n_next

            @pl.when(more)
            def _():
                compute(rb)
                prefetch_group()

            @pl.when(jnp.logical_not(more))
            def _():
                compute(rb)
            return carry
        lax.fori_loop(0, n_rb, body, 0)

    def gate_up(half):
        def compute(rb):
            r0 = pl.multiple_of(rb * rb_rows, rb_rows)
            if half == 0:
                xs = _unpack_pairs(xbuf[cur, pl.ds(r0, rb_rows), :]).astype(BF16)
                xs_bf[pl.ds(r0, rb_rows), :] = xs
            else:
                xs = xs_bf[pl.ds(r0, rb_rows), :]
            g = _dot(xs, wg_ref[0].astype(BF16))
            u = _dot(xs, wu_ref[0].astype(BF16))
            act[pl.ds(r0, rb_rows), half * half_w:(half + 1) * half_w] = (
                g * _sigmoid(g) * u).astype(BF16)
        run_blocks(compute)

    def out_copy(slot, r0):
        return pltpu.make_async_copy(
            ystage.at[slot],
            yb_hbm.at[pl.ds(pl.multiple_of(start + r0, rb_rows), rb_rows), :],
            osem.at[slot])

    def down():
        def compute(rb):
            r0 = pl.multiple_of(rb * rb_rows, rb_rows)
            slot = rb % 2
            y = _dot(act[pl.ds(r0, rb_rows), :], wd_ref[0].astype(BF16))

            @pl.when(rb >= 2)
            def _():
                out_copy(slot, r0).wait()

            ystage[slot] = _pack_pairs(y)
            out_copy(slot, r0).start()
        run_blocks(compute)

        @pl.when(n_rb >= 1)
        def _():
            out_copy((n_rb - 1) % 2, 0).wait()

        @pl.when(n_rb >= 2)
        def _():
            out_copy(n_rb % 2, 0).wait()

        def tail(i, carry):
            row_copy(slot_tok[start_next + i], nxt, i).start()
            return carry
        lax.fori_loop(pf[0], n_next, tail, 0)

    pl.when(t == 0)(functools.partial(gate_up, 0))
    pl.when(t == 1)(functools.partial(gate_up, 1))
    pl.when(t == 2)(down)

    fill = item_fill[a]

    @pl.when(jnp.logical_and(t == 0, fill > 0))
    def _():
        ystage[0] = jnp.zeros(ystage.shape[1:], ystage.dtype)

        def zero_block(bk, carry):
            cp = out_copy(0, bk * rb_rows)
            cp.start()
            cp.wait()
            return carry
        lax.fori_loop(0, fill // rb_rows, zero_block, 0)


def _moe_ffn(h2p, w_gate, w_up, w_down, item_e, item_start, item_rows, item_fill, slot_tok,
             n_slots):
    pw = h2p.shape[1]
    d = 2 * pw
    n_items = item_e.shape[0]
    half_w = D_EXPERT // 2

    def wgu_map(a, t, ie, ist, ir, fl, st):
        return (ie[a], 0, jnp.where(ir[a] > 0, jnp.minimum(t, 1), 1))

    def wd_map(a, t, ie, ist, ir, fl, st):
        return (jnp.where(t >= 2, ie[a], ie[jnp.maximum(a - 1, 0)]), 0, 0)

    grid_spec = pltpu.PrefetchScalarGridSpec(
        num_scalar_prefetch=5,
        grid=(n_items, 3),
        in_specs=[
            pl.BlockSpec(memory_space=pl.ANY),
            pl.BlockSpec((1, d, half_w), wgu_map),
            pl.BlockSpec((1, d, half_w), wgu_map),
            pl.BlockSpec((1, D_EXPERT, d), wd_map),
        ],
        out_specs=pl.BlockSpec(memory_space=pl.ANY),
        scratch_shapes=[
            pltpu.VMEM((2, MOE_ITEM_ROWS, pw), jnp.uint32),
            pltpu.VMEM((MOE_ITEM_ROWS, d), BF16),
            pltpu.VMEM((MOE_ITEM_ROWS, D_EXPERT), BF16),
            pltpu.VMEM((2, MOE_PAD, pw), jnp.uint32),
            pltpu.SMEM((1,), jnp.int32),
            pltpu.SemaphoreType.DMA((2,)),
            pltpu.SemaphoreType.DMA((2,)),
        ],
    )
    return pl.pallas_call(
        _moe_kernel,
        grid_spec=grid_spec,
        out_shape=jax.ShapeDtypeStruct((n_slots, pw), jnp.uint32),
        compiler_params=_cparams(2, 56),
        name="moe",
    )(item_e, item_start, item_rows, item_fill, slot_tok, h2p, w_gate, w_up, w_down)


def _combine_kernel(slot_of, yb_hbm, gates_ref, x1_ref, nw_ref, g2_ref, out_ref, rows, sem):
    i = pl.program_id(0)
    n_tiles = pl.num_programs(0)
    tm = x1_ref.shape[0]

    def gather(tile, buf):
        def issue(r8, carry):
            for dr in range(8):
                r = r8 * 8 + dr
                base = (tile * tm + r) * EXPERT_TOPK
                for k in range(EXPERT_TOPK):
                    pltpu.make_async_copy(yb_hbm.at[pl.ds(slot_of[base + k], 1), :],
                                          rows.at[buf, k, pl.ds(r, 1), :], sem.at[buf]).start()
            return carry
        lax.fori_loop(0, tm // 8, issue, 0)

    @pl.when(i == 0)
    def _():
        gather(0, 0)

    @pl.when(i + 1 < n_tiles)
    def _():
        gather(i + 1, (i + 1) % 2)

    buf = i % 2
    for k in range(EXPERT_TOPK):
        pltpu.make_async_copy(yb_hbm.at[pl.ds(0, tm), :], rows.at[buf, k], sem.at[buf]).wait()

    g = gates_ref[...]
    y = (_unpack_pairs(rows[buf, 0]) * g[:, 0:1] + _unpack_pairs(rows[buf, 1]) * g[:, 1:2])
    out_ref[...] = x1_ref[...] + g2_ref[0:1, :] * (_rms(y) * nw_ref[3:4, :])


def _combine(slot_of, yb, gates, x1, norm_w, mod):
    n, d = x1.shape
    tm = 256
    grid_spec = pltpu.PrefetchScalarGridSpec(
        num_scalar_prefetch=1,
        grid=(n // tm,),
        in_specs=[
            pl.BlockSpec(memory_space=pl.ANY),
            pl.BlockSpec((tm, EXPERT_TOPK), lambda i, so: (i, 0)),
            pl.BlockSpec((tm, d), lambda i, so: (i, 0)),
            pl.BlockSpec((4, d), lambda i, so: (0, 0)),
            pl.BlockSpec((8, d), lambda i, so: (0, 5)),
        ],
        out_specs=pl.BlockSpec((tm, d), lambda i, so: (i, 0)),
        scratch_shapes=[
            pltpu.VMEM((2, EXPERT_TOPK, tm, d // 2), jnp.uint32),
            pltpu.SemaphoreType.DMA((2,)),
        ],
    )
    return pl.pallas_call(
        _combine_kernel,
        grid_spec=grid_spec,
        out_shape=jax.ShapeDtypeStruct((n, d), F32),
        compiler_params=_cparams(1, 32),
        name="combine",
    )(slot_of, yb, gates, x1, norm_w, mod)


def _rope_tables(n):
    rows = n // GRID_W
    r = jnp.broadcast_to(jnp.arange(rows)[:, None], (rows, GRID_W)).reshape(-1).astype(F32)
    cidx = jnp.broadcast_to(jnp.arange(GRID_W)[None, :], (rows, GRID_W)).reshape(-1).astype(F32)
    n_freq = RET_DK // 4
    inv = ROPE_BASE ** (-jnp.arange(n_freq, dtype=F32) / n_freq)
    ang = jnp.concatenate([r[:, None] * inv, cidx[:, None] * inv], axis=-1)
    cos, sin = jnp.cos(ang), jnp.sin(ang)
    cs = jnp.concatenate([cos, cos], axis=-1)
    sn = jnp.concatenate([-sin, sin], axis=-1)
    return cs, sn


def kernel(x, c, ctx, c_ctx, w_mod, b_mod, norm_w, w_in, hg_lb_logits, hg_norm_w, ret_decay_logits,
           ret_norm_w, w_out, router_group_w, router_group_b, router_expert_w, router_expert_b,
           w_gate, w_up, w_down):
    bsz, n, d = x.shape
    l = ctx.shape[1]
    assert bsz == 1 and w_mod.shape[0] == 1, "single sample, single layer"
    x2d, ctx2d = x[0], ctx[0]

    cc = jnp.zeros((8, d), F32).at[0].set(c[0]).at[1].set(c_ctx)
    mod = _modulation(cc, w_mod[0], b_mod[0])

    lb = jnp.cumsum(jax.nn.softmax(hg_lb_logits.astype(F32), axis=0), axis=0)[0]
    rk0 = 5 * HG_WIDTH + RET_HEADS * RET_DK
    cols = jnp.stack([
        jnp.zeros((PROJ_WIDTH,), F32).at[HG_WIDTH:3 * HG_WIDTH].set(lb.reshape(-1)),
        jnp.ones((PROJ_WIDTH,), F32).at[rk0:rk0 + RET_HEADS * RET_DK].set(RET_DK ** -0.5),
    ])
    w_in_bf = w_in[0].astype(BF16)
    p, g = _projection(x2d, norm_w[0], mod, w_in_bf, cols, 0, 1024, "proj_x")
    pc, gc = _projection(ctx2d, norm_w[0], mod, w_in_bf, cols, 1, l, "proj_ctx")

    ret_lg = jax.nn.log_sigmoid(ret_decay_logits[0].astype(F32))
    cs_tab, sn_tab = _rope_tables(n)

    o_f, o_b = _hgrn_scan(p, g, pc, gc)
    r_f, r_b = _ret_scan(p, pc, ret_lg, cs_tab, sn_tab)

    wr = jnp.concatenate([router_group_w[0], router_expert_w[0]], axis=1)
    wr = jnp.pad(wr, ((0, 0), (0, ROUTER_LANES - wr.shape[1])))
    wr_hi = wr.astype(BF16)
    wr_lo = (wr - wr_hi.astype(F32)).astype(BF16)
    br = jnp.zeros((1, ROUTER_LANES), F32)
    x1, h2, logits = _out_projection(
        o_f, o_b, r_f, r_b, p, x2d, w_out[0].astype(BF16), hg_norm_w, ret_norm_w, norm_w[0], mod,
        wr_hi, wr_lo, br)

    n_assign = n * EXPERT_TOPK
    n_slots = n_assign + N_EXPERTS * MOE_PAD
    n_items = N_EXPERTS + n_slots // MOE_ITEM_ROWS
    gates, slot_of, slot_tok, item_e, item_start, item_rows, item_fill = _routing_plan(
        logits, router_group_b[0], router_expert_b[0], n_slots, n_items)

    yb = _moe_ffn(h2, w_gate[0], w_up[0], w_down[0], item_e, item_start, item_rows, item_fill,
                  slot_tok, n_slots)
    out = _combine(slot_of, yb, gates, x1, norm_w[0], mod)
    return out[None]
```

```python
import functools

import jax
import jax.numpy as jnp
from jax import lax
from jax.experimental import pallas as pl
from jax.experimental.pallas import tpu as pltpu

F32 = jnp.float32
BF16 = jnp.bfloat16

D_MODEL = 2048
GRID_W = 64
HG_HEADS = 8
HG_DK = 128
HG_WIDTH = 1024
RET_HEADS = 4
RET_DK = 128
RET_DV = 256
RET_WIDTH = 1024
PROJ_WIDTH = 8192
N_GROUPS = 4
EXPERTS_PER_GROUP = 8
N_EXPERTS = 32
EXPERT_TOPK = 2
D_EXPERT = 1024
RMS_EPS = 1e-6
ROPE_BASE = 10000.0

COL_HQ, COL_HF_FWD, COL_HF_BWD, COL_HI, COL_HGATE = 0, 1, 2, 3, 4
COL_RQ_512, COL_RK_512 = 10, 11
COL_RV, COL_RGATE = 6, 7

HG_CHUNK = 128
HG_DIAG = 16
HG_LOG_CLAMP = 10.0
RET_CHUNK = 128

MOE_PAD = 256
MOE_ITEM_ROWS = 1024
MOE_GATHER_GROUP = 128
ROUTER_LANES = 128


def _cparams(n_axes, vmem_mb):
    return pltpu.CompilerParams(
        dimension_semantics=("arbitrary",) * n_axes,
        vmem_limit_bytes=vmem_mb * 1024 * 1024,
    )


def _sigmoid(x):
    return 1.0 / (1.0 + jnp.exp(-x))


def _dot(a, b):
    return jnp.dot(a, b, preferred_element_type=F32)


def _dot_nt(a, b):
    return lax.dot_general(a, b, (((1,), (1,)), ((), ())), preferred_element_type=F32)


def _dot_tn(a, b):
    return lax.dot_general(a, b, (((0,), (0,)), ((), ())), preferred_element_type=F32)


def _mod_kernel(cc_ref, w_ref, b_ref, o_ref):
    s = cc_ref[...]
    s = s * _sigmoid(s)
    s16 = jnp.concatenate([s, s], axis=0)
    hi = s16.astype(BF16).astype(F32)
    row = lax.broadcasted_iota(jnp.int32, s16.shape, 0)
    lhs = jnp.where(row < 8, hi, s16 - hi).astype(BF16)
    r = _dot(lhs, w_ref[...].astype(BF16))
    o_ref[...] = r[:8] + r[8:] + b_ref[...]


def _modulation(cc, w_mod, b_mod):
    d, n = w_mod.shape
    tn = 1024
    return pl.pallas_call(
        _mod_kernel,
        grid=(n // tn,),
        in_specs=[
            pl.BlockSpec((8, d), lambda j: (0, 0)),
            pl.BlockSpec((d, tn), lambda j: (0, j)),
            pl.BlockSpec((1, tn), lambda j: (0, j)),
        ],
        out_specs=pl.BlockSpec((8, tn), lambda j: (0, j)),
        out_shape=jax.ShapeDtypeStruct((8, n), F32),
        compiler_params=_cparams(1, 40),
        name="mod",
    )(cc, w_mod, b_mod.reshape(1, n))


PROJ_TN = 1024
PROLOGUE_ROWS = 32


def _proj_kernel(x_ref, nw_ref, sh_ref, sc_ref, w_ref, cols_ref, o_ref, g_ref, h_scr, *, mod_row):
    j = pl.program_id(1)

    @pl.when(j == 0)
    def _():
        scale = nw_ref[0:1, :] * (1.0 + sc_ref[mod_row:mod_row + 1, :])
        shift = sh_ref[mod_row:mod_row + 1, :]

        def body(r, carry):
            r0 = pl.multiple_of(r * PROLOGUE_ROWS, PROLOGUE_ROWS)
            x = x_ref[pl.ds(r0, PROLOGUE_ROWS), :]
            y = x * lax.rsqrt(jnp.mean(x * x, axis=-1, keepdims=True) + RMS_EPS)
            h_scr[pl.ds(r0, PROLOGUE_ROWS), :] = (y * scale + shift).astype(BF16)
            return carry
        lax.fori_loop(0, x_ref.shape[0] // PROLOGUE_ROWS, body, 0)

    is_gate = jnp.logical_or(j == 1, j == 2)
    is_silu = jnp.logical_or(j == 0, jnp.logical_or(j == 4, j == 7))
    is_lin = jnp.logical_not(jnp.logical_or(is_gate, is_silu))

    @pl.when(is_lin)
    def _():
        o_ref[...] = (_dot(h_scr[...], w_ref[...]) * cols_ref[1:2, :]).astype(BF16)

    @pl.when(is_silu)
    def _():
        acc = _dot(h_scr[...], w_ref[...])
        o_ref[...] = (acc * _sigmoid(acc)).astype(BF16)

    @pl.when(is_gate)
    def _():
        acc = _dot(h_scr[...], w_ref[...])
        lb = cols_ref[0:1, :]
        g = jnp.maximum(jnp.log(lb + (1.0 - lb) * _sigmoid(acc)), -HG_LOG_CLAMP)
        g_ref[...] = g
        o_ref[...] = g.astype(BF16)


def _projection(x2d, norm_w, mod, w_in, cols, mod_row, tm, name):
    n, d = x2d.shape
    tn = PROJ_TN
    return pl.pallas_call(
        functools.partial(_proj_kernel, mod_row=mod_row),
        grid=(n // tm, PROJ_WIDTH // tn),
        in_specs=[
            pl.BlockSpec((tm, d), lambda i, j: (i, 0)),
            pl.BlockSpec((4, d), lambda i, j: (0, 0)),
            pl.BlockSpec((8, d), lambda i, j: (0, 0)),
            pl.BlockSpec((8, d), lambda i, j: (0, 1)),
            pl.BlockSpec((d, tn), lambda i, j: (0, j)),
            pl.BlockSpec((2, tn), lambda i, j: (0, j)),
        ],
        out_specs=[
            pl.BlockSpec((tm, tn), lambda i, j: (i, j)),
            pl.BlockSpec((tm, tn), lambda i, j: (i, jnp.clip(j - 1, 0, 1))),
        ],
        out_shape=[
            jax.ShapeDtypeStruct((n, PROJ_WIDTH), BF16),
            jax.ShapeDtypeStruct((n, 2 * HG_WIDTH), F32),
        ],
        scratch_shapes=[pltpu.VMEM((tm, d), BF16)],
        compiler_params=_cparams(2, 52),
        name=name,
    )(x2d, norm_w, mod, mod, w_in, cols)


def _tri_ones(c, rev):
    t = lax.broadcasted_iota(jnp.int32, (c, c), 0)
    s = lax.broadcasted_iota(jnp.int32, (c, c), 1)
    return jnp.where((s >= t) if rev else (s <= t), 1.0, 0.0).astype(BF16)


def _cumsum_rows(g, tri):
    hi = g.astype(BF16)
    lo = (g - hi.astype(F32)).astype(BF16)
    return _dot(tri, hi) + _dot(tri, lo)


def _row_refs(b, idxs, blk):
    parts = [jnp.broadcast_to(b[i:i + 1, :], (blk, b.shape[1])) for i in idxs]
    return parts[0] if len(parts) == 1 else jnp.concatenate(parts, axis=0)


def _hgrn_levels(c, rev):
    t = lax.broadcasted_iota(jnp.int32, (c, c), 0)
    s = lax.broadcasted_iota(jnp.int32, (c, c), 1)
    levels = []
    m = c // 2
    while m >= HG_DIAG:
        blk = 2 * m
        same = ((t ^ s) & ~(blk - 1)) == 0
        if rev:
            idxs = [b0 + m for b0 in range(0, c, blk)]
            cross = jnp.logical_and((t & m) == 0, (s & m) != 0)
        else:
            idxs = [b0 + m - 1 for b0 in range(0, c, blk)]
            cross = jnp.logical_and((t & m) != 0, (s & m) == 0)
        levels.append((idxs, blk, jnp.logical_and(same, cross)))
        m //= 2
    blk = HG_DIAG
    same = ((t ^ s) & ~(blk - 1)) == 0
    if rev:
        idxs = [b0 + blk // 2 for b0 in range(0, c, blk)]
        mask = jnp.logical_and(same, s >= t)
    else:
        idxs = [b0 + blk // 2 - 1 for b0 in range(0, c, blk)]
        mask = jnp.logical_and(same, s <= t)
    levels.append((idxs, blk, mask))
    return levels


def _hgrn_head(q_bf, g, b, v_bf, st, rev, levels):
    c = g.shape[0]
    kk = 1.0 - jnp.exp(g)

    o = None
    if levels is not None:
        q = q_bf.astype(F32)
        att = jnp.zeros((c, c), F32)
        for idxs, blk, mask in levels:
            ref = _row_refs(b, idxs, blk)
            a_l = _dot_nt((q * jnp.exp(b - ref)).astype(BF16),
                          (kk * jnp.exp(ref - b)).astype(BF16))
            att = att + jnp.where(mask, a_l, 0.0)
        o = _dot(att.astype(BF16), v_bf) + _dot_nt((q * jnp.exp(b)).astype(BF16), st.astype(BF16))

    b_end = b[0:1, :] if rev else b[c - 1:c, :]
    k_end = (kk * jnp.exp(b_end - b)).astype(BF16)
    st_new = st * jnp.exp(b_end) + _dot_tn(v_bf, k_end)
    return o, st_new


def _hgrn_ctx_kernel(gf_ref, vf_ref, gb_ref, vb_ref, st_ref):
    @pl.when(pl.program_id(0) == 0)
    def _():
        st_ref[...] = jnp.zeros_like(st_ref)

    c = gf_ref.shape[0]
    bf = _cumsum_rows(gf_ref[...], _tri_ones(c, False))
    bb = _cumsum_rows(gb_ref[...], _tri_ones(c, True))
    for h in range(HG_HEADS):
        sl = slice(h * HG_DK, (h + 1) * HG_DK)
        _, st_ref[h] = _hgrn_head(None, gf_ref[:, sl], bf[:, sl], vf_ref[:, sl], st_ref[h],
                                  False, None)
        hb = HG_HEADS + h
        _, st_ref[hb] = _hgrn_head(None, gb_ref[:, sl], bb[:, sl], vb_ref[:, sl], st_ref[hb],
                                   True, None)


def _hgrn_kernel(qf_ref, gf_ref, vf_ref, qb_ref, gb_ref, vb_ref, st0_ref, of_ref, ob_ref, st_ref):
    @pl.when(pl.program_id(0) == 0)
    def _():
        st_ref[...] = st0_ref[...]

    c = gf_ref.shape[0]
    lev_f = _hgrn_levels(c, False)
    lev_b = _hgrn_levels(c, True)
    bf = _cumsum_rows(gf_ref[...], _tri_ones(c, False))
    bb = _cumsum_rows(gb_ref[...], _tri_ones(c, True))
    for h in range(HG_HEADS):
        sl = slice(h * HG_DK, (h + 1) * HG_DK)
        of_ref[:, sl], st_ref[h] = _hgrn_head(qf_ref[:, sl], gf_ref[:, sl], bf[:, sl],
                                              vf_ref[:, sl], st_ref[h], False, lev_f)
        hb = HG_HEADS + h
        ob_ref[:, sl], st_ref[hb] = _hgrn_head(qb_ref[:, sl], gb_ref[:, sl], bb[:, sl],
                                               vb_ref[:, sl], st_ref[hb], True, lev_b)


def _hgrn_scan(p, g, pc, gc):
    c = HG_CHUNK
    w = HG_WIDTH
    n, l = p.shape[0], pc.shape[0]
    n_x, n_ctx = n // c, l // c
    st_shape = (2 * HG_HEADS, HG_DK, HG_DK)

    def spec(chunk_fn, col):
        return pl.BlockSpec((c, w), lambda s: (chunk_fn(s), col))

    st0 = pl.pallas_call(
        _hgrn_ctx_kernel,
        grid=(n_ctx,),
        in_specs=[
            spec(lambda s: s, 0), spec(lambda s: s, COL_HI),
            spec(lambda s: n_ctx - 1 - s, 1), spec(lambda s: n_ctx - 1 - s, COL_HI),
        ],
        out_specs=pl.BlockSpec(st_shape, lambda s: (0, 0, 0)),
        out_shape=jax.ShapeDtypeStruct(st_shape, F32),
        compiler_params=_cparams(1, 32),
        name="hgrn_ctx",
    )(gc, pc, gc, pc)

    fwd = lambda s: s
    bwd = lambda s: n_x - 1 - s
    return pl.pallas_call(
        _hgrn_kernel,
        grid=(n_x,),
        in_specs=[
            spec(fwd, COL_HQ), spec(fwd, 0), spec(fwd, COL_HI),
            spec(bwd, COL_HQ), spec(bwd, 1), spec(bwd, COL_HI),
            pl.BlockSpec(st_shape, lambda s: (0, 0, 0)),
        ],
        out_specs=[spec(fwd, 0), spec(bwd, 0)],
        out_shape=[jax.ShapeDtypeStruct((n, w), F32)] * 2,
        scratch_shapes=[pltpu.VMEM(st_shape, F32)],
        compiler_params=_cparams(1, 32),
        name="hgrn",
    )(p, g, p, p, g, p, st0)


def _rope(x_bf, cs, sn):
    x = x_bf.astype(F32)
    return x * cs + pltpu.roll(x, RET_DK // 2, axis=1) * sn


def _ret_decays(lg, c, rev):
    t = lax.broadcasted_iota(jnp.int32, (c, c), 0)
    s = lax.broadcasted_iota(jnp.int32, (c, c), 1)
    diff = (s - t) if rev else (t - s)
    pair = jnp.where(diff >= 0, jnp.exp(lg * jnp.maximum(diff, 0).astype(F32)), 0.0)
    pos = lax.broadcasted_iota(jnp.int32, (c, RET_DK), 0).astype(F32)
    q_dec = jnp.exp(lg * ((c - pos) if rev else (pos + 1.0)))
    k_dec = jnp.exp(lg * (pos if rev else (c - 1.0 - pos)))
    chunk = jnp.exp(jnp.full((8, RET_DK), lg * c, F32))
    return pair, q_dec, k_dec, chunk


def _ret_head(q, k, v_bf, decays, st):
    pair, q_dec, k_dec, chunk = decays
    o = None
    if q is not None:
        att = _dot_nt(q.astype(BF16), k.astype(BF16)) * pair
        o = _dot(att.astype(BF16), v_bf) + _dot_nt((q * q_dec).astype(BF16), st.astype(BF16))
    st_new = st * chunk[0:1, :] + _dot_tn(v_bf, (k * k_dec).astype(BF16))
    return o, st_new


def _ret_ctx_kernel(lg_ref, kf_ref, vf_ref, kb_ref, vb_ref, st_ref):
    @pl.when(pl.program_id(0) == 0)
    def _():
        st_ref[...] = jnp.zeros_like(st_ref)

    c = kf_ref.shape[0]
    for h in range(RET_HEADS):
        sk = slice(h * RET_DK, (h + 1) * RET_DK)
        sv = slice(h * RET_DV, (h + 1) * RET_DV)
        _, st_ref[h] = _ret_head(None, kf_ref[:, sk].astype(F32), vf_ref[:, sv],
                                 _ret_decays(lg_ref[0, h], c, False), st_ref[h])
        hb = RET_HEADS + h
        _, st_ref[hb] = _ret_head(None, kb_ref[:, sk].astype(F32), vb_ref[:, sv],
                                  _ret_decays(lg_ref[1, h], c, True), st_ref[hb])


def _ret_kernel(lg_ref, qf_ref, kf_ref, vf_ref, csf_ref, snf_ref,
                qb_ref, kb_ref, vb_ref, csb_ref, snb_ref, st0_ref, of_ref, ob_ref,
                st_ref, pair_ref, qd_ref, kd_ref, cd_ref):
    c = qf_ref.shape[0]

    @pl.when(pl.program_id(0) == 0)
    def _():
        st_ref[...] = st0_ref[...]
        for d in range(2):
            for h in range(RET_HEADS):
                i = d * RET_HEADS + h
                pair_ref[i], qd_ref[i], kd_ref[i], cd_ref[i] = _ret_decays(lg_ref[d, h], c, d == 1)

    csf, snf = csf_ref[...], snf_ref[...]
    csb, snb = csb_ref[...], snb_ref[...]
    for h in range(RET_HEADS):
        sk = slice(h * RET_DK, (h + 1) * RET_DK)
        sv = slice(h * RET_DV, (h + 1) * RET_DV)
        of_ref[:, sv], st_ref[h] = _ret_head(
            _rope(qf_ref[:, sk], csf, snf), _rope(kf_ref[:, sk], csf, snf), vf_ref[:, sv],
            (pair_ref[h], qd_ref[h], kd_ref[h], cd_ref[h]), st_ref[h])
        hb = RET_HEADS + h
        ob_ref[:, sv], st_ref[hb] = _ret_head(
            _rope(qb_ref[:, sk], csb, snb), _rope(kb_ref[:, sk], csb, snb), vb_ref[:, sv],
            (pair_ref[hb], qd_ref[hb], kd_ref[hb], cd_ref[hb]), st_ref[hb])


def _ret_scan(p, pc, ret_lg, cs_tab, sn_tab):
    c = RET_CHUNK
    n, l = p.shape[0], pc.shape[0]
    n_x, n_ctx = n // c, l // c
    qk_w = RET_HEADS * RET_DK
    st_shape = (2 * RET_HEADS, RET_DV, RET_DK)
    smem = pl.BlockSpec(memory_space=pltpu.SMEM)

    def kv_specs(chunk_fn):
        return [
            pl.BlockSpec((c, qk_w), lambda s: (chunk_fn(s), COL_RK_512)),
            pl.BlockSpec((c, RET_WIDTH), lambda s: (chunk_fn(s), COL_RV)),
        ]

    st0 = pl.pallas_call(
        _ret_ctx_kernel,
        grid=(n_ctx,),
        in_specs=[smem, *kv_specs(lambda s: s), *kv_specs(lambda s: n_ctx - 1 - s)],
        out_specs=pl.BlockSpec(st_shape, lambda s: (0, 0, 0)),
        out_shape=jax.ShapeDtypeStruct(st_shape, F32),
        compiler_params=_cparams(1, 32),
        name="ret_ctx",
    )(ret_lg, pc, pc, pc, pc)

    def specs(chunk_fn):
        return [
            pl.BlockSpec((c, qk_w), lambda s: (chunk_fn(s), COL_RQ_512)),
            *kv_specs(chunk_fn),
            pl.BlockSpec((c, RET_DK), lambda s: (chunk_fn(s), 0)),
            pl.BlockSpec((c, RET_DK), lambda s: (chunk_fn(s), 0)),
        ]

    fwd = lambda s: s
    bwd = lambda s: n_x - 1 - s
    return pl.pallas_call(
        _ret_kernel,
        grid=(n_x,),
        in_specs=[smem, *specs(fwd), *specs(bwd), pl.BlockSpec(st_shape, lambda s: (0, 0, 0))],
        out_specs=[
            pl.BlockSpec((c, RET_WIDTH), lambda s: (fwd(s), 0)),
            pl.BlockSpec((c, RET_WIDTH), lambda s: (bwd(s), 0)),
        ],
        out_shape=[jax.ShapeDtypeStruct((n, RET_WIDTH), F32)] * 2,
        scratch_shapes=[
            pltpu.VMEM(st_shape, F32),
            pltpu.VMEM((2 * RET_HEADS, c, c), F32),
            pltpu.VMEM((2 * RET_HEADS, c, RET_DK), F32),
            pltpu.VMEM((2 * RET_HEADS, c, RET_DK), F32),
            pltpu.VMEM((2 * RET_HEADS, 8, RET_DK), F32),
        ],
        compiler_params=_cparams(1, 32),
        name="ret",
    )(ret_lg, p, p, p, cs_tab, sn_tab, p, p, p, cs_tab, sn_tab, st0)


def _rms(y):
    return y * lax.rsqrt(jnp.mean(y * y, axis=-1, keepdims=True) + RMS_EPS)


def _outproj_kernel(of_ref, ob_ref, rf_ref, rb_ref, hgate_ref, rgate_ref, x_ref, wout_ref,
                    hgw_ref, rnw_ref, nw_ref, g1_ref, sh2_ref, sc2_ref, wrh_ref, wrl_ref, br_ref,
                    x1_ref, h2_ref, lg_ref, cat_scr):
    o = of_ref[...] + ob_ref[...]
    for h in range(HG_HEADS):
        sl = slice(h * HG_DK, (h + 1) * HG_DK)
        cat_scr[:, sl] = (_rms(o[:, sl]) * hgw_ref[0:1, sl]
                          * hgate_ref[:, sl].astype(F32)).astype(BF16)
    r = rf_ref[...] + rb_ref[...]
    for h in range(RET_HEADS):
        sl = slice(h * RET_DV, (h + 1) * RET_DV)
        cat_scr[:, HG_WIDTH + h * RET_DV:HG_WIDTH + (h + 1) * RET_DV] = (
            _rms(r[:, sl]) * rnw_ref[0:1, sl] * rgate_ref[:, sl].astype(F32)).astype(BF16)

    y = _dot(cat_scr[...], wout_ref[...])
    x1 = x_ref[...] + g1_ref[0:1, :] * (_rms(y) * nw_ref[1:2, :])
    x1_ref[...] = x1
    h2 = _rms(x1) * nw_ref[2:3, :] * (1.0 + sc2_ref[0:1, :]) + sh2_ref[0:1, :]
    h2_ref[...] = _pack_pairs(h2)
    hi = h2.astype(BF16)
    lo = (h2 - hi.astype(F32)).astype(BF16)
    lg_ref[...] = (_dot(hi, wrh_ref[...]) + _dot(lo, wrh_ref[...]) + _dot(hi, wrl_ref[...])
                   + br_ref[...])


def _out_projection(o_f, o_b, r_f, r_b, p, x2d, w_out_bf, hg_norm_w, ret_norm_w, norm_w, mod,
                    wr_hi, wr_lo, br):
    n, d = x2d.shape
    tm = 256
    row = lambda w: pl.BlockSpec((tm, w), lambda i: (i, 0))
    full = lambda a: pl.BlockSpec(a.shape, lambda i: (0,) * a.ndim)
    modcol = lambda k: pl.BlockSpec((8, d), lambda i: (0, k))
    return pl.pallas_call(
        _outproj_kernel,
        grid=(n // tm,),
        in_specs=[
            row(HG_WIDTH), row(HG_WIDTH), row(RET_WIDTH), row(RET_WIDTH),
            pl.BlockSpec((tm, HG_WIDTH), lambda i: (i, COL_HGATE)),
            pl.BlockSpec((tm, RET_WIDTH), lambda i: (i, COL_RGATE)),
            row(d), full(w_out_bf), full(hg_norm_w), full(ret_norm_w), full(norm_w),
            modcol(2), modcol(3), modcol(4),
            full(wr_hi), full(wr_lo), full(br),
        ],
        out_specs=[row(d), row(d // 2), row(ROUTER_LANES)],
        out_shape=[
            jax.ShapeDtypeStruct((n, d), F32),
            jax.ShapeDtypeStruct((n, d // 2), jnp.uint32),
            jax.ShapeDtypeStruct((n, ROUTER_LANES), F32),
        ],
        scratch_shapes=[pltpu.VMEM((tm, d), BF16)],
        compiler_params=_cparams(1, 48),
        name="outproj",
    )(o_f, o_b, r_f, r_b, p, p, x2d, w_out_bf, hg_norm_w, ret_norm_w, norm_w, mod, mod, mod,
      wr_hi, wr_lo, br)


ROUTER_TM = 1024
LANES = 128


def _lane_first(hit, lane_f):
    return jnp.min(jnp.where(hit, lane_f, 1e9), axis=1, keepdims=True)


def _router_kernel(lg_ref, gates_ref, s0_ref, s1_ref, cnt_ref, ltri, carry, pstart):
    ph = pl.program_id(0)
    i = pl.program_id(1)
    tm = lg_ref.shape[0]

    @pl.when(jnp.logical_and(ph == 0, i == 0))
    def _():
        t = lax.broadcasted_iota(jnp.int32, (tm, tm), 0)
        s = lax.broadcasted_iota(jnp.int32, (tm, tm), 1)
        ltri[...] = jnp.where(s < t, 1.0, 0.0).astype(BF16)

    @pl.when(jnp.logical_and(ph == 1, i == 0))
    def _():
        cnt = carry[...]
        padded = jnp.floor((cnt + (MOE_PAD - 1.0)) * (1.0 / MOE_PAD)) * MOE_PAD
        r = lax.broadcasted_iota(jnp.int32, (LANES, LANES), 0)
        c = lax.broadcasted_iota(jnp.int32, (LANES, LANES), 1)
        before = jnp.where(r < c, 1.0, 0.0).astype(BF16)
        pstart[...] = _dot(padded.astype(BF16), before)
        cnt_ref[...] = cnt.astype(jnp.int32)

    @pl.when(i == 0)
    def _():
        carry[...] = jnp.zeros_like(carry)

    lg = lg_ref[...]
    lane = lax.broadcasted_iota(jnp.int32, lg.shape, 1)
    lane_f = lane.astype(F32)
    neg = -jnp.inf

    gl = jnp.where(lane < N_GROUPS, lg, neg)
    gmax = jnp.max(gl, axis=1, keepdims=True)
    grp = _lane_first(gl == gmax, lane_f).astype(jnp.int32)
    p_sel = 1.0 / jnp.sum(jnp.exp(gl - gmax), axis=1, keepdims=True)

    in_grp = jnp.logical_and(
        jnp.logical_and(lane >= N_GROUPS, lane < N_GROUPS + N_EXPERTS),
        ((lane - N_GROUPS) >> 3) == grp)
    el = jnp.where(in_grp, lg, neg)
    v1 = jnp.max(el, axis=1, keepdims=True)
    o1 = lane_f == _lane_first(el == v1, lane_f)
    el2 = jnp.where(o1, neg, el)
    v2 = jnp.max(el2, axis=1, keepdims=True)
    o2 = lane_f == _lane_first(el2 == v2, lane_f)

    osum = (jnp.where(o1, 1.0, 0.0) + jnp.where(o2, 1.0, 0.0)).astype(BF16)
    earlier = _dot(ltri[...], osum) + carry[0:1, :]
    carry[...] = carry[...] + _dot(jnp.ones((8, tm), BF16), osum)

    @pl.when(ph == 1)
    def _():
        ez = jnp.exp(v2 - v1)
        g1 = p_sel / (1.0 + ez)
        gates_ref[...] = jnp.concatenate([g1, g1 * ez], axis=1)
        base = earlier + pstart[0:1, :]
        for o, s_ref in ((o1, s0_ref), (o2, s1_ref)):
            col = jnp.sum(jnp.where(o, base, 0.0), axis=1, keepdims=True)
            wide = jnp.broadcast_to(col, (tm, LANES))
            for b in range(tm // LANES):
                s_ref[b:b + 1, :] = wide[b * LANES:(b + 1) * LANES, :].T[0:1, :].astype(jnp.int32)


def _router(logits):
    t = logits.shape[0]
    tm = ROUTER_TM
    rows = tm // LANES
    return pl.pallas_call(
        _router_kernel,
        grid=(2, t // tm),
        in_specs=[pl.BlockSpec((tm, LANES), lambda ph, i: (i, 0))],
        out_specs=[
            pl.BlockSpec((tm, EXPERT_TOPK), lambda ph, i: (i * ph, 0)),
            pl.BlockSpec((rows, LANES), lambda ph, i: (i * ph, 0)),
            pl.BlockSpec((rows, LANES), lambda ph, i: (i * ph, 0)),
            pl.BlockSpec((8, LANES), lambda ph, i: (0, 0)),
        ],
        out_shape=[
            jax.ShapeDtypeStruct((t, EXPERT_TOPK), F32),
            jax.ShapeDtypeStruct((t // LANES, LANES), jnp.int32),
            jax.ShapeDtypeStruct((t // LANES, LANES), jnp.int32),
            jax.ShapeDtypeStruct((8, LANES), jnp.int32),
        ],
        scratch_shapes=[
            pltpu.VMEM((tm, tm), BF16),
            pltpu.VMEM((8, LANES), F32),
            pltpu.VMEM((8, LANES), F32),
        ],
        compiler_params=_cparams(2, 32),
        name="router",
    )(logits)


def _round_up_pow2(x, m):
    assert m & (m - 1) == 0
    return (x + (m - 1)) & ~(m - 1)


def _plan_kernel(cnt_ref, s0_ref, s1_ref, tok_ref, ie_ref, ist_ref, ir_ref, if_ref):
    n_tok = s0_ref.shape[0] * LANES
    n_slots = tok_ref.shape[0]
    n_items = ie_ref.shape[0]
    r = MOE_ITEM_ROWS

    def scatter(t8, carry):
        row = t8 >> 4
        for d in range(8):
            col = (t8 & 15) * 8 + d
            t = t8 * 8 + d
            tok_ref[s0_ref[row, col]] = t
            tok_ref[s1_ref[row, col]] = t
        return carry
    lax.fori_loop(0, n_tok // 8, scatter, 0)

    a = jnp.int32(0)
    used = jnp.int32(0)
    e_last = jnp.int32(0)
    def pad_slot(i, carry):
        tok_ref[i] = 0
        return carry

    for e in range(N_EXPERTS):
        cnt = cnt_ref[0, N_GROUPS + e]
        pad = _round_up_pow2(cnt, MOE_PAD)
        lax.fori_loop(used + cnt, used + pad, pad_slot, 0)

        def item(sub, a, e=e, pad=pad, used=used):
            ie_ref[a] = e
            ist_ref[a] = used + sub * r
            ir_ref[a] = jnp.minimum(pad - sub * r, r)
            if_ref[a] = 0
            return a + 1
        a = lax.fori_loop(0, _round_up_pow2(pad, r) >> (r.bit_length() - 1), item, a)
        e_last = jnp.where(pad > 0, e, e_last)
        used = used + pad

    lax.fori_loop(used, n_slots, pad_slot, 0)

    def idle(a2, carry):
        fill_start = used + (a2 - a) * r
        fill = jnp.clip(n_slots - fill_start, 0, r)
        ie_ref[a2] = e_last
        ist_ref[a2] = jnp.where(fill > 0, fill_start, 0)
        ir_ref[a2] = 0
        if_ref[a2] = fill
        return carry
    lax.fori_loop(a, n_items, idle, 0)


def _plan(counts, s0, s1, n_slots, n_items):
    smem = pl.BlockSpec(memory_space=pltpu.SMEM)
    item = jax.ShapeDtypeStruct((n_items,), jnp.int32)
    return pl.pallas_call(
        _plan_kernel,
        in_specs=[smem, smem, smem],
        out_specs=[smem] * 5,
        out_shape=[jax.ShapeDtypeStruct((n_slots,), jnp.int32), item, item, item, item],
        name="plan",
    )(counts, s0, s1)


def _unpack_pairs(u):
    lo = lax.bitcast_convert_type(u << 16, F32)
    hi = lax.bitcast_convert_type(u & jnp.uint32(0xFFFF0000), F32)
    return jnp.concatenate([lo, hi], axis=1)


def _pack_pairs(y):
    w = y.shape[1] // 2
    bits = lax.bitcast_convert_type(y.astype(BF16).astype(F32), jnp.uint32)
    return (bits[:, :w] >> 16) | (bits[:, w:] & jnp.uint32(0xFFFF0000))


def _moe_kernel(item_e, item_start, item_rows, item_fill, slot_tok,
                h2p_hbm, wg_ref, wu_ref, wd_ref, yb_hbm,
                xbuf, xs_bf, act, ystage, pf, gsem, osem):
    a = pl.program_id(0)
    t = pl.program_id(1)
    n_items = pl.num_programs(0)
    n = item_rows[a]
    start = item_start[a]
    rb_rows = MOE_PAD
    n_rb = n // rb_rows
    half_w = wg_ref.shape[2]
    cur = a % 2
    nxt = 1 - cur
    a1 = jnp.minimum(a + 1, n_items - 1)
    n_next = jnp.where(a + 1 < n_items, item_rows[a1], 0)
    start_next = item_start[a1]

    def row_copy(tok, slot, i):
        return pltpu.make_async_copy(h2p_hbm.at[pl.ds(tok, 1), :], xbuf.at[slot, pl.ds(i, 1), :],
                                     gsem.at[slot])

    @pl.when(jnp.logical_and(a == 0, t == 0))
    def _():
        def issue(i, carry):
            row_copy(slot_tok[start + i], 0, i).start()
            return carry
        lax.fori_loop(0, n, issue, 0)

    @pl.when(t == 0)
    def _():
        pf[0] = 0

        def drain(bk, carry):
            r0 = pl.multiple_of(bk * rb_rows, rb_rows)
            pltpu.make_async_copy(h2p_hbm.at[pl.ds(0, rb_rows), :],
                                  xbuf.at[cur, pl.ds(r0, rb_rows), :], gsem.at[cur]).wait()
            return carry
        lax.fori_loop(0, n_rb, drain, 0)

    def prefetch_group():
        base = pf[0]
        for q in range(MOE_GATHER_GROUP):
            row_copy(slot_tok[start_next + base + q], nxt, base + q).start()
        pf[0] = base + MOE_GATHER_GROUP

    def run_blocks(compute):
        def body(rb, carry):
            more = pf[0] < n_next

            @pl.when(more)
            def _():
                compute(rb)
                prefetch_group()

            @pl.when(jnp.logical_not(more))
            def _():
                compute(rb)
            return carry
        lax.fori_loop(0, n_rb, body, 0)

    def gate_up(half):
        def compute(rb):
            r0 = pl.multiple_of(rb * rb_rows, rb_rows)
            if half == 0:
                xs = _unpack_pairs(xbuf[cur, pl.ds(r0, rb_rows), :]).astype(BF16)
                xs_bf[pl.ds(r0, rb_rows), :] = xs
            else:
                xs = xs_bf[pl.ds(r0, rb_rows), :]
            g = _dot(xs, wg_ref[0].astype(BF16))
            u = _dot(xs, wu_ref[0].astype(BF16))
            act[pl.ds(r0, rb_rows), half * half_w:(half + 1) * half_w] = (
                g * _sigmoid(g) * u).astype(BF16)
        run_blocks(compute)

    def out_copy(slot, r0):
        return pltpu.make_async_copy(
            ystage.at[slot],
            yb_hbm.at[pl.ds(pl.multiple_of(start + r0, rb_rows), rb_rows), :],
            osem.at[slot])

    def down():
        def compute(rb):
            r0 = pl.multiple_of(rb * rb_rows, rb_rows)
            slot = rb % 2
            y = _dot(act[pl.ds(r0, rb_rows), :], wd_ref[0].astype(BF16))

            @pl.when(rb >= 2)
            def _():
                out_copy(slot, r0).wait()

            ystage[slot] = _pack_pairs(y)
            out_copy(slot, r0).start()
        run_blocks(compute)

        @pl.when(n_rb >= 1)
        def _():
            out_copy((n_rb - 1) % 2, 0).wait()

        @pl.when(n_rb >= 2)
        def _():
            out_copy(n_rb % 2, 0).wait()

        def tail(i, carry):
            row_copy(slot_tok[start_next + i], nxt, i).start()
            return carry
        lax.fori_loop(pf[0], n_next, tail, 0)

    pl.when(t == 0)(functools.partial(gate_up, 0))
    pl.when(t == 1)(functools.partial(gate_up, 1))
    pl.when(t == 2)(down)

    fill = item_fill[a]

    @pl.when(jnp.logical_and(t == 0, fill > 0))
    def _():
        ystage[0] = jnp.zeros(ystage.shape[1:], ystage.dtype)

        def zero_block(bk, carry):
            cp = out_copy(0, bk * rb_rows)
            cp.start()
            cp.wait()
            return carry
        lax.fori_loop(0, fill // rb_rows, zero_block, 0)


def _moe_ffn(h2p, w_gate, w_up, w_down, item_e, item_start, item_rows, item_fill, slot_tok,
             n_slots):
    pw = h2p.shape[1]
    d = 2 * pw
    n_items = item_e.shape[0]
    half_w = D_EXPERT // 2

    def wgu_map(a, t, ie, ist, ir, fl, st):
        return (ie[a], 0, jnp.where(ir[a] > 0, jnp.minimum(t, 1), 1))

    def wd_map(a, t, ie, ist, ir, fl, st):
        return (jnp.where(t >= 2, ie[a], ie[jnp.maximum(a - 1, 0)]), 0, 0)

    grid_spec = pltpu.PrefetchScalarGridSpec(
        num_scalar_prefetch=5,
        grid=(n_items, 3),
        in_specs=[
            pl.BlockSpec(memory_space=pl.ANY),
            pl.BlockSpec((1, d, half_w), wgu_map),
            pl.BlockSpec((1, d, half_w), wgu_map),
            pl.BlockSpec((1, D_EXPERT, d), wd_map),
        ],
        out_specs=pl.BlockSpec(memory_space=pl.ANY),
        scratch_shapes=[
            pltpu.VMEM((2, MOE_ITEM_ROWS, pw), jnp.uint32),
            pltpu.VMEM((MOE_ITEM_ROWS, d), BF16),
            pltpu.VMEM((MOE_ITEM_ROWS, D_EXPERT), BF16),
            pltpu.VMEM((2, MOE_PAD, pw), jnp.uint32),
            pltpu.SMEM((1,), jnp.int32),
            pltpu.SemaphoreType.DMA((2,)),
            pltpu.SemaphoreType.DMA((2,)),
        ],
    )
    return pl.pallas_call(
        _moe_kernel,
        grid_spec=grid_spec,
        out_shape=jax.ShapeDtypeStruct((n_slots, pw), jnp.uint32),
        compiler_params=_cparams(2, 56),
        name="moe",
    )(item_e, item_start, item_rows, item_fill, slot_tok, h2p, w_gate, w_up, w_down)


def _combine_kernel(s0, s1, yb_hbm, gates_ref, x1_ref, nw_ref, g2_ref, out_ref, rows, sem):
    i = pl.program_id(0)
    n_tiles = pl.num_programs(0)
    tm = x1_ref.shape[0]

    def gather(tile, buf):
        def issue(r8, carry):
            t0 = tile * tm + r8 * 8
            for dr in range(8):
                for k, s_k in enumerate((s0, s1)):
                    slot = s_k[t0 >> 7, (t0 & (LANES - 1)) + dr]
                    pltpu.make_async_copy(yb_hbm.at[pl.ds(slot, 1), :],
                                          rows.at[buf, k, pl.ds(r8 * 8 + dr, 1), :],
                                          sem.at[buf]).start()
            return carry
        lax.fori_loop(0, tm // 8, issue, 0)

    @pl.when(i == 0)
    def _():
        gather(0, 0)

    @pl.when(i + 1 < n_tiles)
    def _():
        gather(i + 1, (i + 1) % 2)

    buf = i % 2
    for k in range(EXPERT_TOPK):
        pltpu.make_async_copy(yb_hbm.at[pl.ds(0, tm), :], rows.at[buf, k], sem.at[buf]).wait()

    g = gates_ref[...]
    y = (_unpack_pairs(rows[buf, 0]) * g[:, 0:1] + _unpack_pairs(rows[buf, 1]) * g[:, 1:2])
    out_ref[...] = x1_ref[...] + g2_ref[0:1, :] * (_rms(y) * nw_ref[3:4, :])


def _combine(s0, s1, yb, gates, x1, norm_w, mod):
    n, d = x1.shape
    tm = 256
    grid_spec = pltpu.PrefetchScalarGridSpec(
        num_scalar_prefetch=2,
        grid=(n // tm,),
        in_specs=[
            pl.BlockSpec(memory_space=pl.ANY),
            pl.BlockSpec((tm, EXPERT_TOPK), lambda i, *_: (i, 0)),
            pl.BlockSpec((tm, d), lambda i, *_: (i, 0)),
            pl.BlockSpec((4, d), lambda i, *_: (0, 0)),
            pl.BlockSpec((8, d), lambda i, *_: (0, 5)),
        ],
        out_specs=pl.BlockSpec((tm, d), lambda i, *_: (i, 0)),
        scratch_shapes=[
            pltpu.VMEM((2, EXPERT_TOPK, tm, d // 2), jnp.uint32),
            pltpu.SemaphoreType.DMA((2,)),
        ],
    )
    return pl.pallas_call(
        _combine_kernel,
        grid_spec=grid_spec,
        out_shape=jax.ShapeDtypeStruct((n, d), F32),
        compiler_params=_cparams(1, 32),
        name="combine",
    )(s0, s1, yb, gates, x1, norm_w, mod)


def _rope_tables(n):
    rows = n // GRID_W
    r = jnp.broadcast_to(jnp.arange(rows)[:, None], (rows, GRID_W)).reshape(-1).astype(F32)
    cidx = jnp.broadcast_to(jnp.arange(GRID_W)[None, :], (rows, GRID_W)).reshape(-1).astype(F32)
    n_freq = RET_DK // 4
    inv = ROPE_BASE ** (-jnp.arange(n_freq, dtype=F32) / n_freq)
    ang = jnp.concatenate([r[:, None] * inv, cidx[:, None] * inv], axis=-1)
    cos, sin = jnp.cos(ang), jnp.sin(ang)
    cs = jnp.concatenate([cos, cos], axis=-1)
    sn = jnp.concatenate([-sin, sin], axis=-1)
    return cs, sn


def kernel(x, c, ctx, c_ctx, w_mod, b_mod, norm_w, w_in, hg_lb_logits, hg_norm_w, ret_decay_logits,
           ret_norm_w, w_out, router_group_w, router_group_b, router_expert_w, router_expert_b,
           w_gate, w_up, w_down):
    bsz, n, d = x.shape
    l = ctx.shape[1]
    assert bsz == 1 and w_mod.shape[0] == 1, "single sample, single layer"
    x2d, ctx2d = x[0], ctx[0]

    cc = jnp.zeros((8, d), F32).at[0].set(c[0]).at[1].set(c_ctx)
    mod = _modulation(cc, w_mod[0], b_mod[0])

    lb = jnp.cumsum(jax.nn.softmax(hg_lb_logits.astype(F32), axis=0), axis=0)[0]
    rk0 = 5 * HG_WIDTH + RET_HEADS * RET_DK
    cols = jnp.stack([
        jnp.zeros((PROJ_WIDTH,), F32).at[HG_WIDTH:3 * HG_WIDTH].set(lb.reshape(-1)),
        jnp.ones((PROJ_WIDTH,), F32).at[rk0:rk0 + RET_HEADS * RET_DK].set(RET_DK ** -0.5),
    ])
    w_in_bf = w_in[0].astype(BF16)
    p, g = _projection(x2d, norm_w[0], mod, w_in_bf, cols, 0, 1024, "proj_x")
    pc, gc = _projection(ctx2d, norm_w[0], mod, w_in_bf, cols, 1, l, "proj_ctx")

    ret_lg = jax.nn.log_sigmoid(ret_decay_logits[0].astype(F32))
    cs_tab, sn_tab = _rope_tables(n)

    o_f, o_b = _hgrn_scan(p, g, pc, gc)
    r_f, r_b = _ret_scan(p, pc, ret_lg, cs_tab, sn_tab)

    wr = jnp.concatenate([router_group_w[0], router_expert_w[0]], axis=1)
    wr = jnp.pad(wr, ((0, 0), (0, ROUTER_LANES - wr.shape[1])))
    wr_hi = wr.astype(BF16)
    wr_lo = (wr - wr_hi.astype(F32)).astype(BF16)
    br = jnp.concatenate([router_group_b[0], router_expert_b[0]]).astype(F32)
    br = jnp.pad(br, (0, ROUTER_LANES - br.shape[0])).reshape(1, ROUTER_LANES)
    x1, h2, logits = _out_projection(
        o_f, o_b, r_f, r_b, p, x2d, w_out[0].astype(BF16), hg_norm_w, ret_norm_w, norm_w[0], mod,
        wr_hi, wr_lo, br)

    n_assign = n * EXPERT_TOPK
    n_slots = n_assign + N_EXPERTS * MOE_PAD
    n_items = N_EXPERTS + n_slots // MOE_ITEM_ROWS
    gates, s0, s1, counts = _router(logits)
    slot_tok, item_e, item_start, item_rows, item_fill = _plan(counts, s0, s1, n_slots, n_items)

    yb = _moe_ffn(h2, w_gate[0], w_up[0], w_down[0], item_e, item_start, item_rows, item_fill,
                  slot_tok, n_slots)
    out = _combine(s0, s1, yb, gates, x1, norm_w[0], mod)
    return out[None]
```

```python
import functools

import jax
import jax.numpy as jnp
from jax import lax
from jax.experimental import pallas as pl
from jax.experimental.pallas import tpu as pltpu

F32 = jnp.float32
BF16 = jnp.bfloat16

D_MODEL = 2048
GRID_W = 64
HG_HEADS = 8
HG_DK = 128
HG_WIDTH = 1024
RET_HEADS = 4
RET_DK = 128
RET_DV = 256
RET_WIDTH = 1024
PROJ_WIDTH = 8192
N_GROUPS = 4
EXPERTS_PER_GROUP = 8
N_EXPERTS = 32
EXPERT_TOPK = 2
D_EXPERT = 1024
RMS_EPS = 1e-6
ROPE_BASE = 10000.0

COL_HQ, COL_HF_FWD, COL_HF_BWD, COL_HI, COL_HGATE = 0, 1, 2, 3, 4
COL_RQ_512, COL_RK_512 = 10, 11
COL_RV, COL_RGATE = 6, 7

HG_CHUNK = 128
HG_DIAG = 16
HG_LOG_CLAMP = 10.0
RET_CHUNK = 128

MOE_PAD = 256
MOE_ITEM_ROWS = 1024
MOE_GATHER_GROUP = 128
ROUTER_LANES = 128


def _cparams(n_axes, vmem_mb):
    return pltpu.CompilerParams(
        dimension_semantics=("arbitrary",) * n_axes,
        vmem_limit_bytes=vmem_mb * 1024 * 1024,
    )


def _sigmoid(x):
    return 1.0 / (1.0 + jnp.exp(-x))


def _dot(a, b):
    return jnp.dot(a, b, preferred_element_type=F32)


def _dot_nt(a, b):
    return lax.dot_general(a, b, (((1,), (1,)), ((), ())), preferred_element_type=F32)


def _dot_tn(a, b):
    return lax.dot_general(a, b, (((0,), (0,)), ((), ())), preferred_element_type=F32)


def _mod_kernel(cc_ref, w_ref, b_ref, o_ref):
    s = cc_ref[...]
    s = s * _sigmoid(s)
    s16 = jnp.concatenate([s, s], axis=0)
    hi = s16.astype(BF16).astype(F32)
    row = lax.broadcasted_iota(jnp.int32, s16.shape, 0)
    lhs = jnp.where(row < 8, hi, s16 - hi).astype(BF16)
    r = _dot(lhs, w_ref[...].astype(BF16))
    o_ref[...] = r[:8] + r[8:] + b_ref[...]


def _modulation(cc, w_mod, b_mod):
    d, n = w_mod.shape
    tn = 1024
    return pl.pallas_call(
        _mod_kernel,
        grid=(n // tn,),
        in_specs=[
            pl.BlockSpec((8, d), lambda j: (0, 0)),
            pl.BlockSpec((d, tn), lambda j: (0, j)),
            pl.BlockSpec((1, tn), lambda j: (0, j)),
        ],
        out_specs=pl.BlockSpec((8, tn), lambda j: (0, j)),
        out_shape=jax.ShapeDtypeStruct((8, n), F32),
        compiler_params=_cparams(1, 40),
        name="mod",
    )(cc, w_mod, b_mod.reshape(1, n))


PROJ_TN = 1024
PROLOGUE_ROWS = 32


def _proj_kernel(x_ref, nw_ref, sh_ref, sc_ref, w_ref, cols_ref, o_ref, g_ref, h_scr, *, mod_row):
    j = pl.program_id(1)

    @pl.when(j == 0)
    def _():
        scale = nw_ref[0:1, :] * (1.0 + sc_ref[mod_row:mod_row + 1, :])
        shift = sh_ref[mod_row:mod_row + 1, :]

        def body(r, carry):
            r0 = pl.multiple_of(r * PROLOGUE_ROWS, PROLOGUE_ROWS)
            x = x_ref[pl.ds(r0, PROLOGUE_ROWS), :]
            y = x * lax.rsqrt(jnp.mean(x * x, axis=-1, keepdims=True) + RMS_EPS)
            h_scr[pl.ds(r0, PROLOGUE_ROWS), :] = (y * scale + shift).astype(BF16)
            return carry
        lax.fori_loop(0, x_ref.shape[0] // PROLOGUE_ROWS, body, 0)

    is_gate = jnp.logical_or(j == 1, j == 2)
    is_silu = jnp.logical_or(j == 0, jnp.logical_or(j == 4, j == 7))
    is_lin = jnp.logical_not(jnp.logical_or(is_gate, is_silu))

    @pl.when(is_lin)
    def _():
        o_ref[...] = (_dot(h_scr[...], w_ref[...]) * cols_ref[1:2, :]).astype(BF16)

    @pl.when(is_silu)
    def _():
        acc = _dot(h_scr[...], w_ref[...])
        o_ref[...] = (acc * _sigmoid(acc)).astype(BF16)

    @pl.when(is_gate)
    def _():
        acc = _dot(h_scr[...], w_ref[...])
        lb = cols_ref[0:1, :]
        g = jnp.maximum(jnp.log(lb + (1.0 - lb) * _sigmoid(acc)), -HG_LOG_CLAMP)
        g_ref[...] = g
        o_ref[...] = g.astype(BF16)


def _projection(x2d, norm_w, mod, w_in, cols, mod_row, tm, name):
    n, d = x2d.shape
    tn = PROJ_TN
    return pl.pallas_call(
        functools.partial(_proj_kernel, mod_row=mod_row),
        grid=(n // tm, PROJ_WIDTH // tn),
        in_specs=[
            pl.BlockSpec((tm, d), lambda i, j: (i, 0)),
            pl.BlockSpec((4, d), lambda i, j: (0, 0)),
            pl.BlockSpec((8, d), lambda i, j: (0, 0)),
            pl.BlockSpec((8, d), lambda i, j: (0, 1)),
            pl.BlockSpec((d, tn), lambda i, j: (0, j)),
            pl.BlockSpec((2, tn), lambda i, j: (0, j)),
        ],
        out_specs=[
            pl.BlockSpec((tm, tn), lambda i, j: (i, j)),
            pl.BlockSpec((tm, tn), lambda i, j: (i, jnp.clip(j - 1, 0, 1))),
        ],
        out_shape=[
            jax.ShapeDtypeStruct((n, PROJ_WIDTH), BF16),
            jax.ShapeDtypeStruct((n, 2 * HG_WIDTH), F32),
        ],
        scratch_shapes=[pltpu.VMEM((tm, d), BF16)],
        compiler_params=_cparams(2, 52),
        name=name,
    )(x2d, norm_w, mod, mod, w_in, cols)


def _tri_ones(c, rev):
    t = lax.broadcasted_iota(jnp.int32, (c, c), 0)
    s = lax.broadcasted_iota(jnp.int32, (c, c), 1)
    return jnp.where((s >= t) if rev else (s <= t), 1.0, 0.0).astype(BF16)


def _cumsum_rows(g, tri):
    hi = g.astype(BF16)
    lo = (g - hi.astype(F32)).astype(BF16)
    return _dot(tri, hi) + _dot(tri, lo)


def _row_refs(b, idxs, blk):
    parts = [jnp.broadcast_to(b[i:i + 1, :], (blk, b.shape[1])) for i in idxs]
    return parts[0] if len(parts) == 1 else jnp.concatenate(parts, axis=0)


def _hgrn_levels(c, rev):
    t = lax.broadcasted_iota(jnp.int32, (c, c), 0)
    s = lax.broadcasted_iota(jnp.int32, (c, c), 1)
    levels = []
    m = c // 2
    while m >= HG_DIAG:
        blk = 2 * m
        same = ((t ^ s) & ~(blk - 1)) == 0
        if rev:
            idxs = [b0 + m for b0 in range(0, c, blk)]
            cross = jnp.logical_and((t & m) == 0, (s & m) != 0)
        else:
            idxs = [b0 + m - 1 for b0 in range(0, c, blk)]
            cross = jnp.logical_and((t & m) != 0, (s & m) == 0)
        levels.append((idxs, blk, jnp.logical_and(same, cross)))
        m //= 2
    blk = HG_DIAG
    same = ((t ^ s) & ~(blk - 1)) == 0
    if rev:
        idxs = [b0 + blk // 2 for b0 in range(0, c, blk)]
        mask = jnp.logical_and(same, s >= t)
    else:
        idxs = [b0 + blk // 2 - 1 for b0 in range(0, c, blk)]
        mask = jnp.logical_and(same, s <= t)
    levels.append((idxs, blk, mask))
    return levels


def _hgrn_head(q_bf, g, b, v_bf, st, rev, levels):
    c = g.shape[0]
    kk = 1.0 - jnp.exp(g)

    o = None
    if levels is not None:
        q = q_bf.astype(F32)
        att = jnp.zeros((c, c), F32)
        for idxs, blk, mask in levels:
            ref = _row_refs(b, idxs, blk)
            a_l = _dot_nt((q * jnp.exp(b - ref)).astype(BF16),
                          (kk * jnp.exp(ref - b)).astype(BF16))
            att = att + jnp.where(mask, a_l, 0.0)
        o = _dot(att.astype(BF16), v_bf) + _dot_nt((q * jnp.exp(b)).astype(BF16), st.astype(BF16))

    b_end = b[0:1, :] if rev else b[c - 1:c, :]
    k_end = (kk * jnp.exp(b_end - b)).astype(BF16)
    st_new = st * jnp.exp(b_end) + _dot_tn(v_bf, k_end)
    return o, st_new


def _hgrn_ctx_kernel(gf_ref, vf_ref, gb_ref, vb_ref, st_ref):
    @pl.when(pl.program_id(0) == 0)
    def _():
        st_ref[...] = jnp.zeros_like(st_ref)

    c = gf_ref.shape[0]
    bf = _cumsum_rows(gf_ref[...], _tri_ones(c, False))
    bb = _cumsum_rows(gb_ref[...], _tri_ones(c, True))
    for h in range(HG_HEADS):
        sl = slice(h * HG_DK, (h + 1) * HG_DK)
        _, st_ref[h] = _hgrn_head(None, gf_ref[:, sl], bf[:, sl], vf_ref[:, sl], st_ref[h],
                                  False, None)
        hb = HG_HEADS + h
        _, st_ref[hb] = _hgrn_head(None, gb_ref[:, sl], bb[:, sl], vb_ref[:, sl], st_ref[hb],
                                   True, None)


def _hgrn_kernel(qf_ref, gf_ref, vf_ref, qb_ref, gb_ref, vb_ref, st0_ref, of_ref, ob_ref, st_ref):
    @pl.when(pl.program_id(0) == 0)
    def _():
        st_ref[...] = st0_ref[...]

    c = gf_ref.shape[0]
    lev_f = _hgrn_levels(c, False)
    lev_b = _hgrn_levels(c, True)
    bf = _cumsum_rows(gf_ref[...], _tri_ones(c, False))
    bb = _cumsum_rows(gb_ref[...], _tri_ones(c, True))
    for h in range(HG_HEADS):
        sl = slice(h * HG_DK, (h + 1) * HG_DK)
        of_ref[:, sl], st_ref[h] = _hgrn_head(qf_ref[:, sl], gf_ref[:, sl], bf[:, sl],
                                              vf_ref[:, sl], st_ref[h], False, lev_f)
        hb = HG_HEADS + h
        ob_ref[:, sl], st_ref[hb] = _hgrn_head(qb_ref[:, sl], gb_ref[:, sl], bb[:, sl],
                                               vb_ref[:, sl], st_ref[hb], True, lev_b)


def _hgrn_scan(p, g, pc, gc):
    c = HG_CHUNK
    w = HG_WIDTH
    n, l = p.shape[0], pc.shape[0]
    n_x, n_ctx = n // c, l // c
    st_shape = (2 * HG_HEADS, HG_DK, HG_DK)

    def spec(chunk_fn, col):
        return pl.BlockSpec((c, w), lambda s: (chunk_fn(s), col))

    st0 = pl.pallas_call(
        _hgrn_ctx_kernel,
        grid=(n_ctx,),
        in_specs=[
            spec(lambda s: s, 0), spec(lambda s: s, COL_HI),
            spec(lambda s: n_ctx - 1 - s, 1), spec(lambda s: n_ctx - 1 - s, COL_HI),
        ],
        out_specs=pl.BlockSpec(st_shape, lambda s: (0, 0, 0)),
        out_shape=jax.ShapeDtypeStruct(st_shape, F32),
        compiler_params=_cparams(1, 32),
        name="hgrn_ctx",
    )(gc, pc, gc, pc)

    fwd = lambda s: s
    bwd = lambda s: n_x - 1 - s
    return pl.pallas_call(
        _hgrn_kernel,
        grid=(n_x,),
        in_specs=[
            spec(fwd, COL_HQ), spec(fwd, 0), spec(fwd, COL_HI),
            spec(bwd, COL_HQ), spec(bwd, 1), spec(bwd, COL_HI),
            pl.BlockSpec(st_shape, lambda s: (0, 0, 0)),
        ],
        out_specs=[spec(fwd, 0), spec(bwd, 0)],
        out_shape=[jax.ShapeDtypeStruct((n, w), F32)] * 2,
        scratch_shapes=[pltpu.VMEM(st_shape, F32)],
        compiler_params=_cparams(1, 32),
        name="hgrn",
    )(p, g, p, p, g, p, st0)


def _rope(x_bf, cs, sn):
    x = x_bf.astype(F32)
    return x * cs + pltpu.roll(x, RET_DK // 2, axis=1) * sn


def _ret_decays(lg, c, rev):
    t = lax.broadcasted_iota(jnp.int32, (c, c), 0)
    s = lax.broadcasted_iota(jnp.int32, (c, c), 1)
    diff = (s - t) if rev else (t - s)
    pair = jnp.where(diff >= 0, jnp.exp(lg * jnp.maximum(diff, 0).astype(F32)), 0.0)
    pos = lax.broadcasted_iota(jnp.int32, (c, RET_DK), 0).astype(F32)
    q_dec = jnp.exp(lg * ((c - pos) if rev else (pos + 1.0)))
    k_dec = jnp.exp(lg * (pos if rev else (c - 1.0 - pos)))
    chunk = jnp.exp(jnp.full((8, RET_DK), lg * c, F32))
    return pair, q_dec, k_dec, chunk


def _ret_head(q, k, v_bf, decays, st):
    pair, q_dec, k_dec, chunk = decays
    o = None
    if q is not None:
        att = _dot_nt(q.astype(BF16), k.astype(BF16)) * pair
        o = _dot(att.astype(BF16), v_bf) + _dot_nt((q * q_dec).astype(BF16), st.astype(BF16))
    st_new = st * chunk[0:1, :] + _dot_tn(v_bf, (k * k_dec).astype(BF16))
    return o, st_new


def _ret_ctx_kernel(lg_ref, kf_ref, vf_ref, kb_ref, vb_ref, st_ref):
    @pl.when(pl.program_id(0) == 0)
    def _():
        st_ref[...] = jnp.zeros_like(st_ref)

    c = kf_ref.shape[0]
    for h in range(RET_HEADS):
        sk = slice(h * RET_DK, (h + 1) * RET_DK)
        sv = slice(h * RET_DV, (h + 1) * RET_DV)
        _, st_ref[h] = _ret_head(None, kf_ref[:, sk].astype(F32), vf_ref[:, sv],
                                 _ret_decays(lg_ref[0, h], c, False), st_ref[h])
        hb = RET_HEADS + h
        _, st_ref[hb] = _ret_head(None, kb_ref[:, sk].astype(F32), vb_ref[:, sv],
                                  _ret_decays(lg_ref[1, h], c, True), st_ref[hb])


def _ret_kernel(lg_ref, qf_ref, kf_ref, vf_ref, csf_ref, snf_ref,
                qb_ref, kb_ref, vb_ref, csb_ref, snb_ref, st0_ref, of_ref, ob_ref,
                st_ref, pair_ref, qd_ref, kd_ref, cd_ref):
    c = qf_ref.shape[0]

    @pl.when(pl.program_id(0) == 0)
    def _():
        st_ref[...] = st0_ref[...]
        for d in range(2):
            for h in range(RET_HEADS):
                i = d * RET_HEADS + h
                pair_ref[i], qd_ref[i], kd_ref[i], cd_ref[i] = _ret_decays(lg_ref[d, h], c, d == 1)

    csf, snf = csf_ref[...], snf_ref[...]
    csb, snb = csb_ref[...], snb_ref[...]
    for h in range(RET_HEADS):
        sk = slice(h * RET_DK, (h + 1) * RET_DK)
        sv = slice(h * RET_DV, (h + 1) * RET_DV)
        of_ref[:, sv], st_ref[h] = _ret_head(
            _rope(qf_ref[:, sk], csf, snf), _rope(kf_ref[:, sk], csf, snf), vf_ref[:, sv],
            (pair_ref[h], qd_ref[h], kd_ref[h], cd_ref[h]), st_ref[h])
        hb = RET_HEADS + h
        ob_ref[:, sv], st_ref[hb] = _ret_head(
            _rope(qb_ref[:, sk], csb, snb), _rope(kb_ref[:, sk], csb, snb), vb_ref[:, sv],
            (pair_ref[hb], qd_ref[hb], kd_ref[hb], cd_ref[hb]), st_ref[hb])


def _ret_scan(p, pc, ret_lg, cs_tab, sn_tab):
    c = RET_CHUNK
    n, l = p.shape[0], pc.shape[0]
    n_x, n_ctx = n // c, l // c
    qk_w = RET_HEADS * RET_DK
    st_shape = (2 * RET_HEADS, RET_DV, RET_DK)
    smem = pl.BlockSpec(memory_space=pltpu.SMEM)

    def kv_specs(chunk_fn):
        return [
            pl.BlockSpec((c, qk_w), lambda s: (chunk_fn(s), COL_RK_512)),
            pl.BlockSpec((c, RET_WIDTH), lambda s: (chunk_fn(s), COL_RV)),
        ]

    st0 = pl.pallas_call(
        _ret_ctx_kernel,
        grid=(n_ctx,),
        in_specs=[smem, *kv_specs(lambda s: s), *kv_specs(lambda s: n_ctx - 1 - s)],
        out_specs=pl.BlockSpec(st_shape, lambda s: (0, 0, 0)),
        out_shape=jax.ShapeDtypeStruct(st_shape, F32),
        compiler_params=_cparams(1, 32),
        name="ret_ctx",
    )(ret_lg, pc, pc, pc, pc)

    def specs(chunk_fn):
        return [
            pl.BlockSpec((c, qk_w), lambda s: (chunk_fn(s), COL_RQ_512)),
            *kv_specs(chunk_fn),
            pl.BlockSpec((c, RET_DK), lambda s: (chunk_fn(s), 0)),
            pl.BlockSpec((c, RET_DK), lambda s: (chunk_fn(s), 0)),
        ]

    fwd = lambda s: s
    bwd = lambda s: n_x - 1 - s
    return pl.pallas_call(
        _ret_kernel,
        grid=(n_x,),
        in_specs=[smem, *specs(fwd), *specs(bwd), pl.BlockSpec(st_shape, lambda s: (0, 0, 0))],
        out_specs=[
            pl.BlockSpec((c, RET_WIDTH), lambda s: (fwd(s), 0)),
            pl.BlockSpec((c, RET_WIDTH), lambda s: (bwd(s), 0)),
        ],
        out_shape=[jax.ShapeDtypeStruct((n, RET_WIDTH), F32)] * 2,
        scratch_shapes=[
            pltpu.VMEM(st_shape, F32),
            pltpu.VMEM((2 * RET_HEADS, c, c), F32),
            pltpu.VMEM((2 * RET_HEADS, c, RET_DK), F32),
            pltpu.VMEM((2 * RET_HEADS, c, RET_DK), F32),
            pltpu.VMEM((2 * RET_HEADS, 8, RET_DK), F32),
        ],
        compiler_params=_cparams(1, 32),
        name="ret",
    )(ret_lg, p, p, p, cs_tab, sn_tab, p, p, p, cs_tab, sn_tab, st0)


def _rms(y):
    return y * lax.rsqrt(jnp.mean(y * y, axis=-1, keepdims=True) + RMS_EPS)


def _outproj_kernel(of_ref, ob_ref, rf_ref, rb_ref, hgate_ref, rgate_ref, x_ref, wout_ref,
                    hgw_ref, rnw_ref, nw_ref, g1_ref, sh2_ref, sc2_ref, wr_ref, br_ref,
                    x1_ref, h2_ref, lg_ref, cat_scr):
    tm = x_ref.shape[0]
    n_sub = 1
    for sub in range(n_sub):
        rows = slice(sub * tm // n_sub, (sub + 1) * tm // n_sub)
        for h in range(HG_HEADS):
            sl = slice(h * HG_DK, (h + 1) * HG_DK)
            o = of_ref[rows, sl] + ob_ref[rows, sl]
            cat_scr[rows, sl] = (_rms(o) * hgw_ref[0:1, sl]
                                 * hgate_ref[rows, sl].astype(F32)).astype(BF16)
        for h in range(RET_HEADS):
            sl = slice(h * RET_DV, (h + 1) * RET_DV)
            r = rf_ref[rows, sl] + rb_ref[rows, sl]
            cat_scr[rows, HG_WIDTH + h * RET_DV:HG_WIDTH + (h + 1) * RET_DV] = (
                _rms(r) * rnw_ref[0:1, sl] * rgate_ref[rows, sl].astype(F32)).astype(BF16)

        y = _dot(cat_scr[rows, :], wout_ref[...])
        x1 = x_ref[rows, :] + g1_ref[0:1, :] * (_rms(y) * nw_ref[1:2, :])
        x1_ref[rows, :] = x1
        h2 = _rms(x1) * nw_ref[2:3, :] * (1.0 + sc2_ref[0:1, :]) + sh2_ref[0:1, :]
        h2_ref[rows, :] = _pack_pairs(h2)
        hi = h2.astype(BF16)
        lo = (h2 - hi.astype(F32)).astype(BF16)
        parts = _dot(jnp.concatenate([hi, lo], axis=0), wr_ref[...])
        half = hi.shape[0]
        lg_ref[rows, :] = (parts[:half, :ROUTER_LANES] + parts[half:, :ROUTER_LANES]
                           + parts[:half, ROUTER_LANES:] + br_ref[...])


def _out_projection(o_f, o_b, r_f, r_b, p, x2d, w_out_bf, hg_norm_w, ret_norm_w, norm_w, mod,
                    wr_parts, br):
    n, d = x2d.shape
    tm = 256
    row = lambda w: pl.BlockSpec((tm, w), lambda i: (i, 0))
    full = lambda a: pl.BlockSpec(a.shape, lambda i: (0,) * a.ndim)
    modcol = lambda k: pl.BlockSpec((8, d), lambda i: (0, k))
    return pl.pallas_call(
        _outproj_kernel,
        grid=(n // tm,),
        in_specs=[
            row(HG_WIDTH), row(HG_WIDTH), row(RET_WIDTH), row(RET_WIDTH),
            pl.BlockSpec((tm, HG_WIDTH), lambda i: (i, COL_HGATE)),
            pl.BlockSpec((tm, RET_WIDTH), lambda i: (i, COL_RGATE)),
            row(d), full(w_out_bf), full(hg_norm_w), full(ret_norm_w), full(norm_w),
            modcol(2), modcol(3), modcol(4),
            full(wr_parts), full(br),
        ],
        out_specs=[row(d), row(d // 2), row(ROUTER_LANES)],
        out_shape=[
            jax.ShapeDtypeStruct((n, d), F32),
            jax.ShapeDtypeStruct((n, d // 2), jnp.uint32),
            jax.ShapeDtypeStruct((n, ROUTER_LANES), F32),
        ],
        scratch_shapes=[pltpu.VMEM((tm, d), BF16)],
        compiler_params=_cparams(1, 48),
        name="outproj",
    )(o_f, o_b, r_f, r_b, p, p, x2d, w_out_bf, hg_norm_w, ret_norm_w, norm_w, mod, mod, mod,
      wr_parts, br)


ROUTER_TM = 1024
LANES = 128


def _lane_first(hit, lane_f):
    return jnp.min(jnp.where(hit, lane_f, 1e9), axis=1, keepdims=True)


def _router_kernel(lg_ref, gates_ref, s0_ref, s1_ref, cnt_ref, ltri, carry, pstart):
    ph = pl.program_id(0)
    i = pl.program_id(1)
    tm = lg_ref.shape[0]

    @pl.when(jnp.logical_and(ph == 0, i == 0))
    def _():
        t = lax.broadcasted_iota(jnp.int32, (tm, tm), 0)
        s = lax.broadcasted_iota(jnp.int32, (tm, tm), 1)
        ltri[...] = jnp.where(s < t, 1.0, 0.0).astype(BF16)

    @pl.when(jnp.logical_and(ph == 1, i == 0))
    def _():
        cnt = carry[...]
        padded = jnp.floor((cnt + (MOE_PAD - 1.0)) * (1.0 / MOE_PAD)) * MOE_PAD
        r = lax.broadcasted_iota(jnp.int32, (LANES, LANES), 0)
        c = lax.broadcasted_iota(jnp.int32, (LANES, LANES), 1)
        before = jnp.where(r < c, 1.0, 0.0).astype(BF16)
        pstart[...] = _dot(padded.astype(BF16), before)
        cnt_ref[...] = cnt.astype(jnp.int32)

    @pl.when(i == 0)
    def _():
        carry[...] = jnp.zeros_like(carry)

    lg = lg_ref[...]
    lane = lax.broadcasted_iota(jnp.int32, lg.shape, 1)
    lane_f = lane.astype(F32)
    neg = -jnp.inf

    gl = jnp.where(lane < N_GROUPS, lg, neg)
    gmax = jnp.max(gl, axis=1, keepdims=True)
    grp = _lane_first(gl == gmax, lane_f).astype(jnp.int32)
    p_sel = 1.0 / jnp.sum(jnp.exp(gl - gmax), axis=1, keepdims=True)

    in_grp = jnp.logical_and(
        jnp.logical_and(lane >= N_GROUPS, lane < N_GROUPS + N_EXPERTS),
        ((lane - N_GROUPS) >> 3) == grp)
    el = jnp.where(in_grp, lg, neg)
    v1 = jnp.max(el, axis=1, keepdims=True)
    o1 = lane_f == _lane_first(el == v1, lane_f)
    el2 = jnp.where(o1, neg, el)
    v2 = jnp.max(el2, axis=1, keepdims=True)
    o2 = lane_f == _lane_first(el2 == v2, lane_f)

    osum = (jnp.where(o1, 1.0, 0.0) + jnp.where(o2, 1.0, 0.0)).astype(BF16)
    earlier = _dot(ltri[...], osum) + carry[0:1, :]
    carry[...] = carry[...] + _dot(jnp.ones((8, tm), BF16), osum)

    @pl.when(ph == 1)
    def _():
        ez = jnp.exp(v2 - v1)
        g1 = p_sel / (1.0 + ez)
        gates_ref[...] = jnp.concatenate([g1, g1 * ez], axis=1)
        base = earlier + pstart[0:1, :]
        for o, s_ref in ((o1, s0_ref), (o2, s1_ref)):
            col = jnp.sum(jnp.where(o, base, 0.0), axis=1, keepdims=True)
            wide = jnp.broadcast_to(col, (tm, LANES))
            for b in range(tm // LANES):
                s_ref[b:b + 1, :] = wide[b * LANES:(b + 1) * LANES, :].T[0:1, :].astype(jnp.int32)


def _router(logits):
    t = logits.shape[0]
    tm = ROUTER_TM
    rows = tm // LANES
    return pl.pallas_call(
        _router_kernel,
        grid=(2, t // tm),
        in_specs=[pl.BlockSpec((tm, LANES), lambda ph, i: (i, 0))],
        out_specs=[
            pl.BlockSpec((tm, EXPERT_TOPK), lambda ph, i: (i * ph, 0)),
            pl.BlockSpec((rows, LANES), lambda ph, i: (i * ph, 0)),
            pl.BlockSpec((rows, LANES), lambda ph, i: (i * ph, 0)),
            pl.BlockSpec((8, LANES), lambda ph, i: (0, 0)),
        ],
        out_shape=[
            jax.ShapeDtypeStruct((t, EXPERT_TOPK), F32),
            jax.ShapeDtypeStruct((t // LANES, LANES), jnp.int32),
            jax.ShapeDtypeStruct((t // LANES, LANES), jnp.int32),
            jax.ShapeDtypeStruct((8, LANES), jnp.int32),
        ],
        scratch_shapes=[
            pltpu.VMEM((tm, tm), BF16),
            pltpu.VMEM((8, LANES), F32),
            pltpu.VMEM((8, LANES), F32),
        ],
        compiler_params=_cparams(2, 32),
        name="router",
    )(logits)


def _round_up_pow2(x, m):
    assert m & (m - 1) == 0
    return (x + (m - 1)) & ~(m - 1)


def _plan_kernel(cnt_ref, s0_ref, s1_ref, tok_ref, ie_ref, ist_ref, ir_ref, if_ref):
    n_tok = s0_ref.shape[0]
    n_slots = tok_ref.shape[0]
    n_items = ie_ref.shape[0]
    r = MOE_ITEM_ROWS

    def scatter(t16, carry):
        for d in range(16):
            t = t16 * 16 + d
            tok_ref[s0_ref[t]] = t
            tok_ref[s1_ref[t]] = t
        return carry
    lax.fori_loop(0, n_tok // 16, scatter, 0)

    a = jnp.int32(0)
    used = jnp.int32(0)
    e_last = jnp.int32(0)
    def pad_slot(i, carry):
        tok_ref[i] = 0
        return carry

    for e in range(N_EXPERTS):
        cnt = cnt_ref[0, N_GROUPS + e]
        pad = _round_up_pow2(cnt, MOE_PAD)
        lax.fori_loop(used + cnt, used + pad, pad_slot, 0)

        def item(sub, a, e=e, pad=pad, used=used):
            ie_ref[a] = e
            ist_ref[a] = used + sub * r
            ir_ref[a] = jnp.minimum(pad - sub * r, r)
            if_ref[a] = 0
            return a + 1
        a = lax.fori_loop(0, _round_up_pow2(pad, r) >> (r.bit_length() - 1), item, a)
        e_last = jnp.where(pad > 0, e, e_last)
        used = used + pad

    lax.fori_loop(used, n_slots, pad_slot, 0)

    def idle(a2, carry):
        fill_start = used + (a2 - a) * r
        fill = jnp.clip(n_slots - fill_start, 0, r)
        ie_ref[a2] = e_last
        ist_ref[a2] = jnp.where(fill > 0, fill_start, 0)
        ir_ref[a2] = 0
        if_ref[a2] = fill
        return carry
    lax.fori_loop(a, n_items, idle, 0)


def _plan(counts, s0, s1, n_slots, n_items):
    smem = pl.BlockSpec(memory_space=pltpu.SMEM)
    item = jax.ShapeDtypeStruct((n_items,), jnp.int32)
    return pl.pallas_call(
        _plan_kernel,
        in_specs=[smem, smem, smem],
        out_specs=[smem] * 5,
        out_shape=[jax.ShapeDtypeStruct((n_slots,), jnp.int32), item, item, item, item],
        name="plan",
    )(counts, s0, s1)


def _unpack_pairs(u):
    lo = lax.bitcast_convert_type(u << 16, F32)
    hi = lax.bitcast_convert_type(u & jnp.uint32(0xFFFF0000), F32)
    return jnp.concatenate([lo, hi], axis=1)


def _pack_pairs(y):
    w = y.shape[1] // 2
    bits = lax.bitcast_convert_type(y.astype(BF16).astype(F32), jnp.uint32)
    return (bits[:, :w] >> 16) | (bits[:, w:] & jnp.uint32(0xFFFF0000))


def _moe_kernel(item_e, item_start, item_rows, item_fill, slot_tok,
                h2p_hbm, wg_ref, wu_ref, wd_ref, yb_hbm,
                xbuf, xs_bf, act, ystage, pf, gsem, osem):
    a = pl.program_id(0)
    t = pl.program_id(1)
    n_items = pl.num_programs(0)
    n = item_rows[a]
    start = item_start[a]
    rb_rows = MOE_PAD
    n_rb = n // rb_rows
    half_w = wg_ref.shape[2]
    cur = a % 2
    nxt = 1 - cur
    a1 = jnp.minimum(a + 1, n_items - 1)
    n_next = jnp.where(a + 1 < n_items, item_rows[a1], 0)
    start_next = item_start[a1]

    def row_copy(tok, slot, i):
        return pltpu.make_async_copy(h2p_hbm.at[pl.ds(tok, 1), :], xbuf.at[slot, pl.ds(i, 1), :],
                                     gsem.at[slot])

    @pl.when(jnp.logical_and(a == 0, t == 0))
    def _():
        pf[1] = 0
        pf[2] = 0

        def issue(i, carry):
            row_copy(slot_tok[start + i], 0, i).start()
            return carry
        lax.fori_loop(0, n, issue, 0)

    @pl.when(t == 0)
    def _():
        pf[0] = 0

        def drain(bk, carry):
            r0 = pl.multiple_of(bk * rb_rows, rb_rows)
            pltpu.make_async_copy(h2p_hbm.at[pl.ds(0, rb_rows), :],
                                  xbuf.at[cur, pl.ds(r0, rb_rows), :], gsem.at[cur]).wait()
            return carry
        lax.fori_loop(0, n_rb, drain, 0)

    def prefetch_group():
        base = pf[0]
        for q in range(MOE_GATHER_GROUP):
            row_copy(slot_tok[start_next + base + q], nxt, base + q).start()
        pf[0] = base + MOE_GATHER_GROUP

    def run_blocks(compute, before=None):
        def body(rb, carry):
            if before is not None:
                before(rb)
            more = pf[0] < n_next

            @pl.when(more)
            def _():
                compute(rb, prefetch_group)

            @pl.when(jnp.logical_not(more))
            def _():
                compute(rb, lambda: None)
            return carry
        lax.fori_loop(0, n_rb, body, 0)

    def gate_up(half):
        def compute(rb, mid):
            r0 = pl.multiple_of(rb * rb_rows, rb_rows)
            if half == 0:
                xs = _unpack_pairs(xbuf[cur, pl.ds(r0, rb_rows), :]).astype(BF16)
                xs_bf[pl.ds(r0, rb_rows), :] = xs
            else:
                xs = xs_bf[pl.ds(r0, rb_rows), :]
            g = _dot(xs, wg_ref[0].astype(BF16))
            u = _dot(xs, wu_ref[0].astype(BF16))
            mid()
            act[pl.ds(r0, rb_rows), half * half_w:(half + 1) * half_w] = (
                g * _sigmoid(g) * u).astype(BF16)
        run_blocks(compute)

    def out_copy(slot, r0):
        return pltpu.make_async_copy(
            ystage.at[slot],
            yb_hbm.at[pl.ds(pl.multiple_of(start + r0, rb_rows), rb_rows), :],
            osem.at[slot])

    def wait_stage(slot):
        @pl.when(pf[1 + slot] == 1)
        def _():
            out_copy(slot, 0).wait()
            pf[1 + slot] = 0

    def down():
        def compute(rb, mid):
            r0 = pl.multiple_of(rb * rb_rows, rb_rows)
            slot = rb % 2
            y = _dot(act[pl.ds(r0, rb_rows), :], wd_ref[0].astype(BF16))
            mid()
            ystage[slot] = _pack_pairs(y)
            out_copy(slot, r0).start()

        def free_stage(rb):
            wait_stage(rb % 2)
            pf[1 + rb % 2] = 1
        run_blocks(compute, free_stage)

        def tail(i, carry):
            row_copy(slot_tok[start_next + i], nxt, i).start()
            return carry
        lax.fori_loop(pf[0], n_next, tail, 0)

    pl.when(t == 0)(functools.partial(gate_up, 0))
    pl.when(t == 1)(functools.partial(gate_up, 1))
    pl.when(t == 2)(down)

    fill = item_fill[a]

    @pl.when(jnp.logical_and(t == 0, fill > 0))
    def _():
        wait_stage(0)
        ystage[0] = jnp.zeros(ystage.shape[1:], ystage.dtype)

        def zero_block(bk, carry):
            cp = out_copy(0, bk * rb_rows)
            cp.start()
            cp.wait()
            return carry
        lax.fori_loop(0, fill // rb_rows, zero_block, 0)

    @pl.when(jnp.logical_and(a == n_items - 1, t == pl.num_programs(1) - 1))
    def _():
        wait_stage(0)
        wait_stage(1)


def _moe_ffn(h2p, w_gate, w_up, w_down, item_e, item_start, item_rows, item_fill, slot_tok,
             n_slots):
    pw = h2p.shape[1]
    d = 2 * pw
    n_items = item_e.shape[0]
    half_w = D_EXPERT // 2

    def wgu_map(a, t, ie, ist, ir, fl, st):
        return (ie[a], 0, jnp.where(ir[a] > 0, jnp.minimum(t, 1), 1))

    def wd_map(a, t, ie, ist, ir, fl, st):
        return (jnp.where(t >= 2, ie[a], ie[jnp.maximum(a - 1, 0)]), 0, 0)

    grid_spec = pltpu.PrefetchScalarGridSpec(
        num_scalar_prefetch=5,
        grid=(n_items, 3),
        in_specs=[
            pl.BlockSpec(memory_space=pl.ANY),
            pl.BlockSpec((1, d, half_w), wgu_map),
            pl.BlockSpec((1, d, half_w), wgu_map),
            pl.BlockSpec((1, D_EXPERT, d), wd_map),
        ],
        out_specs=pl.BlockSpec(memory_space=pl.ANY),
        scratch_shapes=[
            pltpu.VMEM((2, MOE_ITEM_ROWS, pw), jnp.uint32),
            pltpu.VMEM((MOE_ITEM_ROWS, d), BF16),
            pltpu.VMEM((MOE_ITEM_ROWS, D_EXPERT), BF16),
            pltpu.VMEM((2, MOE_PAD, pw), jnp.uint32),
            pltpu.SMEM((3,), jnp.int32),
            pltpu.SemaphoreType.DMA((2,)),
            pltpu.SemaphoreType.DMA((2,)),
        ],
    )
    return pl.pallas_call(
        _moe_kernel,
        grid_spec=grid_spec,
        out_shape=jax.ShapeDtypeStruct((n_slots, pw), jnp.uint32),
        compiler_params=_cparams(2, 56),
        name="moe",
    )(item_e, item_start, item_rows, item_fill, slot_tok, h2p, w_gate, w_up, w_down)


def _combine_kernel(s0, s1, yb_hbm, gates_ref, x1_ref, nw_ref, g2_ref, out_ref, rows, sem):
    i = pl.program_id(0)
    n_tiles = pl.num_programs(0)
    tm = x1_ref.shape[0]

    def gather(tile, buf):
        def issue(r8, carry):
            t0 = tile * tm + r8 * 8
            for dr in range(8):
                for k, s_k in enumerate((s0, s1)):
                    slot = s_k[t0 + dr]
                    pltpu.make_async_copy(yb_hbm.at[pl.ds(slot, 1), :],
                                          rows.at[buf, k, pl.ds(r8 * 8 + dr, 1), :],
                                          sem.at[buf]).start()
            return carry
        lax.fori_loop(0, tm // 8, issue, 0)

    @pl.when(i == 0)
    def _():
        gather(0, 0)

    @pl.when(i + 1 < n_tiles)
    def _():
        gather(i + 1, (i + 1) % 2)

    buf = i % 2
    for k in range(EXPERT_TOPK):
        pltpu.make_async_copy(yb_hbm.at[pl.ds(0, tm), :], rows.at[buf, k], sem.at[buf]).wait()

    g = gates_ref[...]
    y = (_unpack_pairs(rows[buf, 0]) * g[:, 0:1] + _unpack_pairs(rows[buf, 1]) * g[:, 1:2])
    out_ref[...] = x1_ref[...] + g2_ref[0:1, :] * (_rms(y) * nw_ref[3:4, :])


def _combine(s0, s1, yb, gates, x1, norm_w, mod):
    n, d = x1.shape
    tm = 256
    grid_spec = pltpu.PrefetchScalarGridSpec(
        num_scalar_prefetch=2,
        grid=(n // tm,),
        in_specs=[
            pl.BlockSpec(memory_space=pl.ANY),
            pl.BlockSpec((tm, EXPERT_TOPK), lambda i, *_: (i, 0)),
            pl.BlockSpec((tm, d), lambda i, *_: (i, 0)),
            pl.BlockSpec((4, d), lambda i, *_: (0, 0)),
            pl.BlockSpec((8, d), lambda i, *_: (0, 5)),
        ],
        out_specs=pl.BlockSpec((tm, d), lambda i, *_: (i, 0)),
        scratch_shapes=[
            pltpu.VMEM((2, EXPERT_TOPK, tm, d // 2), jnp.uint32),
            pltpu.SemaphoreType.DMA((2,)),
        ],
    )
    return pl.pallas_call(
        _combine_kernel,
        grid_spec=grid_spec,
        out_shape=jax.ShapeDtypeStruct((n, d), F32),
        compiler_params=_cparams(1, 32),
        name="combine",
    )(s0, s1, yb, gates, x1, norm_w, mod)


def _rope_tables(n):
    rows = n // GRID_W
    r = jnp.broadcast_to(jnp.arange(rows)[:, None], (rows, GRID_W)).reshape(-1).astype(F32)
    cidx = jnp.broadcast_to(jnp.arange(GRID_W)[None, :], (rows, GRID_W)).reshape(-1).astype(F32)
    n_freq = RET_DK // 4
    inv = ROPE_BASE ** (-jnp.arange(n_freq, dtype=F32) / n_freq)
    ang = jnp.concatenate([r[:, None] * inv, cidx[:, None] * inv], axis=-1)
    cos, sin = jnp.cos(ang), jnp.sin(ang)
    cs = jnp.concatenate([cos, cos], axis=-1)
    sn = jnp.concatenate([-sin, sin], axis=-1)
    return cs, sn


def kernel(x, c, ctx, c_ctx, w_mod, b_mod, norm_w, w_in, hg_lb_logits, hg_norm_w, ret_decay_logits,
           ret_norm_w, w_out, router_group_w, router_group_b, router_expert_w, router_expert_b,
           w_gate, w_up, w_down):
    bsz, n, d = x.shape
    l = ctx.shape[1]
    assert bsz == 1 and w_mod.shape[0] == 1, "single sample, single layer"
    x2d, ctx2d = x[0], ctx[0]

    cc = jnp.zeros((8, d), F32).at[0].set(c[0]).at[1].set(c_ctx)
    mod = _modulation(cc, w_mod[0], b_mod[0])

    lb = jnp.cumsum(jax.nn.softmax(hg_lb_logits.astype(F32), axis=0), axis=0)[0]
    rk0 = 5 * HG_WIDTH + RET_HEADS * RET_DK
    cols = jnp.stack([
        jnp.zeros((PROJ_WIDTH,), F32).at[HG_WIDTH:3 * HG_WIDTH].set(lb.reshape(-1)),
        jnp.ones((PROJ_WIDTH,), F32).at[rk0:rk0 + RET_HEADS * RET_DK].set(RET_DK ** -0.5),
    ])
    w_in_bf = w_in[0].astype(BF16)
    p, g = _projection(x2d, norm_w[0], mod, w_in_bf, cols, 0, 1024, "proj_x")
    pc, gc = _projection(ctx2d, norm_w[0], mod, w_in_bf, cols, 1, l, "proj_ctx")

    ret_lg = jax.nn.log_sigmoid(ret_decay_logits[0].astype(F32))
    cs_tab, sn_tab = _rope_tables(n)

    o_f, o_b = _hgrn_scan(p, g, pc, gc)
    r_f, r_b = _ret_scan(p, pc, ret_lg, cs_tab, sn_tab)

    wr = jnp.concatenate([router_group_w[0], router_expert_w[0]], axis=1)
    wr = jnp.pad(wr, ((0, 0), (0, ROUTER_LANES - wr.shape[1])))
    wr_hi = wr.astype(BF16)
    wr_parts = jnp.concatenate([wr_hi, (wr - wr_hi.astype(F32)).astype(BF16)], axis=1)
    br = jnp.concatenate([router_group_b[0], router_expert_b[0]]).astype(F32)
    br = jnp.pad(br, (0, ROUTER_LANES - br.shape[0])).reshape(1, ROUTER_LANES)
    x1, h2, logits = _out_projection(
        o_f, o_b, r_f, r_b, p, x2d, w_out[0].astype(BF16), hg_norm_w, ret_norm_w, norm_w[0], mod,
        wr_parts, br)

    n_assign = n * EXPERT_TOPK
    n_slots = n_assign + N_EXPERTS * MOE_PAD
    n_items = N_EXPERTS + n_slots // MOE_ITEM_ROWS
    gates, s0, s1, counts = _router(logits)
    s0, s1 = s0.reshape(-1), s1.reshape(-1)
    slot_tok, item_e, item_start, item_rows, item_fill = _plan(counts, s0, s1, n_slots, n_items)

    yb = _moe_ffn(h2, w_gate[0], w_up[0], w_down[0], item_e, item_start, item_rows, item_fill,
                  slot_tok, n_slots)
    out = _combine(s0, s1, yb, gates, x1, norm_w[0], mod)
    return out[None]
```

```python
import functools

import jax
import jax.numpy as jnp
import numpy as np
from jax import lax
from jax.experimental import pallas as pl
from jax.experimental.pallas import tpu as pltpu

F32 = jnp.float32
BF16 = jnp.bfloat16

D_MODEL = 2048
GRID_W = 64
HG_HEADS = 8
HG_DK = 128
HG_WIDTH = 1024
RET_HEADS = 4
RET_DK = 128
RET_DV = 256
RET_WIDTH = 1024
PROJ_WIDTH = 8192
N_GROUPS = 4
EXPERTS_PER_GROUP = 8
N_EXPERTS = 32
EXPERT_TOPK = 2
D_EXPERT = 1024
RMS_EPS = 1e-6
ROPE_BASE = 10000.0

COL_HQ, COL_HF_FWD, COL_HF_BWD, COL_HI, COL_HGATE = 0, 1, 2, 3, 4
COL_RQ_512, COL_RK_512 = 10, 11
COL_RV, COL_RGATE = 6, 7

HG_CHUNK = 128
HG_DIAG = 16
HG_LOG_CLAMP = 10.0
RET_CHUNK = 256

MOE_PAD = 256
MOE_ITEM_ROWS = 1024
MOE_GATHER_GROUP = 128
ROUTER_LANES = 128


def _cparams(n_axes, vmem_mb):
    return pltpu.CompilerParams(
        dimension_semantics=("arbitrary",) * n_axes,
        vmem_limit_bytes=vmem_mb * 1024 * 1024,
    )


def _sigmoid(x):
    return 1.0 / (1.0 + jnp.exp(-x))


def _dot(a, b):
    return jnp.dot(a, b, preferred_element_type=F32)


def _dot_nt(a, b):
    return lax.dot_general(a, b, (((1,), (1,)), ((), ())), preferred_element_type=F32)


def _dot_tn(a, b):
    return lax.dot_general(a, b, (((0,), (0,)), ((), ())), preferred_element_type=F32)


def _mod_kernel(cc_ref, w_ref, b_ref, o_ref):
    s = cc_ref[...]
    s = s * _sigmoid(s)
    s16 = jnp.concatenate([s, s], axis=0)
    hi = s16.astype(BF16).astype(F32)
    row = lax.broadcasted_iota(jnp.int32, s16.shape, 0)
    lhs = jnp.where(row < 8, hi, s16 - hi).astype(BF16)
    r = _dot(lhs, w_ref[...].astype(BF16))
    o_ref[...] = r[:8] + r[8:] + b_ref[...]


def _modulation(cc, w_mod, b_mod):
    d, n = w_mod.shape
    tn = 1024
    return pl.pallas_call(
        _mod_kernel,
        grid=(n // tn,),
        in_specs=[
            pl.BlockSpec((8, d), lambda j: (0, 0)),
            pl.BlockSpec((d, tn), lambda j: (0, j)),
            pl.BlockSpec((1, tn), lambda j: (0, j)),
        ],
        out_specs=pl.BlockSpec((8, tn), lambda j: (0, j)),
        out_shape=jax.ShapeDtypeStruct((8, n), F32),
        compiler_params=_cparams(1, 40),
        name="mod",
    )(cc, w_mod, b_mod.reshape(1, n))


PROJ_TN = 1024
PROLOGUE_ROWS = 32


def _proj_kernel(x_ref, nw_ref, sh_ref, sc_ref, w_ref, cols_ref, o_ref, g_ref, h_scr, *, mod_row):
    j = pl.program_id(1)

    @pl.when(j == 0)
    def _():
        scale = nw_ref[0:1, :] * (1.0 + sc_ref[mod_row:mod_row + 1, :])
        shift = sh_ref[mod_row:mod_row + 1, :]

        def body(r, carry):
            r0 = pl.multiple_of(r * PROLOGUE_ROWS, PROLOGUE_ROWS)
            x = x_ref[pl.ds(r0, PROLOGUE_ROWS), :]
            y = x * lax.rsqrt(jnp.mean(x * x, axis=-1, keepdims=True) + RMS_EPS)
            h_scr[pl.ds(r0, PROLOGUE_ROWS), :] = (y * scale + shift).astype(BF16)
            return carry
        lax.fori_loop(0, x_ref.shape[0] // PROLOGUE_ROWS, body, 0)

    is_gate = jnp.logical_or(j == 1, j == 2)
    is_silu = jnp.logical_or(j == 0, jnp.logical_or(j == 4, j == 7))
    is_lin = jnp.logical_not(jnp.logical_or(is_gate, is_silu))

    @pl.when(is_lin)
    def _():
        o_ref[...] = (_dot(h_scr[...], w_ref[...]) * cols_ref[1:2, :]).astype(BF16)

    @pl.when(is_silu)
    def _():
        acc = _dot(h_scr[...], w_ref[...])
        o_ref[...] = (acc * _sigmoid(acc)).astype(BF16)

    @pl.when(is_gate)
    def _():
        acc = _dot(h_scr[...], w_ref[...])
        lb = cols_ref[0:1, :]
        g = jnp.maximum(jnp.log(lb + (1.0 - lb) * _sigmoid(acc)), -HG_LOG_CLAMP)
        g_ref[...] = g
        o_ref[...] = g.astype(BF16)


def _projection(x2d, norm_w, mod, w_in, cols, mod_row, tm, name):
    n, d = x2d.shape
    tn = PROJ_TN
    return pl.pallas_call(
        functools.partial(_proj_kernel, mod_row=mod_row),
        grid=(n // tm, PROJ_WIDTH // tn),
        in_specs=[
            pl.BlockSpec((tm, d), lambda i, j: (i, 0)),
            pl.BlockSpec((4, d), lambda i, j: (0, 0)),
            pl.BlockSpec((8, d), lambda i, j: (0, 0)),
            pl.BlockSpec((8, d), lambda i, j: (0, 1)),
            pl.BlockSpec((d, tn), lambda i, j: (0, j)),
            pl.BlockSpec((2, tn), lambda i, j: (0, j)),
        ],
        out_specs=[
            pl.BlockSpec((tm, tn), lambda i, j: (i, j)),
            pl.BlockSpec((tm, tn), lambda i, j: (i, jnp.clip(j - 1, 0, 1))),
        ],
        out_shape=[
            jax.ShapeDtypeStruct((n, PROJ_WIDTH), BF16),
            jax.ShapeDtypeStruct((n, 2 * HG_WIDTH), F32),
        ],
        scratch_shapes=[pltpu.VMEM((tm, d), BF16)],
        compiler_params=_cparams(2, 52),
        name=name,
    )(x2d, norm_w, mod, mod, w_in, cols)


def _tri_ones(c, rev):
    t = lax.broadcasted_iota(jnp.int32, (c, c), 0)
    s = lax.broadcasted_iota(jnp.int32, (c, c), 1)
    return jnp.where((s >= t) if rev else (s <= t), 1.0, 0.0).astype(BF16)


def _cumsum_rows(g, tri):
    hi = g.astype(BF16)
    lo = (g - hi.astype(F32)).astype(BF16)
    return _dot(tri, hi) + _dot(tri, lo)


def _row_refs(b, idxs, blk):
    parts = [jnp.broadcast_to(b[i:i + 1, :], (blk, b.shape[1])) for i in idxs]
    return parts[0] if len(parts) == 1 else jnp.concatenate(parts, axis=0)


def _hgrn_levels(c, rev):
    t = lax.broadcasted_iota(jnp.int32, (c, c), 0)
    s = lax.broadcasted_iota(jnp.int32, (c, c), 1)
    levels = []
    m = c // 2
    while m >= HG_DIAG:
        blk = 2 * m
        same = ((t ^ s) & ~(blk - 1)) == 0
        if rev:
            idxs = [b0 + m for b0 in range(0, c, blk)]
            cross = jnp.logical_and((t & m) == 0, (s & m) != 0)
        else:
            idxs = [b0 + m - 1 for b0 in range(0, c, blk)]
            cross = jnp.logical_and((t & m) != 0, (s & m) == 0)
        levels.append((idxs, blk, jnp.logical_and(same, cross)))
        m //= 2
    blk = HG_DIAG
    same = ((t ^ s) & ~(blk - 1)) == 0
    if rev:
        idxs = [b0 + blk // 2 for b0 in range(0, c, blk)]
        mask = jnp.logical_and(same, s >= t)
    else:
        idxs = [b0 + blk // 2 - 1 for b0 in range(0, c, blk)]
        mask = jnp.logical_and(same, s <= t)
    levels.append((idxs, blk, mask))
    return levels


def _hgrn_head(q_bf, g, b, v_bf, st, rev, levels):
    c = g.shape[0]
    kk = 1.0 - jnp.exp(g)

    o = None
    if levels is not None:
        q = q_bf.astype(F32)
        att = jnp.zeros((c, c), F32)
        for idxs, blk, mask in levels:
            ref = _row_refs(b, idxs, blk)
            a_l = _dot_nt((q * jnp.exp(b - ref)).astype(BF16),
                          (kk * jnp.exp(ref - b)).astype(BF16))
            att = att + jnp.where(mask, a_l, 0.0)
        o = _dot(att.astype(BF16), v_bf) + _dot_nt((q * jnp.exp(b)).astype(BF16), st.astype(BF16))

    b_end = b[0:1, :] if rev else b[c - 1:c, :]
    k_end = (kk * jnp.exp(b_end - b)).astype(BF16)
    st_new = st * jnp.exp(b_end) + _dot_tn(v_bf, k_end)
    return o, st_new


def _hgrn_ctx_kernel(gf_ref, vf_ref, gb_ref, vb_ref, st_ref):
    @pl.when(pl.program_id(0) == 0)
    def _():
        st_ref[...] = jnp.zeros_like(st_ref)

    c = gf_ref.shape[0]
    bf = _cumsum_rows(gf_ref[...], _tri_ones(c, False))
    bb = _cumsum_rows(gb_ref[...], _tri_ones(c, True))
    for h in range(HG_HEADS):
        sl = slice(h * HG_DK, (h + 1) * HG_DK)
        _, st_ref[h] = _hgrn_head(None, gf_ref[:, sl], bf[:, sl], vf_ref[:, sl], st_ref[h],
                                  False, None)
        hb = HG_HEADS + h
        _, st_ref[hb] = _hgrn_head(None, gb_ref[:, sl], bb[:, sl], vb_ref[:, sl], st_ref[hb],
                                   True, None)


def _hgrn_kernel(qf_ref, gf_ref, vf_ref, qb_ref, gb_ref, vb_ref, st0_ref, of_ref, ob_ref, st_ref):
    @pl.when(pl.program_id(0) == 0)
    def _():
        st_ref[...] = st0_ref[...]

    c = gf_ref.shape[0]
    lev_f = _hgrn_levels(c, False)
    lev_b = _hgrn_levels(c, True)
    bf = _cumsum_rows(gf_ref[...], _tri_ones(c, False))
    bb = _cumsum_rows(gb_ref[...], _tri_ones(c, True))
    for h in range(HG_HEADS):
        sl = slice(h * HG_DK, (h + 1) * HG_DK)
        of_ref[:, sl], st_ref[h] = _hgrn_head(qf_ref[:, sl], gf_ref[:, sl], bf[:, sl],
                                              vf_ref[:, sl], st_ref[h], False, lev_f)
        hb = HG_HEADS + h
        ob_ref[:, sl], st_ref[hb] = _hgrn_head(qb_ref[:, sl], gb_ref[:, sl], bb[:, sl],
                                               vb_ref[:, sl], st_ref[hb], True, lev_b)


def _hgrn_scan(p, g, pc, gc):
    c = HG_CHUNK
    w = HG_WIDTH
    n, l = p.shape[0], pc.shape[0]
    n_x, n_ctx = n // c, l // c
    st_shape = (2 * HG_HEADS, HG_DK, HG_DK)

    def spec(chunk_fn, col):
        return pl.BlockSpec((c, w), lambda s: (chunk_fn(s), col))

    st0 = pl.pallas_call(
        _hgrn_ctx_kernel,
        grid=(n_ctx,),
        in_specs=[
            spec(lambda s: s, 0), spec(lambda s: s, COL_HI),
            spec(lambda s: n_ctx - 1 - s, 1), spec(lambda s: n_ctx - 1 - s, COL_HI),
        ],
        out_specs=pl.BlockSpec(st_shape, lambda s: (0, 0, 0)),
        out_shape=jax.ShapeDtypeStruct(st_shape, F32),
        compiler_params=_cparams(1, 32),
        name="hgrn_ctx",
    )(gc, pc, gc, pc)

    fwd = lambda s: s
    bwd = lambda s: n_x - 1 - s
    return pl.pallas_call(
        _hgrn_kernel,
        grid=(n_x,),
        in_specs=[
            spec(fwd, COL_HQ), spec(fwd, 0), spec(fwd, COL_HI),
            spec(bwd, COL_HQ), spec(bwd, 1), spec(bwd, COL_HI),
            pl.BlockSpec(st_shape, lambda s: (0, 0, 0)),
        ],
        out_specs=[spec(fwd, 0), spec(bwd, 0)],
        out_shape=[jax.ShapeDtypeStruct((n, w), F32)] * 2,
        scratch_shapes=[pltpu.VMEM(st_shape, F32)],
        compiler_params=_cparams(1, 32),
        name="hgrn",
    )(p, g, p, p, g, p, st0)


def _rope(x_bf, cs, sn):
    x = x_bf.astype(F32)
    return x * cs + pltpu.roll(x, RET_DK // 2, axis=1) * sn


def _ret_decays(lg, c, rev):
    t = lax.broadcasted_iota(jnp.int32, (c, c), 0)
    s = lax.broadcasted_iota(jnp.int32, (c, c), 1)
    diff = (s - t) if rev else (t - s)
    pair = jnp.where(diff >= 0, jnp.exp(lg * jnp.maximum(diff, 0).astype(F32)), 0.0)
    pos = lax.broadcasted_iota(jnp.int32, (c, RET_DK), 0).astype(F32)
    q_dec = jnp.exp(lg * ((c - pos) if rev else (pos + 1.0)))
    k_dec = jnp.exp(lg * (pos if rev else (c - 1.0 - pos)))
    chunk = jnp.exp(jnp.full((8, RET_DK), lg * c, F32))
    return pair, q_dec, k_dec, chunk


def _ret_head(q, k, v_bf, decays, st):
    pair, q_dec, k_dec, chunk = decays
    o = None
    if q is not None:
        att = _dot_nt(q.astype(BF16), k.astype(BF16)) * pair
        o = _dot(att.astype(BF16), v_bf) + _dot_nt((q * q_dec).astype(BF16), st.astype(BF16))
    st_new = st * chunk[0:1, :] + _dot_tn(v_bf, (k * k_dec).astype(BF16))
    return o, st_new


def _ret_ctx_kernel(lg_ref, kf_ref, vf_ref, kb_ref, vb_ref, st_ref):
    @pl.when(pl.program_id(0) == 0)
    def _():
        st_ref[...] = jnp.zeros_like(st_ref)

    c = kf_ref.shape[0]
    for h in range(RET_HEADS):
        sk = slice(h * RET_DK, (h + 1) * RET_DK)
        sv = slice(h * RET_DV, (h + 1) * RET_DV)
        _, st_ref[h] = _ret_head(None, kf_ref[:, sk].astype(F32), vf_ref[:, sv],
                                 _ret_decays(lg_ref[0, h], c, False), st_ref[h])
        hb = RET_HEADS + h
        _, st_ref[hb] = _ret_head(None, kb_ref[:, sk].astype(F32), vb_ref[:, sv],
                                  _ret_decays(lg_ref[1, h], c, True), st_ref[hb])


def _ret_kernel(lg_ref, qf_ref, kf_ref, vf_ref, csf_ref, snf_ref,
                qb_ref, kb_ref, vb_ref, csb_ref, snb_ref, st0_ref, of_ref, ob_ref,
                st_ref, pair_ref, qd_ref, kd_ref, cd_ref):
    c = qf_ref.shape[0]

    @pl.when(pl.program_id(0) == 0)
    def _():
        st_ref[...] = st0_ref[...]
        for d in range(2):
            for h in range(RET_HEADS):
                i = d * RET_HEADS + h
                pair_ref[i], qd_ref[i], kd_ref[i], cd_ref[i] = _ret_decays(lg_ref[d, h], c, d == 1)

    csf, snf = csf_ref[...], snf_ref[...]
    csb, snb = csb_ref[...], snb_ref[...]
    for h in range(RET_HEADS):
        sk = slice(h * RET_DK, (h + 1) * RET_DK)
        sv = slice(h * RET_DV, (h + 1) * RET_DV)
        of_ref[:, sv], st_ref[h] = _ret_head(
            _rope(qf_ref[:, sk], csf, snf), _rope(kf_ref[:, sk], csf, snf), vf_ref[:, sv],
            (pair_ref[h], qd_ref[h], kd_ref[h], cd_ref[h]), st_ref[h])
        hb = RET_HEADS + h
        ob_ref[:, sv], st_ref[hb] = _ret_head(
            _rope(qb_ref[:, sk], csb, snb), _rope(kb_ref[:, sk], csb, snb), vb_ref[:, sv],
            (pair_ref[hb], qd_ref[hb], kd_ref[hb], cd_ref[hb]), st_ref[hb])


def _ret_scan(p, pc, ret_lg, cs_tab, sn_tab):
    c = RET_CHUNK
    n, l = p.shape[0], pc.shape[0]
    n_x, n_ctx = n // c, l // c
    qk_w = RET_HEADS * RET_DK
    st_shape = (2 * RET_HEADS, RET_DV, RET_DK)
    smem = pl.BlockSpec(memory_space=pltpu.SMEM)

    def kv_specs(chunk_fn):
        return [
            pl.BlockSpec((c, qk_w), lambda s: (chunk_fn(s), COL_RK_512)),
            pl.BlockSpec((c, RET_WIDTH), lambda s: (chunk_fn(s), COL_RV)),
        ]

    st0 = pl.pallas_call(
        _ret_ctx_kernel,
        grid=(n_ctx,),
        in_specs=[smem, *kv_specs(lambda s: s), *kv_specs(lambda s: n_ctx - 1 - s)],
        out_specs=pl.BlockSpec(st_shape, lambda s: (0, 0, 0)),
        out_shape=jax.ShapeDtypeStruct(st_shape, F32),
        compiler_params=_cparams(1, 32),
        name="ret_ctx",
    )(ret_lg, pc, pc, pc, pc)

    def specs(chunk_fn):
        return [
            pl.BlockSpec((c, qk_w), lambda s: (chunk_fn(s), COL_RQ_512)),
            *kv_specs(chunk_fn),
            pl.BlockSpec((c, RET_DK), lambda s: (chunk_fn(s), 0)),
            pl.BlockSpec((c, RET_DK), lambda s: (chunk_fn(s), 0)),
        ]

    fwd = lambda s: s
    bwd = lambda s: n_x - 1 - s
    return pl.pallas_call(
        _ret_kernel,
        grid=(n_x,),
        in_specs=[smem, *specs(fwd), *specs(bwd), pl.BlockSpec(st_shape, lambda s: (0, 0, 0))],
        out_specs=[
            pl.BlockSpec((c, RET_WIDTH), lambda s: (fwd(s), 0)),
            pl.BlockSpec((c, RET_WIDTH), lambda s: (bwd(s), 0)),
        ],
        out_shape=[jax.ShapeDtypeStruct((n, RET_WIDTH), F32)] * 2,
        scratch_shapes=[
            pltpu.VMEM(st_shape, F32),
            pltpu.VMEM((2 * RET_HEADS, c, c), F32),
            pltpu.VMEM((2 * RET_HEADS, c, RET_DK), F32),
            pltpu.VMEM((2 * RET_HEADS, c, RET_DK), F32),
            pltpu.VMEM((2 * RET_HEADS, 8, RET_DK), F32),
        ],
        compiler_params=_cparams(1, 32),
        name="ret",
    )(ret_lg, p, p, p, cs_tab, sn_tab, p, p, p, cs_tab, sn_tab, st0)


def _rms(y):
    return y * lax.rsqrt(jnp.mean(y * y, axis=-1, keepdims=True) + RMS_EPS)


def _outproj_kernel(of_ref, ob_ref, rf_ref, rb_ref, hgate_ref, rgate_ref, x_ref, wout_ref,
                    hgw_ref, rnw_ref, nw_ref, g1_ref, sh2_ref, sc2_ref, wr_ref, br_ref,
                    x1_ref, h2_ref, lg_ref, cat_scr):
    tm = x_ref.shape[0]
    n_sub = 1
    for sub in range(n_sub):
        rows = slice(sub * tm // n_sub, (sub + 1) * tm // n_sub)
        for h in range(HG_HEADS):
            sl = slice(h * HG_DK, (h + 1) * HG_DK)
            o = of_ref[rows, sl] + ob_ref[rows, sl]
            cat_scr[rows, sl] = (_rms(o) * hgw_ref[0:1, sl]
                                 * hgate_ref[rows, sl].astype(F32)).astype(BF16)
        for h in range(RET_HEADS):
            sl = slice(h * RET_DV, (h + 1) * RET_DV)
            r = rf_ref[rows, sl] + rb_ref[rows, sl]
            cat_scr[rows, HG_WIDTH + h * RET_DV:HG_WIDTH + (h + 1) * RET_DV] = (
                _rms(r) * rnw_ref[0:1, sl] * rgate_ref[rows, sl].astype(F32)).astype(BF16)

        y = _dot(cat_scr[rows, :], wout_ref[...])
        x1 = x_ref[rows, :] + g1_ref[0:1, :] * (_rms(y) * nw_ref[1:2, :])
        x1_ref[rows, :] = x1
        h2 = _rms(x1) * nw_ref[2:3, :] * (1.0 + sc2_ref[0:1, :]) + sh2_ref[0:1, :]
        h2_ref[rows, :] = _pack_pairs(h2)
        hi = h2.astype(BF16)
        lo = (h2 - hi.astype(F32)).astype(BF16)
        parts = _dot(jnp.concatenate([hi, lo], axis=0), wr_ref[...])
        half = hi.shape[0]
        lg_ref[rows, :] = (parts[:half, :ROUTER_LANES] + parts[half:, :ROUTER_LANES]
                           + parts[:half, ROUTER_LANES:] + br_ref[...])


def _out_projection(o_f, o_b, r_f, r_b, p, x2d, w_out_bf, hg_norm_w, ret_norm_w, norm_w, mod,
                    wr_parts, br):
    n, d = x2d.shape
    tm = 256
    row = lambda w: pl.BlockSpec((tm, w), lambda i: (i, 0))
    full = lambda a: pl.BlockSpec(a.shape, lambda i: (0,) * a.ndim)
    modcol = lambda k: pl.BlockSpec((8, d), lambda i: (0, k))
    return pl.pallas_call(
        _outproj_kernel,
        grid=(n // tm,),
        in_specs=[
            row(HG_WIDTH), row(HG_WIDTH), row(RET_WIDTH), row(RET_WIDTH),
            pl.BlockSpec((tm, HG_WIDTH), lambda i: (i, COL_HGATE)),
            pl.BlockSpec((tm, RET_WIDTH), lambda i: (i, COL_RGATE)),
            row(d), full(w_out_bf), full(hg_norm_w), full(ret_norm_w), full(norm_w),
            modcol(2), modcol(3), modcol(4),
            full(wr_parts), full(br),
        ],
        out_specs=[row(d), row(d // 2), row(ROUTER_LANES)],
        out_shape=[
            jax.ShapeDtypeStruct((n, d), F32),
            jax.ShapeDtypeStruct((n, d // 2), jnp.uint32),
            jax.ShapeDtypeStruct((n, ROUTER_LANES), F32),
        ],
        scratch_shapes=[pltpu.VMEM((tm, d), BF16)],
        compiler_params=_cparams(1, 48),
        name="outproj",
    )(o_f, o_b, r_f, r_b, p, p, x2d, w_out_bf, hg_norm_w, ret_norm_w, norm_w, mod, mod, mod,
      wr_parts, br)


ROUTER_TM = 1024
LANES = 128


def _lane_first(hit, lane_f):
    return jnp.min(jnp.where(hit, lane_f, 1e9), axis=1, keepdims=True)


def _router_kernel(lg_ref, gates_ref, s0_ref, s1_ref, cnt_ref, ltri, carry, pstart):
    ph = pl.program_id(0)
    i = pl.program_id(1)
    tm = lg_ref.shape[0]

    @pl.when(jnp.logical_and(ph == 0, i == 0))
    def _():
        t = lax.broadcasted_iota(jnp.int32, (tm, tm), 0)
        s = lax.broadcasted_iota(jnp.int32, (tm, tm), 1)
        ltri[...] = jnp.where(s < t, 1.0, 0.0).astype(BF16)

    @pl.when(jnp.logical_and(ph == 1, i == 0))
    def _():
        cnt = carry[...]
        padded = jnp.floor((cnt + (MOE_PAD - 1.0)) * (1.0 / MOE_PAD)) * MOE_PAD
        r = lax.broadcasted_iota(jnp.int32, (LANES, LANES), 0)
        c = lax.broadcasted_iota(jnp.int32, (LANES, LANES), 1)
        before = jnp.where(r < c, 1.0, 0.0).astype(BF16)
        pstart[...] = _dot(padded.astype(BF16), before)
        cnt_ref[...] = cnt.astype(jnp.int32)

    @pl.when(i == 0)
    def _():
        carry[...] = jnp.zeros_like(carry)

    lg = lg_ref[...]
    lane = lax.broadcasted_iota(jnp.int32, lg.shape, 1)
    lane_f = lane.astype(F32)
    neg = -jnp.inf

    gl = jnp.where(lane < N_GROUPS, lg, neg)
    gmax = jnp.max(gl, axis=1, keepdims=True)
    grp = _lane_first(gl == gmax, lane_f).astype(jnp.int32)
    p_sel = 1.0 / jnp.sum(jnp.exp(gl - gmax), axis=1, keepdims=True)

    in_grp = jnp.logical_and(
        jnp.logical_and(lane >= N_GROUPS, lane < N_GROUPS + N_EXPERTS),
        ((lane - N_GROUPS) >> 3) == grp)
    el = jnp.where(in_grp, lg, neg)
    v1 = jnp.max(el, axis=1, keepdims=True)
    o1 = lane_f == _lane_first(el == v1, lane_f)
    el2 = jnp.where(o1, neg, el)
    v2 = jnp.max(el2, axis=1, keepdims=True)
    o2 = lane_f == _lane_first(el2 == v2, lane_f)

    osum = (jnp.where(o1, 1.0, 0.0) + jnp.where(o2, 1.0, 0.0)).astype(BF16)
    earlier = _dot(ltri[...], osum) + carry[0:1, :]
    carry[...] = carry[...] + _dot(jnp.ones((8, tm), BF16), osum)

    @pl.when(ph == 1)
    def _():
        ez = jnp.exp(v2 - v1)
        g1 = p_sel / (1.0 + ez)
        gates_ref[...] = jnp.concatenate([g1, g1 * ez], axis=1)
        base = earlier + pstart[0:1, :]
        for o, s_ref in ((o1, s0_ref), (o2, s1_ref)):
            col = jnp.sum(jnp.where(o, base, 0.0), axis=1, keepdims=True)
            wide = jnp.broadcast_to(col, (tm, LANES))
            for b in range(tm // LANES):
                s_ref[b:b + 1, :] = wide[b * LANES:(b + 1) * LANES, :].T[0:1, :].astype(jnp.int32)


def _router(logits):
    t = logits.shape[0]
    tm = ROUTER_TM
    rows = tm // LANES
    return pl.pallas_call(
        _router_kernel,
        grid=(2, t // tm),
        in_specs=[pl.BlockSpec((tm, LANES), lambda ph, i: (i, 0))],
        out_specs=[
            pl.BlockSpec((tm, EXPERT_TOPK), lambda ph, i: (i * ph, 0)),
            pl.BlockSpec((rows, LANES), lambda ph, i: (i * ph, 0)),
            pl.BlockSpec((rows, LANES), lambda ph, i: (i * ph, 0)),
            pl.BlockSpec((8, LANES), lambda ph, i: (0, 0)),
        ],
        out_shape=[
            jax.ShapeDtypeStruct((t, EXPERT_TOPK), F32),
            jax.ShapeDtypeStruct((t // LANES, LANES), jnp.int32),
            jax.ShapeDtypeStruct((t // LANES, LANES), jnp.int32),
            jax.ShapeDtypeStruct((8, LANES), jnp.int32),
        ],
        scratch_shapes=[
            pltpu.VMEM((tm, tm), BF16),
            pltpu.VMEM((8, LANES), F32),
            pltpu.VMEM((8, LANES), F32),
        ],
        compiler_params=_cparams(2, 32),
        name="router",
    )(logits)


def _round_up_pow2(x, m):
    assert m & (m - 1) == 0
    return (x + (m - 1)) & ~(m - 1)


def _plan_kernel(cnt_ref, s0_ref, s1_ref, tok_ref, ie_ref, ist_ref, ir_ref, if_ref, zeros, sem):
    n_tok = s0_ref.shape[0]
    n_slots = tok_ref.shape[0]
    n_items = ie_ref.shape[0]
    r = MOE_ITEM_ROWS

    zeros[...] = jnp.zeros_like(zeros)
    clear = pltpu.make_async_copy(zeros, tok_ref, sem)
    clear.start()
    clear.wait()

    def scatter(t16, carry):
        for d in range(16):
            t = t16 * 16 + d
            tok_ref[s0_ref[t]] = t
            tok_ref[s1_ref[t]] = t
        return carry
    lax.fori_loop(0, n_tok // 16, scatter, 0)

    a = jnp.int32(0)
    used = jnp.int32(0)
    e_last = jnp.int32(0)
    for e in range(N_EXPERTS):
        pad = _round_up_pow2(cnt_ref[0, N_GROUPS + e], MOE_PAD)

        def item(sub, a, e=e, pad=pad, used=used):
            ie_ref[a] = e
            ist_ref[a] = used + sub * r
            ir_ref[a] = jnp.minimum(pad - sub * r, r)
            if_ref[a] = 0
            return a + 1
        a = lax.fori_loop(0, _round_up_pow2(pad, r) >> (r.bit_length() - 1), item, a)
        e_last = jnp.where(pad > 0, e, e_last)
        used = used + pad

    def idle(a2, carry):
        fill_start = used + (a2 - a) * r
        fill = jnp.clip(n_slots - fill_start, 0, r)
        ie_ref[a2] = e_last
        ist_ref[a2] = jnp.where(fill > 0, fill_start, 0)
        ir_ref[a2] = 0
        if_ref[a2] = fill
        return carry
    lax.fori_loop(a, n_items, idle, 0)


def _plan(counts, s0, s1, n_slots, n_items):
    smem = pl.BlockSpec(memory_space=pltpu.SMEM)
    item = jax.ShapeDtypeStruct((n_items,), jnp.int32)
    return pl.pallas_call(
        _plan_kernel,
        in_specs=[smem, smem, smem],
        out_specs=[smem] * 5,
        out_shape=[jax.ShapeDtypeStruct((n_slots,), jnp.int32), item, item, item, item],
        scratch_shapes=[pltpu.VMEM((n_slots,), jnp.int32), pltpu.SemaphoreType.DMA(())],
        name="plan",
    )(counts, s0, s1)


def _unpack_pairs(u):
    lo = lax.bitcast_convert_type(u << 16, F32)
    hi = lax.bitcast_convert_type(u & jnp.uint32(0xFFFF0000), F32)
    return jnp.concatenate([lo, hi], axis=1)


def _pack_pairs(y):
    w = y.shape[1] // 2
    bits = lax.bitcast_convert_type(y.astype(BF16).astype(F32), jnp.uint32)
    return (bits[:, :w] >> 16) | (bits[:, w:] & jnp.uint32(0xFFFF0000))


def _moe_kernel(item_e, item_start, item_rows, item_fill, slot_tok,
                h2p_hbm, wg_ref, wu_ref, wd_ref, yb_hbm,
                xbuf, xs_bf, act, ystage, pf, gsem, osem):
    a = pl.program_id(0)
    t = pl.program_id(1)
    n_items = pl.num_programs(0)
    n = item_rows[a]
    start = item_start[a]
    rb_rows = MOE_PAD
    n_rb = n // rb_rows
    half_w = wg_ref.shape[2]
    cur = a % 2
    nxt = 1 - cur
    a1 = jnp.minimum(a + 1, n_items - 1)
    n_next = jnp.where(a + 1 < n_items, item_rows[a1], 0)
    start_next = item_start[a1]

    def row_copy(tok, slot, i):
        return pltpu.make_async_copy(h2p_hbm.at[pl.ds(tok, 1), :], xbuf.at[slot, pl.ds(i, 1), :],
                                     gsem.at[slot])

    def start_row(tok, slot, i):
        row_copy(tok, slot, i).start(priority=1)

    @pl.when(jnp.logical_and(a == 0, t == 0))
    def _():
        pf[1] = 0
        pf[2] = 0

        def issue(i, carry):
            start_row(slot_tok[start + i], 0, i)
            return carry
        lax.fori_loop(0, n, issue, 0)

    @pl.when(t == 0)
    def _():
        pf[0] = 0

        def drain(bk, carry):
            r0 = pl.multiple_of(bk * rb_rows, rb_rows)
            pltpu.make_async_copy(h2p_hbm.at[pl.ds(0, rb_rows), :],
                                  xbuf.at[cur, pl.ds(r0, rb_rows), :], gsem.at[cur]).wait()
            return carry
        lax.fori_loop(0, n_rb, drain, 0)

    def prefetch_group():
        base = pf[0]
        for q in range(MOE_GATHER_GROUP):
            start_row(slot_tok[start_next + base + q], nxt, base + q)
        pf[0] = base + MOE_GATHER_GROUP

    def run_blocks(compute, before=None):
        def body(rb, carry):
            if before is not None:
                before(rb)
            more = pf[0] < n_next

            @pl.when(more)
            def _():
                compute(rb, prefetch_group)

            @pl.when(jnp.logical_not(more))
            def _():
                compute(rb, lambda: None)
            return carry
        lax.fori_loop(0, n_rb, body, 0)

    def gate_up(half):
        def compute(rb, mid):
            r0 = pl.multiple_of(rb * rb_rows, rb_rows)
            if half == 0:
                xs = _unpack_pairs(xbuf[cur, pl.ds(r0, rb_rows), :]).astype(BF16)
                xs_bf[pl.ds(r0, rb_rows), :] = xs
            else:
                xs = xs_bf[pl.ds(r0, rb_rows), :]
            g = _dot(xs, wg_ref[0].astype(BF16))
            u = _dot(xs, wu_ref[0].astype(BF16))
            mid()
            act[pl.ds(r0, rb_rows), half * half_w:(half + 1) * half_w] = (
                g * _sigmoid(g) * u).astype(BF16)
        run_blocks(compute)

    def out_copy(slot, r0):
        return pltpu.make_async_copy(
            ystage.at[slot],
            yb_hbm.at[pl.ds(pl.multiple_of(start + r0, rb_rows), rb_rows), :],
            osem.at[slot])

    def wait_stage(slot):
        @pl.when(pf[1 + slot] == 1)
        def _():
            out_copy(slot, 0).wait()
            pf[1 + slot] = 0

    def down():
        def compute(rb, mid):
            r0 = pl.multiple_of(rb * rb_rows, rb_rows)
            slot = rb % 2
            y = _dot(act[pl.ds(r0, rb_rows), :], wd_ref[0].astype(BF16))
            mid()
            ystage[slot] = _pack_pairs(y)
            out_copy(slot, r0).start()

        def free_stage(rb):
            wait_stage(rb % 2)
            pf[1 + rb % 2] = 1
        run_blocks(compute, free_stage)

        def tail(i, carry):
            start_row(slot_tok[start_next + i], nxt, i)
            return carry
        lax.fori_loop(pf[0], n_next, tail, 0)

    pl.when(t == 0)(functools.partial(gate_up, 0))
    pl.when(t == 1)(functools.partial(gate_up, 1))
    pl.when(t == 2)(down)

    fill = item_fill[a]

    @pl.when(jnp.logical_and(t == 0, fill > 0))
    def _():
        wait_stage(0)
        ystage[0] = jnp.zeros(ystage.shape[1:], ystage.dtype)

        def zero_block(bk, carry):
            cp = out_copy(0, bk * rb_rows)
            cp.start()
            cp.wait()
            return carry
        lax.fori_loop(0, fill // rb_rows, zero_block, 0)

    @pl.when(jnp.logical_and(a == n_items - 1, t == pl.num_programs(1) - 1))
    def _():
        wait_stage(0)
        wait_stage(1)


def _moe_ffn(h2p, w_gate, w_up, w_down, item_e, item_start, item_rows, item_fill, slot_tok,
             n_slots):
    pw = h2p.shape[1]
    d = 2 * pw
    n_items = item_e.shape[0]
    half_w = D_EXPERT // 2

    def wgu_map(a, t, ie, ist, ir, fl, st):
        return (ie[a], 0, jnp.where(ir[a] > 0, jnp.minimum(t, 1), 1))

    def wd_map(a, t, ie, ist, ir, fl, st):
        return (jnp.where(t >= 2, ie[a], ie[jnp.maximum(a - 1, 0)]), 0, 0)

    grid_spec = pltpu.PrefetchScalarGridSpec(
        num_scalar_prefetch=5,
        grid=(n_items, 3),
        in_specs=[
            pl.BlockSpec(memory_space=pl.ANY),
            pl.BlockSpec((1, d, half_w), wgu_map),
            pl.BlockSpec((1, d, half_w), wgu_map),
            pl.BlockSpec((1, D_EXPERT, d), wd_map),
        ],
        out_specs=pl.BlockSpec(memory_space=pl.ANY),
        scratch_shapes=[
            pltpu.VMEM((2, MOE_ITEM_ROWS, pw), jnp.uint32),
            pltpu.VMEM((MOE_ITEM_ROWS, d), BF16),
            pltpu.VMEM((MOE_ITEM_ROWS, D_EXPERT), BF16),
            pltpu.VMEM((2, MOE_PAD, pw), jnp.uint32),
            pltpu.SMEM((3,), jnp.int32),
            pltpu.SemaphoreType.DMA((2,)),
            pltpu.SemaphoreType.DMA((2,)),
        ],
    )
    return pl.pallas_call(
        _moe_kernel,
        grid_spec=grid_spec,
        out_shape=jax.ShapeDtypeStruct((n_slots, pw), jnp.uint32),
        compiler_params=_cparams(2, 56),
        name="moe",
    )(item_e, item_start, item_rows, item_fill, slot_tok, h2p, w_gate, w_up, w_down)


def _combine_kernel(s0, s1, yb_hbm, gates_ref, x1_ref, nw_ref, g2_ref, out_ref, rows, sem):
    i = pl.program_id(0)
    n_tiles = pl.num_programs(0)
    tm = x1_ref.shape[0]

    def gather(tile, buf):
        def issue(r8, carry):
            t0 = tile * tm + r8 * 8
            for dr in range(8):
                for k, s_k in enumerate((s0, s1)):
                    slot = s_k[t0 + dr]
                    pltpu.make_async_copy(yb_hbm.at[pl.ds(slot, 1), :],
                                          rows.at[buf, k, pl.ds(r8 * 8 + dr, 1), :],
                                          sem.at[buf]).start()
            return carry
        lax.fori_loop(0, tm // 8, issue, 0)

    @pl.when(i == 0)
    def _():
        gather(0, 0)

    @pl.when(i + 1 < n_tiles)
    def _():
        gather(i + 1, (i + 1) % 2)

    buf = i % 2
    for k in range(EXPERT_TOPK):
        pltpu.make_async_copy(yb_hbm.at[pl.ds(0, tm), :], rows.at[buf, k], sem.at[buf]).wait()

    g = gates_ref[...]
    y = (_unpack_pairs(rows[buf, 0]) * g[:, 0:1] + _unpack_pairs(rows[buf, 1]) * g[:, 1:2])
    out_ref[...] = x1_ref[...] + g2_ref[0:1, :] * (_rms(y) * nw_ref[3:4, :])


def _combine(s0, s1, yb, gates, x1, norm_w, mod):
    n, d = x1.shape
    tm = 256
    grid_spec = pltpu.PrefetchScalarGridSpec(
        num_scalar_prefetch=2,
        grid=(n // tm,),
        in_specs=[
            pl.BlockSpec(memory_space=pl.ANY),
            pl.BlockSpec((tm, EXPERT_TOPK), lambda i, *_: (i, 0)),
            pl.BlockSpec((tm, d), lambda i, *_: (i, 0)),
            pl.BlockSpec((4, d), lambda i, *_: (0, 0)),
            pl.BlockSpec((8, d), lambda i, *_: (0, 5)),
        ],
        out_specs=pl.BlockSpec((tm, d), lambda i, *_: (i, 0)),
        scratch_shapes=[
            pltpu.VMEM((2, EXPERT_TOPK, tm, d // 2), jnp.uint32),
            pltpu.SemaphoreType.DMA((2,)),
        ],
    )
    return pl.pallas_call(
        _combine_kernel,
        grid_spec=grid_spec,
        out_shape=jax.ShapeDtypeStruct((n, d), F32),
        compiler_params=_cparams(1, 32),
        name="combine",
    )(s0, s1, yb, gates, x1, norm_w, mod)


def _rope_tables(n):
    rows = n // GRID_W
    r = np.repeat(np.arange(rows, dtype=np.float32), GRID_W)
    cidx = np.tile(np.arange(GRID_W, dtype=np.float32), rows)
    n_freq = RET_DK // 4
    inv = np.float32(ROPE_BASE) ** (-np.arange(n_freq, dtype=np.float32) / np.float32(n_freq))
    ang = np.concatenate([r[:, None] * inv, cidx[:, None] * inv], axis=-1).astype(np.float32)
    cos, sin = np.cos(ang), np.sin(ang)
    cs = np.concatenate([cos, cos], axis=-1)
    sn = np.concatenate([-sin, sin], axis=-1)
    return jnp.asarray(cs, F32), jnp.asarray(sn, F32)


def kernel(x, c, ctx, c_ctx, w_mod, b_mod, norm_w, w_in, hg_lb_logits, hg_norm_w, ret_decay_logits,
           ret_norm_w, w_out, router_group_w, router_group_b, router_expert_w, router_expert_b,
           w_gate, w_up, w_down):
    bsz, n, d = x.shape
    l = ctx.shape[1]
    assert bsz == 1 and w_mod.shape[0] == 1, "single sample, single layer"
    x2d, ctx2d = x[0], ctx[0]

    cc = jnp.zeros((8, d), F32).at[0].set(c[0]).at[1].set(c_ctx)
    mod = _modulation(cc, w_mod[0], b_mod[0])

    lb = jnp.cumsum(jax.nn.softmax(hg_lb_logits.astype(F32), axis=0), axis=0)[0]
    rk0 = 5 * HG_WIDTH + RET_HEADS * RET_DK
    cols = jnp.stack([
        jnp.zeros((PROJ_WIDTH,), F32).at[HG_WIDTH:3 * HG_WIDTH].set(lb.reshape(-1)),
        jnp.ones((PROJ_WIDTH,), F32).at[rk0:rk0 + RET_HEADS * RET_DK].set(RET_DK ** -0.5),
    ])
    w_in_bf = w_in[0].astype(BF16)
    p, g = _projection(x2d, norm_w[0], mod, w_in_bf, cols, 0, 1024, "proj_x")
    pc, gc = _projection(ctx2d, norm_w[0], mod, w_in_bf, cols, 1, l, "proj_ctx")

    ret_lg = jax.nn.log_sigmoid(ret_decay_logits[0].astype(F32))
    cs_tab, sn_tab = _rope_tables(n)

    o_f, o_b = _hgrn_scan(p, g, pc, gc)
    r_f, r_b = _ret_scan(p, pc, ret_lg, cs_tab, sn_tab)

    wr = jnp.concatenate([router_group_w[0], router_expert_w[0]], axis=1)
    wr = jnp.pad(wr, ((0, 0), (0, ROUTER_LANES - wr.shape[1])))
    wr_hi = wr.astype(BF16)
    wr_parts = jnp.concatenate([wr_hi, (wr - wr_hi.astype(F32)).astype(BF16)], axis=1)
    br = jnp.concatenate([router_group_b[0], router_expert_b[0]]).astype(F32)
    br = jnp.pad(br, (0, ROUTER_LANES - br.shape[0])).reshape(1, ROUTER_LANES)
    x1, h2, logits = _out_projection(
        o_f, o_b, r_f, r_b, p, x2d, w_out[0].astype(BF16), hg_norm_w, ret_norm_w, norm_w[0], mod,
        wr_parts, br)

    n_assign = n * EXPERT_TOPK
    n_slots = n_assign + N_EXPERTS * MOE_PAD
    n_items = (n_slots + N_EXPERTS * (MOE_ITEM_ROWS - MOE_PAD)) // MOE_ITEM_ROWS
    gates, s0, s1, counts = _router(logits)
    s0, s1 = s0.reshape(-1), s1.reshape(-1)
    slot_tok, item_e, item_start, item_rows, item_fill = _plan(counts, s0, s1, n_slots, n_items)

    yb = _moe_ffn(h2, w_gate[0], w_up[0], w_down[0], item_e, item_start, item_rows, item_fill,
                  slot_tok, n_slots)
    out = _combine(s0, s1, yb, gates, x1, norm_w[0], mod)
    return out[None]
```

```python
import functools

import jax
import jax.numpy as jnp
import numpy as np
from jax import lax
from jax.experimental import pallas as pl
from jax.experimental.pallas import tpu as pltpu

F32 = jnp.float32
BF16 = jnp.bfloat16

D_MODEL = 2048
GRID_W = 64
HG_HEADS = 8
HG_DK = 128
HG_WIDTH = 1024
RET_HEADS = 4
RET_DK = 128
RET_DV = 256
RET_WIDTH = 1024
PROJ_WIDTH = 8192
N_GROUPS = 4
EXPERTS_PER_GROUP = 8
N_EXPERTS = 32
EXPERT_TOPK = 2
D_EXPERT = 1024
RMS_EPS = 1e-6
ROPE_BASE = 10000.0

COL_HQ, COL_HF_FWD, COL_HF_BWD, COL_HI, COL_HGATE = 0, 1, 2, 3, 4
COL_RQ_512, COL_RK_512 = 10, 11
COL_RV, COL_RGATE = 6, 7

HG_CHUNK = 128
HG_DIAG = 16
HG_LOG_CLAMP = 10.0
RET_CHUNK = 256

MOE_PAD = 256
MOE_ITEM_ROWS = 1024
MOE_GATHER_GROUP = 128
ROUTER_LANES = 128
LANES = 128
TILE_ROWS = 8


def _cparams(n_axes, vmem_mb):
    return pltpu.CompilerParams(
        dimension_semantics=("arbitrary",) * n_axes,
        vmem_limit_bytes=vmem_mb * 1024 * 1024,
    )


def _sigmoid(x):
    return 1.0 / (1.0 + jnp.exp(-x))


def _dot(a, b):
    return jnp.dot(a, b, preferred_element_type=F32)


def _dot_nt(a, b):
    return lax.dot_general(a, b, (((1,), (1,)), ((), ())), preferred_element_type=F32)


def _dot_tn(a, b):
    return lax.dot_general(a, b, (((0,), (0,)), ((), ())), preferred_element_type=F32)


def _mod_kernel(cc_ref, w_ref, b_ref, o_ref):
    s = cc_ref[...]
    s = s * _sigmoid(s)
    s16 = jnp.concatenate([s, s], axis=0)
    hi = s16.astype(BF16).astype(F32)
    row = lax.broadcasted_iota(jnp.int32, s16.shape, 0)
    lhs = jnp.where(row < 8, hi, s16 - hi).astype(BF16)
    r = _dot(lhs, w_ref[...].astype(BF16))
    o_ref[...] = r[:8] + r[8:] + b_ref[...]


def _modulation(cc, w_mod, b_mod):
    d, n = w_mod.shape
    tn = 1024
    return pl.pallas_call(
        _mod_kernel,
        grid=(n // tn,),
        in_specs=[
            pl.BlockSpec((8, d), lambda j: (0, 0)),
            pl.BlockSpec((d, tn), lambda j: (0, j)),
            pl.BlockSpec((1, tn), lambda j: (0, j)),
        ],
        out_specs=pl.BlockSpec((8, tn), lambda j: (0, j)),
        out_shape=jax.ShapeDtypeStruct((8, n), F32),
        compiler_params=_cparams(1, 40),
        name="mod",
    )(cc, w_mod, b_mod.reshape(1, n))


PROJ_TN = 1024
PROLOGUE_ROWS = 32


def _proj_kernel(x_ref, nw_ref, sh_ref, sc_ref, w_ref, cols_ref, o_ref, g_ref, h_scr, *, mod_row):
    j = pl.program_id(1)

    @pl.when(j == 0)
    def _():
        scale = nw_ref[0:1, :] * (1.0 + sc_ref[mod_row:mod_row + 1, :])
        shift = sh_ref[mod_row:mod_row + 1, :]

        def body(r, carry):
            r0 = pl.multiple_of(r * PROLOGUE_ROWS, PROLOGUE_ROWS)
            x = x_ref[pl.ds(r0, PROLOGUE_ROWS), :]
            y = x * lax.rsqrt(jnp.mean(x * x, axis=-1, keepdims=True) + RMS_EPS)
            h_scr[pl.ds(r0, PROLOGUE_ROWS), :] = (y * scale + shift).astype(BF16)
            return carry
        lax.fori_loop(0, x_ref.shape[0] // PROLOGUE_ROWS, body, 0)

    is_gate = jnp.logical_or(j == 1, j == 2)
    is_silu = jnp.logical_or(j == 0, jnp.logical_or(j == 4, j == 7))
    is_lin = jnp.logical_not(jnp.logical_or(is_gate, is_silu))

    @pl.when(is_lin)
    def _():
        o_ref[...] = (_dot(h_scr[...], w_ref[...]) * cols_ref[1:2, :]).astype(BF16)

    @pl.when(is_silu)
    def _():
        acc = _dot(h_scr[...], w_ref[...])
        o_ref[...] = (acc * _sigmoid(acc)).astype(BF16)

    @pl.when(is_gate)
    def _():
        acc = _dot(h_scr[...], w_ref[...])
        lb = cols_ref[0:1, :]
        g = jnp.maximum(jnp.log(lb + (1.0 - lb) * _sigmoid(acc)), -HG_LOG_CLAMP)
        g_ref[...] = g
        o_ref[...] = g.astype(BF16)


def _projection(x2d, norm_w, mod, w_in, cols, mod_row, tm, name):
    n, d = x2d.shape
    tn = PROJ_TN
    return pl.pallas_call(
        functools.partial(_proj_kernel, mod_row=mod_row),
        grid=(n // tm, PROJ_WIDTH // tn),
        in_specs=[
            pl.BlockSpec((tm, d), lambda i, j: (i, 0)),
            pl.BlockSpec((4, d), lambda i, j: (0, 0)),
            pl.BlockSpec((8, d), lambda i, j: (0, 0)),
            pl.BlockSpec((8, d), lambda i, j: (0, 1)),
            pl.BlockSpec((d, tn), lambda i, j: (0, j)),
            pl.BlockSpec((2, tn), lambda i, j: (0, j)),
        ],
        out_specs=[
            pl.BlockSpec((tm, tn), lambda i, j: (i, j)),
            pl.BlockSpec((tm, tn), lambda i, j: (i, jnp.clip(j - 1, 0, 1))),
        ],
        out_shape=[
            jax.ShapeDtypeStruct((n, PROJ_WIDTH), BF16),
            jax.ShapeDtypeStruct((n, 2 * HG_WIDTH), F32),
        ],
        scratch_shapes=[pltpu.VMEM((tm, d), BF16)],
        compiler_params=_cparams(2, 52),
        name=name,
    )(x2d, norm_w, mod, mod, w_in, cols)


def _tri_ones(c, rev):
    t = lax.broadcasted_iota(jnp.int32, (c, c), 0)
    s = lax.broadcasted_iota(jnp.int32, (c, c), 1)
    return jnp.where((s >= t) if rev else (s <= t), 1.0, 0.0).astype(BF16)


def _cumsum_rows(g, tri):
    hi = g.astype(BF16)
    lo = (g - hi.astype(F32)).astype(BF16)
    return _dot(tri, hi) + _dot(tri, lo)


def _row_refs(b, idxs, blk):
    parts = [jnp.broadcast_to(b[i:i + 1, :], (blk, b.shape[1])) for i in idxs]
    return parts[0] if len(parts) == 1 else jnp.concatenate(parts, axis=0)


def _hgrn_levels(c, rev):
    t = lax.broadcasted_iota(jnp.int32, (c, c), 0)
    s = lax.broadcasted_iota(jnp.int32, (c, c), 1)
    levels = []
    m = c // 2
    while m >= HG_DIAG:
        blk = 2 * m
        same = ((t ^ s) & ~(blk - 1)) == 0
        if rev:
            idxs = [b0 + m for b0 in range(0, c, blk)]
            cross = jnp.logical_and((t & m) == 0, (s & m) != 0)
        else:
            idxs = [b0 + m - 1 for b0 in range(0, c, blk)]
            cross = jnp.logical_and((t & m) != 0, (s & m) == 0)
        levels.append((idxs, blk, jnp.logical_and(same, cross)))
        m //= 2
    blk = HG_DIAG
    same = ((t ^ s) & ~(blk - 1)) == 0
    if rev:
        idxs = [b0 + blk // 2 for b0 in range(0, c, blk)]
        mask = jnp.logical_and(same, s >= t)
    else:
        idxs = [b0 + blk // 2 - 1 for b0 in range(0, c, blk)]
        mask = jnp.logical_and(same, s <= t)
    levels.append((idxs, blk, mask))
    return levels


def _hgrn_head(q_bf, g, b, v_bf, st, rev, levels):
    c = g.shape[0]
    kk = 1.0 - jnp.exp(g)

    o = None
    if levels is not None:
        q = q_bf.astype(F32)
        att = jnp.zeros((c, c), F32)
        for idxs, blk, mask in levels:
            ref = _row_refs(b, idxs, blk)
            a_l = _dot_nt((q * jnp.exp(b - ref)).astype(BF16),
                          (kk * jnp.exp(ref - b)).astype(BF16))
            att = att + jnp.where(mask, a_l, 0.0)
        o = _dot(att.astype(BF16), v_bf) + _dot_nt((q * jnp.exp(b)).astype(BF16), st.astype(BF16))

    b_end = b[0:1, :] if rev else b[c - 1:c, :]
    k_end = (kk * jnp.exp(b_end - b)).astype(BF16)
    st_new = st * jnp.exp(b_end) + _dot_tn(v_bf, k_end)
    return o, st_new


def _hgrn_ctx_kernel(gf_ref, vf_ref, gb_ref, vb_ref, st_ref):
    @pl.when(pl.program_id(0) == 0)
    def _():
        st_ref[...] = jnp.zeros_like(st_ref)

    c = gf_ref.shape[0]
    bf = _cumsum_rows(gf_ref[...], _tri_ones(c, False))
    bb = _cumsum_rows(gb_ref[...], _tri_ones(c, True))
    for h in range(HG_HEADS):
        sl = slice(h * HG_DK, (h + 1) * HG_DK)
        _, st_ref[h] = _hgrn_head(None, gf_ref[:, sl], bf[:, sl], vf_ref[:, sl], st_ref[h],
                                  False, None)
        hb = HG_HEADS + h
        _, st_ref[hb] = _hgrn_head(None, gb_ref[:, sl], bb[:, sl], vb_ref[:, sl], st_ref[hb],
                                   True, None)


def _hgrn_kernel(qf_ref, gf_ref, vf_ref, qb_ref, gb_ref, vb_ref, st0_ref, of_ref, ob_ref, st_ref):
    @pl.when(pl.program_id(0) == 0)
    def _():
        st_ref[...] = st0_ref[...]

    c = gf_ref.shape[0]
    lev_f = _hgrn_levels(c, False)
    lev_b = _hgrn_levels(c, True)
    bf = _cumsum_rows(gf_ref[...], _tri_ones(c, False))
    bb = _cumsum_rows(gb_ref[...], _tri_ones(c, True))
    for h in range(HG_HEADS):
        sl = slice(h * HG_DK, (h + 1) * HG_DK)
        of_ref[:, sl], st_ref[h] = _hgrn_head(qf_ref[:, sl], gf_ref[:, sl], bf[:, sl],
                                              vf_ref[:, sl], st_ref[h], False, lev_f)
        hb = HG_HEADS + h
        ob_ref[:, sl], st_ref[hb] = _hgrn_head(qb_ref[:, sl], gb_ref[:, sl], bb[:, sl],
                                               vb_ref[:, sl], st_ref[hb], True, lev_b)


def _hgrn_scan(p, g, pc, gc):
    c = HG_CHUNK
    w = HG_WIDTH
    n, l = p.shape[0], pc.shape[0]
    n_x, n_ctx = n // c, l // c
    st_shape = (2 * HG_HEADS, HG_DK, HG_DK)

    def spec(chunk_fn, col):
        return pl.BlockSpec((c, w), lambda s: (chunk_fn(s), col))

    st0 = pl.pallas_call(
        _hgrn_ctx_kernel,
        grid=(n_ctx,),
        in_specs=[
            spec(lambda s: s, 0), spec(lambda s: s, COL_HI),
            spec(lambda s: n_ctx - 1 - s, 1), spec(lambda s: n_ctx - 1 - s, COL_HI),
        ],
        out_specs=pl.BlockSpec(st_shape, lambda s: (0, 0, 0)),
        out_shape=jax.ShapeDtypeStruct(st_shape, F32),
        compiler_params=_cparams(1, 32),
        name="hgrn_ctx",
    )(gc, pc, gc, pc)

    fwd = lambda s: s
    bwd = lambda s: n_x - 1 - s
    return pl.pallas_call(
        _hgrn_kernel,
        grid=(n_x,),
        in_specs=[
            spec(fwd, COL_HQ), spec(fwd, 0), spec(fwd, COL_HI),
            spec(bwd, COL_HQ), spec(bwd, 1), spec(bwd, COL_HI),
            pl.BlockSpec(st_shape, lambda s: (0, 0, 0)),
        ],
        out_specs=[spec(fwd, 0), spec(bwd, 0)],
        out_shape=[jax.ShapeDtypeStruct((n, w), F32)] * 2,
        scratch_shapes=[pltpu.VMEM(st_shape, F32)],
        compiler_params=_cparams(1, 32),
        name="hgrn",
    )(p, g, p, p, g, p, st0)


def _rope(x_bf, cs, sn):
    x = x_bf.astype(F32)
    return x * cs + pltpu.roll(x, RET_DK // 2, axis=1) * sn


def _ret_decays(lg, c, rev):
    t = lax.broadcasted_iota(jnp.int32, (c, c), 0)
    s = lax.broadcasted_iota(jnp.int32, (c, c), 1)
    diff = (s - t) if rev else (t - s)
    pair = jnp.where(diff >= 0, jnp.exp(lg * jnp.maximum(diff, 0).astype(F32)), 0.0)
    pos = lax.broadcasted_iota(jnp.int32, (c, RET_DK), 0).astype(F32)
    q_dec = jnp.exp(lg * ((c - pos) if rev else (pos + 1.0)))
    k_dec = jnp.exp(lg * (pos if rev else (c - 1.0 - pos)))
    chunk = jnp.exp(jnp.full((8, RET_DK), lg * c, F32))
    return pair, q_dec, k_dec, chunk


def _ret_head(q, k, v_bf, decays, st):
    pair, q_dec, k_dec, chunk = decays
    o = None
    if q is not None:
        att = _dot_nt(q.astype(BF16), k.astype(BF16)) * pair
        o = _dot(att.astype(BF16), v_bf) + _dot_nt((q * q_dec).astype(BF16), st.astype(BF16))
    st_new = st * chunk[0:1, :] + _dot_tn(v_bf, (k * k_dec).astype(BF16))
    return o, st_new


def _ret_ctx_kernel(lg_ref, kf_ref, vf_ref, kb_ref, vb_ref, st_ref):
    @pl.when(pl.program_id(0) == 0)
    def _():
        st_ref[...] = jnp.zeros_like(st_ref)

    c = kf_ref.shape[0]
    for h in range(RET_HEADS):
        sk = slice(h * RET_DK, (h + 1) * RET_DK)
        sv = slice(h * RET_DV, (h + 1) * RET_DV)
        _, st_ref[h] = _ret_head(None, kf_ref[:, sk].astype(F32), vf_ref[:, sv],
                                 _ret_decays(lg_ref[0, h], c, False), st_ref[h])
        hb = RET_HEADS + h
        _, st_ref[hb] = _ret_head(None, kb_ref[:, sk].astype(F32), vb_ref[:, sv],
                                  _ret_decays(lg_ref[1, h], c, True), st_ref[hb])


def _ret_kernel(lg_ref, qf_ref, kf_ref, vf_ref, csf_ref, snf_ref,
                qb_ref, kb_ref, vb_ref, csb_ref, snb_ref, st0_ref, of_ref, ob_ref,
                st_ref, pair_ref, qd_ref, kd_ref, cd_ref):
    c = qf_ref.shape[0]

    @pl.when(pl.program_id(0) == 0)
    def _():
        st_ref[...] = st0_ref[...]
        for d in range(2):
            for h in range(RET_HEADS):
                i = d * RET_HEADS + h
                pair_ref[i], qd_ref[i], kd_ref[i], cd_ref[i] = _ret_decays(lg_ref[d, h], c, d == 1)

    csf, snf = csf_ref[...], snf_ref[...]
    csb, snb = csb_ref[...], snb_ref[...]
    for h in range(RET_HEADS):
        sk = slice(h * RET_DK, (h + 1) * RET_DK)
        sv = slice(h * RET_DV, (h + 1) * RET_DV)
        of_ref[:, sv], st_ref[h] = _ret_head(
            _rope(qf_ref[:, sk], csf, snf), _rope(kf_ref[:, sk], csf, snf), vf_ref[:, sv],
            (pair_ref[h], qd_ref[h], kd_ref[h], cd_ref[h]), st_ref[h])
        hb = RET_HEADS + h
        ob_ref[:, sv], st_ref[hb] = _ret_head(
            _rope(qb_ref[:, sk], csb, snb), _rope(kb_ref[:, sk], csb, snb), vb_ref[:, sv],
            (pair_ref[hb], qd_ref[hb], kd_ref[hb], cd_ref[hb]), st_ref[hb])


def _ret_scan(p, pc, ret_lg, cs_tab, sn_tab):
    c = RET_CHUNK
    n, l = p.shape[0], pc.shape[0]
    n_x, n_ctx = n // c, l // c
    qk_w = RET_HEADS * RET_DK
    st_shape = (2 * RET_HEADS, RET_DV, RET_DK)
    smem = pl.BlockSpec(memory_space=pltpu.SMEM)

    def kv_specs(chunk_fn):
        return [
            pl.BlockSpec((c, qk_w), lambda s: (chunk_fn(s), COL_RK_512)),
            pl.BlockSpec((c, RET_WIDTH), lambda s: (chunk_fn(s), COL_RV)),
        ]

    st0 = pl.pallas_call(
        _ret_ctx_kernel,
        grid=(n_ctx,),
        in_specs=[smem, *kv_specs(lambda s: s), *kv_specs(lambda s: n_ctx - 1 - s)],
        out_specs=pl.BlockSpec(st_shape, lambda s: (0, 0, 0)),
        out_shape=jax.ShapeDtypeStruct(st_shape, F32),
        compiler_params=_cparams(1, 32),
        name="ret_ctx",
    )(ret_lg, pc, pc, pc, pc)

    def specs(chunk_fn):
        return [
            pl.BlockSpec((c, qk_w), lambda s: (chunk_fn(s), COL_RQ_512)),
            *kv_specs(chunk_fn),
            pl.BlockSpec((c, RET_DK), lambda s: (chunk_fn(s), 0)),
            pl.BlockSpec((c, RET_DK), lambda s: (chunk_fn(s), 0)),
        ]

    fwd = lambda s: s
    bwd = lambda s: n_x - 1 - s
    return pl.pallas_call(
        _ret_kernel,
        grid=(n_x,),
        in_specs=[smem, *specs(fwd), *specs(bwd), pl.BlockSpec(st_shape, lambda s: (0, 0, 0))],
        out_specs=[
            pl.BlockSpec((c, RET_WIDTH), lambda s: (fwd(s), 0)),
            pl.BlockSpec((c, RET_WIDTH), lambda s: (bwd(s), 0)),
        ],
        out_shape=[jax.ShapeDtypeStruct((n, RET_WIDTH), F32)] * 2,
        scratch_shapes=[
            pltpu.VMEM(st_shape, F32),
            pltpu.VMEM((2 * RET_HEADS, c, c), F32),
            pltpu.VMEM((2 * RET_HEADS, c, RET_DK), F32),
            pltpu.VMEM((2 * RET_HEADS, c, RET_DK), F32),
            pltpu.VMEM((2 * RET_HEADS, 8, RET_DK), F32),
        ],
        compiler_params=_cparams(1, 32),
        name="ret",
    )(ret_lg, p, p, p, cs_tab, sn_tab, p, p, p, cs_tab, sn_tab, st0)


def _rms(y):
    return y * lax.rsqrt(jnp.mean(y * y, axis=-1, keepdims=True) + RMS_EPS)


def _outproj_kernel(of_ref, ob_ref, rf_ref, rb_ref, hgate_ref, rgate_ref, x_ref, wout_ref,
                    hgw_ref, rnw_ref, nw_ref, g1_ref, sh2_ref, sc2_ref, wr_ref, br_ref,
                    x1_ref, h2_ref, lg_ref, cat_scr):
    tm = x_ref.shape[0]
    n_sub = 1
    for sub in range(n_sub):
        rows = slice(sub * tm // n_sub, (sub + 1) * tm // n_sub)
        for h in range(HG_HEADS):
            sl = slice(h * HG_DK, (h + 1) * HG_DK)
            o = of_ref[rows, sl] + ob_ref[rows, sl]
            cat_scr[rows, sl] = (_rms(o) * hgw_ref[0:1, sl]
                                 * hgate_ref[rows, sl].astype(F32)).astype(BF16)
        for h in range(RET_HEADS):
            sl = slice(h * RET_DV, (h + 1) * RET_DV)
            r = rf_ref[rows, sl] + rb_ref[rows, sl]
            cat_scr[rows, HG_WIDTH + h * RET_DV:HG_WIDTH + (h + 1) * RET_DV] = (
                _rms(r) * rnw_ref[0:1, sl] * rgate_ref[rows, sl].astype(F32)).astype(BF16)

        y = _dot(cat_scr[rows, :], wout_ref[...])
        x1 = x_ref[rows, :] + g1_ref[0:1, :] * (_rms(y) * nw_ref[1:2, :])
        x1_ref[rows, :] = x1
        h2 = _rms(x1) * nw_ref[2:3, :] * (1.0 + sc2_ref[0:1, :]) + sh2_ref[0:1, :]
        assert n_sub == 1
        _store_row_tiles(h2_ref, _pack_pairs(h2))
        hi = h2.astype(BF16)
        lo = (h2 - hi.astype(F32)).astype(BF16)
        parts = _dot(jnp.concatenate([hi, lo], axis=0), wr_ref[...])
        half = hi.shape[0]
        lg_ref[rows, :] = (parts[:half, :ROUTER_LANES] + parts[half:, :ROUTER_LANES]
                           + parts[:half, ROUTER_LANES:] + br_ref[...])


def _out_projection(o_f, o_b, r_f, r_b, p, x2d, w_out_bf, hg_norm_w, ret_norm_w, norm_w, mod,
                    wr_parts, br):
    n, d = x2d.shape
    tm = 256
    row = lambda w: pl.BlockSpec((tm, w), lambda i: (i, 0))
    full = lambda a: pl.BlockSpec(a.shape, lambda i: (0,) * a.ndim)
    modcol = lambda k: pl.BlockSpec((8, d), lambda i: (0, k))
    return pl.pallas_call(
        _outproj_kernel,
        grid=(n // tm,),
        in_specs=[
            row(HG_WIDTH), row(HG_WIDTH), row(RET_WIDTH), row(RET_WIDTH),
            pl.BlockSpec((tm, HG_WIDTH), lambda i: (i, COL_HGATE)),
            pl.BlockSpec((tm, RET_WIDTH), lambda i: (i, COL_RGATE)),
            row(d), full(w_out_bf), full(hg_norm_w), full(ret_norm_w), full(norm_w),
            modcol(2), modcol(3), modcol(4),
            full(wr_parts), full(br),
        ],
        out_specs=[row(d), pl.BlockSpec((tm * TILE_ROWS, LANES), lambda i: (i, 0)),
                   row(ROUTER_LANES)],
        out_shape=[
            jax.ShapeDtypeStruct((n, d), F32),
            jax.ShapeDtypeStruct((n * TILE_ROWS, LANES), jnp.uint32),
            jax.ShapeDtypeStruct((n, ROUTER_LANES), F32),
        ],
        scratch_shapes=[pltpu.VMEM((tm, d), BF16)],
        compiler_params=_cparams(1, 48),
        name="outproj",
    )(o_f, o_b, r_f, r_b, p, p, x2d, w_out_bf, hg_norm_w, ret_norm_w, norm_w, mod, mod, mod,
      wr_parts, br)


ROUTER_TM = 1024


def _lane_first(hit, lane_f):
    return jnp.min(jnp.where(hit, lane_f, 1e9), axis=1, keepdims=True)


def _router_kernel(lg_ref, gates_ref, s0_ref, s1_ref, cnt_ref, ltri, carry, pstart):
    ph = pl.program_id(0)
    i = pl.program_id(1)
    tm = lg_ref.shape[0]

    @pl.when(jnp.logical_and(ph == 0, i == 0))
    def _():
        t = lax.broadcasted_iota(jnp.int32, (tm, tm), 0)
        s = lax.broadcasted_iota(jnp.int32, (tm, tm), 1)
        ltri[...] = jnp.where(s < t, 1.0, 0.0).astype(BF16)

    @pl.when(jnp.logical_and(ph == 1, i == 0))
    def _():
        cnt = carry[...]
        padded = jnp.floor((cnt + (MOE_PAD - 1.0)) * (1.0 / MOE_PAD)) * MOE_PAD
        r = lax.broadcasted_iota(jnp.int32, (LANES, LANES), 0)
        c = lax.broadcasted_iota(jnp.int32, (LANES, LANES), 1)
        before = jnp.where(r < c, 1.0, 0.0).astype(BF16)
        pstart[...] = _dot(padded.astype(BF16), before)
        cnt_ref[...] = cnt.astype(jnp.int32)

    @pl.when(i == 0)
    def _():
        carry[...] = jnp.zeros_like(carry)

    lg = lg_ref[...]
    lane = lax.broadcasted_iota(jnp.int32, lg.shape, 1)
    lane_f = lane.astype(F32)
    neg = -jnp.inf

    gl = jnp.where(lane < N_GROUPS, lg, neg)
    gmax = jnp.max(gl, axis=1, keepdims=True)
    grp = _lane_first(gl == gmax, lane_f).astype(jnp.int32)
    p_sel = 1.0 / jnp.sum(jnp.exp(gl - gmax), axis=1, keepdims=True)

    in_grp = jnp.logical_and(
        jnp.logical_and(lane >= N_GROUPS, lane < N_GROUPS + N_EXPERTS),
        ((lane - N_GROUPS) >> 3) == grp)
    el = jnp.where(in_grp, lg, neg)
    v1 = jnp.max(el, axis=1, keepdims=True)
    o1 = lane_f == _lane_first(el == v1, lane_f)
    el2 = jnp.where(o1, neg, el)
    v2 = jnp.max(el2, axis=1, keepdims=True)
    o2 = lane_f == _lane_first(el2 == v2, lane_f)

    osum = (jnp.where(o1, 1.0, 0.0) + jnp.where(o2, 1.0, 0.0)).astype(BF16)
    earlier = _dot(ltri[...], osum) + carry[0:1, :]
    carry[...] = carry[...] + _dot(jnp.ones((8, tm), BF16), osum)

    @pl.when(ph == 1)
    def _():
        ez = jnp.exp(v2 - v1)
        g1 = p_sel / (1.0 + ez)
        gates_ref[...] = jnp.concatenate([g1, g1 * ez], axis=1)
        base = earlier + pstart[0:1, :]
        for o, s_ref in ((o1, s0_ref), (o2, s1_ref)):
            col = jnp.sum(jnp.where(o, base, 0.0), axis=1, keepdims=True)
            wide = jnp.broadcast_to(col, (tm, LANES))
            for b in range(tm // LANES):
                s_ref[b:b + 1, :] = wide[b * LANES:(b + 1) * LANES, :].T[0:1, :].astype(jnp.int32)


def _router(logits):
    t = logits.shape[0]
    tm = ROUTER_TM
    rows = tm // LANES
    return pl.pallas_call(
        _router_kernel,
        grid=(2, t // tm),
        in_specs=[pl.BlockSpec((tm, LANES), lambda ph, i: (i, 0))],
        out_specs=[
            pl.BlockSpec((tm, EXPERT_TOPK), lambda ph, i: (i * ph, 0)),
            pl.BlockSpec((rows, LANES), lambda ph, i: (i * ph, 0)),
            pl.BlockSpec((rows, LANES), lambda ph, i: (i * ph, 0)),
            pl.BlockSpec((8, LANES), lambda ph, i: (0, 0)),
        ],
        out_shape=[
            jax.ShapeDtypeStruct((t, EXPERT_TOPK), F32),
            jax.ShapeDtypeStruct((t // LANES, LANES), jnp.int32),
            jax.ShapeDtypeStruct((t // LANES, LANES), jnp.int32),
            jax.ShapeDtypeStruct((8, LANES), jnp.int32),
        ],
        scratch_shapes=[
            pltpu.VMEM((tm, tm), BF16),
            pltpu.VMEM((8, LANES), F32),
            pltpu.VMEM((8, LANES), F32),
        ],
        compiler_params=_cparams(2, 32),
        name="router",
    )(logits)


def _round_up_pow2(x, m):
    assert m & (m - 1) == 0
    return (x + (m - 1)) & ~(m - 1)


def _plan_kernel(cnt_ref, s0_ref, s1_ref, tok_ref, ie_ref, ist_ref, ir_ref, if_ref, zeros, sem):
    n_tok = s0_ref.shape[0]
    n_slots = tok_ref.shape[0]
    n_items = ie_ref.shape[0]
    r = MOE_ITEM_ROWS

    zeros[...] = jnp.zeros_like(zeros)
    clear = pltpu.make_async_copy(zeros, tok_ref, sem)
    clear.start()
    clear.wait()

    def scatter(t16, carry):
        for d in range(16):
            t = t16 * 16 + d
            tok_ref[s0_ref[t]] = t
            tok_ref[s1_ref[t]] = t
        return carry
    lax.fori_loop(0, n_tok // 16, scatter, 0)

    a = jnp.int32(0)
    used = jnp.int32(0)
    e_last = jnp.int32(0)
    for e in range(N_EXPERTS):
        pad = _round_up_pow2(cnt_ref[0, N_GROUPS + e], MOE_PAD)

        def item(sub, a, e=e, pad=pad, used=used):
            ie_ref[a] = e
            ist_ref[a] = used + sub * r
            ir_ref[a] = jnp.minimum(pad - sub * r, r)
            if_ref[a] = 0
            return a + 1
        a = lax.fori_loop(0, _round_up_pow2(pad, r) >> (r.bit_length() - 1), item, a)
        e_last = jnp.where(pad > 0, e, e_last)
        used = used + pad

    def idle(a2, carry):
        fill_start = used + (a2 - a) * r
        fill = jnp.clip(n_slots - fill_start, 0, r)
        ie_ref[a2] = e_last
        ist_ref[a2] = jnp.where(fill > 0, fill_start, 0)
        ir_ref[a2] = 0
        if_ref[a2] = fill
        return carry
    lax.fori_loop(a, n_items, idle, 0)


def _plan(counts, s0, s1, n_slots, n_items):
    smem = pl.BlockSpec(memory_space=pltpu.SMEM)
    item = jax.ShapeDtypeStruct((n_items,), jnp.int32)
    return pl.pallas_call(
        _plan_kernel,
        in_specs=[smem, smem, smem],
        out_specs=[smem] * 5,
        out_shape=[jax.ShapeDtypeStruct((n_slots,), jnp.int32), item, item, item, item],
        scratch_shapes=[pltpu.VMEM((n_slots,), jnp.int32), pltpu.SemaphoreType.DMA(())],
        name="plan",
    )(counts, s0, s1)


def _unpack_pairs(u):
    lo = lax.bitcast_convert_type(u << 16, F32)
    hi = lax.bitcast_convert_type(u & jnp.uint32(0xFFFF0000), F32)
    return jnp.concatenate([lo, hi], axis=1)


def _store_row_tiles(dst, packed):
    r = packed.shape[0]
    for s in range(TILE_ROWS):
        dst[pl.ds(s, r, stride=TILE_ROWS), :] = packed[:, s * LANES:(s + 1) * LANES]


def _load_row_tiles(src, first_row, r):
    return jnp.concatenate(
        [src[pl.ds(first_row * TILE_ROWS + s, r, stride=TILE_ROWS), :] for s in range(TILE_ROWS)],
        axis=1)


def _pack_pairs(y):
    w = y.shape[1] // 2
    bits = lax.bitcast_convert_type(y.astype(BF16).astype(F32), jnp.uint32)
    return (bits[:, :w] >> 16) | (bits[:, w:] & jnp.uint32(0xFFFF0000))


def _moe_kernel(item_e, item_start, item_rows, item_fill, slot_tok,
                h2p_hbm, wg_ref, wu_ref, wd_ref, yb_hbm,
                xbuf, xs_bf, act, ystage, pf, gsem, osem):
    a = pl.program_id(0)
    t = pl.program_id(1)
    n_items = pl.num_programs(0)
    n = item_rows[a]
    start = item_start[a]
    rb_rows = MOE_PAD
    n_rb = n // rb_rows
    half_w = wg_ref.shape[2]
    cur = a % 2
    nxt = 1 - cur
    a1 = jnp.minimum(a + 1, n_items - 1)
    n_next = jnp.where(a + 1 < n_items, item_rows[a1], 0)
    start_next = item_start[a1]

    def tile_rows(i, count=1):
        return pl.ds(pl.multiple_of(i * TILE_ROWS, TILE_ROWS), count * TILE_ROWS)

    def row_copy(tok, slot, i):
        return pltpu.make_async_copy(h2p_hbm.at[tile_rows(tok), :], xbuf.at[slot, tile_rows(i), :],
                                     gsem.at[slot])

    def start_row(tok, slot, i):
        row_copy(tok, slot, i).start(priority=1)

    @pl.when(jnp.logical_and(a == 0, t == 0))
    def _():
        pf[1] = 0
        pf[2] = 0

        def issue(i, carry):
            start_row(slot_tok[start + i], 0, i)
            return carry
        lax.fori_loop(0, n, issue, 0)

    @pl.when(t == 0)
    def _():
        pf[0] = 0

        def drain(bk, carry):
            r0 = pl.multiple_of(bk * rb_rows, rb_rows)
            pltpu.make_async_copy(h2p_hbm.at[tile_rows(0, rb_rows), :],
                                  xbuf.at[cur, tile_rows(r0, rb_rows), :], gsem.at[cur]).wait()
            return carry
        lax.fori_loop(0, n_rb, drain, 0)

    def prefetch_group():
        base = pf[0]
        for q in range(MOE_GATHER_GROUP):
            start_row(slot_tok[start_next + base + q], nxt, base + q)
        pf[0] = base + MOE_GATHER_GROUP

    def run_blocks(compute, before=None):
        def body(rb, carry):
            if before is not None:
                before(rb)
            more = pf[0] < n_next

            @pl.when(more)
            def _():
                compute(rb, prefetch_group)

            @pl.when(jnp.logical_not(more))
            def _():
                compute(rb, lambda: None)
            return carry
        lax.fori_loop(0, n_rb, body, 0)

    def gate_up(half):
        def compute(rb, mid):
            r0 = pl.multiple_of(rb * rb_rows, rb_rows)
            if half == 0:
                xs = _unpack_pairs(_load_row_tiles(xbuf.at[cur], r0, rb_rows)).astype(BF16)
                xs_bf[pl.ds(r0, rb_rows), :] = xs
            else:
                xs = xs_bf[pl.ds(r0, rb_rows), :]
            g = _dot(xs, wg_ref[0].astype(BF16))
            u = _dot(xs, wu_ref[0].astype(BF16))
            mid()
            act[pl.ds(r0, rb_rows), half * half_w:(half + 1) * half_w] = (
                g * _sigmoid(g) * u).astype(BF16)
        run_blocks(compute)

    def out_copy(slot, r0):
        return pltpu.make_async_copy(
            ystage.at[slot],
            yb_hbm.at[tile_rows(pl.multiple_of(start + r0, rb_rows), rb_rows), :],
            osem.at[slot])

    def wait_stage(slot):
        @pl.when(pf[1 + slot] == 1)
        def _():
            out_copy(slot, 0).wait()
            pf[1 + slot] = 0

    def down():
        def compute(rb, mid):
            r0 = pl.multiple_of(rb * rb_rows, rb_rows)
            slot = rb % 2
            y = _dot(act[pl.ds(r0, rb_rows), :], wd_ref[0].astype(BF16))
            mid()
            _store_row_tiles(ystage.at[slot], _pack_pairs(y))
            out_copy(slot, r0).start()

        def free_stage(rb):
            wait_stage(rb % 2)
            pf[1 + rb % 2] = 1
        run_blocks(compute, free_stage)

        def tail(i, carry):
            start_row(slot_tok[start_next + i], nxt, i)
            return carry
        lax.fori_loop(pf[0], n_next, tail, 0)

    pl.when(t == 0)(functools.partial(gate_up, 0))
    pl.when(t == 1)(functools.partial(gate_up, 1))
    pl.when(t == 2)(down)

    fill = item_fill[a]

    @pl.when(jnp.logical_and(t == 0, fill > 0))
    def _():
        wait_stage(0)
        ystage[0] = jnp.zeros(ystage.shape[1:], ystage.dtype)

        def zero_block(bk, carry):
            cp = out_copy(0, bk * rb_rows)
            cp.start()
            cp.wait()
            return carry
        lax.fori_loop(0, fill // rb_rows, zero_block, 0)

    @pl.when(jnp.logical_and(a == n_items - 1, t == pl.num_programs(1) - 1))
    def _():
        wait_stage(0)
        wait_stage(1)


def _moe_ffn(h2p, w_gate, w_up, w_down, item_e, item_start, item_rows, item_fill, slot_tok,
             n_slots):
    d = w_gate.shape[1]
    assert d == 2 * TILE_ROWS * LANES and h2p.shape[1] == LANES, "one packed token row per tile"
    n_items = item_e.shape[0]
    half_w = D_EXPERT // 2

    def wgu_map(a, t, ie, ist, ir, fl, st):
        return (ie[a], 0, jnp.where(ir[a] > 0, jnp.minimum(t, 1), 1))

    def wd_map(a, t, ie, ist, ir, fl, st):
        return (jnp.where(t >= 2, ie[a], ie[jnp.maximum(a - 1, 0)]), 0, 0)

    grid_spec = pltpu.PrefetchScalarGridSpec(
        num_scalar_prefetch=5,
        grid=(n_items, 3),
        in_specs=[
            pl.BlockSpec(memory_space=pl.ANY),
            pl.BlockSpec((1, d, half_w), wgu_map),
            pl.BlockSpec((1, d, half_w), wgu_map),
            pl.BlockSpec((1, D_EXPERT, d), wd_map),
        ],
        out_specs=pl.BlockSpec(memory_space=pl.ANY),
        scratch_shapes=[
            pltpu.VMEM((2, MOE_ITEM_ROWS * TILE_ROWS, LANES), jnp.uint32),
            pltpu.VMEM((MOE_ITEM_ROWS, d), BF16),
            pltpu.VMEM((MOE_ITEM_ROWS, D_EXPERT), BF16),
            pltpu.VMEM((2, MOE_PAD * TILE_ROWS, LANES), jnp.uint32),
            pltpu.SMEM((3,), jnp.int32),
            pltpu.SemaphoreType.DMA((2,)),
            pltpu.SemaphoreType.DMA((2,)),
        ],
    )
    return pl.pallas_call(
        _moe_kernel,
        grid_spec=grid_spec,
        out_shape=jax.ShapeDtypeStruct((n_slots * TILE_ROWS, LANES), jnp.uint32),
        compiler_params=_cparams(2, 56),
        name="moe",
    )(item_e, item_start, item_rows, item_fill, slot_tok, h2p, w_gate, w_up, w_down)


def _combine_kernel(s0, s1, yb_hbm, gates_ref, x1_ref, nw_ref, g2_ref, out_ref, rows, sem):
    i = pl.program_id(0)
    n_tiles = pl.num_programs(0)
    tm = x1_ref.shape[0]

    def tile_rows(i, count=1):
        return pl.ds(pl.multiple_of(i * TILE_ROWS, TILE_ROWS), count * TILE_ROWS)

    def gather(tile, buf):
        def issue(r8, carry):
            t0 = tile * tm + r8 * 8
            for dr in range(8):
                for k, s_k in enumerate((s0, s1)):
                    pltpu.make_async_copy(yb_hbm.at[tile_rows(s_k[t0 + dr]), :],
                                          rows.at[buf, k, tile_rows(r8 * 8 + dr), :],
                                          sem.at[buf]).start()
            return carry
        lax.fori_loop(0, tm // 8, issue, 0)

    @pl.when(i == 0)
    def _():
        gather(0, 0)

    @pl.when(i + 1 < n_tiles)
    def _():
        gather(i + 1, (i + 1) % 2)

    buf = i % 2
    for k in range(EXPERT_TOPK):
        pltpu.make_async_copy(yb_hbm.at[tile_rows(0, tm), :], rows.at[buf, k], sem.at[buf]).wait()

    g = gates_ref[...]
    y = (_unpack_pairs(_load_row_tiles(rows.at[buf, 0], 0, tm)) * g[:, 0:1]
         + _unpack_pairs(_load_row_tiles(rows.at[buf, 1], 0, tm)) * g[:, 1:2])
    out_ref[...] = x1_ref[...] + g2_ref[0:1, :] * (_rms(y) * nw_ref[3:4, :])


def _combine(s0, s1, yb, gates, x1, norm_w, mod):
    n, d = x1.shape
    tm = 256
    grid_spec = pltpu.PrefetchScalarGridSpec(
        num_scalar_prefetch=2,
        grid=(n // tm,),
        in_specs=[
            pl.BlockSpec(memory_space=pl.ANY),
            pl.BlockSpec((tm, EXPERT_TOPK), lambda i, *_: (i, 0)),
            pl.BlockSpec((tm, d), lambda i, *_: (i, 0)),
            pl.BlockSpec((4, d), lambda i, *_: (0, 0)),
            pl.BlockSpec((8, d), lambda i, *_: (0, 5)),
        ],
        out_specs=pl.BlockSpec((tm, d), lambda i, *_: (i, 0)),
        scratch_shapes=[
            pltpu.VMEM((2, EXPERT_TOPK, tm * TILE_ROWS, LANES), jnp.uint32),
            pltpu.SemaphoreType.DMA((2,)),
        ],
    )
    return pl.pallas_call(
        _combine_kernel,
        grid_spec=grid_spec,
        out_shape=jax.ShapeDtypeStruct((n, d), F32),
        compiler_params=_cparams(1, 32),
        name="combine",
    )(s0, s1, yb, gates, x1, norm_w, mod)


def _rope_tables(n):
    rows = n // GRID_W
    r = np.repeat(np.arange(rows, dtype=np.float32), GRID_W)
    cidx = np.tile(np.arange(GRID_W, dtype=np.float32), rows)
    n_freq = RET_DK // 4
    inv = np.float32(ROPE_BASE) ** (-np.arange(n_freq, dtype=np.float32) / np.float32(n_freq))
    ang = np.concatenate([r[:, None] * inv, cidx[:, None] * inv], axis=-1).astype(np.float32)
    cos, sin = np.cos(ang), np.sin(ang)
    cs = np.concatenate([cos, cos], axis=-1)
    sn = np.concatenate([-sin, sin], axis=-1)
    return jnp.asarray(cs, F32), jnp.asarray(sn, F32)


def kernel(x, c, ctx, c_ctx, w_mod, b_mod, norm_w, w_in, hg_lb_logits, hg_norm_w, ret_decay_logits,
           ret_norm_w, w_out, router_group_w, router_group_b, router_expert_w, router_expert_b,
           w_gate, w_up, w_down):
    bsz, n, d = x.shape
    l = ctx.shape[1]
    assert bsz == 1 and w_mod.shape[0] == 1, "single sample, single layer"
    x2d, ctx2d = x[0], ctx[0]

    cc = jnp.zeros((8, d), F32).at[0].set(c[0]).at[1].set(c_ctx)
    mod = _modulation(cc, w_mod[0], b_mod[0])

    lb = jnp.cumsum(jax.nn.softmax(hg_lb_logits.astype(F32), axis=0), axis=0)[0]
    rk0 = 5 * HG_WIDTH + RET_HEADS * RET_DK
    cols = jnp.stack([
        jnp.zeros((PROJ_WIDTH,), F32).at[HG_WIDTH:3 * HG_WIDTH].set(lb.reshape(-1)),
        jnp.ones((PROJ_WIDTH,), F32).at[rk0:rk0 + RET_HEADS * RET_DK].set(RET_DK ** -0.5),
    ])
    w_in_bf = w_in[0].astype(BF16)
    p, g = _projection(x2d, norm_w[0], mod, w_in_bf, cols, 0, 1024, "proj_x")
    pc, gc = _projection(ctx2d, norm_w[0], mod, w_in_bf, cols, 1, l, "proj_ctx")

    ret_lg = jax.nn.log_sigmoid(ret_decay_logits[0].astype(F32))
    cs_tab, sn_tab = _rope_tables(n)

    o_f, o_b = _hgrn_scan(p, g, pc, gc)
    r_f, r_b = _ret_scan(p, pc, ret_lg, cs_tab, sn_tab)

    wr = jnp.concatenate([router_group_w[0], router_expert_w[0]], axis=1)
    wr = jnp.pad(wr, ((0, 0), (0, ROUTER_LANES - wr.shape[1])))
    wr_hi = wr.astype(BF16)
    wr_parts = jnp.concatenate([wr_hi, (wr - wr_hi.astype(F32)).astype(BF16)], axis=1)
    br = jnp.concatenate([router_group_b[0], router_expert_b[0]]).astype(F32)
    br = jnp.pad(br, (0, ROUTER_LANES - br.shape[0])).reshape(1, ROUTER_LANES)
    x1, h2, logits = _out_projection(
        o_f, o_b, r_f, r_b, p, x2d, w_out[0].astype(BF16), hg_norm_w, ret_norm_w, norm_w[0], mod,
        wr_parts, br)

    n_assign = n * EXPERT_TOPK
    n_slots = n_assign + N_EXPERTS * MOE_PAD
    n_items = (n_slots + N_EXPERTS * (MOE_ITEM_ROWS - MOE_PAD)) // MOE_ITEM_ROWS
    gates, s0, s1, counts = _router(logits)
    s0, s1 = s0.reshape(-1), s1.reshape(-1)
    slot_tok, item_e, item_start, item_rows, item_fill = _plan(counts, s0, s1, n_slots, n_items)

    yb = _moe_ffn(h2, w_gate[0], w_up[0], w_down[0], item_e, item_start, item_rows, item_fill,
                  slot_tok, n_slots)
    out = _combine(s0, s1, yb, gates, x1, norm_w[0], mod)
    return out[None]
```

```python
import functools

import jax
import jax.numpy as jnp
import numpy as np
from jax import lax
from jax.experimental import pallas as pl
from jax.experimental.pallas import tpu as pltpu

F32 = jnp.float32
BF16 = jnp.bfloat16

D_MODEL = 2048
GRID_W = 64
HG_HEADS = 8
HG_DK = 128
HG_WIDTH = 1024
RET_HEADS = 4
RET_DK = 128
RET_DV = 256
RET_WIDTH = 1024
PROJ_WIDTH = 8192
N_GROUPS = 4
EXPERTS_PER_GROUP = 8
N_EXPERTS = 32
EXPERT_TOPK = 2
D_EXPERT = 1024
RMS_EPS = 1e-6
ROPE_BASE = 10000.0

COL_HQ, COL_HF_FWD, COL_HF_BWD, COL_HI, COL_HGATE = 0, 1, 2, 3, 4
COL_RQ_512, COL_RK_512 = 10, 11
COL_RV, COL_RGATE = 6, 7

HG_CHUNK = 128
HG_DIAG = 16
HG_LOG_CLAMP = 10.0
RET_CHUNK = 256

MOE_PAD = 128
MOE_BLOCK = 256
MOE_ITEM_ROWS = 1024
MOE_GATHER_GROUP = 128
ROUTER_LANES = 128
LANES = 128
TILE_ROWS = 8


def _cparams(n_axes, vmem_mb):
    return pltpu.CompilerParams(
        dimension_semantics=("arbitrary",) * n_axes,
        vmem_limit_bytes=vmem_mb * 1024 * 1024,
    )


def _sigmoid(x):
    return 1.0 / (1.0 + jnp.exp(-x))


def _dot(a, b):
    return jnp.dot(a, b, preferred_element_type=F32)


def _dot_nt(a, b):
    return lax.dot_general(a, b, (((1,), (1,)), ((), ())), preferred_element_type=F32)


def _dot_tn(a, b):
    return lax.dot_general(a, b, (((0,), (0,)), ((), ())), preferred_element_type=F32)


def _mod_kernel(cc_ref, w_ref, b_ref, o_ref):
    s = cc_ref[...]
    s = s * _sigmoid(s)
    s16 = jnp.concatenate([s, s], axis=0)
    hi = s16.astype(BF16).astype(F32)
    row = lax.broadcasted_iota(jnp.int32, s16.shape, 0)
    lhs = jnp.where(row < 8, hi, s16 - hi).astype(BF16)
    r = _dot(lhs, w_ref[...].astype(BF16))
    o_ref[...] = r[:8] + r[8:] + b_ref[...]


def _modulation(cc, w_mod, b_mod):
    d, n = w_mod.shape
    tn = 1024
    return pl.pallas_call(
        _mod_kernel,
        grid=(n // tn,),
        in_specs=[
            pl.BlockSpec((8, d), lambda j: (0, 0)),
            pl.BlockSpec((d, tn), lambda j: (0, j)),
            pl.BlockSpec((1, tn), lambda j: (0, j)),
        ],
        out_specs=pl.BlockSpec((8, tn), lambda j: (0, j)),
        out_shape=jax.ShapeDtypeStruct((8, n), F32),
        compiler_params=_cparams(1, 40),
        name="mod",
    )(cc, w_mod, b_mod.reshape(1, n))


PROJ_TN = 1024
PROLOGUE_ROWS = 32


def _proj_kernel(x_ref, nw_ref, sh_ref, sc_ref, w_ref, cols_ref, o_ref, g_ref, h_scr, *, mod_row):
    j = pl.program_id(1)

    @pl.when(j == 0)
    def _():
        scale = nw_ref[0:1, :] * (1.0 + sc_ref[mod_row:mod_row + 1, :])
        shift = sh_ref[mod_row:mod_row + 1, :]

        def body(r, carry):
            r0 = pl.multiple_of(r * PROLOGUE_ROWS, PROLOGUE_ROWS)
            x = x_ref[pl.ds(r0, PROLOGUE_ROWS), :]
            y = x * lax.rsqrt(jnp.mean(x * x, axis=-1, keepdims=True) + RMS_EPS)
            h_scr[pl.ds(r0, PROLOGUE_ROWS), :] = (y * scale + shift).astype(BF16)
            return carry
        lax.fori_loop(0, x_ref.shape[0] // PROLOGUE_ROWS, body, 0)

    is_gate = jnp.logical_or(j == 1, j == 2)
    is_silu = jnp.logical_or(j == 0, jnp.logical_or(j == 4, j == 7))
    is_lin = jnp.logical_not(jnp.logical_or(is_gate, is_silu))

    @pl.when(is_lin)
    def _():
        o_ref[...] = (_dot(h_scr[...], w_ref[...]) * cols_ref[1:2, :]).astype(BF16)

    @pl.when(is_silu)
    def _():
        acc = _dot(h_scr[...], w_ref[...])
        o_ref[...] = (acc * _sigmoid(acc)).astype(BF16)

    @pl.when(is_gate)
    def _():
        acc = _dot(h_scr[...], w_ref[...])
        lb = cols_ref[0:1, :]
        g = jnp.maximum(jnp.log(lb + (1.0 - lb) * _sigmoid(acc)), -HG_LOG_CLAMP)
        g_ref[...] = g
        o_ref[...] = g.astype(BF16)


def _projection(x2d, norm_w, mod, w_in, cols, mod_row, tm, name):
    n, d = x2d.shape
    tn = PROJ_TN
    return pl.pallas_call(
        functools.partial(_proj_kernel, mod_row=mod_row),
        grid=(n // tm, PROJ_WIDTH // tn),
        in_specs=[
            pl.BlockSpec((tm, d), lambda i, j: (i, 0)),
            pl.BlockSpec((4, d), lambda i, j: (0, 0)),
            pl.BlockSpec((8, d), lambda i, j: (0, 0)),
            pl.BlockSpec((8, d), lambda i, j: (0, 1)),
            pl.BlockSpec((d, tn), lambda i, j: (0, j)),
            pl.BlockSpec((2, tn), lambda i, j: (0, j)),
        ],
        out_specs=[
            pl.BlockSpec((tm, tn), lambda i, j: (i, j)),
            pl.BlockSpec((tm, tn), lambda i, j: (i, jnp.clip(j - 1, 0, 1))),
        ],
        out_shape=[
            jax.ShapeDtypeStruct((n, PROJ_WIDTH), BF16),
            jax.ShapeDtypeStruct((n, 2 * HG_WIDTH), F32),
        ],
        scratch_shapes=[pltpu.VMEM((tm, d), BF16)],
        compiler_params=_cparams(2, 52),
        name=name,
    )(x2d, norm_w, mod, mod, w_in, cols)


def _tri_ones(c, rev):
    t = lax.broadcasted_iota(jnp.int32, (c, c), 0)
    s = lax.broadcasted_iota(jnp.int32, (c, c), 1)
    return jnp.where((s >= t) if rev else (s <= t), 1.0, 0.0).astype(BF16)


def _cumsum_rows(g, tri):
    hi = g.astype(BF16)
    lo = (g - hi.astype(F32)).astype(BF16)
    return _dot(tri, hi) + _dot(tri, lo)


def _row_refs(b, idxs, blk):
    parts = [jnp.broadcast_to(b[i:i + 1, :], (blk, b.shape[1])) for i in idxs]
    return parts[0] if len(parts) == 1 else jnp.concatenate(parts, axis=0)


def _hgrn_levels(c, rev):
    t = lax.broadcasted_iota(jnp.int32, (c, c), 0)
    s = lax.broadcasted_iota(jnp.int32, (c, c), 1)
    levels = []
    m = c // 2
    while m >= HG_DIAG:
        blk = 2 * m
        same = ((t ^ s) & ~(blk - 1)) == 0
        if rev:
            idxs = [b0 + m for b0 in range(0, c, blk)]
            cross = jnp.logical_and((t & m) == 0, (s & m) != 0)
        else:
            idxs = [b0 + m - 1 for b0 in range(0, c, blk)]
            cross = jnp.logical_and((t & m) != 0, (s & m) == 0)
        levels.append((idxs, blk, jnp.logical_and(same, cross)))
        m //= 2
    blk = HG_DIAG
    same = ((t ^ s) & ~(blk - 1)) == 0
    if rev:
        idxs = [b0 + blk // 2 for b0 in range(0, c, blk)]
        mask = jnp.logical_and(same, s >= t)
    else:
        idxs = [b0 + blk // 2 - 1 for b0 in range(0, c, blk)]
        mask = jnp.logical_and(same, s <= t)
    levels.append((idxs, blk, mask))
    return levels


def _hgrn_head(q_bf, g, b, v_bf, st, rev, levels):
    c = g.shape[0]
    kk = 1.0 - jnp.exp(g)

    o = None
    if levels is not None:
        q = q_bf.astype(F32)
        att = jnp.zeros((c, c), F32)
        for idxs, blk, mask in levels:
            ref = _row_refs(b, idxs, blk)
            a_l = _dot_nt((q * jnp.exp(b - ref)).astype(BF16),
                          (kk * jnp.exp(ref - b)).astype(BF16))
            att = att + jnp.where(mask, a_l, 0.0)
        o = _dot(att.astype(BF16), v_bf) + _dot_nt((q * jnp.exp(b)).astype(BF16), st.astype(BF16))

    b_end = b[0:1, :] if rev else b[c - 1:c, :]
    k_end = (kk * jnp.exp(b_end - b)).astype(BF16)
    st_new = st * jnp.exp(b_end) + _dot_tn(v_bf, k_end)
    return o, st_new


def _hgrn_ctx_kernel(gf_ref, vf_ref, gb_ref, vb_ref, st_ref):
    @pl.when(pl.program_id(0) == 0)
    def _():
        st_ref[...] = jnp.zeros_like(st_ref)

    c = gf_ref.shape[0]
    bf = _cumsum_rows(gf_ref[...], _tri_ones(c, False))
    bb = _cumsum_rows(gb_ref[...], _tri_ones(c, True))
    for h in range(HG_HEADS):
        sl = slice(h * HG_DK, (h + 1) * HG_DK)
        _, st_ref[h] = _hgrn_head(None, gf_ref[:, sl], bf[:, sl], vf_ref[:, sl], st_ref[h],
                                  False, None)
        hb = HG_HEADS + h
        _, st_ref[hb] = _hgrn_head(None, gb_ref[:, sl], bb[:, sl], vb_ref[:, sl], st_ref[hb],
                                   True, None)


def _hgrn_kernel(qf_ref, gf_ref, vf_ref, qb_ref, gb_ref, vb_ref, st0_ref, of_ref, ob_ref, st_ref):
    @pl.when(pl.program_id(0) == 0)
    def _():
        st_ref[...] = st0_ref[...]

    c = gf_ref.shape[0]
    lev_f = _hgrn_levels(c, False)
    lev_b = _hgrn_levels(c, True)
    bf = _cumsum_rows(gf_ref[...], _tri_ones(c, False))
    bb = _cumsum_rows(gb_ref[...], _tri_ones(c, True))
    for h in range(HG_HEADS):
        sl = slice(h * HG_DK, (h + 1) * HG_DK)
        of_ref[:, sl], st_ref[h] = _hgrn_head(qf_ref[:, sl], gf_ref[:, sl], bf[:, sl],
                                              vf_ref[:, sl], st_ref[h], False, lev_f)
        hb = HG_HEADS + h
        ob_ref[:, sl], st_ref[hb] = _hgrn_head(qb_ref[:, sl], gb_ref[:, sl], bb[:, sl],
                                               vb_ref[:, sl], st_ref[hb], True, lev_b)


def _hgrn_scan(p, g, pc, gc):
    c = HG_CHUNK
    w = HG_WIDTH
    n, l = p.shape[0], pc.shape[0]
    n_x, n_ctx = n // c, l // c
    st_shape = (2 * HG_HEADS, HG_DK, HG_DK)

    def spec(chunk_fn, col):
        return pl.BlockSpec((c, w), lambda s: (chunk_fn(s), col))

    st0 = pl.pallas_call(
        _hgrn_ctx_kernel,
        grid=(n_ctx,),
        in_specs=[
            spec(lambda s: s, 0), spec(lambda s: s, COL_HI),
            spec(lambda s: n_ctx - 1 - s, 1), spec(lambda s: n_ctx - 1 - s, COL_HI),
        ],
        out_specs=pl.BlockSpec(st_shape, lambda s: (0, 0, 0)),
        out_shape=jax.ShapeDtypeStruct(st_shape, F32),
        compiler_params=_cparams(1, 32),
        name="hgrn_ctx",
    )(gc, pc, gc, pc)

    fwd = lambda s: s
    bwd = lambda s: n_x - 1 - s
    return pl.pallas_call(
        _hgrn_kernel,
        grid=(n_x,),
        in_specs=[
            spec(fwd, COL_HQ), spec(fwd, 0), spec(fwd, COL_HI),
            spec(bwd, COL_HQ), spec(bwd, 1), spec(bwd, COL_HI),
            pl.BlockSpec(st_shape, lambda s: (0, 0, 0)),
        ],
        out_specs=[spec(fwd, 0), spec(bwd, 0)],
        out_shape=[jax.ShapeDtypeStruct((n, w), F32)] * 2,
        scratch_shapes=[pltpu.VMEM(st_shape, F32)],
        compiler_params=_cparams(1, 32),
        name="hgrn",
    )(p, g, p, p, g, p, st0)


def _rope(x_bf, cs, sn):
    x = x_bf.astype(F32)
    return x * cs + pltpu.roll(x, RET_DK // 2, axis=1) * sn


def _ret_decays(lg, c, rev):
    t = lax.broadcasted_iota(jnp.int32, (c, c), 0)
    s = lax.broadcasted_iota(jnp.int32, (c, c), 1)
    diff = (s - t) if rev else (t - s)
    pair = jnp.where(diff >= 0, jnp.exp(lg * jnp.maximum(diff, 0).astype(F32)), 0.0)
    pos = lax.broadcasted_iota(jnp.int32, (c, RET_DK), 0).astype(F32)
    q_dec = jnp.exp(lg * ((c - pos) if rev else (pos + 1.0)))
    k_dec = jnp.exp(lg * (pos if rev else (c - 1.0 - pos)))
    chunk = jnp.exp(jnp.full((8, RET_DK), lg * c, F32))
    return pair, q_dec, k_dec, chunk


def _ret_head(q, k, v_bf, decays, st):
    pair, q_dec, k_dec, chunk = decays
    o = None
    if q is not None:
        att = _dot_nt(q.astype(BF16), k.astype(BF16)) * pair
        o = _dot(att.astype(BF16), v_bf) + _dot_nt((q * q_dec).astype(BF16), st.astype(BF16))
    st_new = st * chunk[0:1, :] + _dot_tn(v_bf, (k * k_dec).astype(BF16))
    return o, st_new


def _ret_ctx_kernel(lg_ref, kf_ref, vf_ref, kb_ref, vb_ref, st_ref):
    @pl.when(pl.program_id(0) == 0)
    def _():
        st_ref[...] = jnp.zeros_like(st_ref)

    c = kf_ref.shape[0]
    for h in range(RET_HEADS):
        sk = slice(h * RET_DK, (h + 1) * RET_DK)
        sv = slice(h * RET_DV, (h + 1) * RET_DV)
        _, st_ref[h] = _ret_head(None, kf_ref[:, sk].astype(F32), vf_ref[:, sv],
                                 _ret_decays(lg_ref[0, h], c, False), st_ref[h])
        hb = RET_HEADS + h
        _, st_ref[hb] = _ret_head(None, kb_ref[:, sk].astype(F32), vb_ref[:, sv],
                                  _ret_decays(lg_ref[1, h], c, True), st_ref[hb])


def _ret_kernel(lg_ref, qf_ref, kf_ref, vf_ref, csf_ref, snf_ref,
                qb_ref, kb_ref, vb_ref, csb_ref, snb_ref, st0_ref, of_ref, ob_ref,
                st_ref, pair_ref, qd_ref, kd_ref, cd_ref):
    c = qf_ref.shape[0]

    @pl.when(pl.program_id(0) == 0)
    def _():
        st_ref[...] = st0_ref[...]
        for d in range(2):
            for h in range(RET_HEADS):
                i = d * RET_HEADS + h
                pair_ref[i], qd_ref[i], kd_ref[i], cd_ref[i] = _ret_decays(lg_ref[d, h], c, d == 1)

    csf, snf = csf_ref[...], snf_ref[...]
    csb, snb = csb_ref[...], snb_ref[...]
    for h in range(RET_HEADS):
        sk = slice(h * RET_DK, (h + 1) * RET_DK)
        sv = slice(h * RET_DV, (h + 1) * RET_DV)
        of_ref[:, sv], st_ref[h] = _ret_head(
            _rope(qf_ref[:, sk], csf, snf), _rope(kf_ref[:, sk], csf, snf), vf_ref[:, sv],
            (pair_ref[h], qd_ref[h], kd_ref[h], cd_ref[h]), st_ref[h])
        hb = RET_HEADS + h
        ob_ref[:, sv], st_ref[hb] = _ret_head(
            _rope(qb_ref[:, sk], csb, snb), _rope(kb_ref[:, sk], csb, snb), vb_ref[:, sv],
            (pair_ref[hb], qd_ref[hb], kd_ref[hb], cd_ref[hb]), st_ref[hb])


def _ret_scan(p, pc, ret_lg, cs_tab, sn_tab):
    c = RET_CHUNK
    n, l = p.shape[0], pc.shape[0]
    n_x, n_ctx = n // c, l // c
    qk_w = RET_HEADS * RET_DK
    st_shape = (2 * RET_HEADS, RET_DV, RET_DK)
    smem = pl.BlockSpec(memory_space=pltpu.SMEM)

    def kv_specs(chunk_fn):
        return [
            pl.BlockSpec((c, qk_w), lambda s: (chunk_fn(s), COL_RK_512)),
            pl.BlockSpec((c, RET_WIDTH), lambda s: (chunk_fn(s), COL_RV)),
        ]

    st0 = pl.pallas_call(
        _ret_ctx_kernel,
        grid=(n_ctx,),
        in_specs=[smem, *kv_specs(lambda s: s), *kv_specs(lambda s: n_ctx - 1 - s)],
        out_specs=pl.BlockSpec(st_shape, lambda s: (0, 0, 0)),
        out_shape=jax.ShapeDtypeStruct(st_shape, F32),
        compiler_params=_cparams(1, 32),
        name="ret_ctx",
    )(ret_lg, pc, pc, pc, pc)

    def specs(chunk_fn):
        return [
            pl.BlockSpec((c, qk_w), lambda s: (chunk_fn(s), COL_RQ_512)),
            *kv_specs(chunk_fn),
            pl.BlockSpec((c, RET_DK), lambda s: (chunk_fn(s), 0)),
            pl.BlockSpec((c, RET_DK), lambda s: (chunk_fn(s), 0)),
        ]

    fwd = lambda s: s
    bwd = lambda s: n_x - 1 - s
    return pl.pallas_call(
        _ret_kernel,
        grid=(n_x,),
        in_specs=[smem, *specs(fwd), *specs(bwd), pl.BlockSpec(st_shape, lambda s: (0, 0, 0))],
        out_specs=[
            pl.BlockSpec((c, RET_WIDTH), lambda s: (fwd(s), 0)),
            pl.BlockSpec((c, RET_WIDTH), lambda s: (bwd(s), 0)),
        ],
        out_shape=[jax.ShapeDtypeStruct((n, RET_WIDTH), F32)] * 2,
        scratch_shapes=[
            pltpu.VMEM(st_shape, F32),
            pltpu.VMEM((2 * RET_HEADS, c, c), F32),
            pltpu.VMEM((2 * RET_HEADS, c, RET_DK), F32),
            pltpu.VMEM((2 * RET_HEADS, c, RET_DK), F32),
            pltpu.VMEM((2 * RET_HEADS, 8, RET_DK), F32),
        ],
        compiler_params=_cparams(1, 32),
        name="ret",
    )(ret_lg, p, p, p, cs_tab, sn_tab, p, p, p, cs_tab, sn_tab, st0)


def _rms(y):
    return y * lax.rsqrt(jnp.mean(y * y, axis=-1, keepdims=True) + RMS_EPS)


def _outproj_kernel(of_ref, ob_ref, rf_ref, rb_ref, hgate_ref, rgate_ref, x_ref, wout_ref,
                    hgw_ref, rnw_ref, nw_ref, g1_ref, sh2_ref, sc2_ref, wr_ref, br_ref,
                    x1_ref, h2_ref, lg_ref, cat_scr):
    tm = x_ref.shape[0]
    n_sub = 1
    for sub in range(n_sub):
        rows = slice(sub * tm // n_sub, (sub + 1) * tm // n_sub)
        for h in range(HG_HEADS):
            sl = slice(h * HG_DK, (h + 1) * HG_DK)
            o = of_ref[rows, sl] + ob_ref[rows, sl]
            cat_scr[rows, sl] = (_rms(o) * hgw_ref[0:1, sl]
                                 * hgate_ref[rows, sl].astype(F32)).astype(BF16)
        for h in range(RET_HEADS):
            sl = slice(h * RET_DV, (h + 1) * RET_DV)
            r = rf_ref[rows, sl] + rb_ref[rows, sl]
            cat_scr[rows, HG_WIDTH + h * RET_DV:HG_WIDTH + (h + 1) * RET_DV] = (
                _rms(r) * rnw_ref[0:1, sl] * rgate_ref[rows, sl].astype(F32)).astype(BF16)

        y = _dot(cat_scr[rows, :], wout_ref[...])
        x1 = x_ref[rows, :] + g1_ref[0:1, :] * (_rms(y) * nw_ref[1:2, :])
        x1_ref[rows, :] = x1
        h2 = _rms(x1) * nw_ref[2:3, :] * (1.0 + sc2_ref[0:1, :]) + sh2_ref[0:1, :]
        assert n_sub == 1
        _store_row_tiles(h2_ref, _pack_pairs(h2))
        hi = h2.astype(BF16)
        lo = (h2 - hi.astype(F32)).astype(BF16)
        parts = _dot(jnp.concatenate([hi, lo], axis=0), wr_ref[...])
        half = hi.shape[0]
        lg_ref[rows, :] = (parts[:half, :ROUTER_LANES] + parts[half:, :ROUTER_LANES]
                           + parts[:half, ROUTER_LANES:] + br_ref[...])


def _out_projection(o_f, o_b, r_f, r_b, p, x2d, w_out_bf, hg_norm_w, ret_norm_w, norm_w, mod,
                    wr_parts, br):
    n, d = x2d.shape
    tm = 256
    row = lambda w: pl.BlockSpec((tm, w), lambda i: (i, 0))
    full = lambda a: pl.BlockSpec(a.shape, lambda i: (0,) * a.ndim)
    modcol = lambda k: pl.BlockSpec((8, d), lambda i: (0, k))
    return pl.pallas_call(
        _outproj_kernel,
        grid=(n // tm,),
        in_specs=[
            row(HG_WIDTH), row(HG_WIDTH), row(RET_WIDTH), row(RET_WIDTH),
            pl.BlockSpec((tm, HG_WIDTH), lambda i: (i, COL_HGATE)),
            pl.BlockSpec((tm, RET_WIDTH), lambda i: (i, COL_RGATE)),
            row(d), full(w_out_bf), full(hg_norm_w), full(ret_norm_w), full(norm_w),
            modcol(2), modcol(3), modcol(4),
            full(wr_parts), full(br),
        ],
        out_specs=[row(d), pl.BlockSpec((tm * TILE_ROWS, LANES), lambda i: (i, 0)),
                   row(ROUTER_LANES)],
        out_shape=[
            jax.ShapeDtypeStruct((n, d), F32),
            jax.ShapeDtypeStruct((n * TILE_ROWS, LANES), jnp.uint32),
            jax.ShapeDtypeStruct((n, ROUTER_LANES), F32),
        ],
        scratch_shapes=[pltpu.VMEM((tm, d), BF16)],
        compiler_params=_cparams(1, 48),
        name="outproj",
    )(o_f, o_b, r_f, r_b, p, p, x2d, w_out_bf, hg_norm_w, ret_norm_w, norm_w, mod, mod, mod,
      wr_parts, br)


ROUTER_TM = 1024


def _lane_first(hit, lane_f):
    return jnp.min(jnp.where(hit, lane_f, 1e9), axis=1, keepdims=True)


def _router_kernel(lg_ref, gates_ref, s0_ref, s1_ref, cnt_ref, ltri, carry, pstart):
    ph = pl.program_id(0)
    i = pl.program_id(1)
    tm = lg_ref.shape[0]

    @pl.when(jnp.logical_and(ph == 0, i == 0))
    def _():
        t = lax.broadcasted_iota(jnp.int32, (tm, tm), 0)
        s = lax.broadcasted_iota(jnp.int32, (tm, tm), 1)
        ltri[...] = jnp.where(s < t, 1.0, 0.0).astype(BF16)

    @pl.when(jnp.logical_and(ph == 1, i == 0))
    def _():
        cnt = carry[...]
        padded = jnp.floor((cnt + (MOE_PAD - 1.0)) * (1.0 / MOE_PAD)) * MOE_PAD
        r = lax.broadcasted_iota(jnp.int32, (LANES, LANES), 0)
        c = lax.broadcasted_iota(jnp.int32, (LANES, LANES), 1)
        before = jnp.where(r < c, 1.0, 0.0).astype(BF16)
        pstart[...] = _dot(padded.astype(BF16), before)
        cnt_ref[...] = cnt.astype(jnp.int32)

    @pl.when(i == 0)
    def _():
        carry[...] = jnp.zeros_like(carry)

    lg = lg_ref[...]
    lane = lax.broadcasted_iota(jnp.int32, lg.shape, 1)
    lane_f = lane.astype(F32)
    neg = -jnp.inf

    gl = jnp.where(lane < N_GROUPS, lg, neg)
    gmax = jnp.max(gl, axis=1, keepdims=True)
    grp = _lane_first(gl == gmax, lane_f).astype(jnp.int32)
    p_sel = 1.0 / jnp.sum(jnp.exp(gl - gmax), axis=1, keepdims=True)

    in_grp = jnp.logical_and(
        jnp.logical_and(lane >= N_GROUPS, lane < N_GROUPS + N_EXPERTS),
        ((lane - N_GROUPS) >> 3) == grp)
    el = jnp.where(in_grp, lg, neg)
    v1 = jnp.max(el, axis=1, keepdims=True)
    o1 = lane_f == _lane_first(el == v1, lane_f)
    el2 = jnp.where(o1, neg, el)
    v2 = jnp.max(el2, axis=1, keepdims=True)
    o2 = lane_f == _lane_first(el2 == v2, lane_f)

    osum = (jnp.where(o1, 1.0, 0.0) + jnp.where(o2, 1.0, 0.0)).astype(BF16)
    earlier = _dot(ltri[...], osum) + carry[0:1, :]
    carry[...] = carry[...] + _dot(jnp.ones((8, tm), BF16), osum)

    @pl.when(ph == 1)
    def _():
        ez = jnp.exp(v2 - v1)
        g1 = p_sel / (1.0 + ez)
        gates_ref[...] = jnp.concatenate([g1, g1 * ez], axis=1)
        base = earlier + pstart[0:1, :]
        for o, s_ref in ((o1, s0_ref), (o2, s1_ref)):
            col = jnp.sum(jnp.where(o, base, 0.0), axis=1, keepdims=True)
            wide = jnp.broadcast_to(col, (tm, LANES))
            for b in range(tm // LANES):
                s_ref[b:b + 1, :] = wide[b * LANES:(b + 1) * LANES, :].T[0:1, :].astype(jnp.int32)


def _router(logits):
    t = logits.shape[0]
    tm = ROUTER_TM
    rows = tm // LANES
    return pl.pallas_call(
        _router_kernel,
        grid=(2, t // tm),
        in_specs=[pl.BlockSpec((tm, LANES), lambda ph, i: (i, 0))],
        out_specs=[
            pl.BlockSpec((tm, EXPERT_TOPK), lambda ph, i: (i * ph, 0)),
            pl.BlockSpec((rows, LANES), lambda ph, i: (i * ph, 0)),
            pl.BlockSpec((rows, LANES), lambda ph, i: (i * ph, 0)),
            pl.BlockSpec((8, LANES), lambda ph, i: (0, 0)),
        ],
        out_shape=[
            jax.ShapeDtypeStruct((t, EXPERT_TOPK), F32),
            jax.ShapeDtypeStruct((t // LANES, LANES), jnp.int32),
            jax.ShapeDtypeStruct((t // LANES, LANES), jnp.int32),
            jax.ShapeDtypeStruct((8, LANES), jnp.int32),
        ],
        scratch_shapes=[
            pltpu.VMEM((tm, tm), BF16),
            pltpu.VMEM((8, LANES), F32),
            pltpu.VMEM((8, LANES), F32),
        ],
        compiler_params=_cparams(2, 32),
        name="router",
    )(logits)


def _round_up_pow2(x, m):
    assert m & (m - 1) == 0
    return (x + (m - 1)) & ~(m - 1)


def _plan_kernel(cnt_ref, s0_ref, s1_ref, tok_ref, ie_ref, ist_ref, ir_ref, if_ref, zeros, sem):
    n_tok = s0_ref.shape[0]
    n_slots = tok_ref.shape[0]
    n_items = ie_ref.shape[0]
    r = MOE_ITEM_ROWS

    zeros[...] = jnp.zeros_like(zeros)
    clear = pltpu.make_async_copy(zeros, tok_ref, sem)
    clear.start()
    clear.wait()

    def scatter(t16, carry):
        for d in range(16):
            t = t16 * 16 + d
            tok_ref[s0_ref[t]] = t
            tok_ref[s1_ref[t]] = t
        return carry
    lax.fori_loop(0, n_tok // 16, scatter, 0)

    a = jnp.int32(0)
    used = jnp.int32(0)
    e_last = jnp.int32(0)
    for e in range(N_EXPERTS):
        pad = _round_up_pow2(cnt_ref[0, N_GROUPS + e], MOE_PAD)

        def item(sub, a, e=e, pad=pad, used=used):
            ie_ref[a] = e
            ist_ref[a] = used + sub * r
            ir_ref[a] = jnp.minimum(pad - sub * r, r)
            if_ref[a] = 0
            return a + 1
        a = lax.fori_loop(0, _round_up_pow2(pad, r) >> (r.bit_length() - 1), item, a)
        e_last = jnp.where(pad > 0, e, e_last)
        used = used + pad

    def idle(a2, carry):
        fill_start = used + (a2 - a) * r
        fill = jnp.clip(n_slots - fill_start, 0, r)
        ie_ref[a2] = e_last
        ist_ref[a2] = jnp.where(fill > 0, fill_start, 0)
        ir_ref[a2] = 0
        if_ref[a2] = fill
        return carry
    lax.fori_loop(a, n_items, idle, 0)


def _plan(counts, s0, s1, n_slots, n_items):
    smem = pl.BlockSpec(memory_space=pltpu.SMEM)
    item = jax.ShapeDtypeStruct((n_items,), jnp.int32)
    return pl.pallas_call(
        _plan_kernel,
        in_specs=[smem, smem, smem],
        out_specs=[smem] * 5,
        out_shape=[jax.ShapeDtypeStruct((n_slots,), jnp.int32), item, item, item, item],
        scratch_shapes=[pltpu.VMEM((n_slots,), jnp.int32), pltpu.SemaphoreType.DMA(())],
        name="plan",
    )(counts, s0, s1)


def _unpack_pairs(u):
    lo = lax.bitcast_convert_type(u << 16, F32)
    hi = lax.bitcast_convert_type(u & jnp.uint32(0xFFFF0000), F32)
    return jnp.concatenate([lo, hi], axis=1)


def _store_row_tiles(dst, packed):
    r = packed.shape[0]
    for s in range(TILE_ROWS):
        dst[pl.ds(s, r, stride=TILE_ROWS), :] = packed[:, s * LANES:(s + 1) * LANES]


def _load_row_tiles(src, first_row, r):
    return jnp.concatenate(
        [src[pl.ds(first_row * TILE_ROWS + s, r, stride=TILE_ROWS), :] for s in range(TILE_ROWS)],
        axis=1)


def _pack_pairs(y):
    w = y.shape[1] // 2
    bits = lax.bitcast_convert_type(y.astype(BF16).astype(F32), jnp.uint32)
    return (bits[:, :w] >> 16) | (bits[:, w:] & jnp.uint32(0xFFFF0000))


def _moe_kernel(item_e, item_start, item_rows, item_fill, slot_tok,
                h2p_hbm, wg_ref, wu_ref, wd_ref, yb_hbm,
                xbuf, xs_bf, act, ystage, pf, gsem, osem):
    a = pl.program_id(0)
    t = pl.program_id(1)
    n_items = pl.num_programs(0)
    n = item_rows[a]
    start = item_start[a]
    n_full = n // MOE_BLOCK
    n_tail = n - n_full * MOE_BLOCK
    half_w = wg_ref.shape[2]
    cur = a % 2
    nxt = 1 - cur
    a1 = jnp.minimum(a + 1, n_items - 1)
    n_next = jnp.where(a + 1 < n_items, item_rows[a1], 0)
    start_next = item_start[a1]

    def tile_rows(i, count=1):
        return pl.ds(pl.multiple_of(i * TILE_ROWS, TILE_ROWS), count * TILE_ROWS)

    def row_copy(tok, slot, i):
        return pltpu.make_async_copy(h2p_hbm.at[tile_rows(tok), :], xbuf.at[slot, tile_rows(i), :],
                                     gsem.at[slot])

    def start_row(tok, slot, i):
        row_copy(tok, slot, i).start(priority=1)

    @pl.when(jnp.logical_and(a == 0, t == 0))
    def _():
        pf[1] = 0
        pf[2] = 0

        def issue(i, carry):
            start_row(slot_tok[start + i], 0, i)
            return carry
        lax.fori_loop(0, n, issue, 0)

    @pl.when(t == 0)
    def _():
        pf[0] = 0

        def drain(bk, carry):
            r0 = pl.multiple_of(bk * MOE_PAD, MOE_PAD)
            pltpu.make_async_copy(h2p_hbm.at[tile_rows(0, MOE_PAD), :],
                                  xbuf.at[cur, tile_rows(r0, MOE_PAD), :], gsem.at[cur]).wait()
            return carry
        lax.fori_loop(0, n // MOE_PAD, drain, 0)

    def prefetch_group():
        base = pf[0]
        for q in range(MOE_GATHER_GROUP):
            start_row(slot_tok[start_next + base + q], nxt, base + q)
        pf[0] = base + MOE_GATHER_GROUP

    def run_blocks(compute):
        def one(b, rows):
            r0 = pl.multiple_of(b * MOE_BLOCK, MOE_BLOCK)
            more = pf[0] < n_next

            @pl.when(more)
            def _():
                compute(b, r0, rows, prefetch_group)

            @pl.when(jnp.logical_not(more))
            def _():
                compute(b, r0, rows, lambda: None)

        def body(b, carry):
            one(b, MOE_BLOCK)
            return carry
        lax.fori_loop(0, n_full, body, 0)

        @pl.when(n_tail > 0)
        def _():
            one(n_full, MOE_PAD)

    def gate_up(half):
        def compute(b, r0, rows, mid):
            if half == 0:
                xs = _unpack_pairs(_load_row_tiles(xbuf.at[cur], r0, rows)).astype(BF16)
                xs_bf[pl.ds(r0, rows), :] = xs
            else:
                xs = xs_bf[pl.ds(r0, rows), :]
            g = _dot(xs, wg_ref[0].astype(BF16))
            u = _dot(xs, wu_ref[0].astype(BF16))
            mid()
            act[pl.ds(r0, rows), half * half_w:(half + 1) * half_w] = (
                g * _sigmoid(g) * u).astype(BF16)
        run_blocks(compute)

    def out_copy(slot, r0, rows):
        return pltpu.make_async_copy(
            ystage.at[slot, tile_rows(0, rows), :],
            yb_hbm.at[tile_rows(pl.multiple_of(start + r0, MOE_PAD), rows), :],
            osem.at[slot])

    def wait_stage(slot):
        for rows in (MOE_BLOCK, MOE_PAD):
            @pl.when(pf[1 + slot] == rows)
            def _(rows=rows):
                out_copy(slot, 0, rows).wait()
                pf[1 + slot] = 0

    def down():
        def compute(b, r0, rows, mid):
            slot = b % 2
            wait_stage(slot)
            y = _dot(act[pl.ds(r0, rows), :], wd_ref[0].astype(BF16))
            mid()
            _store_row_tiles(ystage.at[slot], _pack_pairs(y))
            out_copy(slot, r0, rows).start()
            pf[1 + slot] = rows
        run_blocks(compute)

        def tail(i, carry):
            start_row(slot_tok[start_next + i], nxt, i)
            return carry
        lax.fori_loop(pf[0], n_next, tail, 0)

    pl.when(t == 0)(functools.partial(gate_up, 0))
    pl.when(t == 1)(functools.partial(gate_up, 1))
    pl.when(t == 2)(down)

    fill = item_fill[a]

    @pl.when(jnp.logical_and(t == 0, fill > 0))
    def _():
        wait_stage(0)
        ystage[0] = jnp.zeros(ystage.shape[1:], ystage.dtype)

        def zero_block(bk, carry):
            cp = out_copy(0, bk * MOE_PAD, MOE_PAD)
            cp.start()
            cp.wait()
            return carry
        lax.fori_loop(0, fill // MOE_PAD, zero_block, 0)

    @pl.when(jnp.logical_and(a == n_items - 1, t == pl.num_programs(1) - 1))
    def _():
        wait_stage(0)
        wait_stage(1)


def _moe_ffn(h2p, w_gate, w_up, w_down, item_e, item_start, item_rows, item_fill, slot_tok,
             n_slots):
    d = w_gate.shape[1]
    assert d == 2 * TILE_ROWS * LANES and h2p.shape[1] == LANES, "one packed token row per tile"
    n_items = item_e.shape[0]
    half_w = D_EXPERT // 2

    def wgu_map(a, t, ie, ist, ir, fl, st):
        return (ie[a], 0, jnp.where(ir[a] > 0, jnp.minimum(t, 1), 1))

    def wd_map(a, t, ie, ist, ir, fl, st):
        return (jnp.where(t >= 2, ie[a], ie[jnp.maximum(a - 1, 0)]), 0, 0)

    grid_spec = pltpu.PrefetchScalarGridSpec(
        num_scalar_prefetch=5,
        grid=(n_items, 3),
        in_specs=[
            pl.BlockSpec(memory_space=pl.ANY),
            pl.BlockSpec((1, d, half_w), wgu_map),
            pl.BlockSpec((1, d, half_w), wgu_map),
            pl.BlockSpec((1, D_EXPERT, d), wd_map),
        ],
        out_specs=pl.BlockSpec(memory_space=pl.ANY),
        scratch_shapes=[
            pltpu.VMEM((2, MOE_ITEM_ROWS * TILE_ROWS, LANES), jnp.uint32),
            pltpu.VMEM((MOE_ITEM_ROWS, d), BF16),
            pltpu.VMEM((MOE_ITEM_ROWS, D_EXPERT), BF16),
            pltpu.VMEM((2, MOE_BLOCK * TILE_ROWS, LANES), jnp.uint32),
            pltpu.SMEM((3,), jnp.int32),
            pltpu.SemaphoreType.DMA((2,)),
            pltpu.SemaphoreType.DMA((2,)),
        ],
    )
    return pl.pallas_call(
        _moe_kernel,
        grid_spec=grid_spec,
        out_shape=jax.ShapeDtypeStruct((n_slots * TILE_ROWS, LANES), jnp.uint32),
        compiler_params=_cparams(2, 56),
        name="moe",
    )(item_e, item_start, item_rows, item_fill, slot_tok, h2p, w_gate, w_up, w_down)


def _combine_kernel(s0, s1, yb_hbm, gates_ref, x1_ref, nw_ref, g2_ref, out_ref, rows, sem):
    i = pl.program_id(0)
    n_tiles = pl.num_programs(0)
    tm = x1_ref.shape[0]

    def tile_rows(i, count=1):
        return pl.ds(pl.multiple_of(i * TILE_ROWS, TILE_ROWS), count * TILE_ROWS)

    def gather(tile, buf):
        def issue(r8, carry):
            t0 = tile * tm + r8 * 8
            for dr in range(8):
                for k, s_k in enumerate((s0, s1)):
                    pltpu.make_async_copy(yb_hbm.at[tile_rows(s_k[t0 + dr]), :],
                                          rows.at[buf, k, tile_rows(r8 * 8 + dr), :],
                                          sem.at[buf]).start()
            return carry
        lax.fori_loop(0, tm // 8, issue, 0)

    @pl.when(i == 0)
    def _():
        gather(0, 0)

    @pl.when(i + 1 < n_tiles)
    def _():
        gather(i + 1, (i + 1) % 2)

    buf = i % 2
    for k in range(EXPERT_TOPK):
        pltpu.make_async_copy(yb_hbm.at[tile_rows(0, tm), :], rows.at[buf, k], sem.at[buf]).wait()

    g = gates_ref[...]
    y = (_unpack_pairs(_load_row_tiles(rows.at[buf, 0], 0, tm)) * g[:, 0:1]
         + _unpack_pairs(_load_row_tiles(rows.at[buf, 1], 0, tm)) * g[:, 1:2])
    out_ref[...] = x1_ref[...] + g2_ref[0:1, :] * (_rms(y) * nw_ref[3:4, :])


def _combine(s0, s1, yb, gates, x1, norm_w, mod):
    n, d = x1.shape
    tm = 256
    grid_spec = pltpu.PrefetchScalarGridSpec(
        num_scalar_prefetch=2,
        grid=(n // tm,),
        in_specs=[
            pl.BlockSpec(memory_space=pl.ANY),
            pl.BlockSpec((tm, EXPERT_TOPK), lambda i, *_: (i, 0)),
            pl.BlockSpec((tm, d), lambda i, *_: (i, 0)),
            pl.BlockSpec((4, d), lambda i, *_: (0, 0)),
            pl.BlockSpec((8, d), lambda i, *_: (0, 5)),
        ],
        out_specs=pl.BlockSpec((tm, d), lambda i, *_: (i, 0)),
        scratch_shapes=[
            pltpu.VMEM((2, EXPERT_TOPK, tm * TILE_ROWS, LANES), jnp.uint32),
            pltpu.SemaphoreType.DMA((2,)),
        ],
    )
    return pl.pallas_call(
        _combine_kernel,
        grid_spec=grid_spec,
        out_shape=jax.ShapeDtypeStruct((n, d), F32),
        compiler_params=_cparams(1, 32),
        name="combine",
    )(s0, s1, yb, gates, x1, norm_w, mod)


def _rope_tables(n):
    rows = n // GRID_W
    r = np.repeat(np.arange(rows, dtype=np.float32), GRID_W)
    cidx = np.tile(np.arange(GRID_W, dtype=np.float32), rows)
    n_freq = RET_DK // 4
    inv = np.float32(ROPE_BASE) ** (-np.arange(n_freq, dtype=np.float32) / np.float32(n_freq))
    ang = np.concatenate([r[:, None] * inv, cidx[:, None] * inv], axis=-1).astype(np.float32)
    cos, sin = np.cos(ang), np.sin(ang)
    cs = np.concatenate([cos, cos], axis=-1)
    sn = np.concatenate([-sin, sin], axis=-1)
    return jnp.asarray(cs, F32), jnp.asarray(sn, F32)


def kernel(x, c, ctx, c_ctx, w_mod, b_mod, norm_w, w_in, hg_lb_logits, hg_norm_w, ret_decay_logits,
           ret_norm_w, w_out, router_group_w, router_group_b, router_expert_w, router_expert_b,
           w_gate, w_up, w_down):
    bsz, n, d = x.shape
    l = ctx.shape[1]
    assert bsz == 1 and w_mod.shape[0] == 1, "single sample, single layer"
    x2d, ctx2d = x[0], ctx[0]

    cc = jnp.zeros((8, d), F32).at[0].set(c[0]).at[1].set(c_ctx)
    mod = _modulation(cc, w_mod[0], b_mod[0])

    lb = jnp.cumsum(jax.nn.softmax(hg_lb_logits.astype(F32), axis=0), axis=0)[0]
    rk0 = 5 * HG_WIDTH + RET_HEADS * RET_DK
    cols = jnp.stack([
        jnp.zeros((PROJ_WIDTH,), F32).at[HG_WIDTH:3 * HG_WIDTH].set(lb.reshape(-1)),
        jnp.ones((PROJ_WIDTH,), F32).at[rk0:rk0 + RET_HEADS * RET_DK].set(RET_DK ** -0.5),
    ])
    w_in_bf = w_in[0].astype(BF16)
    p, g = _projection(x2d, norm_w[0], mod, w_in_bf, cols, 0, 1024, "proj_x")
    pc, gc = _projection(ctx2d, norm_w[0], mod, w_in_bf, cols, 1, l, "proj_ctx")

    ret_lg = jax.nn.log_sigmoid(ret_decay_logits[0].astype(F32))
    cs_tab, sn_tab = _rope_tables(n)

    o_f, o_b = _hgrn_scan(p, g, pc, gc)
    r_f, r_b = _ret_scan(p, pc, ret_lg, cs_tab, sn_tab)

    wr = jnp.concatenate([router_group_w[0], router_expert_w[0]], axis=1)
    wr = jnp.pad(wr, ((0, 0), (0, ROUTER_LANES - wr.shape[1])))
    wr_hi = wr.astype(BF16)
    wr_parts = jnp.concatenate([wr_hi, (wr - wr_hi.astype(F32)).astype(BF16)], axis=1)
    br = jnp.concatenate([router_group_b[0], router_expert_b[0]]).astype(F32)
    br = jnp.pad(br, (0, ROUTER_LANES - br.shape[0])).reshape(1, ROUTER_LANES)
    x1, h2, logits = _out_projection(
        o_f, o_b, r_f, r_b, p, x2d, w_out[0].astype(BF16), hg_norm_w, ret_norm_w, norm_w[0], mod,
        wr_parts, br)

    n_assign = n * EXPERT_TOPK
    n_slots = n_assign + N_EXPERTS * MOE_PAD
    n_items = (n_slots + N_EXPERTS * (MOE_ITEM_ROWS - MOE_PAD)) // MOE_ITEM_ROWS
    gates, s0, s1, counts = _router(logits)
    s0, s1 = s0.reshape(-1), s1.reshape(-1)
    slot_tok, item_e, item_start, item_rows, item_fill = _plan(counts, s0, s1, n_slots, n_items)

    yb = _moe_ffn(h2, w_gate[0], w_up[0], w_down[0], item_e, item_start, item_rows, item_fill,
                  slot_tok, n_slots)
    out = _combine(s0, s1, yb, gates, x1, norm_w[0], mod)
    return out[None]
```

```python
import functools

import jax
import jax.numpy as jnp
import numpy as np
from jax import lax
from jax.experimental import pallas as pl
from jax.experimental.pallas import tpu as pltpu

F32 = jnp.float32
BF16 = jnp.bfloat16

D_MODEL = 2048
GRID_W = 64
HG_HEADS = 8
HG_DK = 128
HG_WIDTH = 1024
RET_HEADS = 4
RET_DK = 128
RET_DV = 256
RET_WIDTH = 1024
PROJ_WIDTH = 8192
N_GROUPS = 4
EXPERTS_PER_GROUP = 8
N_EXPERTS = 32
EXPERT_TOPK = 2
D_EXPERT = 1024
RMS_EPS = 1e-6
ROPE_BASE = 10000.0

COL_HQ, COL_HF_FWD, COL_HF_BWD, COL_HI, COL_HGATE = 0, 1, 2, 3, 4
COL_RQ_512, COL_RK_512 = 10, 11
COL_RV, COL_RGATE = 6, 7

HG_CHUNK = 128
HG_DIAG = 16
HG_LOG_CLAMP = 10.0
RET_CHUNK = 256

MOE_PAD = 128
MOE_BLOCK = 256
MOE_ITEM_ROWS = 1024
MOE_GATHER_GROUP = 128
ROUTER_LANES = 128
LANES = 128
TILE_ROWS = 8


def _cparams(n_axes, vmem_mb):
    return pltpu.CompilerParams(
        dimension_semantics=("arbitrary",) * n_axes,
        vmem_limit_bytes=vmem_mb * 1024 * 1024,
    )


def _sigmoid(x):
    return 1.0 / (1.0 + jnp.exp(-x))


def _dot(a, b):
    return jnp.dot(a, b, preferred_element_type=F32)


def _dot_nt(a, b):
    return lax.dot_general(a, b, (((1,), (1,)), ((), ())), preferred_element_type=F32)


def _dot_tn(a, b):
    return lax.dot_general(a, b, (((0,), (0,)), ((), ())), preferred_element_type=F32)


def _mod_kernel(cc_ref, w_ref, b_ref, o_ref):
    s = cc_ref[...]
    s = s * _sigmoid(s)
    s16 = jnp.concatenate([s, s], axis=0)
    hi = s16.astype(BF16).astype(F32)
    row = lax.broadcasted_iota(jnp.int32, s16.shape, 0)
    lhs = jnp.where(row < 8, hi, s16 - hi).astype(BF16)
    r = _dot(lhs, w_ref[...].astype(BF16))
    o_ref[...] = r[:8] + r[8:] + b_ref[...]


def _modulation(cc, w_mod, b_mod):
    d, n = w_mod.shape
    tn = 1024
    return pl.pallas_call(
        _mod_kernel,
        grid=(n // tn,),
        in_specs=[
            pl.BlockSpec((8, d), lambda j: (0, 0)),
            pl.BlockSpec((d, tn), lambda j: (0, j)),
            pl.BlockSpec((1, tn), lambda j: (0, j)),
        ],
        out_specs=pl.BlockSpec((8, tn), lambda j: (0, j)),
        out_shape=jax.ShapeDtypeStruct((8, n), F32),
        compiler_params=_cparams(1, 40),
        name="mod",
    )(cc, w_mod, b_mod.reshape(1, n))


PROJ_TN = 1024
PROLOGUE_ROWS = 32


def _proj_kernel(x_ref, nw_ref, sh_ref, sc_ref, w_ref, cols_ref, o_ref, g_ref, h_scr, *, mod_row):
    i = pl.program_id(0)
    j = pl.program_id(1)
    last_j = pl.num_programs(1) - 1
    cur = i % 2

    def normed(x):
        scale = nw_ref[0:1, :] * (1.0 + sc_ref[mod_row:mod_row + 1, :])
        shift = sh_ref[mod_row:mod_row + 1, :]
        y = x * lax.rsqrt(jnp.mean(x * x, axis=-1, keepdims=True) + RMS_EPS)
        return (y * scale + shift).astype(BF16)

    @pl.when(jnp.logical_and(i == 0, j == 0))
    def _():
        def body(r, carry):
            r0 = pl.multiple_of(r * PROLOGUE_ROWS, PROLOGUE_ROWS)
            h_scr[0, pl.ds(r0, PROLOGUE_ROWS), :] = normed(x_ref[pl.ds(r0, PROLOGUE_ROWS), :])
            return carry
        lax.fori_loop(0, x_ref.shape[0] // PROLOGUE_ROWS, body, 0)

    is_gate = jnp.logical_or(j == 1, j == 2)
    is_silu = jnp.logical_or(j == 0, j == 4)
    is_lin = jnp.logical_not(jnp.logical_or(jnp.logical_or(is_gate, is_silu), j == last_j))

    @pl.when(is_lin)
    def _():
        o_ref[...] = (_dot(h_scr[cur], w_ref[...]) * cols_ref[1:2, :]).astype(BF16)

    @pl.when(is_silu)
    def _():
        acc = _dot(h_scr[cur], w_ref[...])
        o_ref[...] = (acc * _sigmoid(acc)).astype(BF16)

    @pl.when(is_gate)
    def _():
        acc = _dot(h_scr[cur], w_ref[...])
        lb = cols_ref[0:1, :]
        g = jnp.maximum(jnp.log(lb + (1.0 - lb) * _sigmoid(acc)), -HG_LOG_CLAMP)
        g_ref[...] = g
        o_ref[...] = g.astype(BF16)

    @pl.when(j == last_j)
    def _():
        acc = _dot(h_scr[cur], w_ref[...])
        o_ref[...] = (acc * _sigmoid(acc)).astype(BF16)
        h_scr[1 - cur] = normed(x_ref[...])


def _projection(x2d, norm_w, mod, w_in, cols, mod_row, tm, name):
    n, d = x2d.shape
    tn = PROJ_TN
    n_i, n_j = n // tm, PROJ_WIDTH // tn

    def x_map(i, j):
        return (jnp.minimum(i + (j == n_j - 1).astype(jnp.int32), n_i - 1), 0)

    return pl.pallas_call(
        functools.partial(_proj_kernel, mod_row=mod_row),
        grid=(n_i, n_j),
        in_specs=[
            pl.BlockSpec((tm, d), x_map),
            pl.BlockSpec((4, d), lambda i, j: (0, 0)),
            pl.BlockSpec((8, d), lambda i, j: (0, 0)),
            pl.BlockSpec((8, d), lambda i, j: (0, 1)),
            pl.BlockSpec((d, tn), lambda i, j: (0, j)),
            pl.BlockSpec((2, tn), lambda i, j: (0, j)),
        ],
        out_specs=[
            pl.BlockSpec((tm, tn), lambda i, j: (i, j)),
            pl.BlockSpec((tm, tn), lambda i, j: (i, jnp.clip(j - 1, 0, 1))),
        ],
        out_shape=[
            jax.ShapeDtypeStruct((n, PROJ_WIDTH), BF16),
            jax.ShapeDtypeStruct((n, 2 * HG_WIDTH), F32),
        ],
        scratch_shapes=[pltpu.VMEM((2, tm, d), BF16)],
        compiler_params=_cparams(2, 56),
        name=name,
    )(x2d, norm_w, mod, mod, w_in, cols)


def _tri_ones(c, rev):
    t = lax.broadcasted_iota(jnp.int32, (c, c), 0)
    s = lax.broadcasted_iota(jnp.int32, (c, c), 1)
    return jnp.where((s >= t) if rev else (s <= t), 1.0, 0.0).astype(BF16)


def _cumsum_rows(g, tri):
    hi = g.astype(BF16)
    lo = (g - hi.astype(F32)).astype(BF16)
    return _dot(tri, hi) + _dot(tri, lo)


def _row_refs(b, idxs, blk):
    parts = [jnp.broadcast_to(b[i:i + 1, :], (blk, b.shape[1])) for i in idxs]
    return parts[0] if len(parts) == 1 else jnp.concatenate(parts, axis=0)


def _hgrn_levels(c, rev):
    t = lax.broadcasted_iota(jnp.int32, (c, c), 0)
    s = lax.broadcasted_iota(jnp.int32, (c, c), 1)
    levels = []
    m = c // 2
    while m >= HG_DIAG:
        blk = 2 * m
        same = ((t ^ s) & ~(blk - 1)) == 0
        if rev:
            idxs = [b0 + m for b0 in range(0, c, blk)]
            cross = jnp.logical_and((t & m) == 0, (s & m) != 0)
        else:
            idxs = [b0 + m - 1 for b0 in range(0, c, blk)]
            cross = jnp.logical_and((t & m) != 0, (s & m) == 0)
        levels.append((idxs, blk, jnp.logical_and(same, cross)))
        m //= 2
    blk = HG_DIAG
    same = ((t ^ s) & ~(blk - 1)) == 0
    if rev:
        idxs = [b0 + blk // 2 for b0 in range(0, c, blk)]
        mask = jnp.logical_and(same, s >= t)
    else:
        idxs = [b0 + blk // 2 - 1 for b0 in range(0, c, blk)]
        mask = jnp.logical_and(same, s <= t)
    levels.append((idxs, blk, mask))
    return levels


def _hgrn_head(q_bf, g, b, v_bf, st, rev, levels):
    c = g.shape[0]
    kk = 1.0 - jnp.exp(g)

    o = None
    if levels is not None:
        q = q_bf.astype(F32)
        att = jnp.zeros((c, c), F32)
        for idxs, blk, mask in levels:
            ref = _row_refs(b, idxs, blk)
            a_l = _dot_nt((q * jnp.exp(b - ref)).astype(BF16),
                          (kk * jnp.exp(ref - b)).astype(BF16))
            att = att + jnp.where(mask, a_l, 0.0)
        o = _dot(att.astype(BF16), v_bf) + _dot_nt((q * jnp.exp(b)).astype(BF16), st.astype(BF16))

    b_end = b[0:1, :] if rev else b[c - 1:c, :]
    k_end = (kk * jnp.exp(b_end - b)).astype(BF16)
    st_new = st * jnp.exp(b_end) + _dot_tn(v_bf, k_end)
    return o, st_new


def _hgrn_ctx_kernel(gf_ref, vf_ref, gb_ref, vb_ref, st_ref):
    @pl.when(pl.program_id(0) == 0)
    def _():
        st_ref[...] = jnp.zeros_like(st_ref)

    c = gf_ref.shape[0]
    bf = _cumsum_rows(gf_ref[...], _tri_ones(c, False))
    bb = _cumsum_rows(gb_ref[...], _tri_ones(c, True))
    for h in range(HG_HEADS):
        sl = slice(h * HG_DK, (h + 1) * HG_DK)
        _, st_ref[h] = _hgrn_head(None, gf_ref[:, sl], bf[:, sl], vf_ref[:, sl], st_ref[h],
                                  False, None)
        hb = HG_HEADS + h
        _, st_ref[hb] = _hgrn_head(None, gb_ref[:, sl], bb[:, sl], vb_ref[:, sl], st_ref[hb],
                                   True, None)


def _hgrn_kernel(qf_ref, gf_ref, vf_ref, qb_ref, gb_ref, vb_ref, st0_ref, of_ref, ob_ref, st_ref):
    @pl.when(pl.program_id(0) == 0)
    def _():
        st_ref[...] = st0_ref[...]

    c = gf_ref.shape[0]
    lev_f = _hgrn_levels(c, False)
    lev_b = _hgrn_levels(c, True)
    bf = _cumsum_rows(gf_ref[...], _tri_ones(c, False))
    bb = _cumsum_rows(gb_ref[...], _tri_ones(c, True))
    for h in range(HG_HEADS):
        sl = slice(h * HG_DK, (h + 1) * HG_DK)
        of_ref[:, sl], st_ref[h] = _hgrn_head(qf_ref[:, sl], gf_ref[:, sl], bf[:, sl],
                                              vf_ref[:, sl], st_ref[h], False, lev_f)
        hb = HG_HEADS + h
        ob_ref[:, sl], st_ref[hb] = _hgrn_head(qb_ref[:, sl], gb_ref[:, sl], bb[:, sl],
                                               vb_ref[:, sl], st_ref[hb], True, lev_b)


def _hgrn_scan(p, g, pc, gc):
    c = HG_CHUNK
    w = HG_WIDTH
    n, l = p.shape[0], pc.shape[0]
    n_x, n_ctx = n // c, l // c
    st_shape = (2 * HG_HEADS, HG_DK, HG_DK)

    def spec(chunk_fn, col):
        return pl.BlockSpec((c, w), lambda s: (chunk_fn(s), col))

    st0 = pl.pallas_call(
        _hgrn_ctx_kernel,
        grid=(n_ctx,),
        in_specs=[
            spec(lambda s: s, 0), spec(lambda s: s, COL_HI),
            spec(lambda s: n_ctx - 1 - s, 1), spec(lambda s: n_ctx - 1 - s, COL_HI),
        ],
        out_specs=pl.BlockSpec(st_shape, lambda s: (0, 0, 0)),
        out_shape=jax.ShapeDtypeStruct(st_shape, F32),
        compiler_params=_cparams(1, 32),
        name="hgrn_ctx",
    )(gc, pc, gc, pc)

    fwd = lambda s: s
    bwd = lambda s: n_x - 1 - s
    return pl.pallas_call(
        _hgrn_kernel,
        grid=(n_x,),
        in_specs=[
            spec(fwd, COL_HQ), spec(fwd, 0), spec(fwd, COL_HI),
            spec(bwd, COL_HQ), spec(bwd, 1), spec(bwd, COL_HI),
            pl.BlockSpec(st_shape, lambda s: (0, 0, 0)),
        ],
        out_specs=[spec(fwd, 0), spec(bwd, 0)],
        out_shape=[jax.ShapeDtypeStruct((n, w), F32)] * 2,
        scratch_shapes=[pltpu.VMEM(st_shape, F32)],
        compiler_params=_cparams(1, 32),
        name="hgrn",
    )(p, g, p, p, g, p, st0)


def _rope(x_bf, cs, sn):
    x = x_bf.astype(F32)
    return x * cs + pltpu.roll(x, RET_DK // 2, axis=1) * sn


def _ret_decays(lg, c, rev):
    t = lax.broadcasted_iota(jnp.int32, (c, c), 0)
    s = lax.broadcasted_iota(jnp.int32, (c, c), 1)
    diff = (s - t) if rev else (t - s)
    pair = jnp.where(diff >= 0, jnp.exp(lg * jnp.maximum(diff, 0).astype(F32)), 0.0)
    pos = lax.broadcasted_iota(jnp.int32, (c, RET_DK), 0).astype(F32)
    q_dec = jnp.exp(lg * ((c - pos) if rev else (pos + 1.0)))
    k_dec = jnp.exp(lg * (pos if rev else (c - 1.0 - pos)))
    chunk = jnp.exp(jnp.full((8, RET_DK), lg * c, F32))
    return pair, q_dec, k_dec, chunk


def _ret_head(q, k, v_bf, decays, st):
    pair, q_dec, k_dec, chunk = decays
    o = None
    if q is not None:
        att = _dot_nt(q.astype(BF16), k.astype(BF16)) * pair
        o = _dot(att.astype(BF16), v_bf) + _dot_nt((q * q_dec).astype(BF16), st.astype(BF16))
    st_new = st * chunk[0:1, :] + _dot_tn(v_bf, (k * k_dec).astype(BF16))
    return o, st_new


def _ret_ctx_kernel(lg_ref, kf_ref, vf_ref, kb_ref, vb_ref, st_ref):
    @pl.when(pl.program_id(0) == 0)
    def _():
        st_ref[...] = jnp.zeros_like(st_ref)

    c = kf_ref.shape[0]
    for h in range(RET_HEADS):
        sk = slice(h * RET_DK, (h + 1) * RET_DK)
        sv = slice(h * RET_DV, (h + 1) * RET_DV)
        _, st_ref[h] = _ret_head(None, kf_ref[:, sk].astype(F32), vf_ref[:, sv],
                                 _ret_decays(lg_ref[0, h], c, False), st_ref[h])
        hb = RET_HEADS + h
        _, st_ref[hb] = _ret_head(None, kb_ref[:, sk].astype(F32), vb_ref[:, sv],
                                  _ret_decays(lg_ref[1, h], c, True), st_ref[hb])


def _ret_kernel(lg_ref, qf_ref, kf_ref, vf_ref, csf_ref, snf_ref,
                qb_ref, kb_ref, vb_ref, csb_ref, snb_ref, st0_ref, of_ref, ob_ref,
                st_ref, pair_ref, qd_ref, kd_ref, cd_ref):
    c = qf_ref.shape[0]

    @pl.when(pl.program_id(0) == 0)
    def _():
        st_ref[...] = st0_ref[...]
        for d in range(2):
            for h in range(RET_HEADS):
                i = d * RET_HEADS + h
                pair_ref[i], qd_ref[i], kd_ref[i], cd_ref[i] = _ret_decays(lg_ref[d, h], c, d == 1)

    csf, snf = csf_ref[...], snf_ref[...]
    csb, snb = csb_ref[...], snb_ref[...]
    for h in range(RET_HEADS):
        sk = slice(h * RET_DK, (h + 1) * RET_DK)
        sv = slice(h * RET_DV, (h + 1) * RET_DV)
        of_ref[:, sv], st_ref[h] = _ret_head(
            _rope(qf_ref[:, sk], csf, snf), _rope(kf_ref[:, sk], csf, snf), vf_ref[:, sv],
            (pair_ref[h], qd_ref[h], kd_ref[h], cd_ref[h]), st_ref[h])
        hb = RET_HEADS + h
        ob_ref[:, sv], st_ref[hb] = _ret_head(
            _rope(qb_ref[:, sk], csb, snb), _rope(kb_ref[:, sk], csb, snb), vb_ref[:, sv],
            (pair_ref[hb], qd_ref[hb], kd_ref[hb], cd_ref[hb]), st_ref[hb])


def _ret_scan(p, pc, ret_lg, cs_tab, sn_tab):
    c = RET_CHUNK
    n, l = p.shape[0], pc.shape[0]
    n_x, n_ctx = n // c, l // c
    qk_w = RET_HEADS * RET_DK
    st_shape = (2 * RET_HEADS, RET_DV, RET_DK)
    smem = pl.BlockSpec(memory_space=pltpu.SMEM)

    def kv_specs(chunk_fn):
        return [
            pl.BlockSpec((c, qk_w), lambda s: (chunk_fn(s), COL_RK_512)),
            pl.BlockSpec((c, RET_WIDTH), lambda s: (chunk_fn(s), COL_RV)),
        ]

    st0 = pl.pallas_call(
        _ret_ctx_kernel,
        grid=(n_ctx,),
        in_specs=[smem, *kv_specs(lambda s: s), *kv_specs(lambda s: n_ctx - 1 - s)],
        out_specs=pl.BlockSpec(st_shape, lambda s: (0, 0, 0)),
        out_shape=jax.ShapeDtypeStruct(st_shape, F32),
        compiler_params=_cparams(1, 32),
        name="ret_ctx",
    )(ret_lg, pc, pc, pc, pc)

    def specs(chunk_fn):
        return [
            pl.BlockSpec((c, qk_w), lambda s: (chunk_fn(s), COL_RQ_512)),
            *kv_specs(chunk_fn),
            pl.BlockSpec((c, RET_DK), lambda s: (chunk_fn(s), 0)),
            pl.BlockSpec((c, RET_DK), lambda s: (chunk_fn(s), 0)),
        ]

    fwd = lambda s: s
    bwd = lambda s: n_x - 1 - s
    return pl.pallas_call(
        _ret_kernel,
        grid=(n_x,),
        in_specs=[smem, *specs(fwd), *specs(bwd), pl.BlockSpec(st_shape, lambda s: (0, 0, 0))],
        out_specs=[
            pl.BlockSpec((c, RET_WIDTH), lambda s: (fwd(s), 0)),
            pl.BlockSpec((c, RET_WIDTH), lambda s: (bwd(s), 0)),
        ],
        out_shape=[jax.ShapeDtypeStruct((n, RET_WIDTH), F32)] * 2,
        scratch_shapes=[
            pltpu.VMEM(st_shape, F32),
            pltpu.VMEM((2 * RET_HEADS, c, c), F32),
            pltpu.VMEM((2 * RET_HEADS, c, RET_DK), F32),
            pltpu.VMEM((2 * RET_HEADS, c, RET_DK), F32),
            pltpu.VMEM((2 * RET_HEADS, 8, RET_DK), F32),
        ],
        compiler_params=_cparams(1, 32),
        name="ret",
    )(ret_lg, p, p, p, cs_tab, sn_tab, p, p, p, cs_tab, sn_tab, st0)


def _rms(y):
    return y * lax.rsqrt(jnp.mean(y * y, axis=-1, keepdims=True) + RMS_EPS)


def _outproj_kernel(of_ref, ob_ref, rf_ref, rb_ref, hgate_ref, rgate_ref, x_ref, wout_ref,
                    hgw_ref, rnw_ref, nw_ref, g1_ref, sh2_ref, sc2_ref, wr_ref, br_ref,
                    x1_ref, h2_ref, lg_ref, cat_scr):
    tm = x_ref.shape[0]
    n_sub = 1
    for sub in range(n_sub):
        rows = slice(sub * tm // n_sub, (sub + 1) * tm // n_sub)
        for h in range(HG_HEADS):
            sl = slice(h * HG_DK, (h + 1) * HG_DK)
            o = of_ref[rows, sl] + ob_ref[rows, sl]
            cat_scr[rows, sl] = (_rms(o) * hgw_ref[0:1, sl]
                                 * hgate_ref[rows, sl].astype(F32)).astype(BF16)
        for h in range(RET_HEADS):
            sl = slice(h * RET_DV, (h + 1) * RET_DV)
            r = rf_ref[rows, sl] + rb_ref[rows, sl]
            cat_scr[rows, HG_WIDTH + h * RET_DV:HG_WIDTH + (h + 1) * RET_DV] = (
                _rms(r) * rnw_ref[0:1, sl] * rgate_ref[rows, sl].astype(F32)).astype(BF16)

        y = _dot(cat_scr[rows, :], wout_ref[...])
        x1 = x_ref[rows, :] + g1_ref[0:1, :] * (_rms(y) * nw_ref[1:2, :])
        x1_ref[rows, :] = x1
        h2 = _rms(x1) * nw_ref[2:3, :] * (1.0 + sc2_ref[0:1, :]) + sh2_ref[0:1, :]
        assert n_sub == 1
        _store_row_tiles(h2_ref, _pack_pairs(h2))
        hi = h2.astype(BF16)
        lo = (h2 - hi.astype(F32)).astype(BF16)
        parts = _dot(jnp.concatenate([hi, lo], axis=0), wr_ref[...])
        half = hi.shape[0]
        lg_ref[rows, :] = (parts[:half, :ROUTER_LANES] + parts[half:, :ROUTER_LANES]
                           + parts[:half, ROUTER_LANES:] + br_ref[...])


def _out_projection(o_f, o_b, r_f, r_b, p, x2d, w_out_bf, hg_norm_w, ret_norm_w, norm_w, mod,
                    wr_parts, br):
    n, d = x2d.shape
    tm = 256
    row = lambda w: pl.BlockSpec((tm, w), lambda i: (i, 0))
    full = lambda a: pl.BlockSpec(a.shape, lambda i: (0,) * a.ndim)
    modcol = lambda k: pl.BlockSpec((8, d), lambda i: (0, k))
    return pl.pallas_call(
        _outproj_kernel,
        grid=(n // tm,),
        in_specs=[
            row(HG_WIDTH), row(HG_WIDTH), row(RET_WIDTH), row(RET_WIDTH),
            pl.BlockSpec((tm, HG_WIDTH), lambda i: (i, COL_HGATE)),
            pl.BlockSpec((tm, RET_WIDTH), lambda i: (i, COL_RGATE)),
            row(d), full(w_out_bf), full(hg_norm_w), full(ret_norm_w), full(norm_w),
            modcol(2), modcol(3), modcol(4),
            full(wr_parts), full(br),
        ],
        out_specs=[row(d), pl.BlockSpec((tm * TILE_ROWS, LANES), lambda i: (i, 0)),
                   row(ROUTER_LANES)],
        out_shape=[
            jax.ShapeDtypeStruct((n, d), F32),
            jax.ShapeDtypeStruct((n * TILE_ROWS, LANES), jnp.uint32),
            jax.ShapeDtypeStruct((n, ROUTER_LANES), F32),
        ],
        scratch_shapes=[pltpu.VMEM((tm, d), BF16)],
        compiler_params=_cparams(1, 48),
        name="outproj",
    )(o_f, o_b, r_f, r_b, p, p, x2d, w_out_bf, hg_norm_w, ret_norm_w, norm_w, mod, mod, mod,
      wr_parts, br)


ROUTER_TM = 1024


def _lane_first(hit, lane_f):
    return jnp.min(jnp.where(hit, lane_f, 1e9), axis=1, keepdims=True)


def _router_kernel(lg_ref, gates_ref, s0_ref, s1_ref, cnt_ref, ltri, carry, pstart):
    ph = pl.program_id(0)
    i = pl.program_id(1)
    tm = lg_ref.shape[0]

    @pl.when(jnp.logical_and(ph == 0, i == 0))
    def _():
        t = lax.broadcasted_iota(jnp.int32, (tm, tm), 0)
        s = lax.broadcasted_iota(jnp.int32, (tm, tm), 1)
        ltri[...] = jnp.where(s < t, 1.0, 0.0).astype(BF16)

    @pl.when(jnp.logical_and(ph == 1, i == 0))
    def _():
        cnt = carry[...]
        padded = jnp.floor((cnt + (MOE_PAD - 1.0)) * (1.0 / MOE_PAD)) * MOE_PAD
        r = lax.broadcasted_iota(jnp.int32, (LANES, LANES), 0)
        c = lax.broadcasted_iota(jnp.int32, (LANES, LANES), 1)
        before = jnp.where(r < c, 1.0, 0.0).astype(BF16)
        pstart[...] = _dot(padded.astype(BF16), before)
        cnt_ref[...] = cnt.astype(jnp.int32)

    @pl.when(i == 0)
    def _():
        carry[...] = jnp.zeros_like(carry)

    lg = lg_ref[...]
    lane = lax.broadcasted_iota(jnp.int32, lg.shape, 1)
    lane_f = lane.astype(F32)
    neg = -jnp.inf

    gl = jnp.where(lane < N_GROUPS, lg, neg)
    gmax = jnp.max(gl, axis=1, keepdims=True)
    grp = _lane_first(gl == gmax, lane_f).astype(jnp.int32)
    p_sel = 1.0 / jnp.sum(jnp.exp(gl - gmax), axis=1, keepdims=True)

    in_grp = jnp.logical_and(
        jnp.logical_and(lane >= N_GROUPS, lane < N_GROUPS + N_EXPERTS),
        ((lane - N_GROUPS) >> 3) == grp)
    el = jnp.where(in_grp, lg, neg)
    v1 = jnp.max(el, axis=1, keepdims=True)
    o1 = lane_f == _lane_first(el == v1, lane_f)
    el2 = jnp.where(o1, neg, el)
    v2 = jnp.max(el2, axis=1, keepdims=True)
    o2 = lane_f == _lane_first(el2 == v2, lane_f)

    osum = (jnp.where(o1, 1.0, 0.0) + jnp.where(o2, 1.0, 0.0)).astype(BF16)
    earlier = _dot(ltri[...], osum) + carry[0:1, :]
    carry[...] = carry[...] + _dot(jnp.ones((8, tm), BF16), osum)

    @pl.when(ph == 1)
    def _():
        ez = jnp.exp(v2 - v1)
        g1 = p_sel / (1.0 + ez)
        gates_ref[...] = jnp.concatenate([g1, g1 * ez], axis=1)
        base = earlier + pstart[0:1, :]
        for o, s_ref in ((o1, s0_ref), (o2, s1_ref)):
            col = jnp.sum(jnp.where(o, base, 0.0), axis=1, keepdims=True)
            wide = jnp.broadcast_to(col, (tm, LANES))
            for b in range(tm // LANES):
                s_ref[b:b + 1, :] = wide[b * LANES:(b + 1) * LANES, :].T[0:1, :].astype(jnp.int32)


def _router(logits):
    t = logits.shape[0]
    tm = ROUTER_TM
    rows = tm // LANES
    return pl.pallas_call(
        _router_kernel,
        grid=(2, t // tm),
        in_specs=[pl.BlockSpec((tm, LANES), lambda ph, i: (i, 0))],
        out_specs=[
            pl.BlockSpec((tm, EXPERT_TOPK), lambda ph, i: (i * ph, 0)),
            pl.BlockSpec((rows, LANES), lambda ph, i: (i * ph, 0)),
            pl.BlockSpec((rows, LANES), lambda ph, i: (i * ph, 0)),
            pl.BlockSpec((8, LANES), lambda ph, i: (0, 0)),
        ],
        out_shape=[
            jax.ShapeDtypeStruct((t, EXPERT_TOPK), F32),
            jax.ShapeDtypeStruct((t // LANES, LANES), jnp.int32),
            jax.ShapeDtypeStruct((t // LANES, LANES), jnp.int32),
            jax.ShapeDtypeStruct((8, LANES), jnp.int32),
        ],
        scratch_shapes=[
            pltpu.VMEM((tm, tm), BF16),
            pltpu.VMEM((8, LANES), F32),
            pltpu.VMEM((8, LANES), F32),
        ],
        compiler_params=_cparams(2, 32),
        name="router",
    )(logits)


def _round_up_pow2(x, m):
    assert m & (m - 1) == 0
    return (x + (m - 1)) & ~(m - 1)


def _plan_kernel(cnt_ref, s0_ref, s1_ref, tok_ref, ie_ref, ist_ref, ir_ref, if_ref, zeros, sem):
    n_tok = s0_ref.shape[0]
    n_slots = tok_ref.shape[0]
    n_items = ie_ref.shape[0]
    r = MOE_ITEM_ROWS

    zeros[...] = jnp.zeros_like(zeros)
    clear = pltpu.make_async_copy(zeros, tok_ref, sem)
    clear.start()
    clear.wait()

    def scatter(t16, carry):
        for d in range(16):
            t = t16 * 16 + d
            tok_ref[s0_ref[t]] = t
            tok_ref[s1_ref[t]] = t
        return carry
    lax.fori_loop(0, n_tok // 16, scatter, 0)

    a = jnp.int32(0)
    used = jnp.int32(0)
    e_last = jnp.int32(0)
    for e in range(N_EXPERTS):
        pad = _round_up_pow2(cnt_ref[0, N_GROUPS + e], MOE_PAD)

        def item(sub, a, e=e, pad=pad, used=used):
            ie_ref[a] = e
            ist_ref[a] = used + sub * r
            ir_ref[a] = jnp.minimum(pad - sub * r, r)
            if_ref[a] = 0
            return a + 1
        a = lax.fori_loop(0, _round_up_pow2(pad, r) >> (r.bit_length() - 1), item, a)
        e_last = jnp.where(pad > 0, e, e_last)
        used = used + pad

    def idle(a2, carry):
        fill_start = used + (a2 - a) * r
        fill = jnp.clip(n_slots - fill_start, 0, r)
        ie_ref[a2] = e_last
        ist_ref[a2] = jnp.where(fill > 0, fill_start, 0)
        ir_ref[a2] = 0
        if_ref[a2] = fill
        return carry
    lax.fori_loop(a, n_items, idle, 0)


def _plan(counts, s0, s1, n_slots, n_items):
    smem = pl.BlockSpec(memory_space=pltpu.SMEM)
    item = jax.ShapeDtypeStruct((n_items,), jnp.int32)
    return pl.pallas_call(
        _plan_kernel,
        in_specs=[smem, smem, smem],
        out_specs=[smem] * 5,
        out_shape=[jax.ShapeDtypeStruct((n_slots,), jnp.int32), item, item, item, item],
        scratch_shapes=[pltpu.VMEM((n_slots,), jnp.int32), pltpu.SemaphoreType.DMA(())],
        name="plan",
    )(counts, s0, s1)


def _unpack_pairs(u):
    lo = lax.bitcast_convert_type(u << 16, F32)
    hi = lax.bitcast_convert_type(u & jnp.uint32(0xFFFF0000), F32)
    return jnp.concatenate([lo, hi], axis=1)


def _store_row_tiles(dst, packed):
    r = packed.shape[0]
    for s in range(TILE_ROWS):
        dst[pl.ds(s, r, stride=TILE_ROWS), :] = packed[:, s * LANES:(s + 1) * LANES]


def _load_row_tiles(src, first_row, r):
    return jnp.concatenate(
        [src[pl.ds(first_row * TILE_ROWS + s, r, stride=TILE_ROWS), :] for s in range(TILE_ROWS)],
        axis=1)


def _pack_pairs(y):
    w = y.shape[1] // 2
    bits = lax.bitcast_convert_type(y.astype(BF16).astype(F32), jnp.uint32)
    return (bits[:, :w] >> 16) | (bits[:, w:] & jnp.uint32(0xFFFF0000))


def _moe_kernel(item_e, item_start, item_rows, item_fill, slot_tok,
                h2p_hbm, wg_ref, wu_ref, wd_ref, yb_hbm,
                xbuf, xs_bf, act, ystage, pf, gsem, osem):
    a = pl.program_id(0)
    t = pl.program_id(1)
    n_items = pl.num_programs(0)
    n = item_rows[a]
    start = item_start[a]
    n_full = n // MOE_BLOCK
    n_tail = n - n_full * MOE_BLOCK
    half_w = wg_ref.shape[2]
    cur = a % 2
    nxt = 1 - cur
    a1 = jnp.minimum(a + 1, n_items - 1)
    n_next = jnp.where(a + 1 < n_items, item_rows[a1], 0)
    start_next = item_start[a1]

    def tile_rows(i, count=1):
        return pl.ds(pl.multiple_of(i * TILE_ROWS, TILE_ROWS), count * TILE_ROWS)

    def row_copy(tok, slot, i):
        return pltpu.make_async_copy(h2p_hbm.at[tile_rows(tok), :], xbuf.at[slot, tile_rows(i), :],
                                     gsem.at[slot])

    def start_row(tok, slot, i):
        row_copy(tok, slot, i).start(priority=1)

    @pl.when(jnp.logical_and(a == 0, t == 0))
    def _():
        pf[1] = 0
        pf[2] = 0

        def issue(i, carry):
            start_row(slot_tok[start + i], 0, i)
            return carry
        lax.fori_loop(0, n, issue, 0)

    @pl.when(t == 0)
    def _():
        pf[0] = 0

        def drain(bk, carry):
            r0 = pl.multiple_of(bk * MOE_PAD, MOE_PAD)
            pltpu.make_async_copy(h2p_hbm.at[tile_rows(0, MOE_PAD), :],
                                  xbuf.at[cur, tile_rows(r0, MOE_PAD), :], gsem.at[cur]).wait()
            return carry
        lax.fori_loop(0, n // MOE_PAD, drain, 0)

    def prefetch_group():
        base = pf[0]
        for q in range(MOE_GATHER_GROUP):
            start_row(slot_tok[start_next + base + q], nxt, base + q)
        pf[0] = base + MOE_GATHER_GROUP

    def run_blocks(compute):
        def one(b, rows):
            r0 = pl.multiple_of(b * MOE_BLOCK, MOE_BLOCK)
            more = pf[0] < n_next

            @pl.when(more)
            def _():
                compute(b, r0, rows, prefetch_group)

            @pl.when(jnp.logical_not(more))
            def _():
                compute(b, r0, rows, lambda: None)

        def body(b, carry):
            one(b, MOE_BLOCK)
            return carry
        lax.fori_loop(0, n_full, body, 0)

        @pl.when(n_tail > 0)
        def _():
            one(n_full, MOE_PAD)

    def gate_up(half):
        def compute(b, r0, rows, mid):
            if half == 0:
                xs = _unpack_pairs(_load_row_tiles(xbuf.at[cur], r0, rows)).astype(BF16)
                xs_bf[pl.ds(r0, rows), :] = xs
            else:
                xs = xs_bf[pl.ds(r0, rows), :]
            g = _dot(xs, wg_ref[0].astype(BF16))
            u = _dot(xs, wu_ref[0].astype(BF16))
            mid()
            act[pl.ds(r0, rows), half * half_w:(half + 1) * half_w] = (
                g * _sigmoid(g) * u).astype(BF16)
        run_blocks(compute)

    def out_copy(slot, r0, rows):
        return pltpu.make_async_copy(
            ystage.at[slot, tile_rows(0, rows), :],
            yb_hbm.at[tile_rows(pl.multiple_of(start + r0, MOE_PAD), rows), :],
            osem.at[slot])

    def wait_stage(slot):
        for rows in (MOE_BLOCK, MOE_PAD):
            @pl.when(pf[1 + slot] == rows)
            def _(rows=rows):
                out_copy(slot, 0, rows).wait()
                pf[1 + slot] = 0

    def down():
        def compute(b, r0, rows, mid):
            slot = b % 2
            wait_stage(slot)
            y = _dot(act[pl.ds(r0, rows), :], wd_ref[0].astype(BF16))
            mid()
            _store_row_tiles(ystage.at[slot], _pack_pairs(y))
            out_copy(slot, r0, rows).start()
            pf[1 + slot] = rows
        run_blocks(compute)

        def tail(i, carry):
            start_row(slot_tok[start_next + i], nxt, i)
            return carry
        lax.fori_loop(pf[0], n_next, tail, 0)

    pl.when(t == 0)(functools.partial(gate_up, 0))
    pl.when(t == 1)(functools.partial(gate_up, 1))
    pl.when(t == 2)(down)

    fill = item_fill[a]

    @pl.when(jnp.logical_and(t == 0, fill > 0))
    def _():
        wait_stage(0)
        ystage[0] = jnp.zeros(ystage.shape[1:], ystage.dtype)

        def zero_block(bk, carry):
            cp = out_copy(0, bk * MOE_PAD, MOE_PAD)
            cp.start()
            cp.wait()
            return carry
        lax.fori_loop(0, fill // MOE_PAD, zero_block, 0)

    @pl.when(jnp.logical_and(a == n_items - 1, t == pl.num_programs(1) - 1))
    def _():
        wait_stage(0)
        wait_stage(1)


def _moe_ffn(h2p, w_gate, w_up, w_down, item_e, item_start, item_rows, item_fill, slot_tok,
             n_slots):
    d = w_gate.shape[1]
    assert d == 2 * TILE_ROWS * LANES and h2p.shape[1] == LANES, "one packed token row per tile"
    n_items = item_e.shape[0]
    half_w = D_EXPERT // 2

    def wgu_map(a, t, ie, ist, ir, fl, st):
        return (ie[a], 0, jnp.where(ir[a] > 0, jnp.minimum(t, 1), 1))

    def wd_map(a, t, ie, ist, ir, fl, st):
        return (jnp.where(t >= 2, ie[a], ie[jnp.maximum(a - 1, 0)]), 0, 0)

    grid_spec = pltpu.PrefetchScalarGridSpec(
        num_scalar_prefetch=5,
        grid=(n_items, 3),
        in_specs=[
            pl.BlockSpec(memory_space=pl.ANY),
            pl.BlockSpec((1, d, half_w), wgu_map),
            pl.BlockSpec((1, d, half_w), wgu_map),
            pl.BlockSpec((1, D_EXPERT, d), wd_map),
        ],
        out_specs=pl.BlockSpec(memory_space=pl.ANY),
        scratch_shapes=[
            pltpu.VMEM((2, MOE_ITEM_ROWS * TILE_ROWS, LANES), jnp.uint32),
            pltpu.VMEM((MOE_ITEM_ROWS, d), BF16),
            pltpu.VMEM((MOE_ITEM_ROWS, D_EXPERT), BF16),
            pltpu.VMEM((2, MOE_BLOCK * TILE_ROWS, LANES), jnp.uint32),
            pltpu.SMEM((3,), jnp.int32),
            pltpu.SemaphoreType.DMA((2,)),
            pltpu.SemaphoreType.DMA((2,)),
        ],
    )
    return pl.pallas_call(
        _moe_kernel,
        grid_spec=grid_spec,
        out_shape=jax.ShapeDtypeStruct((n_slots * TILE_ROWS, LANES), jnp.uint32),
        compiler_params=_cparams(2, 56),
        name="moe",
    )(item_e, item_start, item_rows, item_fill, slot_tok, h2p, w_gate, w_up, w_down)


def _combine_kernel(s0, s1, yb_hbm, gates_ref, x1_ref, nw_ref, g2_ref, out_ref, rows, sem):
    i = pl.program_id(0)
    n_tiles = pl.num_programs(0)
    tm = x1_ref.shape[0]

    def tile_rows(i, count=1):
        return pl.ds(pl.multiple_of(i * TILE_ROWS, TILE_ROWS), count * TILE_ROWS)

    def gather(tile, buf):
        def issue(r8, carry):
            t0 = tile * tm + r8 * 8
            for dr in range(8):
                for k, s_k in enumerate((s0, s1)):
                    pltpu.make_async_copy(yb_hbm.at[tile_rows(s_k[t0 + dr]), :],
                                          rows.at[buf, k, tile_rows(r8 * 8 + dr), :],
                                          sem.at[buf]).start()
            return carry
        lax.fori_loop(0, tm // 8, issue, 0)

    @pl.when(i == 0)
    def _():
        gather(0, 0)

    @pl.when(i + 1 < n_tiles)
    def _():
        gather(i + 1, (i + 1) % 2)

    buf = i % 2
    for k in range(EXPERT_TOPK):
        pltpu.make_async_copy(yb_hbm.at[tile_rows(0, tm), :], rows.at[buf, k], sem.at[buf]).wait()

    g = gates_ref[...]
    y = (_unpack_pairs(_load_row_tiles(rows.at[buf, 0], 0, tm)) * g[:, 0:1]
         + _unpack_pairs(_load_row_tiles(rows.at[buf, 1], 0, tm)) * g[:, 1:2])
    out_ref[...] = x1_ref[...] + g2_ref[0:1, :] * (_rms(y) * nw_ref[3:4, :])


def _combine(s0, s1, yb, gates, x1, norm_w, mod):
    n, d = x1.shape
    tm = 512
    grid_spec = pltpu.PrefetchScalarGridSpec(
        num_scalar_prefetch=2,
        grid=(n // tm,),
        in_specs=[
            pl.BlockSpec(memory_space=pl.ANY),
            pl.BlockSpec((tm, EXPERT_TOPK), lambda i, *_: (i, 0)),
            pl.BlockSpec((tm, d), lambda i, *_: (i, 0)),
            pl.BlockSpec((4, d), lambda i, *_: (0, 0)),
            pl.BlockSpec((8, d), lambda i, *_: (0, 5)),
        ],
        out_specs=pl.BlockSpec((tm, d), lambda i, *_: (i, 0)),
        scratch_shapes=[
            pltpu.VMEM((2, EXPERT_TOPK, tm * TILE_ROWS, LANES), jnp.uint32),
            pltpu.SemaphoreType.DMA((2,)),
        ],
    )
    return pl.pallas_call(
        _combine_kernel,
        grid_spec=grid_spec,
        out_shape=jax.ShapeDtypeStruct((n, d), F32),
        compiler_params=_cparams(1, 40),
        name="combine",
    )(s0, s1, yb, gates, x1, norm_w, mod)


def _rope_tables(n):
    rows = n // GRID_W
    r = np.repeat(np.arange(rows, dtype=np.float32), GRID_W)
    cidx = np.tile(np.arange(GRID_W, dtype=np.float32), rows)
    n_freq = RET_DK // 4
    inv = np.float32(ROPE_BASE) ** (-np.arange(n_freq, dtype=np.float32) / np.float32(n_freq))
    ang = np.concatenate([r[:, None] * inv, cidx[:, None] * inv], axis=-1).astype(np.float32)
    cos, sin = np.cos(ang), np.sin(ang)
    cs = np.concatenate([cos, cos], axis=-1)
    sn = np.concatenate([-sin, sin], axis=-1)
    return jnp.asarray(cs, F32), jnp.asarray(sn, F32)


def kernel(x, c, ctx, c_ctx, w_mod, b_mod, norm_w, w_in, hg_lb_logits, hg_norm_w, ret_decay_logits,
           ret_norm_w, w_out, router_group_w, router_group_b, router_expert_w, router_expert_b,
           w_gate, w_up, w_down):
    bsz, n, d = x.shape
    l = ctx.shape[1]
    assert bsz == 1 and w_mod.shape[0] == 1, "single sample, single layer"
    x2d, ctx2d = x[0], ctx[0]

    cc = jnp.zeros((8, d), F32).at[0].set(c[0]).at[1].set(c_ctx)
    mod = _modulation(cc, w_mod[0], b_mod[0])

    lb = jnp.cumsum(jax.nn.softmax(hg_lb_logits.astype(F32), axis=0), axis=0)[0]
    rk0 = 5 * HG_WIDTH + RET_HEADS * RET_DK
    cols = jnp.stack([
        jnp.zeros((PROJ_WIDTH,), F32).at[HG_WIDTH:3 * HG_WIDTH].set(lb.reshape(-1)),
        jnp.ones((PROJ_WIDTH,), F32).at[rk0:rk0 + RET_HEADS * RET_DK].set(RET_DK ** -0.5),
    ])
    w_in_bf = w_in[0].astype(BF16)
    p, g = _projection(x2d, norm_w[0], mod, w_in_bf, cols, 0, 1024, "proj_x")
    pc, gc = _projection(ctx2d, norm_w[0], mod, w_in_bf, cols, 1, l, "proj_ctx")

    ret_lg = jax.nn.log_sigmoid(ret_decay_logits[0].astype(F32))
    cs_tab, sn_tab = _rope_tables(n)

    o_f, o_b = _hgrn_scan(p, g, pc, gc)
    r_f, r_b = _ret_scan(p, pc, ret_lg, cs_tab, sn_tab)

    wr = jnp.concatenate([router_group_w[0], router_expert_w[0]], axis=1)
    wr = jnp.pad(wr, ((0, 0), (0, ROUTER_LANES - wr.shape[1])))
    wr_hi = wr.astype(BF16)
    wr_parts = jnp.concatenate([wr_hi, (wr - wr_hi.astype(F32)).astype(BF16)], axis=1)
    br = jnp.concatenate([router_group_b[0], router_expert_b[0]]).astype(F32)
    br = jnp.pad(br, (0, ROUTER_LANES - br.shape[0])).reshape(1, ROUTER_LANES)
    x1, h2, logits = _out_projection(
        o_f, o_b, r_f, r_b, p, x2d, w_out[0].astype(BF16), hg_norm_w, ret_norm_w, norm_w[0], mod,
        wr_parts, br)

    n_assign = n * EXPERT_TOPK
    n_slots = n_assign + N_EXPERTS * MOE_PAD
    n_items = (n_slots + N_EXPERTS * (MOE_ITEM_ROWS - MOE_PAD)) // MOE_ITEM_ROWS
    gates, s0, s1, counts = _router(logits)
    s0, s1 = s0.reshape(-1), s1.reshape(-1)
    slot_tok, item_e, item_start, item_rows, item_fill = _plan(counts, s0, s1, n_slots, n_items)

    yb = _moe_ffn(h2, w_gate[0], w_up[0], w_down[0], item_e, item_start, item_rows, item_fill,
                  slot_tok, n_slots)
    out = _combine(s0, s1, yb, gates, x1, norm_w[0], mod)
    return out[None]
```

```python
import functools

import jax
import jax.numpy as jnp
import numpy as np
from jax import lax
from jax.experimental import pallas as pl
from jax.experimental.pallas import tpu as pltpu

F32 = jnp.float32
BF16 = jnp.bfloat16

D_MODEL = 2048
GRID_W = 64
HG_HEADS = 8
HG_DK = 128
HG_WIDTH = 1024
RET_HEADS = 4
RET_DK = 128
RET_DV = 256
RET_WIDTH = 1024
PROJ_WIDTH = 8192
N_GROUPS = 4
EXPERTS_PER_GROUP = 8
N_EXPERTS = 32
EXPERT_TOPK = 2
D_EXPERT = 1024
RMS_EPS = 1e-6
ROPE_BASE = 10000.0

COL_HQ, COL_HF_FWD, COL_HF_BWD, COL_HI, COL_HGATE = 0, 1, 2, 3, 4
COL_RQ_512, COL_RK_512 = 10, 11
COL_RV, COL_RGATE = 6, 7

HG_CHUNK = 128
HG_DIAG = 16
HG_LOG_CLAMP = 10.0
RET_CHUNK = 256

MOE_PAD = 128
MOE_BLOCK = 512
MOE_ITEM_ROWS = 1024
MOE_GATHER_GROUP = 128
ROUTER_LANES = 128
LANES = 128
TILE_ROWS = 8


def _cparams(n_axes, vmem_mb):
    return pltpu.CompilerParams(
        dimension_semantics=("arbitrary",) * n_axes,
        vmem_limit_bytes=vmem_mb * 1024 * 1024,
    )


def _sigmoid(x):
    return 1.0 / (1.0 + jnp.exp(-x))


def _dot(a, b):
    return jnp.dot(a, b, preferred_element_type=F32)


def _dot_nt(a, b):
    return lax.dot_general(a, b, (((1,), (1,)), ((), ())), preferred_element_type=F32)


def _dot_tn(a, b):
    return lax.dot_general(a, b, (((0,), (0,)), ((), ())), preferred_element_type=F32)


def _mod_kernel(cc_ref, w_ref, b_ref, o_ref):
    s = cc_ref[...]
    s = s * _sigmoid(s)
    s16 = jnp.concatenate([s, s], axis=0)
    hi = s16.astype(BF16).astype(F32)
    row = lax.broadcasted_iota(jnp.int32, s16.shape, 0)
    lhs = jnp.where(row < 8, hi, s16 - hi).astype(BF16)
    r = _dot(lhs, w_ref[...].astype(BF16))
    o_ref[...] = r[:8] + r[8:] + b_ref[...]


def _modulation(cc, w_mod, b_mod):
    d, n = w_mod.shape
    tn = 1024
    return pl.pallas_call(
        _mod_kernel,
        grid=(n // tn,),
        in_specs=[
            pl.BlockSpec((8, d), lambda j: (0, 0)),
            pl.BlockSpec((d, tn), lambda j: (0, j)),
            pl.BlockSpec((1, tn), lambda j: (0, j)),
        ],
        out_specs=pl.BlockSpec((8, tn), lambda j: (0, j)),
        out_shape=jax.ShapeDtypeStruct((8, n), F32),
        compiler_params=_cparams(1, 40),
        name="mod",
    )(cc, w_mod, b_mod.reshape(1, n))


PROJ_TN = 1024
PROLOGUE_ROWS = 32


def _proj_kernel(x_ref, nw_ref, sh_ref, sc_ref, w_ref, cols_ref, o_ref, g_ref, h_scr, *, mod_row):
    i = pl.program_id(0)
    j = pl.program_id(1)
    last_j = pl.num_programs(1) - 1
    cur = i % 2

    def normed(x):
        scale = nw_ref[0:1, :] * (1.0 + sc_ref[mod_row:mod_row + 1, :])
        shift = sh_ref[mod_row:mod_row + 1, :]
        y = x * lax.rsqrt(jnp.mean(x * x, axis=-1, keepdims=True) + RMS_EPS)
        return (y * scale + shift).astype(BF16)

    @pl.when(jnp.logical_and(i == 0, j == 0))
    def _():
        def body(r, carry):
            r0 = pl.multiple_of(r * PROLOGUE_ROWS, PROLOGUE_ROWS)
            h_scr[0, pl.ds(r0, PROLOGUE_ROWS), :] = normed(x_ref[pl.ds(r0, PROLOGUE_ROWS), :])
            return carry
        lax.fori_loop(0, x_ref.shape[0] // PROLOGUE_ROWS, body, 0)

    is_gate = jnp.logical_or(j == 1, j == 2)
    is_silu = jnp.logical_or(j == 0, j == 4)
    is_lin = jnp.logical_not(jnp.logical_or(jnp.logical_or(is_gate, is_silu), j == last_j))

    @pl.when(is_lin)
    def _():
        o_ref[...] = (_dot(h_scr[cur], w_ref[...]) * cols_ref[1:2, :]).astype(BF16)

    @pl.when(is_silu)
    def _():
        acc = _dot(h_scr[cur], w_ref[...])
        o_ref[...] = (acc * _sigmoid(acc)).astype(BF16)

    @pl.when(is_gate)
    def _():
        acc = _dot(h_scr[cur], w_ref[...])
        lb = cols_ref[0:1, :]
        g = jnp.maximum(jnp.log(lb + (1.0 - lb) * _sigmoid(acc)), -HG_LOG_CLAMP)
        g_ref[...] = g
        o_ref[...] = g.astype(BF16)

    @pl.when(j == last_j)
    def _():
        acc = _dot(h_scr[cur], w_ref[...])
        o_ref[...] = (acc * _sigmoid(acc)).astype(BF16)
        h_scr[1 - cur] = normed(x_ref[...])


def _projection(x2d, norm_w, mod, w_in, cols, mod_row, tm, name):
    n, d = x2d.shape
    tn = PROJ_TN
    n_i, n_j = n // tm, PROJ_WIDTH // tn

    def x_map(i, j):
        return (jnp.minimum(i + (j == n_j - 1).astype(jnp.int32), n_i - 1), 0)

    return pl.pallas_call(
        functools.partial(_proj_kernel, mod_row=mod_row),
        grid=(n_i, n_j),
        in_specs=[
            pl.BlockSpec((tm, d), x_map),
            pl.BlockSpec((4, d), lambda i, j: (0, 0)),
            pl.BlockSpec((8, d), lambda i, j: (0, 0)),
            pl.BlockSpec((8, d), lambda i, j: (0, 1)),
            pl.BlockSpec((d, tn), lambda i, j: (0, j)),
            pl.BlockSpec((2, tn), lambda i, j: (0, j)),
        ],
        out_specs=[
            pl.BlockSpec((tm, tn), lambda i, j: (i, j)),
            pl.BlockSpec((tm, tn), lambda i, j: (i, jnp.clip(j - 1, 0, 1))),
        ],
        out_shape=[
            jax.ShapeDtypeStruct((n, PROJ_WIDTH), BF16),
            jax.ShapeDtypeStruct((n, 2 * HG_WIDTH), F32),
        ],
        scratch_shapes=[pltpu.VMEM((2, tm, d), BF16)],
        compiler_params=_cparams(2, 56),
        name=name,
    )(x2d, norm_w, mod, mod, w_in, cols)


def _tri_ones(c, rev):
    t = lax.broadcasted_iota(jnp.int32, (c, c), 0)
    s = lax.broadcasted_iota(jnp.int32, (c, c), 1)
    return jnp.where((s >= t) if rev else (s <= t), 1.0, 0.0).astype(BF16)


def _cumsum_rows(g, tri):
    hi = g.astype(BF16)
    lo = (g - hi.astype(F32)).astype(BF16)
    return _dot(tri, hi) + _dot(tri, lo)


def _row_refs(b, idxs, blk):
    parts = [jnp.broadcast_to(b[i:i + 1, :], (blk, b.shape[1])) for i in idxs]
    return parts[0] if len(parts) == 1 else jnp.concatenate(parts, axis=0)


def _hgrn_levels(c, rev):
    t = lax.broadcasted_iota(jnp.int32, (c, c), 0)
    s = lax.broadcasted_iota(jnp.int32, (c, c), 1)
    levels = []
    m = c // 2
    while m >= HG_DIAG:
        blk = 2 * m
        same = ((t ^ s) & ~(blk - 1)) == 0
        if rev:
            idxs = [b0 + m for b0 in range(0, c, blk)]
            cross = jnp.logical_and((t & m) == 0, (s & m) != 0)
        else:
            idxs = [b0 + m - 1 for b0 in range(0, c, blk)]
            cross = jnp.logical_and((t & m) != 0, (s & m) == 0)
        levels.append((idxs, blk, jnp.logical_and(same, cross)))
        m //= 2
    blk = HG_DIAG
    same = ((t ^ s) & ~(blk - 1)) == 0
    if rev:
        idxs = [b0 + blk // 2 for b0 in range(0, c, blk)]
        mask = jnp.logical_and(same, s >= t)
    else:
        idxs = [b0 + blk // 2 - 1 for b0 in range(0, c, blk)]
        mask = jnp.logical_and(same, s <= t)
    levels.append((idxs, blk, mask))
    return levels


def _hgrn_head(q_bf, g, b, v_bf, st, rev, levels):
    c = g.shape[0]
    kk = 1.0 - jnp.exp(g)

    o = None
    if levels is not None:
        q = q_bf.astype(F32)
        att = jnp.zeros((c, c), F32)
        for idxs, blk, mask in levels:
            ref = _row_refs(b, idxs, blk)
            a_l = _dot_nt((q * jnp.exp(b - ref)).astype(BF16),
                          (kk * jnp.exp(ref - b)).astype(BF16))
            att = att + jnp.where(mask, a_l, 0.0)
        o = _dot(att.astype(BF16), v_bf) + _dot_nt((q * jnp.exp(b)).astype(BF16), st.astype(BF16))

    b_end = b[0:1, :] if rev else b[c - 1:c, :]
    k_end = (kk * jnp.exp(b_end - b)).astype(BF16)
    st_new = st * jnp.exp(b_end) + _dot_tn(v_bf, k_end)
    return o, st_new


def _hgrn_ctx_kernel(gf_ref, vf_ref, gb_ref, vb_ref, st_ref):
    @pl.when(pl.program_id(0) == 0)
    def _():
        st_ref[...] = jnp.zeros_like(st_ref)

    c = gf_ref.shape[0]
    bf = _cumsum_rows(gf_ref[...], _tri_ones(c, False))
    bb = _cumsum_rows(gb_ref[...], _tri_ones(c, True))
    for h in range(HG_HEADS):
        sl = slice(h * HG_DK, (h + 1) * HG_DK)
        _, st_ref[h] = _hgrn_head(None, gf_ref[:, sl], bf[:, sl], vf_ref[:, sl], st_ref[h],
                                  False, None)
        hb = HG_HEADS + h
        _, st_ref[hb] = _hgrn_head(None, gb_ref[:, sl], bb[:, sl], vb_ref[:, sl], st_ref[hb],
                                   True, None)


def _hgrn_kernel(qf_ref, gf_ref, vf_ref, qb_ref, gb_ref, vb_ref, st0_ref, of_ref, ob_ref, st_ref):
    @pl.when(pl.program_id(0) == 0)
    def _():
        st_ref[...] = st0_ref[...]

    c = gf_ref.shape[0]
    lev_f = _hgrn_levels(c, False)
    lev_b = _hgrn_levels(c, True)
    bf = _cumsum_rows(gf_ref[...], _tri_ones(c, False))
    bb = _cumsum_rows(gb_ref[...], _tri_ones(c, True))
    for h in range(HG_HEADS):
        sl = slice(h * HG_DK, (h + 1) * HG_DK)
        of_ref[:, sl], st_ref[h] = _hgrn_head(qf_ref[:, sl], gf_ref[:, sl], bf[:, sl],
                                              vf_ref[:, sl], st_ref[h], False, lev_f)
        hb = HG_HEADS + h
        ob_ref[:, sl], st_ref[hb] = _hgrn_head(qb_ref[:, sl], gb_ref[:, sl], bb[:, sl],
                                               vb_ref[:, sl], st_ref[hb], True, lev_b)


def _hgrn_scan(p, g, pc, gc):
    c = HG_CHUNK
    w = HG_WIDTH
    n, l = p.shape[0], pc.shape[0]
    n_x, n_ctx = n // c, l // c
    st_shape = (2 * HG_HEADS, HG_DK, HG_DK)

    def spec(chunk_fn, col):
        return pl.BlockSpec((c, w), lambda s: (chunk_fn(s), col))

    st0 = pl.pallas_call(
        _hgrn_ctx_kernel,
        grid=(n_ctx,),
        in_specs=[
            spec(lambda s: s, 0), spec(lambda s: s, COL_HI),
            spec(lambda s: n_ctx - 1 - s, 1), spec(lambda s: n_ctx - 1 - s, COL_HI),
        ],
        out_specs=pl.BlockSpec(st_shape, lambda s: (0, 0, 0)),
        out_shape=jax.ShapeDtypeStruct(st_shape, F32),
        compiler_params=_cparams(1, 32),
        name="hgrn_ctx",
    )(gc, pc, gc, pc)

    fwd = lambda s: s
    bwd = lambda s: n_x - 1 - s
    return pl.pallas_call(
        _hgrn_kernel,
        grid=(n_x,),
        in_specs=[
            spec(fwd, COL_HQ), spec(fwd, 0), spec(fwd, COL_HI),
            spec(bwd, COL_HQ), spec(bwd, 1), spec(bwd, COL_HI),
            pl.BlockSpec(st_shape, lambda s: (0, 0, 0)),
        ],
        out_specs=[spec(fwd, 0), spec(bwd, 0)],
        out_shape=[jax.ShapeDtypeStruct((n, w), F32)] * 2,
        scratch_shapes=[pltpu.VMEM(st_shape, F32)],
        compiler_params=_cparams(1, 32),
        name="hgrn",
    )(p, g, p, p, g, p, st0)


def _rope(x_bf, cs, sn):
    x = x_bf.astype(F32)
    return x * cs + pltpu.roll(x, RET_DK // 2, axis=1) * sn


def _ret_decays(lg, c, rev):
    t = lax.broadcasted_iota(jnp.int32, (c, c), 0)
    s = lax.broadcasted_iota(jnp.int32, (c, c), 1)
    diff = (s - t) if rev else (t - s)
    pair = jnp.where(diff >= 0, jnp.exp(lg * jnp.maximum(diff, 0).astype(F32)), 0.0)
    pos = lax.broadcasted_iota(jnp.int32, (c, RET_DK), 0).astype(F32)
    q_dec = jnp.exp(lg * ((c - pos) if rev else (pos + 1.0)))
    k_dec = jnp.exp(lg * (pos if rev else (c - 1.0 - pos)))
    chunk = jnp.exp(jnp.full((8, RET_DK), lg * c, F32))
    return pair, q_dec, k_dec, chunk


def _ret_head(q, k, v_bf, decays, st):
    pair, q_dec, k_dec, chunk = decays
    o = None
    if q is not None:
        att = _dot_nt(q.astype(BF16), k.astype(BF16)) * pair
        o = _dot(att.astype(BF16), v_bf) + _dot_nt((q * q_dec).astype(BF16), st.astype(BF16))
    st_new = st * chunk[0:1, :] + _dot_tn(v_bf, (k * k_dec).astype(BF16))
    return o, st_new


def _ret_ctx_kernel(lg_ref, kf_ref, vf_ref, kb_ref, vb_ref, st_ref):
    @pl.when(pl.program_id(0) == 0)
    def _():
        st_ref[...] = jnp.zeros_like(st_ref)

    c = kf_ref.shape[0]
    for h in range(RET_HEADS):
        sk = slice(h * RET_DK, (h + 1) * RET_DK)
        sv = slice(h * RET_DV, (h + 1) * RET_DV)
        _, st_ref[h] = _ret_head(None, kf_ref[:, sk].astype(F32), vf_ref[:, sv],
                                 _ret_decays(lg_ref[0, h], c, False), st_ref[h])
        hb = RET_HEADS + h
        _, st_ref[hb] = _ret_head(None, kb_ref[:, sk].astype(F32), vb_ref[:, sv],
                                  _ret_decays(lg_ref[1, h], c, True), st_ref[hb])


def _ret_kernel(lg_ref, qf_ref, kf_ref, vf_ref, csf_ref, snf_ref,
                qb_ref, kb_ref, vb_ref, csb_ref, snb_ref, st0_ref, of_ref, ob_ref,
                st_ref, pair_ref, qd_ref, kd_ref, cd_ref):
    c = qf_ref.shape[0]

    @pl.when(pl.program_id(0) == 0)
    def _():
        st_ref[...] = st0_ref[...]
        for d in range(2):
            for h in range(RET_HEADS):
                i = d * RET_HEADS + h
                pair_ref[i], qd_ref[i], kd_ref[i], cd_ref[i] = _ret_decays(lg_ref[d, h], c, d == 1)

    csf, snf = csf_ref[...], snf_ref[...]
    csb, snb = csb_ref[...], snb_ref[...]
    for h in range(RET_HEADS):
        sk = slice(h * RET_DK, (h + 1) * RET_DK)
        sv = slice(h * RET_DV, (h + 1) * RET_DV)
        of_ref[:, sv], st_ref[h] = _ret_head(
            _rope(qf_ref[:, sk], csf, snf), _rope(kf_ref[:, sk], csf, snf), vf_ref[:, sv],
            (pair_ref[h], qd_ref[h], kd_ref[h], cd_ref[h]), st_ref[h])
        hb = RET_HEADS + h
        ob_ref[:, sv], st_ref[hb] = _ret_head(
            _rope(qb_ref[:, sk], csb, snb), _rope(kb_ref[:, sk], csb, snb), vb_ref[:, sv],
            (pair_ref[hb], qd_ref[hb], kd_ref[hb], cd_ref[hb]), st_ref[hb])


def _ret_scan(p, pc, ret_lg, cs_tab, sn_tab):
    c = RET_CHUNK
    n, l = p.shape[0], pc.shape[0]
    n_x, n_ctx = n // c, l // c
    qk_w = RET_HEADS * RET_DK
    st_shape = (2 * RET_HEADS, RET_DV, RET_DK)
    smem = pl.BlockSpec(memory_space=pltpu.SMEM)

    def kv_specs(chunk_fn):
        return [
            pl.BlockSpec((c, qk_w), lambda s: (chunk_fn(s), COL_RK_512)),
            pl.BlockSpec((c, RET_WIDTH), lambda s: (chunk_fn(s), COL_RV)),
        ]

    st0 = pl.pallas_call(
        _ret_ctx_kernel,
        grid=(n_ctx,),
        in_specs=[smem, *kv_specs(lambda s: s), *kv_specs(lambda s: n_ctx - 1 - s)],
        out_specs=pl.BlockSpec(st_shape, lambda s: (0, 0, 0)),
        out_shape=jax.ShapeDtypeStruct(st_shape, F32),
        compiler_params=_cparams(1, 32),
        name="ret_ctx",
    )(ret_lg, pc, pc, pc, pc)

    def specs(chunk_fn):
        return [
            pl.BlockSpec((c, qk_w), lambda s: (chunk_fn(s), COL_RQ_512)),
            *kv_specs(chunk_fn),
            pl.BlockSpec((c, RET_DK), lambda s: (chunk_fn(s), 0)),
            pl.BlockSpec((c, RET_DK), lambda s: (chunk_fn(s), 0)),
        ]

    fwd = lambda s: s
    bwd = lambda s: n_x - 1 - s
    return pl.pallas_call(
        _ret_kernel,
        grid=(n_x,),
        in_specs=[smem, *specs(fwd), *specs(bwd), pl.BlockSpec(st_shape, lambda s: (0, 0, 0))],
        out_specs=[
            pl.BlockSpec((c, RET_WIDTH), lambda s: (fwd(s), 0)),
            pl.BlockSpec((c, RET_WIDTH), lambda s: (bwd(s), 0)),
        ],
        out_shape=[jax.ShapeDtypeStruct((n, RET_WIDTH), F32)] * 2,
        scratch_shapes=[
            pltpu.VMEM(st_shape, F32),
            pltpu.VMEM((2 * RET_HEADS, c, c), F32),
            pltpu.VMEM((2 * RET_HEADS, c, RET_DK), F32),
            pltpu.VMEM((2 * RET_HEADS, c, RET_DK), F32),
            pltpu.VMEM((2 * RET_HEADS, 8, RET_DK), F32),
        ],
        compiler_params=_cparams(1, 32),
        name="ret",
    )(ret_lg, p, p, p, cs_tab, sn_tab, p, p, p, cs_tab, sn_tab, st0)


def _rms(y):
    return y * lax.rsqrt(jnp.mean(y * y, axis=-1, keepdims=True) + RMS_EPS)


def _outproj_kernel(of_ref, ob_ref, rf_ref, rb_ref, hgate_ref, rgate_ref, x_ref, wout_ref,
                    hgw_ref, rnw_ref, nw_ref, g1_ref, sh2_ref, sc2_ref, wr_ref, br_ref,
                    x1_ref, h2_ref, lg_ref, cat_scr):
    tm = x_ref.shape[0]
    n_sub = 1
    for sub in range(n_sub):
        rows = slice(sub * tm // n_sub, (sub + 1) * tm // n_sub)
        for h in range(HG_HEADS):
            sl = slice(h * HG_DK, (h + 1) * HG_DK)
            o = of_ref[rows, sl] + ob_ref[rows, sl]
            cat_scr[rows, sl] = (_rms(o) * hgw_ref[0:1, sl]
                                 * hgate_ref[rows, sl].astype(F32)).astype(BF16)
        for h in range(RET_HEADS):
            sl = slice(h * RET_DV, (h + 1) * RET_DV)
            r = rf_ref[rows, sl] + rb_ref[rows, sl]
            cat_scr[rows, HG_WIDTH + h * RET_DV:HG_WIDTH + (h + 1) * RET_DV] = (
                _rms(r) * rnw_ref[0:1, sl] * rgate_ref[rows, sl].astype(F32)).astype(BF16)

        y = _dot(cat_scr[rows, :], wout_ref[...])
        x1 = x_ref[rows, :] + g1_ref[0:1, :] * (_rms(y) * nw_ref[1:2, :])
        x1_ref[rows, :] = x1
        h2 = _rms(x1) * nw_ref[2:3, :] * (1.0 + sc2_ref[0:1, :]) + sh2_ref[0:1, :]
        assert n_sub == 1
        _store_row_tiles(h2_ref, _pack_pairs(h2))
        hi = h2.astype(BF16)
        lo = (h2 - hi.astype(F32)).astype(BF16)
        parts = _dot(jnp.concatenate([hi, lo], axis=0), wr_ref[...])
        half = hi.shape[0]
        lg_ref[rows, :] = (parts[:half, :ROUTER_LANES] + parts[half:, :ROUTER_LANES]
                           + parts[:half, ROUTER_LANES:] + br_ref[...])


def _out_projection(o_f, o_b, r_f, r_b, p, x2d, w_out_bf, hg_norm_w, ret_norm_w, norm_w, mod,
                    wr_parts, br):
    n, d = x2d.shape
    tm = 256
    row = lambda w: pl.BlockSpec((tm, w), lambda i: (i, 0))
    full = lambda a: pl.BlockSpec(a.shape, lambda i: (0,) * a.ndim)
    modcol = lambda k: pl.BlockSpec((8, d), lambda i: (0, k))
    return pl.pallas_call(
        _outproj_kernel,
        grid=(n // tm,),
        in_specs=[
            row(HG_WIDTH), row(HG_WIDTH), row(RET_WIDTH), row(RET_WIDTH),
            pl.BlockSpec((tm, HG_WIDTH), lambda i: (i, COL_HGATE)),
            pl.BlockSpec((tm, RET_WIDTH), lambda i: (i, COL_RGATE)),
            row(d), full(w_out_bf), full(hg_norm_w), full(ret_norm_w), full(norm_w),
            modcol(2), modcol(3), modcol(4),
            full(wr_parts), full(br),
        ],
        out_specs=[row(d), pl.BlockSpec((tm * TILE_ROWS, LANES), lambda i: (i, 0)),
                   row(ROUTER_LANES)],
        out_shape=[
            jax.ShapeDtypeStruct((n, d), F32),
            jax.ShapeDtypeStruct((n * TILE_ROWS, LANES), jnp.uint32),
            jax.ShapeDtypeStruct((n, ROUTER_LANES), F32),
        ],
        scratch_shapes=[pltpu.VMEM((tm, d), BF16)],
        compiler_params=_cparams(1, 48),
        name="outproj",
    )(o_f, o_b, r_f, r_b, p, p, x2d, w_out_bf, hg_norm_w, ret_norm_w, norm_w, mod, mod, mod,
      wr_parts, br)


ROUTER_TM = 1024


def _lane_first(hit, lane_f):
    return jnp.min(jnp.where(hit, lane_f, 1e9), axis=1, keepdims=True)


def _router_kernel(lg_ref, gates_ref, s0_ref, s1_ref, cnt_ref, ltri, carry, pstart):
    ph = pl.program_id(0)
    i = pl.program_id(1)
    tm = lg_ref.shape[0]

    @pl.when(jnp.logical_and(ph == 0, i == 0))
    def _():
        t = lax.broadcasted_iota(jnp.int32, (tm, tm), 0)
        s = lax.broadcasted_iota(jnp.int32, (tm, tm), 1)
        ltri[...] = jnp.where(s < t, 1.0, 0.0).astype(BF16)

    @pl.when(jnp.logical_and(ph == 1, i == 0))
    def _():
        cnt = carry[...]
        padded = jnp.floor((cnt + (MOE_PAD - 1.0)) * (1.0 / MOE_PAD)) * MOE_PAD
        r = lax.broadcasted_iota(jnp.int32, (LANES, LANES), 0)
        c = lax.broadcasted_iota(jnp.int32, (LANES, LANES), 1)
        before = jnp.where(r < c, 1.0, 0.0).astype(BF16)
        pstart[...] = _dot(padded.astype(BF16), before)
        cnt_ref[...] = cnt.astype(jnp.int32)

    @pl.when(i == 0)
    def _():
        carry[...] = jnp.zeros_like(carry)

    lg = lg_ref[...]
    lane = lax.broadcasted_iota(jnp.int32, lg.shape, 1)
    lane_f = lane.astype(F32)
    neg = -jnp.inf

    gl = jnp.where(lane < N_GROUPS, lg, neg)
    gmax = jnp.max(gl, axis=1, keepdims=True)
    grp = _lane_first(gl == gmax, lane_f).astype(jnp.int32)
    p_sel = 1.0 / jnp.sum(jnp.exp(gl - gmax), axis=1, keepdims=True)

    in_grp = jnp.logical_and(
        jnp.logical_and(lane >= N_GROUPS, lane < N_GROUPS + N_EXPERTS),
        ((lane - N_GROUPS) >> 3) == grp)
    el = jnp.where(in_grp, lg, neg)
    v1 = jnp.max(el, axis=1, keepdims=True)
    o1 = lane_f == _lane_first(el == v1, lane_f)
    el2 = jnp.where(o1, neg, el)
    v2 = jnp.max(el2, axis=1, keepdims=True)
    o2 = lane_f == _lane_first(el2 == v2, lane_f)

    osum = (jnp.where(o1, 1.0, 0.0) + jnp.where(o2, 1.0, 0.0)).astype(BF16)
    earlier = _dot(ltri[...], osum) + carry[0:1, :]
    carry[...] = carry[...] + _dot(jnp.ones((8, tm), BF16), osum)

    @pl.when(ph == 1)
    def _():
        ez = jnp.exp(v2 - v1)
        g1 = p_sel / (1.0 + ez)
        gates_ref[...] = jnp.concatenate([g1, g1 * ez], axis=1)
        base = earlier + pstart[0:1, :]
        for o, s_ref in ((o1, s0_ref), (o2, s1_ref)):
            col = jnp.sum(jnp.where(o, base, 0.0), axis=1, keepdims=True)
            wide = jnp.broadcast_to(col, (tm, LANES))
            for b in range(tm // LANES):
                s_ref[b:b + 1, :] = wide[b * LANES:(b + 1) * LANES, :].T[0:1, :].astype(jnp.int32)


def _router(logits):
    t = logits.shape[0]
    tm = ROUTER_TM
    rows = tm // LANES
    return pl.pallas_call(
        _router_kernel,
        grid=(2, t // tm),
        in_specs=[pl.BlockSpec((tm, LANES), lambda ph, i: (i, 0))],
        out_specs=[
            pl.BlockSpec((tm, EXPERT_TOPK), lambda ph, i: (i * ph, 0)),
            pl.BlockSpec((rows, LANES), lambda ph, i: (i * ph, 0)),
            pl.BlockSpec((rows, LANES), lambda ph, i: (i * ph, 0)),
            pl.BlockSpec((8, LANES), lambda ph, i: (0, 0)),
        ],
        out_shape=[
            jax.ShapeDtypeStruct((t, EXPERT_TOPK), F32),
            jax.ShapeDtypeStruct((t // LANES, LANES), jnp.int32),
            jax.ShapeDtypeStruct((t // LANES, LANES), jnp.int32),
            jax.ShapeDtypeStruct((8, LANES), jnp.int32),
        ],
        scratch_shapes=[
            pltpu.VMEM((tm, tm), BF16),
            pltpu.VMEM((8, LANES), F32),
            pltpu.VMEM((8, LANES), F32),
        ],
        compiler_params=_cparams(2, 32),
        name="router",
    )(logits)


def _round_up_pow2(x, m):
    assert m & (m - 1) == 0
    return (x + (m - 1)) & ~(m - 1)


def _plan_kernel(cnt_ref, s0_ref, s1_ref, tok_ref, ie_ref, ist_ref, ir_ref, if_ref, zeros, sem):
    n_tok = s0_ref.shape[0]
    n_slots = tok_ref.shape[0]
    n_items = ie_ref.shape[0]
    r = MOE_ITEM_ROWS

    zeros[...] = jnp.zeros_like(zeros)
    clear = pltpu.make_async_copy(zeros, tok_ref, sem)
    clear.start()
    clear.wait()

    def scatter(t16, carry):
        for d in range(16):
            t = t16 * 16 + d
            tok_ref[s0_ref[t]] = t
            tok_ref[s1_ref[t]] = t
        return carry
    lax.fori_loop(0, n_tok // 16, scatter, 0)

    a = jnp.int32(0)
    used = jnp.int32(0)
    e_last = jnp.int32(0)
    for e in range(N_EXPERTS):
        pad = _round_up_pow2(cnt_ref[0, N_GROUPS + e], MOE_PAD)

        def item(sub, a, e=e, pad=pad, used=used):
            ie_ref[a] = e
            ist_ref[a] = used + sub * r
            ir_ref[a] = jnp.minimum(pad - sub * r, r)
            if_ref[a] = 0
            return a + 1
        a = lax.fori_loop(0, _round_up_pow2(pad, r) >> (r.bit_length() - 1), item, a)
        e_last = jnp.where(pad > 0, e, e_last)
        used = used + pad

    def idle(a2, carry):
        fill_start = used + (a2 - a) * r
        fill = jnp.clip(n_slots - fill_start, 0, r)
        ie_ref[a2] = e_last
        ist_ref[a2] = jnp.where(fill > 0, fill_start, 0)
        ir_ref[a2] = 0
        if_ref[a2] = fill
        return carry
    lax.fori_loop(a, n_items, idle, 0)


def _plan(counts, s0, s1, n_slots, n_items):
    smem = pl.BlockSpec(memory_space=pltpu.SMEM)
    item = jax.ShapeDtypeStruct((n_items,), jnp.int32)
    return pl.pallas_call(
        _plan_kernel,
        in_specs=[smem, smem, smem],
        out_specs=[smem] * 5,
        out_shape=[jax.ShapeDtypeStruct((n_slots,), jnp.int32), item, item, item, item],
        scratch_shapes=[pltpu.VMEM((n_slots,), jnp.int32), pltpu.SemaphoreType.DMA(())],
        name="plan",
    )(counts, s0, s1)


def _unpack_pairs(u):
    lo = lax.bitcast_convert_type(u << 16, F32)
    hi = lax.bitcast_convert_type(u & jnp.uint32(0xFFFF0000), F32)
    return jnp.concatenate([lo, hi], axis=1)


def _store_row_tiles(dst, packed):
    r = packed.shape[0]
    for s in range(TILE_ROWS):
        dst[pl.ds(s, r, stride=TILE_ROWS), :] = packed[:, s * LANES:(s + 1) * LANES]


def _load_row_tiles(src, first_row, r):
    return jnp.concatenate(
        [src[pl.ds(first_row * TILE_ROWS + s, r, stride=TILE_ROWS), :] for s in range(TILE_ROWS)],
        axis=1)


def _pack_pairs(y):
    w = y.shape[1] // 2
    bits = lax.bitcast_convert_type(y.astype(BF16).astype(F32), jnp.uint32)
    return (bits[:, :w] >> 16) | (bits[:, w:] & jnp.uint32(0xFFFF0000))


def _moe_block_sizes():
    sizes, rows = [], MOE_BLOCK
    while rows >= MOE_PAD:
        sizes.append(rows)
        rows //= 2
    return sizes


def _moe_kernel(item_e, item_start, item_rows, item_fill, slot_tok,
                h2p_hbm, wg_ref, wu_ref, wd_ref, yb_hbm,
                xbuf, act, ystage, pf, gsem, osem):
    a = pl.program_id(0)
    t = pl.program_id(1)
    n_items = pl.num_programs(0)
    n = item_rows[a]
    start = item_start[a]
    half_w = wg_ref.shape[2]
    cur = a % 2
    nxt = 1 - cur
    a1 = jnp.minimum(a + 1, n_items - 1)
    n_next = jnp.where(a + 1 < n_items, item_rows[a1], 0)
    start_next = item_start[a1]

    def tile_rows(i, count=1):
        return pl.ds(pl.multiple_of(i * TILE_ROWS, TILE_ROWS), count * TILE_ROWS)

    def row_copy(tok, slot, i):
        return pltpu.make_async_copy(h2p_hbm.at[tile_rows(tok), :], xbuf.at[slot, tile_rows(i), :],
                                     gsem.at[slot])

    def start_row(tok, slot, i):
        row_copy(tok, slot, i).start(priority=1)

    @pl.when(jnp.logical_and(a == 0, t == 0))
    def _():
        pf[1] = 0
        pf[2] = 0

        def issue(i, carry):
            start_row(slot_tok[start + i], 0, i)
            return carry
        lax.fori_loop(0, n, issue, 0)

    @pl.when(t == 0)
    def _():
        pf[0] = 0

        def drain(bk, carry):
            r0 = pl.multiple_of(bk * MOE_PAD, MOE_PAD)
            pltpu.make_async_copy(h2p_hbm.at[tile_rows(0, MOE_PAD), :],
                                  xbuf.at[cur, tile_rows(r0, MOE_PAD), :], gsem.at[cur]).wait()
            return carry
        lax.fori_loop(0, n // MOE_PAD, drain, 0)

    def prefetch_group():
        base = pf[0]
        for q in range(MOE_GATHER_GROUP):
            start_row(slot_tok[start_next + base + q], nxt, base + q)
        pf[0] = base + MOE_GATHER_GROUP

    def run_blocks(compute):
        def one(b, r0, rows):
            r0 = pl.multiple_of(r0, MOE_PAD)
            more = pf[0] < n_next

            def gather():
                prefetch_group()
                for _ in range(max(rows // (2 * MOE_GATHER_GROUP), 1) - 1):
                    pl.when(pf[0] < n_next)(prefetch_group)

            @pl.when(more)
            def _():
                compute(b, r0, rows, gather)

            @pl.when(jnp.logical_not(more))
            def _():
                compute(b, r0, rows, lambda: None)

        n_big = n // MOE_BLOCK

        def body(b, carry):
            one(b, b * MOE_BLOCK, MOE_BLOCK)
            return carry
        lax.fori_loop(0, n_big, body, 0)

        b, r0, rows = n_big, n_big * MOE_BLOCK, MOE_BLOCK // 2
        while rows >= MOE_PAD:
            has = (n & rows) != 0
            pl.when(has)(functools.partial(one, b, r0, rows))
            b = b + has.astype(jnp.int32)
            r0 = r0 + jnp.where(has, rows, 0)
            rows //= 2

    def gate_up(half):
        def compute(b, r0, rows, mid):
            xs = _unpack_pairs(_load_row_tiles(xbuf.at[cur], r0, rows)).astype(BF16)
            g = _dot(xs, wg_ref[0].astype(BF16))
            u = _dot(xs, wu_ref[0].astype(BF16))
            mid()
            act[pl.ds(r0, rows), half * half_w:(half + 1) * half_w] = (
                g * _sigmoid(g) * u).astype(BF16)
        run_blocks(compute)

    def out_copy(slot, r0, rows):
        return pltpu.make_async_copy(
            ystage.at[slot, tile_rows(0, rows), :],
            yb_hbm.at[tile_rows(pl.multiple_of(start + r0, MOE_PAD), rows), :],
            osem.at[slot])

    def wait_stage(slot):
        for rows in _moe_block_sizes():
            @pl.when(pf[1 + slot] == rows)
            def _(rows=rows):
                out_copy(slot, 0, rows).wait()
                pf[1 + slot] = 0

    def down():
        def compute(b, r0, rows, mid):
            slot = b % 2
            wait_stage(slot)
            y = _dot(act[pl.ds(r0, rows), :], wd_ref[0].astype(BF16))
            mid()
            _store_row_tiles(ystage.at[slot], _pack_pairs(y))
            out_copy(slot, r0, rows).start()
            pf[1 + slot] = rows
        run_blocks(compute)

        def tail(i, carry):
            start_row(slot_tok[start_next + i], nxt, i)
            return carry
        lax.fori_loop(pf[0], n_next, tail, 0)

    pl.when(t == 0)(functools.partial(gate_up, 0))
    pl.when(t == 1)(functools.partial(gate_up, 1))
    pl.when(t == 2)(down)

    fill = item_fill[a]

    @pl.when(jnp.logical_and(t == 0, fill > 0))
    def _():
        wait_stage(0)
        ystage[0] = jnp.zeros(ystage.shape[1:], ystage.dtype)

        def zero_block(bk, carry):
            cp = out_copy(0, bk * MOE_PAD, MOE_PAD)
            cp.start()
            cp.wait()
            return carry
        lax.fori_loop(0, fill // MOE_PAD, zero_block, 0)

    @pl.when(jnp.logical_and(a == n_items - 1, t == pl.num_programs(1) - 1))
    def _():
        wait_stage(0)
        wait_stage(1)


def _moe_ffn(h2p, w_gate, w_up, w_down, item_e, item_start, item_rows, item_fill, slot_tok,
             n_slots):
    d = w_gate.shape[1]
    assert d == 2 * TILE_ROWS * LANES and h2p.shape[1] == LANES, "one packed token row per tile"
    n_items = item_e.shape[0]
    half_w = D_EXPERT // 2

    def wgu_map(a, t, ie, ist, ir, fl, st):
        return (ie[a], 0, jnp.where(ir[a] > 0, jnp.minimum(t, 1), 1))

    def wd_map(a, t, ie, ist, ir, fl, st):
        return (jnp.where(t >= 2, ie[a], ie[jnp.maximum(a - 1, 0)]), 0, 0)

    grid_spec = pltpu.PrefetchScalarGridSpec(
        num_scalar_prefetch=5,
        grid=(n_items, 3),
        in_specs=[
            pl.BlockSpec(memory_space=pl.ANY),
            pl.BlockSpec((1, d, half_w), wgu_map),
            pl.BlockSpec((1, d, half_w), wgu_map),
            pl.BlockSpec((1, D_EXPERT, d), wd_map),
        ],
        out_specs=pl.BlockSpec(memory_space=pl.ANY),
        scratch_shapes=[
            pltpu.VMEM((2, MOE_ITEM_ROWS * TILE_ROWS, LANES), jnp.uint32),
            pltpu.VMEM((MOE_ITEM_ROWS, D_EXPERT), BF16),
            pltpu.VMEM((2, MOE_BLOCK * TILE_ROWS, LANES), jnp.uint32),
            pltpu.SMEM((3,), jnp.int32),
            pltpu.SemaphoreType.DMA((2,)),
            pltpu.SemaphoreType.DMA((2,)),
        ],
    )
    return pl.pallas_call(
        _moe_kernel,
        grid_spec=grid_spec,
        out_shape=jax.ShapeDtypeStruct((n_slots * TILE_ROWS, LANES), jnp.uint32),
        compiler_params=_cparams(2, 56),
        name="moe",
    )(item_e, item_start, item_rows, item_fill, slot_tok, h2p, w_gate, w_up, w_down)


def _combine_kernel(s0, s1, yb_hbm, gates_ref, x1_ref, nw_ref, g2_ref, out_ref, rows, sem):
    i = pl.program_id(0)
    n_tiles = pl.num_programs(0)
    tm = x1_ref.shape[0]

    def tile_rows(i, count=1):
        return pl.ds(pl.multiple_of(i * TILE_ROWS, TILE_ROWS), count * TILE_ROWS)

    def gather(tile, buf):
        def issue(r8, carry):
            t0 = tile * tm + r8 * 8
            for dr in range(8):
                for k, s_k in enumerate((s0, s1)):
                    pltpu.make_async_copy(yb_hbm.at[tile_rows(s_k[t0 + dr]), :],
                                          rows.at[buf, k, tile_rows(r8 * 8 + dr), :],
                                          sem.at[buf]).start()
            return carry
        lax.fori_loop(0, tm // 8, issue, 0)

    @pl.when(i == 0)
    def _():
        gather(0, 0)

    @pl.when(i + 1 < n_tiles)
    def _():
        gather(i + 1, (i + 1) % 2)

    buf = i % 2
    for k in range(EXPERT_TOPK):
        pltpu.make_async_copy(yb_hbm.at[tile_rows(0, tm), :], rows.at[buf, k], sem.at[buf]).wait()

    g = gates_ref[...]
    y = (_unpack_pairs(_load_row_tiles(rows.at[buf, 0], 0, tm)) * g[:, 0:1]
         + _unpack_pairs(_load_row_tiles(rows.at[buf, 1], 0, tm)) * g[:, 1:2])
    out_ref[...] = x1_ref[...] + g2_ref[0:1, :] * (_rms(y) * nw_ref[3:4, :])


def _combine(s0, s1, yb, gates, x1, norm_w, mod):
    n, d = x1.shape
    tm = 256
    grid_spec = pltpu.PrefetchScalarGridSpec(
        num_scalar_prefetch=2,
        grid=(n // tm,),
        in_specs=[
            pl.BlockSpec(memory_space=pl.ANY),
            pl.BlockSpec((tm, EXPERT_TOPK), lambda i, *_: (i, 0)),
            pl.BlockSpec((tm, d), lambda i, *_: (i, 0)),
            pl.BlockSpec((4, d), lambda i, *_: (0, 0)),
            pl.BlockSpec((8, d), lambda i, *_: (0, 5)),
        ],
        out_specs=pl.BlockSpec((tm, d), lambda i, *_: (i, 0)),
        scratch_shapes=[
            pltpu.VMEM((2, EXPERT_TOPK, tm * TILE_ROWS, LANES), jnp.uint32),
            pltpu.SemaphoreType.DMA((2,)),
        ],
    )
    return pl.pallas_call(
        _combine_kernel,
        grid_spec=grid_spec,
        out_shape=jax.ShapeDtypeStruct((n, d), F32),
        compiler_params=_cparams(1, 40),
        name="combine",
    )(s0, s1, yb, gates, x1, norm_w, mod)


def _rope_tables(n):
    rows = n // GRID_W
    r = np.repeat(np.arange(rows, dtype=np.float32), GRID_W)
    cidx = np.tile(np.arange(GRID_W, dtype=np.float32), rows)
    n_freq = RET_DK // 4
    inv = np.float32(ROPE_BASE) ** (-np.arange(n_freq, dtype=np.float32) / np.float32(n_freq))
    ang = np.concatenate([r[:, None] * inv, cidx[:, None] * inv], axis=-1).astype(np.float32)
    cos, sin = np.cos(ang), np.sin(ang)
    cs = np.concatenate([cos, cos], axis=-1)
    sn = np.concatenate([-sin, sin], axis=-1)
    return jnp.asarray(cs, F32), jnp.asarray(sn, F32)


def kernel(x, c, ctx, c_ctx, w_mod, b_mod, norm_w, w_in, hg_lb_logits, hg_norm_w, ret_decay_logits,
           ret_norm_w, w_out, router_group_w, router_group_b, router_expert_w, router_expert_b,
           w_gate, w_up, w_down):
    bsz, n, d = x.shape
    l = ctx.shape[1]
    assert bsz == 1 and w_mod.shape[0] == 1, "single sample, single layer"
    x2d, ctx2d = x[0], ctx[0]

    cc = jnp.zeros((8, d), F32).at[0].set(c[0]).at[1].set(c_ctx)
    mod = _modulation(cc, w_mod[0], b_mod[0])

    lb = jnp.cumsum(jax.nn.softmax(hg_lb_logits.astype(F32), axis=0), axis=0)[0]
    rk0 = 5 * HG_WIDTH + RET_HEADS * RET_DK
    cols = jnp.stack([
        jnp.zeros((PROJ_WIDTH,), F32).at[HG_WIDTH:3 * HG_WIDTH].set(lb.reshape(-1)),
        jnp.ones((PROJ_WIDTH,), F32).at[rk0:rk0 + RET_HEADS * RET_DK].set(RET_DK ** -0.5),
    ])
    w_in_bf = w_in[0].astype(BF16)
    p, g = _projection(x2d, norm_w[0], mod, w_in_bf, cols, 0, 1024, "proj_x")
    pc, gc = _projection(ctx2d, norm_w[0], mod, w_in_bf, cols, 1, l, "proj_ctx")

    ret_lg = jax.nn.log_sigmoid(ret_decay_logits[0].astype(F32))
    cs_tab, sn_tab = _rope_tables(n)

    o_f, o_b = _hgrn_scan(p, g, pc, gc)
    r_f, r_b = _ret_scan(p, pc, ret_lg, cs_tab, sn_tab)

    wr = jnp.concatenate([router_group_w[0], router_expert_w[0]], axis=1)
    wr = jnp.pad(wr, ((0, 0), (0, ROUTER_LANES - wr.shape[1])))
    wr_hi = wr.astype(BF16)
    wr_parts = jnp.concatenate([wr_hi, (wr - wr_hi.astype(F32)).astype(BF16)], axis=1)
    br = jnp.concatenate([router_group_b[0], router_expert_b[0]]).astype(F32)
    br = jnp.pad(br, (0, ROUTER_LANES - br.shape[0])).reshape(1, ROUTER_LANES)
    x1, h2, logits = _out_projection(
        o_f, o_b, r_f, r_b, p, x2d, w_out[0].astype(BF16), hg_norm_w, ret_norm_w, norm_w[0], mod,
        wr_parts, br)

    n_assign = n * EXPERT_TOPK
    n_slots = n_assign + N_EXPERTS * MOE_PAD
    n_items = (n_slots + N_EXPERTS * (MOE_ITEM_ROWS - MOE_PAD)) // MOE_ITEM_ROWS
    gates, s0, s1, counts = _router(logits)
    s0, s1 = s0.reshape(-1), s1.reshape(-1)
    slot_tok, item_e, item_start, item_rows, item_fill = _plan(counts, s0, s1, n_slots, n_items)

    yb = _moe_ffn(h2, w_gate[0], w_up[0], w_down[0], item_e, item_start, item_rows, item_fill,
                  slot_tok, n_slots)
    out = _combine(s0, s1, yb, gates, x1, norm_w[0], mod)
    return out[None]
```

```python
import functools

import jax
import jax.numpy as jnp
import numpy as np
from jax import lax
from jax.experimental import pallas as pl
from jax.experimental.pallas import tpu as pltpu

F32 = jnp.float32
BF16 = jnp.bfloat16

D_MODEL = 2048
GRID_W = 64
HG_HEADS = 8
HG_DK = 128
HG_WIDTH = 1024
RET_HEADS = 4
RET_DK = 128
RET_DV = 256
RET_WIDTH = 1024
PROJ_WIDTH = 8192
N_GROUPS = 4
EXPERTS_PER_GROUP = 8
N_EXPERTS = 32
EXPERT_TOPK = 2
D_EXPERT = 1024
RMS_EPS = 1e-6
ROPE_BASE = 10000.0

COL_HQ, COL_HF_FWD, COL_HF_BWD, COL_HI, COL_HGATE = 0, 1, 2, 3, 4
COL_RQ_512, COL_RK_512 = 10, 11
COL_RV, COL_RGATE = 6, 7

HG_CHUNK = 128
HG_STEP_HEADS = 8
HG_DIAG = 16
HG_LOG_CLAMP = 10.0
RET_CHUNK = 256

MOE_PAD = 128
MOE_BLOCK = 256
MOE_ITEM_ROWS = 1024
MOE_GATHER_GROUP = 128
ROUTER_LANES = 128
LANES = 128
TILE_ROWS = 8


def _cparams(n_axes, vmem_mb, **flags):
    return pltpu.CompilerParams(
        dimension_semantics=("arbitrary",) * n_axes,
        vmem_limit_bytes=vmem_mb * 1024 * 1024,
        flags=flags or None,
    )


def _sigmoid(x):
    return 1.0 / (1.0 + jnp.exp(-x))


def _dot(a, b):
    return jnp.dot(a, b, preferred_element_type=F32)


def _dot_nt(a, b):
    return lax.dot_general(a, b, (((1,), (1,)), ((), ())), preferred_element_type=F32)


def _dot_tn(a, b):
    return lax.dot_general(a, b, (((0,), (0,)), ((), ())), preferred_element_type=F32)


def _mod_kernel(cc_ref, w_ref, b_ref, o_ref):
    s = cc_ref[...]
    s = s * _sigmoid(s)
    s16 = jnp.concatenate([s, s], axis=0)
    hi = s16.astype(BF16).astype(F32)
    row = lax.broadcasted_iota(jnp.int32, s16.shape, 0)
    lhs = jnp.where(row < 8, hi, s16 - hi).astype(BF16)
    r = _dot(lhs, w_ref[...].astype(BF16))
    o_ref[...] = r[:8] + r[8:] + b_ref[...]


def _modulation(cc, w_mod, b_mod):
    d, n = w_mod.shape
    tn = 1024
    return pl.pallas_call(
        _mod_kernel,
        grid=(n // tn,),
        in_specs=[
            pl.BlockSpec((8, d), lambda j: (0, 0)),
            pl.BlockSpec((d, tn), lambda j: (0, j)),
            pl.BlockSpec((1, tn), lambda j: (0, j)),
        ],
        out_specs=pl.BlockSpec((8, tn), lambda j: (0, j)),
        out_shape=jax.ShapeDtypeStruct((8, n), F32),
        compiler_params=_cparams(1, 40),
        name="mod",
    )(cc, w_mod, b_mod.reshape(1, n))


PROJ_TN = 1024
PROLOGUE_ROWS = 32


def _proj_kernel(x_ref, nw_ref, sh_ref, sc_ref, w_ref, cols_ref, o_ref, g_ref, h_scr, *, mod_row):
    i = pl.program_id(0)
    j = pl.program_id(1)
    last_j = pl.num_programs(1) - 1
    cur = i % 2

    def normed(x):
        scale = nw_ref[0:1, :] * (1.0 + sc_ref[mod_row:mod_row + 1, :])
        shift = sh_ref[mod_row:mod_row + 1, :]
        y = x * lax.rsqrt(jnp.mean(x * x, axis=-1, keepdims=True) + RMS_EPS)
        return (y * scale + shift).astype(BF16)

    @pl.when(jnp.logical_and(i == 0, j == 0))
    def _():
        def body(r, carry):
            r0 = pl.multiple_of(r * PROLOGUE_ROWS, PROLOGUE_ROWS)
            h_scr[0, pl.ds(r0, PROLOGUE_ROWS), :] = normed(x_ref[pl.ds(r0, PROLOGUE_ROWS), :])
            return carry
        lax.fori_loop(0, x_ref.shape[0] // PROLOGUE_ROWS, body, 0)

    is_gate = jnp.logical_or(j == 1, j == 2)
    is_silu = jnp.logical_or(j == 0, j == 4)
    is_lin = jnp.logical_not(jnp.logical_or(jnp.logical_or(is_gate, is_silu), j == last_j))

    @pl.when(is_lin)
    def _():
        o_ref[...] = (_dot(h_scr[cur], w_ref[...]) * cols_ref[1:2, :]).astype(BF16)

    @pl.when(is_silu)
    def _():
        acc = _dot(h_scr[cur], w_ref[...])
        o_ref[...] = (acc * _sigmoid(acc)).astype(BF16)

    @pl.when(is_gate)
    def _():
        acc = _dot(h_scr[cur], w_ref[...])
        lb = cols_ref[0:1, :]
        g = jnp.maximum(jnp.log(lb + (1.0 - lb) * _sigmoid(acc)), -HG_LOG_CLAMP)
        g_ref[...] = g
        o_ref[...] = g.astype(BF16)

    @pl.when(j == last_j)
    def _():
        acc = _dot(h_scr[cur], w_ref[...])
        o_ref[...] = (acc * _sigmoid(acc)).astype(BF16)
        h_scr[1 - cur] = normed(x_ref[...])


def _projection(x2d, norm_w, mod, w_in, cols, mod_row, tm, name):
    n, d = x2d.shape
    tn = PROJ_TN
    n_i, n_j = n // tm, PROJ_WIDTH // tn

    def x_map(i, j):
        return (jnp.minimum(i + (j == n_j - 1).astype(jnp.int32), n_i - 1), 0)

    return pl.pallas_call(
        functools.partial(_proj_kernel, mod_row=mod_row),
        grid=(n_i, n_j),
        in_specs=[
            pl.BlockSpec((tm, d), x_map),
            pl.BlockSpec((4, d), lambda i, j: (0, 0)),
            pl.BlockSpec((8, d), lambda i, j: (0, 0)),
            pl.BlockSpec((8, d), lambda i, j: (0, 1)),
            pl.BlockSpec((d, tn), lambda i, j: (0, j)),
            pl.BlockSpec((2, tn), lambda i, j: (0, j)),
        ],
        out_specs=[
            pl.BlockSpec((tm, tn), lambda i, j: (i, j)),
            pl.BlockSpec((tm, tn), lambda i, j: (i, jnp.clip(j - 1, 0, 1))),
        ],
        out_shape=[
            jax.ShapeDtypeStruct((n, PROJ_WIDTH), BF16),
            jax.ShapeDtypeStruct((n, 2 * HG_WIDTH), F32),
        ],
        scratch_shapes=[pltpu.VMEM((2, tm, d), BF16)],
        compiler_params=_cparams(2, 56),
        name=name,
    )(x2d, norm_w, mod, mod, w_in, cols)


def _tri_ones(c, rev):
    t = lax.broadcasted_iota(jnp.int32, (c, c), 0)
    s = lax.broadcasted_iota(jnp.int32, (c, c), 1)
    return jnp.where((s >= t) if rev else (s <= t), 1.0, 0.0).astype(BF16)


def _cumsum_rows(g, tri):
    hi = g.astype(BF16)
    lo = (g - hi.astype(F32)).astype(BF16)
    return _dot(tri, hi) + _dot(tri, lo)


def _row_refs(b, idxs, blk):
    parts = [jnp.broadcast_to(b[i:i + 1, :], (blk, b.shape[1])) for i in idxs]
    return parts[0] if len(parts) == 1 else jnp.concatenate(parts, axis=0)


def _hgrn_levels(c, rev):
    t = lax.broadcasted_iota(jnp.int32, (c, c), 0)
    s = lax.broadcasted_iota(jnp.int32, (c, c), 1)
    levels = []
    m = c // 2
    while m >= HG_DIAG:
        blk = 2 * m
        same = ((t ^ s) & ~(blk - 1)) == 0
        if rev:
            idxs = [b0 + m for b0 in range(0, c, blk)]
            cross = jnp.logical_and((t & m) == 0, (s & m) != 0)
        else:
            idxs = [b0 + m - 1 for b0 in range(0, c, blk)]
            cross = jnp.logical_and((t & m) != 0, (s & m) == 0)
        levels.append((idxs, blk, jnp.logical_and(same, cross)))
        m //= 2
    blk = HG_DIAG
    same = ((t ^ s) & ~(blk - 1)) == 0
    if rev:
        idxs = [b0 + blk // 2 for b0 in range(0, c, blk)]
        mask = jnp.logical_and(same, s >= t)
    else:
        idxs = [b0 + blk // 2 - 1 for b0 in range(0, c, blk)]
        mask = jnp.logical_and(same, s <= t)
    levels.append((idxs, blk, mask))
    return levels


def _hgrn_head(q_bf, g, b, v_bf, st, rev, levels):
    c = g.shape[0]
    kk = 1.0 - jnp.exp(g)

    o = None
    if levels is not None:
        q = q_bf.astype(F32)
        att = jnp.zeros((c, c), F32)
        for idxs, blk, mask in levels:
            ref = _row_refs(b, idxs, blk)
            a_l = _dot_nt((q * jnp.exp(b - ref)).astype(BF16),
                          (kk * jnp.exp(ref - b)).astype(BF16))
            att = att + jnp.where(mask, a_l, 0.0)
        o = _dot(att.astype(BF16), v_bf) + _dot_nt((q * jnp.exp(b)).astype(BF16), st.astype(BF16))

    b_end = b[0:1, :] if rev else b[c - 1:c, :]
    k_end = (kk * jnp.exp(b_end - b)).astype(BF16)
    st_new = st * jnp.exp(b_end) + _dot_tn(v_bf, k_end)
    return o, st_new


def _hgrn_ctx_kernel(gf_ref, vf_ref, gb_ref, vb_ref, st_ref):
    @pl.when(pl.program_id(0) == 0)
    def _():
        st_ref[...] = jnp.zeros_like(st_ref)

    c = gf_ref.shape[0]
    bf = _cumsum_rows(gf_ref[...], _tri_ones(c, False))
    bb = _cumsum_rows(gb_ref[...], _tri_ones(c, True))
    for h in range(HG_HEADS):
        sl = slice(h * HG_DK, (h + 1) * HG_DK)
        _, st_ref[h] = _hgrn_head(None, gf_ref[:, sl], bf[:, sl], vf_ref[:, sl], st_ref[h],
                                  False, None)
        hb = HG_HEADS + h
        _, st_ref[hb] = _hgrn_head(None, gb_ref[:, sl], bb[:, sl], vb_ref[:, sl], st_ref[hb],
                                   True, None)


def _hgrn_kernel(qf_ref, gf_ref, vf_ref, qb_ref, gb_ref, vb_ref, st0_ref, of_ref, ob_ref, st_ref):
    hg = pl.program_id(1)

    @pl.when(jnp.logical_and(pl.program_id(0) == 0, hg == 0))
    def _():
        st_ref[...] = st0_ref[...]

    c = gf_ref.shape[0]
    lev_f = _hgrn_levels(c, False)
    lev_b = _hgrn_levels(c, True)
    bf = _cumsum_rows(gf_ref[...], _tri_ones(c, False))
    bb = _cumsum_rows(gb_ref[...], _tri_ones(c, True))
    for h in range(HG_STEP_HEADS):
        sl = slice(h * HG_DK, (h + 1) * HG_DK)
        hf = hg * HG_STEP_HEADS + h
        of_ref[:, sl], st_ref[hf] = _hgrn_head(qf_ref[:, sl], gf_ref[:, sl], bf[:, sl],
                                               vf_ref[:, sl], st_ref[hf], False, lev_f)
        hb = HG_HEADS + hf
        ob_ref[:, sl], st_ref[hb] = _hgrn_head(qb_ref[:, sl], gb_ref[:, sl], bb[:, sl],
                                               vb_ref[:, sl], st_ref[hb], True, lev_b)


def _hgrn_scan(p, g, pc, gc):
    c = HG_CHUNK
    w = HG_WIDTH
    n, l = p.shape[0], pc.shape[0]
    n_x, n_ctx = n // c, l // c
    st_shape = (2 * HG_HEADS, HG_DK, HG_DK)

    def spec(chunk_fn, col):
        return pl.BlockSpec((c, w), lambda s: (chunk_fn(s), col))

    st0 = pl.pallas_call(
        _hgrn_ctx_kernel,
        grid=(n_ctx,),
        in_specs=[
            spec(lambda s: s, 0), spec(lambda s: s, COL_HI),
            spec(lambda s: n_ctx - 1 - s, 1), spec(lambda s: n_ctx - 1 - s, COL_HI),
        ],
        out_specs=pl.BlockSpec(st_shape, lambda s: (0, 0, 0)),
        out_shape=jax.ShapeDtypeStruct(st_shape, F32),
        compiler_params=_cparams(1, 32),
        name="hgrn_ctx",
    )(gc, pc, gc, pc)

    n_hg = HG_HEADS // HG_STEP_HEADS
    wg = HG_STEP_HEADS * HG_DK
    fwd = lambda s: s
    bwd = lambda s: n_x - 1 - s

    def gspec(chunk_fn, col):
        return pl.BlockSpec((c, wg), lambda s, hg: (chunk_fn(s), col * n_hg + hg))

    return pl.pallas_call(
        _hgrn_kernel,
        grid=(n_x, n_hg),
        in_specs=[
            gspec(fwd, COL_HQ), gspec(fwd, 0), gspec(fwd, COL_HI),
            gspec(bwd, COL_HQ), gspec(bwd, 1), gspec(bwd, COL_HI),
            pl.BlockSpec(st_shape, lambda s, hg: (0, 0, 0)),
        ],
        out_specs=[gspec(fwd, 0), gspec(bwd, 0)],
        out_shape=[jax.ShapeDtypeStruct((n, w), F32)] * 2,
        scratch_shapes=[pltpu.VMEM(st_shape, F32)],
        compiler_params=_cparams(2, 32),
        name="hgrn",
    )(p, g, p, p, g, p, st0)


def _rope(x_bf, cs, sn):
    x = x_bf.astype(F32)
    return x * cs + pltpu.roll(x, RET_DK // 2, axis=1) * sn


def _ret_decays(lg, c, rev):
    t = lax.broadcasted_iota(jnp.int32, (c, c), 0)
    s = lax.broadcasted_iota(jnp.int32, (c, c), 1)
    diff = (s - t) if rev else (t - s)
    pair = jnp.where(diff >= 0, jnp.exp(lg * jnp.maximum(diff, 0).astype(F32)), 0.0)
    pos = lax.broadcasted_iota(jnp.int32, (c, RET_DK), 0).astype(F32)
    q_dec = jnp.exp(lg * ((c - pos) if rev else (pos + 1.0)))
    k_dec = jnp.exp(lg * (pos if rev else (c - 1.0 - pos)))
    chunk = jnp.exp(jnp.full((8, RET_DK), lg * c, F32))
    return pair, q_dec, k_dec, chunk


def _ret_head(q, k, v_bf, decays, st):
    pair, q_dec, k_dec, chunk = decays
    o = None
    if q is not None:
        att = _dot_nt(q.astype(BF16), k.astype(BF16)) * pair
        o = _dot(att.astype(BF16), v_bf) + _dot_nt((q * q_dec).astype(BF16), st.astype(BF16))
    st_new = st * chunk[0:1, :] + _dot_tn(v_bf, (k * k_dec).astype(BF16))
    return o, st_new


def _ret_ctx_kernel(lg_ref, kf_ref, vf_ref, kb_ref, vb_ref, st_ref):
    @pl.when(pl.program_id(0) == 0)
    def _():
        st_ref[...] = jnp.zeros_like(st_ref)

    c = kf_ref.shape[0]
    for h in range(RET_HEADS):
        sk = slice(h * RET_DK, (h + 1) * RET_DK)
        sv = slice(h * RET_DV, (h + 1) * RET_DV)
        _, st_ref[h] = _ret_head(None, kf_ref[:, sk].astype(F32), vf_ref[:, sv],
                                 _ret_decays(lg_ref[0, h], c, False), st_ref[h])
        hb = RET_HEADS + h
        _, st_ref[hb] = _ret_head(None, kb_ref[:, sk].astype(F32), vb_ref[:, sv],
                                  _ret_decays(lg_ref[1, h], c, True), st_ref[hb])


def _ret_kernel(lg_ref, qf_ref, kf_ref, vf_ref, csf_ref, snf_ref,
                qb_ref, kb_ref, vb_ref, csb_ref, snb_ref, st0_ref, of_ref, ob_ref,
                st_ref, pair_ref, qd_ref, kd_ref, cd_ref):
    c = qf_ref.shape[0]

    @pl.when(pl.program_id(0) == 0)
    def _():
        st_ref[...] = st0_ref[...]
        for d in range(2):
            for h in range(RET_HEADS):
                i = d * RET_HEADS + h
                pair_ref[i], qd_ref[i], kd_ref[i], cd_ref[i] = _ret_decays(lg_ref[d, h], c, d == 1)

    csf, snf = csf_ref[...], snf_ref[...]
    csb, snb = csb_ref[...], snb_ref[...]
    for h in range(RET_HEADS):
        sk = slice(h * RET_DK, (h + 1) * RET_DK)
        sv = slice(h * RET_DV, (h + 1) * RET_DV)
        of_ref[:, sv], st_ref[h] = _ret_head(
            _rope(qf_ref[:, sk], csf, snf), _rope(kf_ref[:, sk], csf, snf), vf_ref[:, sv],
            (pair_ref[h], qd_ref[h], kd_ref[h], cd_ref[h]), st_ref[h])
        hb = RET_HEADS + h
        ob_ref[:, sv], st_ref[hb] = _ret_head(
            _rope(qb_ref[:, sk], csb, snb), _rope(kb_ref[:, sk], csb, snb), vb_ref[:, sv],
            (pair_ref[hb], qd_ref[hb], kd_ref[hb], cd_ref[hb]), st_ref[hb])


def _ret_scan(p, pc, ret_lg, cs_tab, sn_tab):
    c = RET_CHUNK
    n, l = p.shape[0], pc.shape[0]
    n_x, n_ctx = n // c, l // c
    qk_w = RET_HEADS * RET_DK
    st_shape = (2 * RET_HEADS, RET_DV, RET_DK)
    smem = pl.BlockSpec(memory_space=pltpu.SMEM)

    def kv_specs(chunk_fn):
        return [
            pl.BlockSpec((c, qk_w), lambda s: (chunk_fn(s), COL_RK_512)),
            pl.BlockSpec((c, RET_WIDTH), lambda s: (chunk_fn(s), COL_RV)),
        ]

    st0 = pl.pallas_call(
        _ret_ctx_kernel,
        grid=(n_ctx,),
        in_specs=[smem, *kv_specs(lambda s: s), *kv_specs(lambda s: n_ctx - 1 - s)],
        out_specs=pl.BlockSpec(st_shape, lambda s: (0, 0, 0)),
        out_shape=jax.ShapeDtypeStruct(st_shape, F32),
        compiler_params=_cparams(1, 32),
        name="ret_ctx",
    )(ret_lg, pc, pc, pc, pc)

    def specs(chunk_fn):
        return [
            pl.BlockSpec((c, qk_w), lambda s: (chunk_fn(s), COL_RQ_512)),
            *kv_specs(chunk_fn),
            pl.BlockSpec((c, RET_DK), lambda s: (chunk_fn(s), 0)),
            pl.BlockSpec((c, RET_DK), lambda s: (chunk_fn(s), 0)),
        ]

    fwd = lambda s: s
    bwd = lambda s: n_x - 1 - s
    return pl.pallas_call(
        _ret_kernel,
        grid=(n_x,),
        in_specs=[smem, *specs(fwd), *specs(bwd), pl.BlockSpec(st_shape, lambda s: (0, 0, 0))],
        out_specs=[
            pl.BlockSpec((c, RET_WIDTH), lambda s: (fwd(s), 0)),
            pl.BlockSpec((c, RET_WIDTH), lambda s: (bwd(s), 0)),
        ],
        out_shape=[jax.ShapeDtypeStruct((n, RET_WIDTH), F32)] * 2,
        scratch_shapes=[
            pltpu.VMEM(st_shape, F32),
            pltpu.VMEM((2 * RET_HEADS, c, c), F32),
            pltpu.VMEM((2 * RET_HEADS, c, RET_DK), F32),
            pltpu.VMEM((2 * RET_HEADS, c, RET_DK), F32),
            pltpu.VMEM((2 * RET_HEADS, 8, RET_DK), F32),
        ],
        compiler_params=_cparams(1, 32),
        name="ret",
    )(ret_lg, p, p, p, cs_tab, sn_tab, p, p, p, cs_tab, sn_tab, st0)


def _rms(y):
    return y * lax.rsqrt(jnp.mean(y * y, axis=-1, keepdims=True) + RMS_EPS)


def _outproj_kernel(of_ref, ob_ref, rf_ref, rb_ref, hgate_ref, rgate_ref, x_ref, wout_ref,
                    hgw_ref, rnw_ref, nw_ref, g1_ref, sh2_ref, sc2_ref, wr_ref, br_ref,
                    x1_ref, h2_ref, lg_ref, cat_scr):
    for h in range(HG_HEADS):
        sl = slice(h * HG_DK, (h + 1) * HG_DK)
        o = of_ref[:, sl] + ob_ref[:, sl]
        cat_scr[:, sl] = (_rms(o) * hgw_ref[0:1, sl] * hgate_ref[:, sl].astype(F32)).astype(BF16)
    for h in range(RET_HEADS):
        sl = slice(h * RET_DV, (h + 1) * RET_DV)
        r = rf_ref[:, sl] + rb_ref[:, sl]
        cat_scr[:, HG_WIDTH + h * RET_DV:HG_WIDTH + (h + 1) * RET_DV] = (
            _rms(r) * rnw_ref[0:1, sl] * rgate_ref[:, sl].astype(F32)).astype(BF16)

    y = _dot(cat_scr[...], wout_ref[...])
    x1 = x_ref[...] + g1_ref[0:1, :] * (_rms(y) * nw_ref[1:2, :])
    x1_ref[...] = x1
    h2 = _rms(x1) * nw_ref[2:3, :] * (1.0 + sc2_ref[0:1, :]) + sh2_ref[0:1, :]
    _store_row_tiles(h2_ref, _pack_pairs(h2))
    hi = h2.astype(BF16)
    lo = (h2 - hi.astype(F32)).astype(BF16)
    parts = _dot(jnp.concatenate([hi, lo], axis=0), wr_ref[...])
    half = hi.shape[0]
    lg_ref[...] = (parts[:half, :ROUTER_LANES] + parts[half:, :ROUTER_LANES]
                   + parts[:half, ROUTER_LANES:] + br_ref[...])


def _out_projection(o_f, o_b, r_f, r_b, p, x2d, w_out_bf, hg_norm_w, ret_norm_w, norm_w, mod,
                    wr_parts, br):
    n, d = x2d.shape
    tm = 256
    row = lambda w: pl.BlockSpec((tm, w), lambda i: (i, 0))
    full = lambda a: pl.BlockSpec(a.shape, lambda i: (0,) * a.ndim)
    modcol = lambda k: pl.BlockSpec((8, d), lambda i: (0, k))
    return pl.pallas_call(
        _outproj_kernel,
        grid=(n // tm,),
        in_specs=[
            row(HG_WIDTH), row(HG_WIDTH), row(RET_WIDTH), row(RET_WIDTH),
            pl.BlockSpec((tm, HG_WIDTH), lambda i: (i, COL_HGATE)),
            pl.BlockSpec((tm, RET_WIDTH), lambda i: (i, COL_RGATE)),
            row(d), full(w_out_bf), full(hg_norm_w), full(ret_norm_w), full(norm_w),
            modcol(2), modcol(3), modcol(4),
            full(wr_parts), full(br),
        ],
        out_specs=[row(d), pl.BlockSpec((tm * TILE_ROWS, LANES), lambda i: (i, 0)),
                   row(ROUTER_LANES)],
        out_shape=[
            jax.ShapeDtypeStruct((n, d), F32),
            jax.ShapeDtypeStruct((n * TILE_ROWS, LANES), jnp.uint32),
            jax.ShapeDtypeStruct((n, ROUTER_LANES), F32),
        ],
        scratch_shapes=[pltpu.VMEM((tm, d), BF16)],
        compiler_params=_cparams(1, 48),
        name="outproj",
    )(o_f, o_b, r_f, r_b, p, p, x2d, w_out_bf, hg_norm_w, ret_norm_w, norm_w, mod, mod, mod,
      wr_parts, br)


ROUTER_TM = 1024


def _lane_first(hit, lane_f):
    return jnp.min(jnp.where(hit, lane_f, 1e9), axis=1, keepdims=True)


def _router_kernel(lg_ref, gates_ref, s0_ref, s1_ref, cnt_ref, ltri, carry, pstart):
    ph = pl.program_id(0)
    i = pl.program_id(1)
    tm = lg_ref.shape[0]

    @pl.when(jnp.logical_and(ph == 0, i == 0))
    def _():
        t = lax.broadcasted_iota(jnp.int32, (tm, tm), 0)
        s = lax.broadcasted_iota(jnp.int32, (tm, tm), 1)
        ltri[...] = jnp.where(s < t, 1.0, 0.0).astype(BF16)

    @pl.when(jnp.logical_and(ph == 1, i == 0))
    def _():
        cnt = carry[...]
        padded = jnp.floor((cnt + (MOE_PAD - 1.0)) * (1.0 / MOE_PAD)) * MOE_PAD
        r = lax.broadcasted_iota(jnp.int32, (LANES, LANES), 0)
        c = lax.broadcasted_iota(jnp.int32, (LANES, LANES), 1)
        before = jnp.where(r < c, 1.0, 0.0).astype(BF16)
        pstart[...] = _dot(padded.astype(BF16), before)
        cnt_ref[...] = cnt.astype(jnp.int32)

    @pl.when(i == 0)
    def _():
        carry[...] = jnp.zeros_like(carry)

    lg = lg_ref[...]
    lane = lax.broadcasted_iota(jnp.int32, lg.shape, 1)
    lane_f = lane.astype(F32)
    neg = -jnp.inf

    gl = jnp.where(lane < N_GROUPS, lg, neg)
    gmax = jnp.max(gl, axis=1, keepdims=True)
    grp = _lane_first(gl == gmax, lane_f).astype(jnp.int32)
    p_sel = 1.0 / jnp.sum(jnp.exp(gl - gmax), axis=1, keepdims=True)

    in_grp = jnp.logical_and(
        jnp.logical_and(lane >= N_GROUPS, lane < N_GROUPS + N_EXPERTS),
        ((lane - N_GROUPS) >> 3) == grp)
    el = jnp.where(in_grp, lg, neg)
    v1 = jnp.max(el, axis=1, keepdims=True)
    o1 = lane_f == _lane_first(el == v1, lane_f)
    el2 = jnp.where(o1, neg, el)
    v2 = jnp.max(el2, axis=1, keepdims=True)
    o2 = lane_f == _lane_first(el2 == v2, lane_f)

    osum = (jnp.where(o1, 1.0, 0.0) + jnp.where(o2, 1.0, 0.0)).astype(BF16)
    earlier = _dot(ltri[...], osum) + carry[0:1, :]
    carry[...] = carry[...] + _dot(jnp.ones((8, tm), BF16), osum)

    @pl.when(ph == 1)
    def _():
        ez = jnp.exp(v2 - v1)
        g1 = p_sel / (1.0 + ez)
        gates_ref[...] = jnp.concatenate([g1, g1 * ez], axis=1)
        base = earlier + pstart[0:1, :]
        for o, s_ref in ((o1, s0_ref), (o2, s1_ref)):
            col = jnp.sum(jnp.where(o, base, 0.0), axis=1, keepdims=True)
            wide = jnp.broadcast_to(col, (tm, LANES))
            for b in range(tm // LANES):
                s_ref[b:b + 1, :] = wide[b * LANES:(b + 1) * LANES, :].T[0:1, :].astype(jnp.int32)


def _router(logits):
    t = logits.shape[0]
    tm = ROUTER_TM
    rows = tm // LANES
    return pl.pallas_call(
        _router_kernel,
        grid=(2, t // tm),
        in_specs=[pl.BlockSpec((tm, LANES), lambda ph, i: (i, 0))],
        out_specs=[
            pl.BlockSpec((tm, EXPERT_TOPK), lambda ph, i: (i * ph, 0)),
            pl.BlockSpec((rows, LANES), lambda ph, i: (i * ph, 0)),
            pl.BlockSpec((rows, LANES), lambda ph, i: (i * ph, 0)),
            pl.BlockSpec((8, LANES), lambda ph, i: (0, 0)),
        ],
        out_shape=[
            jax.ShapeDtypeStruct((t, EXPERT_TOPK), F32),
            jax.ShapeDtypeStruct((t // LANES, LANES), jnp.int32),
            jax.ShapeDtypeStruct((t // LANES, LANES), jnp.int32),
            jax.ShapeDtypeStruct((8, LANES), jnp.int32),
        ],
        scratch_shapes=[
            pltpu.VMEM((tm, tm), BF16),
            pltpu.VMEM((8, LANES), F32),
            pltpu.VMEM((8, LANES), F32),
        ],
        compiler_params=_cparams(2, 32),
        name="router",
    )(logits)


def _round_up_pow2(x, m):
    assert m & (m - 1) == 0
    return (x + (m - 1)) & ~(m - 1)


def _plan_kernel(cnt_ref, s0_ref, s1_ref, tok_ref, ie_ref, ist_ref, ir_ref, if_ref, zeros, sem):
    n_tok = s0_ref.shape[0]
    n_slots = tok_ref.shape[0]
    n_items = ie_ref.shape[0]
    r = MOE_ITEM_ROWS

    zeros[...] = jnp.zeros_like(zeros)
    clear = pltpu.make_async_copy(zeros, tok_ref, sem)
    clear.start()
    clear.wait()

    def scatter(t16, carry):
        for d in range(16):
            t = t16 * 16 + d
            tok_ref[s0_ref[t]] = t
            tok_ref[s1_ref[t]] = t
        return carry
    lax.fori_loop(0, n_tok // 16, scatter, 0)

    a = jnp.int32(0)
    used = jnp.int32(0)
    e_last = jnp.int32(0)
    for e in range(N_EXPERTS):
        pad = _round_up_pow2(cnt_ref[0, N_GROUPS + e], MOE_PAD)

        def item(sub, a, e=e, pad=pad, used=used):
            ie_ref[a] = e
            ist_ref[a] = used + sub * r
            ir_ref[a] = jnp.minimum(pad - sub * r, r)
            if_ref[a] = 0
            return a + 1
        a = lax.fori_loop(0, _round_up_pow2(pad, r) >> (r.bit_length() - 1), item, a)
        e_last = jnp.where(pad > 0, e, e_last)
        used = used + pad

    def idle(a2, carry):
        fill_start = used + (a2 - a) * r
        fill = jnp.clip(n_slots - fill_start, 0, r)
        ie_ref[a2] = e_last
        ist_ref[a2] = jnp.where(fill > 0, fill_start, 0)
        ir_ref[a2] = 0
        if_ref[a2] = fill
        return carry
    lax.fori_loop(a, n_items, idle, 0)


def _plan(counts, s0, s1, n_slots, n_items):
    smem = pl.BlockSpec(memory_space=pltpu.SMEM)
    item = jax.ShapeDtypeStruct((n_items,), jnp.int32)
    return pl.pallas_call(
        _plan_kernel,
        in_specs=[smem, smem, smem],
        out_specs=[smem] * 5,
        out_shape=[jax.ShapeDtypeStruct((n_slots,), jnp.int32), item, item, item, item],
        scratch_shapes=[pltpu.VMEM((n_slots,), jnp.int32), pltpu.SemaphoreType.DMA(())],
        name="plan",
    )(counts, s0, s1)


def _unpack_pairs(u):
    lo = lax.bitcast_convert_type(u << 16, F32)
    hi = lax.bitcast_convert_type(u & jnp.uint32(0xFFFF0000), F32)
    return jnp.concatenate([lo, hi], axis=1)


def _store_row_tiles(dst, packed):
    r = packed.shape[0]
    for s in range(TILE_ROWS):
        dst[pl.ds(s, r, stride=TILE_ROWS), :] = packed[:, s * LANES:(s + 1) * LANES]


def _load_row_tiles(src, first_row, r):
    return jnp.concatenate(
        [src[pl.ds(first_row * TILE_ROWS + s, r, stride=TILE_ROWS), :] for s in range(TILE_ROWS)],
        axis=1)


def _pack_pairs(y):
    w = y.shape[1] // 2
    bits = lax.bitcast_convert_type(y.astype(BF16).astype(F32), jnp.uint32)
    return (bits[:, :w] >> 16) | (bits[:, w:] & jnp.uint32(0xFFFF0000))


def _moe_block_sizes():
    sizes, rows = [], MOE_BLOCK
    while rows >= MOE_PAD:
        sizes.append(rows)
        rows //= 2
    return sizes


def _moe_kernel(item_e, item_start, item_rows, item_fill, slot_tok,
                h2p_hbm, wg_ref, wu_ref, wd_ref, yb_hbm,
                xbuf, act, ystage, pf, gsem, osem):
    a = pl.program_id(0)
    t = pl.program_id(1)
    n_items = pl.num_programs(0)
    n = item_rows[a]
    start = item_start[a]
    half_w = wg_ref.shape[2]
    cur = a % 2
    nxt = 1 - cur
    a1 = jnp.minimum(a + 1, n_items - 1)
    n_next = jnp.where(a + 1 < n_items, item_rows[a1], 0)
    start_next = item_start[a1]

    def tile_rows(i, count=1):
        return pl.ds(pl.multiple_of(i * TILE_ROWS, TILE_ROWS), count * TILE_ROWS)

    def row_copy(tok, slot, i):
        return pltpu.make_async_copy(h2p_hbm.at[tile_rows(tok), :], xbuf.at[slot, tile_rows(i), :],
                                     gsem.at[slot])

    def start_row(tok, slot, i):
        row_copy(tok, slot, i).start(priority=1)

    @pl.when(jnp.logical_and(a == 0, t == 0))
    def _():
        pf[1] = 0
        pf[2] = 0

        def issue(i, carry):
            start_row(slot_tok[start + i], 0, i)
            return carry
        lax.fori_loop(0, n, issue, 0)

    @pl.when(t == 0)
    def _():
        pf[0] = 0

        def drain(bk, carry):
            r0 = pl.multiple_of(bk * MOE_PAD, MOE_PAD)
            pltpu.make_async_copy(h2p_hbm.at[tile_rows(0, MOE_PAD), :],
                                  xbuf.at[cur, tile_rows(r0, MOE_PAD), :], gsem.at[cur]).wait()
            return carry
        lax.fori_loop(0, n // MOE_PAD, drain, 0)

    def prefetch_group():
        base = pf[0]
        for q in range(MOE_GATHER_GROUP):
            start_row(slot_tok[start_next + base + q], nxt, base + q)
        pf[0] = base + MOE_GATHER_GROUP

    def run_blocks(compute):
        def one(b, r0, rows):
            r0 = pl.multiple_of(r0, MOE_PAD)
            more = pf[0] < n_next

            def gather():
                prefetch_group()
                for _ in range(max(rows // (2 * MOE_GATHER_GROUP), 1) - 1):
                    pl.when(pf[0] < n_next)(prefetch_group)

            @pl.when(more)
            def _():
                compute(b, r0, rows, gather)

            @pl.when(jnp.logical_not(more))
            def _():
                compute(b, r0, rows, lambda: None)

        n_big = n // MOE_BLOCK

        def body(b, carry):
            one(b, b * MOE_BLOCK, MOE_BLOCK)
            return carry
        lax.fori_loop(0, n_big, body, 0)

        b, r0, rows = n_big, n_big * MOE_BLOCK, MOE_BLOCK // 2
        while rows >= MOE_PAD:
            has = (n & rows) != 0
            pl.when(has)(functools.partial(one, b, r0, rows))
            b = b + has.astype(jnp.int32)
            r0 = r0 + jnp.where(has, rows, 0)
            rows //= 2

    def gate_up(half):
        def compute(b, r0, rows, mid):
            xs = _unpack_pairs(_load_row_tiles(xbuf.at[cur], r0, rows)).astype(BF16)
            g = _dot(xs, wg_ref[0].astype(BF16))
            u = _dot(xs, wu_ref[0].astype(BF16))
            mid()
            act[pl.ds(r0, rows), half * half_w:(half + 1) * half_w] = (
                g * _sigmoid(g) * u).astype(BF16)
        run_blocks(compute)

    def out_copy(slot, r0, rows):
        return pltpu.make_async_copy(
            ystage.at[slot, tile_rows(0, rows), :],
            yb_hbm.at[tile_rows(pl.multiple_of(start + r0, MOE_PAD), rows), :],
            osem.at[slot])

    def wait_stage(slot):
        for rows in _moe_block_sizes():
            @pl.when(pf[1 + slot] == rows)
            def _(rows=rows):
                out_copy(slot, 0, rows).wait()
                pf[1 + slot] = 0

    def down():
        def compute(b, r0, rows, mid):
            slot = b % 2
            wait_stage(slot)
            y = _dot(act[pl.ds(r0, rows), :], wd_ref[0].astype(BF16))
            mid()
            _store_row_tiles(ystage.at[slot], _pack_pairs(y))
            out_copy(slot, r0, rows).start()
            pf[1 + slot] = rows
        run_blocks(compute)

        def tail(i, carry):
            start_row(slot_tok[start_next + i], nxt, i)
            return carry
        lax.fori_loop(pf[0], n_next, tail, 0)

    pl.when(t == 0)(functools.partial(gate_up, 0))
    pl.when(t == 1)(functools.partial(gate_up, 1))
    pl.when(t == 2)(down)

    fill = item_fill[a]

    @pl.when(jnp.logical_and(t == 0, fill > 0))
    def _():
        wait_stage(0)
        ystage[0] = jnp.zeros(ystage.shape[1:], ystage.dtype)

        def zero_block(bk, carry):
            cp = out_copy(0, bk * MOE_PAD, MOE_PAD)
            cp.start()
            cp.wait()
            return carry
        lax.fori_loop(0, fill // MOE_PAD, zero_block, 0)

    @pl.when(jnp.logical_and(a == n_items - 1, t == pl.num_programs(1) - 1))
    def _():
        wait_stage(0)
        wait_stage(1)


def _moe_ffn(h2p, w_gate, w_up, w_down, item_e, item_start, item_rows, item_fill, slot_tok,
             n_slots):
    d = w_gate.shape[1]
    assert d == 2 * TILE_ROWS * LANES and h2p.shape[1] == LANES, "one packed token row per tile"
    n_items = item_e.shape[0]
    half_w = D_EXPERT // 2

    def wgu_map(a, t, ie, ist, ir, fl, st):
        return (ie[a], 0, jnp.where(ir[a] > 0, jnp.minimum(t, 1), 1))

    def wd_map(a, t, ie, ist, ir, fl, st):
        return (jnp.where(t >= 2, ie[a], ie[jnp.maximum(a - 1, 0)]), 0, 0)

    grid_spec = pltpu.PrefetchScalarGridSpec(
        num_scalar_prefetch=5,
        grid=(n_items, 3),
        in_specs=[
            pl.BlockSpec(memory_space=pl.ANY),
            pl.BlockSpec((1, d, half_w), wgu_map),
            pl.BlockSpec((1, d, half_w), wgu_map),
            pl.BlockSpec((1, D_EXPERT, d), wd_map),
        ],
        out_specs=pl.BlockSpec(memory_space=pl.ANY),
        scratch_shapes=[
            pltpu.VMEM((2, MOE_ITEM_ROWS * TILE_ROWS, LANES), jnp.uint32),
            pltpu.VMEM((MOE_ITEM_ROWS, D_EXPERT), BF16),
            pltpu.VMEM((2, MOE_BLOCK * TILE_ROWS, LANES), jnp.uint32),
            pltpu.SMEM((3,), jnp.int32),
            pltpu.SemaphoreType.DMA((2,)),
            pltpu.SemaphoreType.DMA((2,)),
        ],
    )
    return pl.pallas_call(
        _moe_kernel,
        grid_spec=grid_spec,
        out_shape=jax.ShapeDtypeStruct((n_slots * TILE_ROWS, LANES), jnp.uint32),
        compiler_params=_cparams(2, 56),
        name="moe",
    )(item_e, item_start, item_rows, item_fill, slot_tok, h2p, w_gate, w_up, w_down)


def _combine_kernel(s0, s1, yb_hbm, gates_ref, x1_ref, nw_ref, g2_ref, out_ref, rows, sem):
    i = pl.program_id(0)
    n_tiles = pl.num_programs(0)
    tm = x1_ref.shape[0]

    def tile_rows(i, count=1):
        return pl.ds(pl.multiple_of(i * TILE_ROWS, TILE_ROWS), count * TILE_ROWS)

    def gather(tile, buf):
        def issue(r8, carry):
            t0 = tile * tm + r8 * 8
            for dr in range(8):
                for k, s_k in enumerate((s0, s1)):
                    pltpu.make_async_copy(yb_hbm.at[tile_rows(s_k[t0 + dr]), :],
                                          rows.at[buf, k, tile_rows(r8 * 8 + dr), :],
                                          sem.at[buf]).start()
            return carry
        lax.fori_loop(0, tm // 8, issue, 0)

    @pl.when(i == 0)
    def _():
        gather(0, 0)

    @pl.when(i + 1 < n_tiles)
    def _():
        gather(i + 1, (i + 1) % 2)

    buf = i % 2
    for k in range(EXPERT_TOPK):
        pltpu.make_async_copy(yb_hbm.at[tile_rows(0, tm), :], rows.at[buf, k], sem.at[buf]).wait()

    g = gates_ref[...]
    y = (_unpack_pairs(_load_row_tiles(rows.at[buf, 0], 0, tm)) * g[:, 0:1]
         + _unpack_pairs(_load_row_tiles(rows.at[buf, 1], 0, tm)) * g[:, 1:2])
    out_ref[...] = x1_ref[...] + g2_ref[0:1, :] * (_rms(y) * nw_ref[3:4, :])


def _combine(s0, s1, yb, gates, x1, norm_w, mod):
    n, d = x1.shape
    tm = 256
    grid_spec = pltpu.PrefetchScalarGridSpec(
        num_scalar_prefetch=2,
        grid=(n // tm,),
        in_specs=[
            pl.BlockSpec(memory_space=pl.ANY),
            pl.BlockSpec((tm, EXPERT_TOPK), lambda i, *_: (i, 0)),
            pl.BlockSpec((tm, d), lambda i, *_: (i, 0)),
            pl.BlockSpec((4, d), lambda i, *_: (0, 0)),
            pl.BlockSpec((8, d), lambda i, *_: (0, 5)),
        ],
        out_specs=pl.BlockSpec((tm, d), lambda i, *_: (i, 0)),
        scratch_shapes=[
            pltpu.VMEM((2, EXPERT_TOPK, tm * TILE_ROWS, LANES), jnp.uint32),
            pltpu.SemaphoreType.DMA((2,)),
        ],
    )
    return pl.pallas_call(
        _combine_kernel,
        grid_spec=grid_spec,
        out_shape=jax.ShapeDtypeStruct((n, d), F32),
        compiler_params=_cparams(1, 40),
        name="combine",
    )(s0, s1, yb, gates, x1, norm_w, mod)


def _rope_tables(n):
    rows = n // GRID_W
    r = np.repeat(np.arange(rows, dtype=np.float32), GRID_W)
    cidx = np.tile(np.arange(GRID_W, dtype=np.float32), rows)
    n_freq = RET_DK // 4
    inv = np.float32(ROPE_BASE) ** (-np.arange(n_freq, dtype=np.float32) / np.float32(n_freq))
    ang = np.concatenate([r[:, None] * inv, cidx[:, None] * inv], axis=-1).astype(np.float32)
    cos, sin = np.cos(ang), np.sin(ang)
    cs = np.concatenate([cos, cos], axis=-1)
    sn = np.concatenate([-sin, sin], axis=-1)
    return jnp.asarray(cs, F32), jnp.asarray(sn, F32)


def kernel(x, c, ctx, c_ctx, w_mod, b_mod, norm_w, w_in, hg_lb_logits, hg_norm_w, ret_decay_logits,
           ret_norm_w, w_out, router_group_w, router_group_b, router_expert_w, router_expert_b,
           w_gate, w_up, w_down):
    bsz, n, d = x.shape
    l = ctx.shape[1]
    assert bsz == 1 and w_mod.shape[0] == 1, "single sample, single layer"
    x2d, ctx2d = x[0], ctx[0]

    cc = jnp.zeros((8, d), F32).at[0].set(c[0]).at[1].set(c_ctx)
    mod = _modulation(cc, w_mod[0], b_mod[0])

    lb = jnp.cumsum(jax.nn.softmax(hg_lb_logits.astype(F32), axis=0), axis=0)[0]
    rk0 = 5 * HG_WIDTH + RET_HEADS * RET_DK
    cols = jnp.stack([
        jnp.zeros((PROJ_WIDTH,), F32).at[HG_WIDTH:3 * HG_WIDTH].set(lb.reshape(-1)),
        jnp.ones((PROJ_WIDTH,), F32).at[rk0:rk0 + RET_HEADS * RET_DK].set(RET_DK ** -0.5),
    ])
    w_in_bf = w_in[0].astype(BF16)
    p, g = _projection(x2d, norm_w[0], mod, w_in_bf, cols, 0, 1024, "proj_x")
    pc, gc = _projection(ctx2d, norm_w[0], mod, w_in_bf, cols, 1, l, "proj_ctx")

    ret_lg = jax.nn.log_sigmoid(ret_decay_logits[0].astype(F32))
    cs_tab, sn_tab = _rope_tables(n)

    o_f, o_b = _hgrn_scan(p, g, pc, gc)
    r_f, r_b = _ret_scan(p, pc, ret_lg, cs_tab, sn_tab)

    wr = jnp.concatenate([router_group_w[0], router_expert_w[0]], axis=1)
    wr = jnp.pad(wr, ((0, 0), (0, ROUTER_LANES - wr.shape[1])))
    wr_hi = wr.astype(BF16)
    wr_parts = jnp.concatenate([wr_hi, (wr - wr_hi.astype(F32)).astype(BF16)], axis=1)
    br = jnp.concatenate([router_group_b[0], router_expert_b[0]]).astype(F32)
    br = jnp.pad(br, (0, ROUTER_LANES - br.shape[0])).reshape(1, ROUTER_LANES)
    x1, h2, logits = _out_projection(
        o_f, o_b, r_f, r_b, p, x2d, w_out[0].astype(BF16), hg_norm_w, ret_norm_w, norm_w[0], mod,
        wr_parts, br)

    n_assign = n * EXPERT_TOPK
    n_slots = n_assign + N_EXPERTS * MOE_PAD
    n_items = (n_slots + N_EXPERTS * (MOE_ITEM_ROWS - MOE_PAD)) // MOE_ITEM_ROWS
    gates, s0, s1, counts = _router(logits)
    s0, s1 = s0.reshape(-1), s1.reshape(-1)
    slot_tok, item_e, item_start, item_rows, item_fill = _plan(counts, s0, s1, n_slots, n_items)

    yb = _moe_ffn(h2, w_gate[0], w_up[0], w_down[0], item_e, item_start, item_rows, item_fill,
                  slot_tok, n_slots)
    out = _combine(s0, s1, yb, gates, x1, norm_w[0], mod)
    return out[None]
```

```python
import functools

import jax
import jax.numpy as jnp
import numpy as np
from jax import lax
from jax.experimental import pallas as pl
from jax.experimental.pallas import tpu as pltpu

F32 = jnp.float32
BF16 = jnp.bfloat16

D_MODEL = 2048
GRID_W = 64
HG_HEADS = 8
HG_DK = 128
HG_WIDTH = 1024
RET_HEADS = 4
RET_DK = 128
RET_DV = 256
RET_WIDTH = 1024
PROJ_WIDTH = 8192
N_GROUPS = 4
EXPERTS_PER_GROUP = 8
N_EXPERTS = 32
EXPERT_TOPK = 2
D_EXPERT = 1024
RMS_EPS = 1e-6
ROPE_BASE = 10000.0

COL_HQ, COL_HF_FWD, COL_HF_BWD, COL_HI, COL_HGATE = 0, 1, 2, 3, 4
COL_RQ_512, COL_RK_512 = 10, 11
COL_RV, COL_RGATE = 6, 7

HG_CHUNK = 256
HG_STEP_HEADS = 8
HG_DIAG = 16
HG_LOG_CLAMP = 10.0
RET_CHUNK = 256

MOE_PAD = 128
MOE_BLOCK = 256
MOE_ITEM_ROWS = 1024
MOE_GATHER_GROUP = 128
ROUTER_LANES = 128
LANES = 128
TILE_ROWS = 8


def _cparams(n_axes, vmem_mb, **flags):
    return pltpu.CompilerParams(
        dimension_semantics=("arbitrary",) * n_axes,
        vmem_limit_bytes=vmem_mb * 1024 * 1024,
        flags=flags or None,
    )


def _sigmoid(x):
    return 1.0 / (1.0 + jnp.exp(-x))


def _dot(a, b):
    return jnp.dot(a, b, preferred_element_type=F32)


def _dot_nt(a, b):
    return lax.dot_general(a, b, (((1,), (1,)), ((), ())), preferred_element_type=F32)


def _dot_tn(a, b):
    return lax.dot_general(a, b, (((0,), (0,)), ((), ())), preferred_element_type=F32)


def _mod_kernel(cc_ref, w_ref, b_ref, o_ref):
    s = cc_ref[...]
    s = s * _sigmoid(s)
    s16 = jnp.concatenate([s, s], axis=0)
    hi = s16.astype(BF16).astype(F32)
    row = lax.broadcasted_iota(jnp.int32, s16.shape, 0)
    lhs = jnp.where(row < 8, hi, s16 - hi).astype(BF16)
    r = _dot(lhs, w_ref[...].astype(BF16))
    o_ref[...] = r[:8] + r[8:] + b_ref[...]


def _modulation(cc, w_mod, b_mod):
    d, n = w_mod.shape
    tn = 1024
    return pl.pallas_call(
        _mod_kernel,
        grid=(n // tn,),
        in_specs=[
            pl.BlockSpec((8, d), lambda j: (0, 0)),
            pl.BlockSpec((d, tn), lambda j: (0, j)),
            pl.BlockSpec((1, tn), lambda j: (0, j)),
        ],
        out_specs=pl.BlockSpec((8, tn), lambda j: (0, j)),
        out_shape=jax.ShapeDtypeStruct((8, n), F32),
        compiler_params=_cparams(1, 40),
        name="mod",
    )(cc, w_mod, b_mod.reshape(1, n))


PROJ_TN = 1024
PROLOGUE_ROWS = 32


def _proj_kernel(x_ref, nw_ref, sh_ref, sc_ref, w_ref, cols_ref, o_ref, g_ref, h_scr, *, mod_row):
    i = pl.program_id(0)
    j = pl.program_id(1)
    last_j = pl.num_programs(1) - 1
    cur = i % 2

    def normed(x):
        scale = nw_ref[0:1, :] * (1.0 + sc_ref[mod_row:mod_row + 1, :])
        shift = sh_ref[mod_row:mod_row + 1, :]
        y = x * lax.rsqrt(jnp.mean(x * x, axis=-1, keepdims=True) + RMS_EPS)
        return (y * scale + shift).astype(BF16)

    @pl.when(jnp.logical_and(i == 0, j == 0))
    def _():
        def body(r, carry):
            r0 = pl.multiple_of(r * PROLOGUE_ROWS, PROLOGUE_ROWS)
            h_scr[0, pl.ds(r0, PROLOGUE_ROWS), :] = normed(x_ref[pl.ds(r0, PROLOGUE_ROWS), :])
            return carry
        lax.fori_loop(0, x_ref.shape[0] // PROLOGUE_ROWS, body, 0)

    is_gate = jnp.logical_or(j == 1, j == 2)
    is_silu = jnp.logical_or(j == 0, j == 4)
    is_lin = jnp.logical_not(jnp.logical_or(jnp.logical_or(is_gate, is_silu), j == last_j))

    @pl.when(is_lin)
    def _():
        o_ref[...] = (_dot(h_scr[cur], w_ref[...]) * cols_ref[1:2, :]).astype(BF16)

    @pl.when(is_silu)
    def _():
        acc = _dot(h_scr[cur], w_ref[...])
        o_ref[...] = (acc * _sigmoid(acc)).astype(BF16)

    @pl.when(is_gate)
    def _():
        acc = _dot(h_scr[cur], w_ref[...])
        lb = cols_ref[0:1, :]
        g = jnp.maximum(jnp.log(lb + (1.0 - lb) * _sigmoid(acc)), -HG_LOG_CLAMP)
        g_ref[...] = g
        o_ref[...] = g.astype(BF16)

    @pl.when(j == last_j)
    def _():
        acc = _dot(h_scr[cur], w_ref[...])
        o_ref[...] = (acc * _sigmoid(acc)).astype(BF16)
        h_scr[1 - cur] = normed(x_ref[...])


def _projection(x2d, norm_w, mod, w_in, cols, mod_row, tm, name):
    n, d = x2d.shape
    tn = PROJ_TN
    n_i, n_j = n // tm, PROJ_WIDTH // tn

    def x_map(i, j):
        return (jnp.minimum(i + (j == n_j - 1).astype(jnp.int32), n_i - 1), 0)

    return pl.pallas_call(
        functools.partial(_proj_kernel, mod_row=mod_row),
        grid=(n_i, n_j),
        in_specs=[
            pl.BlockSpec((tm, d), x_map),
            pl.BlockSpec((4, d), lambda i, j: (0, 0)),
            pl.BlockSpec((8, d), lambda i, j: (0, 0)),
            pl.BlockSpec((8, d), lambda i, j: (0, 1)),
            pl.BlockSpec((d, tn), lambda i, j: (0, j)),
            pl.BlockSpec((2, tn), lambda i, j: (0, j)),
        ],
        out_specs=[
            pl.BlockSpec((tm, tn), lambda i, j: (i, j)),
            pl.BlockSpec((tm, tn), lambda i, j: (i, jnp.clip(j - 1, 0, 1))),
        ],
        out_shape=[
            jax.ShapeDtypeStruct((n, PROJ_WIDTH), BF16),
            jax.ShapeDtypeStruct((n, 2 * HG_WIDTH), F32),
        ],
        scratch_shapes=[pltpu.VMEM((2, tm, d), BF16)],
        compiler_params=_cparams(2, 56),
        name=name,
    )(x2d, norm_w, mod, mod, w_in, cols)


def _tri_ones(c, rev):
    t = lax.broadcasted_iota(jnp.int32, (c, c), 0)
    s = lax.broadcasted_iota(jnp.int32, (c, c), 1)
    return jnp.where((s >= t) if rev else (s <= t), 1.0, 0.0).astype(BF16)


def _cumsum_rows(g, tri):
    hi = g.astype(BF16)
    lo = (g - hi.astype(F32)).astype(BF16)
    return _dot(tri, hi) + _dot(tri, lo)


def _row_refs(b, idxs, blk):
    parts = [jnp.broadcast_to(b[i:i + 1, :], (blk, b.shape[1])) for i in idxs]
    return parts[0] if len(parts) == 1 else jnp.concatenate(parts, axis=0)


def _hgrn_levels(c, rev):
    t = lax.broadcasted_iota(jnp.int32, (c, c), 0)
    s = lax.broadcasted_iota(jnp.int32, (c, c), 1)
    levels = []
    m = c // 2
    while m >= HG_DIAG:
        blk = 2 * m
        same = ((t ^ s) & ~(blk - 1)) == 0
        if rev:
            idxs = [b0 + m for b0 in range(0, c, blk)]
            cross = jnp.logical_and((t & m) == 0, (s & m) != 0)
        else:
            idxs = [b0 + m - 1 for b0 in range(0, c, blk)]
            cross = jnp.logical_and((t & m) != 0, (s & m) == 0)
        levels.append((idxs, blk, jnp.logical_and(same, cross)))
        m //= 2
    blk = HG_DIAG
    same = ((t ^ s) & ~(blk - 1)) == 0
    if rev:
        idxs = [b0 + blk // 2 for b0 in range(0, c, blk)]
        mask = jnp.logical_and(same, s >= t)
    else:
        idxs = [b0 + blk // 2 - 1 for b0 in range(0, c, blk)]
        mask = jnp.logical_and(same, s <= t)
    levels.append((idxs, blk, mask))
    return levels


def _hgrn_head(q_bf, g, b, v_bf, st, rev, levels):
    c = g.shape[0]
    kk = 1.0 - jnp.exp(g)

    o = None
    if levels is not None:
        q = q_bf.astype(F32)
        att = jnp.zeros((c, c), F32)
        for idxs, blk, mask in levels:
            ref = _row_refs(b, idxs, blk)
            a_l = _dot_nt((q * jnp.exp(b - ref)).astype(BF16),
                          (kk * jnp.exp(ref - b)).astype(BF16))
            att = att + jnp.where(mask, a_l, 0.0)
        o = _dot(att.astype(BF16), v_bf) + _dot_nt((q * jnp.exp(b)).astype(BF16), st.astype(BF16))

    b_end = b[0:1, :] if rev else b[c - 1:c, :]
    k_end = (kk * jnp.exp(b_end - b)).astype(BF16)
    st_new = st * jnp.exp(b_end) + _dot_tn(v_bf, k_end)
    return o, st_new


def _hgrn_ctx_kernel(gf_ref, vf_ref, gb_ref, vb_ref, st_ref):
    @pl.when(pl.program_id(0) == 0)
    def _():
        st_ref[...] = jnp.zeros_like(st_ref)

    c = gf_ref.shape[0]
    bf = _cumsum_rows(gf_ref[...], _tri_ones(c, False))
    bb = _cumsum_rows(gb_ref[...], _tri_ones(c, True))
    for h in range(HG_HEADS):
        sl = slice(h * HG_DK, (h + 1) * HG_DK)
        _, st_ref[h] = _hgrn_head(None, gf_ref[:, sl], bf[:, sl], vf_ref[:, sl], st_ref[h],
                                  False, None)
        hb = HG_HEADS + h
        _, st_ref[hb] = _hgrn_head(None, gb_ref[:, sl], bb[:, sl], vb_ref[:, sl], st_ref[hb],
                                   True, None)


def _hgrn_kernel(qf_ref, gf_ref, vf_ref, qb_ref, gb_ref, vb_ref, st0_ref, of_ref, ob_ref, st_ref):
    hg = pl.program_id(1)

    @pl.when(jnp.logical_and(pl.program_id(0) == 0, hg == 0))
    def _():
        st_ref[...] = st0_ref[...]

    c = gf_ref.shape[0]
    lev_f = _hgrn_levels(c, False)
    lev_b = _hgrn_levels(c, True)
    bf = _cumsum_rows(gf_ref[...], _tri_ones(c, False))
    bb = _cumsum_rows(gb_ref[...], _tri_ones(c, True))
    for h in range(HG_STEP_HEADS):
        sl = slice(h * HG_DK, (h + 1) * HG_DK)
        hf = hg * HG_STEP_HEADS + h
        of_ref[:, sl], st_ref[hf] = _hgrn_head(qf_ref[:, sl], gf_ref[:, sl], bf[:, sl],
                                               vf_ref[:, sl], st_ref[hf], False, lev_f)
        hb = HG_HEADS + hf
        ob_ref[:, sl], st_ref[hb] = _hgrn_head(qb_ref[:, sl], gb_ref[:, sl], bb[:, sl],
                                               vb_ref[:, sl], st_ref[hb], True, lev_b)


def _hgrn_scan(p, g, pc, gc):
    c = HG_CHUNK
    w = HG_WIDTH
    n, l = p.shape[0], pc.shape[0]
    n_x, n_ctx = n // c, l // c
    st_shape = (2 * HG_HEADS, HG_DK, HG_DK)

    def spec(chunk_fn, col):
        return pl.BlockSpec((c, w), lambda s: (chunk_fn(s), col))

    st0 = pl.pallas_call(
        _hgrn_ctx_kernel,
        grid=(n_ctx,),
        in_specs=[
            spec(lambda s: s, 0), spec(lambda s: s, COL_HI),
            spec(lambda s: n_ctx - 1 - s, 1), spec(lambda s: n_ctx - 1 - s, COL_HI),
        ],
        out_specs=pl.BlockSpec(st_shape, lambda s: (0, 0, 0)),
        out_shape=jax.ShapeDtypeStruct(st_shape, F32),
        compiler_params=_cparams(1, 32),
        name="hgrn_ctx",
    )(gc, pc, gc, pc)

    n_hg = HG_HEADS // HG_STEP_HEADS
    wg = HG_STEP_HEADS * HG_DK
    fwd = lambda s: s
    bwd = lambda s: n_x - 1 - s

    def gspec(chunk_fn, col):
        return pl.BlockSpec((c, wg), lambda s, hg: (chunk_fn(s), col * n_hg + hg))

    return pl.pallas_call(
        _hgrn_kernel,
        grid=(n_x, n_hg),
        in_specs=[
            gspec(fwd, COL_HQ), gspec(fwd, 0), gspec(fwd, COL_HI),
            gspec(bwd, COL_HQ), gspec(bwd, 1), gspec(bwd, COL_HI),
            pl.BlockSpec(st_shape, lambda s, hg: (0, 0, 0)),
        ],
        out_specs=[gspec(fwd, 0), gspec(bwd, 0)],
        out_shape=[jax.ShapeDtypeStruct((n, w), F32)] * 2,
        scratch_shapes=[pltpu.VMEM(st_shape, F32)],
        compiler_params=_cparams(2, 32),
        name="hgrn",
    )(p, g, p, p, g, p, st0)


def _rope(x_bf, cs, sn):
    x = x_bf.astype(F32)
    return x * cs + pltpu.roll(x, RET_DK // 2, axis=1) * sn


def _ret_decays(lg, c, rev):
    t = lax.broadcasted_iota(jnp.int32, (c, c), 0)
    s = lax.broadcasted_iota(jnp.int32, (c, c), 1)
    diff = (s - t) if rev else (t - s)
    pair = jnp.where(diff >= 0, jnp.exp(lg * jnp.maximum(diff, 0).astype(F32)), 0.0)
    pos = lax.broadcasted_iota(jnp.int32, (c, RET_DK), 0).astype(F32)
    q_dec = jnp.exp(lg * ((c - pos) if rev else (pos + 1.0)))
    k_dec = jnp.exp(lg * (pos if rev else (c - 1.0 - pos)))
    chunk = jnp.exp(jnp.full((8, RET_DK), lg * c, F32))
    return pair, q_dec, k_dec, chunk


def _ret_head(q, k, v_bf, decays, st):
    pair, q_dec, k_dec, chunk = decays
    o = None
    if q is not None:
        att = _dot_nt(q.astype(BF16), k.astype(BF16)) * pair
        o = _dot(att.astype(BF16), v_bf) + _dot_nt((q * q_dec).astype(BF16), st.astype(BF16))
    st_new = st * chunk[0:1, :] + _dot_tn(v_bf, (k * k_dec).astype(BF16))
    return o, st_new


def _ret_ctx_kernel(lg_ref, kf_ref, vf_ref, kb_ref, vb_ref, st_ref):
    @pl.when(pl.program_id(0) == 0)
    def _():
        st_ref[...] = jnp.zeros_like(st_ref)

    c = kf_ref.shape[0]
    for h in range(RET_HEADS):
        sk = slice(h * RET_DK, (h + 1) * RET_DK)
        sv = slice(h * RET_DV, (h + 1) * RET_DV)
        _, st_ref[h] = _ret_head(None, kf_ref[:, sk].astype(F32), vf_ref[:, sv],
                                 _ret_decays(lg_ref[0, h], c, False), st_ref[h])
        hb = RET_HEADS + h
        _, st_ref[hb] = _ret_head(None, kb_ref[:, sk].astype(F32), vb_ref[:, sv],
                                  _ret_decays(lg_ref[1, h], c, True), st_ref[hb])


def _ret_kernel(lg_ref, qf_ref, kf_ref, vf_ref, csf_ref, snf_ref,
                qb_ref, kb_ref, vb_ref, csb_ref, snb_ref, st0_ref, of_ref, ob_ref,
                st_ref, pair_ref, qd_ref, kd_ref, cd_ref):
    c = qf_ref.shape[0]

    @pl.when(pl.program_id(0) == 0)
    def _():
        st_ref[...] = st0_ref[...]
        for d in range(2):
            for h in range(RET_HEADS):
                i = d * RET_HEADS + h
                pair_ref[i], qd_ref[i], kd_ref[i], cd_ref[i] = _ret_decays(lg_ref[d, h], c, d == 1)

    csf, snf = csf_ref[...], snf_ref[...]
    csb, snb = csb_ref[...], snb_ref[...]
    for h in range(RET_HEADS):
        sk = slice(h * RET_DK, (h + 1) * RET_DK)
        sv = slice(h * RET_DV, (h + 1) * RET_DV)
        of_ref[:, sv], st_ref[h] = _ret_head(
            _rope(qf_ref[:, sk], csf, snf), _rope(kf_ref[:, sk], csf, snf), vf_ref[:, sv],
            (pair_ref[h], qd_ref[h], kd_ref[h], cd_ref[h]), st_ref[h])
        hb = RET_HEADS + h
        ob_ref[:, sv], st_ref[hb] = _ret_head(
            _rope(qb_ref[:, sk], csb, snb), _rope(kb_ref[:, sk], csb, snb), vb_ref[:, sv],
            (pair_ref[hb], qd_ref[hb], kd_ref[hb], cd_ref[hb]), st_ref[hb])


def _ret_scan(p, pc, ret_lg, cs_tab, sn_tab):
    c = RET_CHUNK
    n, l = p.shape[0], pc.shape[0]
    n_x, n_ctx = n // c, l // c
    qk_w = RET_HEADS * RET_DK
    st_shape = (2 * RET_HEADS, RET_DV, RET_DK)
    smem = pl.BlockSpec(memory_space=pltpu.SMEM)

    def kv_specs(chunk_fn):
        return [
            pl.BlockSpec((c, qk_w), lambda s: (chunk_fn(s), COL_RK_512)),
            pl.BlockSpec((c, RET_WIDTH), lambda s: (chunk_fn(s), COL_RV)),
        ]

    st0 = pl.pallas_call(
        _ret_ctx_kernel,
        grid=(n_ctx,),
        in_specs=[smem, *kv_specs(lambda s: s), *kv_specs(lambda s: n_ctx - 1 - s)],
        out_specs=pl.BlockSpec(st_shape, lambda s: (0, 0, 0)),
        out_shape=jax.ShapeDtypeStruct(st_shape, F32),
        compiler_params=_cparams(1, 32),
        name="ret_ctx",
    )(ret_lg, pc, pc, pc, pc)

    def specs(chunk_fn):
        return [
            pl.BlockSpec((c, qk_w), lambda s: (chunk_fn(s), COL_RQ_512)),
            *kv_specs(chunk_fn),
            pl.BlockSpec((c, RET_DK), lambda s: (chunk_fn(s), 0)),
            pl.BlockSpec((c, RET_DK), lambda s: (chunk_fn(s), 0)),
        ]

    fwd = lambda s: s
    bwd = lambda s: n_x - 1 - s
    return pl.pallas_call(
        _ret_kernel,
        grid=(n_x,),
        in_specs=[smem, *specs(fwd), *specs(bwd), pl.BlockSpec(st_shape, lambda s: (0, 0, 0))],
        out_specs=[
            pl.BlockSpec((c, RET_WIDTH), lambda s: (fwd(s), 0)),
            pl.BlockSpec((c, RET_WIDTH), lambda s: (bwd(s), 0)),
        ],
        out_shape=[jax.ShapeDtypeStruct((n, RET_WIDTH), F32)] * 2,
        scratch_shapes=[
            pltpu.VMEM(st_shape, F32),
            pltpu.VMEM((2 * RET_HEADS, c, c), F32),
            pltpu.VMEM((2 * RET_HEADS, c, RET_DK), F32),
            pltpu.VMEM((2 * RET_HEADS, c, RET_DK), F32),
            pltpu.VMEM((2 * RET_HEADS, 8, RET_DK), F32),
        ],
        compiler_params=_cparams(1, 32),
        name="ret",
    )(ret_lg, p, p, p, cs_tab, sn_tab, p, p, p, cs_tab, sn_tab, st0)


def _rms(y):
    return y * lax.rsqrt(jnp.mean(y * y, axis=-1, keepdims=True) + RMS_EPS)


def _outproj_kernel(of_ref, ob_ref, rf_ref, rb_ref, hgate_ref, rgate_ref, x_ref, wout_ref,
                    hgw_ref, rnw_ref, nw_ref, g1_ref, sh2_ref, sc2_ref, wr_ref, br_ref,
                    x1_ref, h2_ref, lg_ref, cat_scr):
    for h in range(HG_HEADS):
        sl = slice(h * HG_DK, (h + 1) * HG_DK)
        o = of_ref[:, sl] + ob_ref[:, sl]
        cat_scr[:, sl] = (_rms(o) * hgw_ref[0:1, sl] * hgate_ref[:, sl].astype(F32)).astype(BF16)
    for h in range(RET_HEADS):
        sl = slice(h * RET_DV, (h + 1) * RET_DV)
        r = rf_ref[:, sl] + rb_ref[:, sl]
        cat_scr[:, HG_WIDTH + h * RET_DV:HG_WIDTH + (h + 1) * RET_DV] = (
            _rms(r) * rnw_ref[0:1, sl] * rgate_ref[:, sl].astype(F32)).astype(BF16)

    y = _dot(cat_scr[...], wout_ref[...])
    x1 = x_ref[...] + g1_ref[0:1, :] * (_rms(y) * nw_ref[1:2, :])
    x1_ref[...] = x1
    h2 = _rms(x1) * nw_ref[2:3, :] * (1.0 + sc2_ref[0:1, :]) + sh2_ref[0:1, :]
    _store_row_tiles(h2_ref, _pack_pairs(h2))
    hi = h2.astype(BF16)
    lo = (h2 - hi.astype(F32)).astype(BF16)
    parts = _dot(jnp.concatenate([hi, lo], axis=0), wr_ref[...])
    half = hi.shape[0]
    lg_ref[...] = (parts[:half, :ROUTER_LANES] + parts[half:, :ROUTER_LANES]
                   + parts[:half, ROUTER_LANES:] + br_ref[...])


def _out_projection(o_f, o_b, r_f, r_b, p, x2d, w_out_bf, hg_norm_w, ret_norm_w, norm_w, mod,
                    wr_parts, br):
    n, d = x2d.shape
    tm = 256
    row = lambda w: pl.BlockSpec((tm, w), lambda i: (i, 0))
    full = lambda a: pl.BlockSpec(a.shape, lambda i: (0,) * a.ndim)
    modcol = lambda k: pl.BlockSpec((8, d), lambda i: (0, k))
    return pl.pallas_call(
        _outproj_kernel,
        grid=(n // tm,),
        in_specs=[
            row(HG_WIDTH), row(HG_WIDTH), row(RET_WIDTH), row(RET_WIDTH),
            pl.BlockSpec((tm, HG_WIDTH), lambda i: (i, COL_HGATE)),
            pl.BlockSpec((tm, RET_WIDTH), lambda i: (i, COL_RGATE)),
            row(d), full(w_out_bf), full(hg_norm_w), full(ret_norm_w), full(norm_w),
            modcol(2), modcol(3), modcol(4),
            full(wr_parts), full(br),
        ],
        out_specs=[row(d), pl.BlockSpec((tm * TILE_ROWS, LANES), lambda i: (i, 0)),
                   row(ROUTER_LANES)],
        out_shape=[
            jax.ShapeDtypeStruct((n, d), F32),
            jax.ShapeDtypeStruct((n * TILE_ROWS, LANES), jnp.uint32),
            jax.ShapeDtypeStruct((n, ROUTER_LANES), F32),
        ],
        scratch_shapes=[pltpu.VMEM((tm, d), BF16)],
        compiler_params=_cparams(1, 48),
        name="outproj",
    )(o_f, o_b, r_f, r_b, p, p, x2d, w_out_bf, hg_norm_w, ret_norm_w, norm_w, mod, mod, mod,
      wr_parts, br)


ROUTER_TM = 1024


def _lane_first(hit, lane_f):
    return jnp.min(jnp.where(hit, lane_f, 1e9), axis=1, keepdims=True)


def _router_kernel(lg_ref, gates_ref, s0_ref, s1_ref, cnt_ref, ltri, carry, pstart):
    ph = pl.program_id(0)
    i = pl.program_id(1)
    tm = lg_ref.shape[0]

    @pl.when(jnp.logical_and(ph == 0, i == 0))
    def _():
        t = lax.broadcasted_iota(jnp.int32, (tm, tm), 0)
        s = lax.broadcasted_iota(jnp.int32, (tm, tm), 1)
        ltri[...] = jnp.where(s < t, 1.0, 0.0).astype(BF16)

    @pl.when(jnp.logical_and(ph == 1, i == 0))
    def _():
        cnt = carry[...]
        padded = jnp.floor((cnt + (MOE_PAD - 1.0)) * (1.0 / MOE_PAD)) * MOE_PAD
        r = lax.broadcasted_iota(jnp.int32, (LANES, LANES), 0)
        c = lax.broadcasted_iota(jnp.int32, (LANES, LANES), 1)
        before = jnp.where(r < c, 1.0, 0.0).astype(BF16)
        pstart[...] = _dot(padded.astype(BF16), before)
        cnt_ref[...] = cnt.astype(jnp.int32)

    @pl.when(i == 0)
    def _():
        carry[...] = jnp.zeros_like(carry)

    lg = lg_ref[...]
    lane = lax.broadcasted_iota(jnp.int32, lg.shape, 1)
    lane_f = lane.astype(F32)
    neg = -jnp.inf

    gl = jnp.where(lane < N_GROUPS, lg, neg)
    gmax = jnp.max(gl, axis=1, keepdims=True)
    grp = _lane_first(gl == gmax, lane_f).astype(jnp.int32)
    p_sel = 1.0 / jnp.sum(jnp.exp(gl - gmax), axis=1, keepdims=True)

    in_grp = jnp.logical_and(
        jnp.logical_and(lane >= N_GROUPS, lane < N_GROUPS + N_EXPERTS),
        ((lane - N_GROUPS) >> 3) == grp)
    el = jnp.where(in_grp, lg, neg)
    v1 = jnp.max(el, axis=1, keepdims=True)
    o1 = lane_f == _lane_first(el == v1, lane_f)
    el2 = jnp.where(o1, neg, el)
    v2 = jnp.max(el2, axis=1, keepdims=True)
    o2 = lane_f == _lane_first(el2 == v2, lane_f)

    osum = (jnp.where(o1, 1.0, 0.0) + jnp.where(o2, 1.0, 0.0)).astype(BF16)
    earlier = _dot(ltri[...], osum) + carry[0:1, :]
    carry[...] = carry[...] + _dot(jnp.ones((8, tm), BF16), osum)

    @pl.when(ph == 1)
    def _():
        ez = jnp.exp(v2 - v1)
        g1 = p_sel / (1.0 + ez)
        gates_ref[...] = jnp.concatenate([g1, g1 * ez], axis=1)
        base = earlier + pstart[0:1, :]
        for o, s_ref in ((o1, s0_ref), (o2, s1_ref)):
            col = jnp.sum(jnp.where(o, base, 0.0), axis=1, keepdims=True)
            wide = jnp.broadcast_to(col, (tm, LANES))
            for b in range(tm // LANES):
                s_ref[b:b + 1, :] = wide[b * LANES:(b + 1) * LANES, :].T[0:1, :].astype(jnp.int32)


def _router(logits):
    t = logits.shape[0]
    tm = ROUTER_TM
    rows = tm // LANES
    return pl.pallas_call(
        _router_kernel,
        grid=(2, t // tm),
        in_specs=[pl.BlockSpec((tm, LANES), lambda ph, i: (i, 0))],
        out_specs=[
            pl.BlockSpec((tm, EXPERT_TOPK), lambda ph, i: (i * ph, 0)),
            pl.BlockSpec((rows, LANES), lambda ph, i: (i * ph, 0)),
            pl.BlockSpec((rows, LANES), lambda ph, i: (i * ph, 0)),
            pl.BlockSpec((8, LANES), lambda ph, i: (0, 0)),
        ],
        out_shape=[
            jax.ShapeDtypeStruct((t, EXPERT_TOPK), F32),
            jax.ShapeDtypeStruct((t // LANES, LANES), jnp.int32),
            jax.ShapeDtypeStruct((t // LANES, LANES), jnp.int32),
            jax.ShapeDtypeStruct((8, LANES), jnp.int32),
        ],
        scratch_shapes=[
            pltpu.VMEM((tm, tm), BF16),
            pltpu.VMEM((8, LANES), F32),
            pltpu.VMEM((8, LANES), F32),
        ],
        compiler_params=_cparams(2, 32),
        name="router",
    )(logits)


def _round_up_pow2(x, m):
    assert m & (m - 1) == 0
    return (x + (m - 1)) & ~(m - 1)


def _plan_kernel(cnt_ref, s0_ref, s1_ref, tok_ref, ie_ref, ist_ref, ir_ref, if_ref, zeros, sem):
    n_tok = s0_ref.shape[0]
    n_slots = tok_ref.shape[0]
    n_items = ie_ref.shape[0]
    r = MOE_ITEM_ROWS

    zeros[...] = jnp.zeros_like(zeros)
    clear = pltpu.make_async_copy(zeros, tok_ref, sem)
    clear.start()
    clear.wait()

    def scatter(t16, carry):
        for d in range(16):
            t = t16 * 16 + d
            tok_ref[s0_ref[t]] = t
            tok_ref[s1_ref[t]] = t
        return carry
    lax.fori_loop(0, n_tok // 16, scatter, 0)

    a = jnp.int32(0)
    used = jnp.int32(0)
    e_last = jnp.int32(0)
    for e in range(N_EXPERTS):
        pad = _round_up_pow2(cnt_ref[0, N_GROUPS + e], MOE_PAD)

        def item(sub, a, e=e, pad=pad, used=used):
            ie_ref[a] = e
            ist_ref[a] = used + sub * r
            ir_ref[a] = jnp.minimum(pad - sub * r, r)
            if_ref[a] = 0
            return a + 1
        a = lax.fori_loop(0, _round_up_pow2(pad, r) >> (r.bit_length() - 1), item, a)
        e_last = jnp.where(pad > 0, e, e_last)
        used = used + pad

    def idle(a2, carry):
        fill_start = used + (a2 - a) * r
        fill = jnp.clip(n_slots - fill_start, 0, r)
        ie_ref[a2] = e_last
        ist_ref[a2] = jnp.where(fill > 0, fill_start, 0)
        ir_ref[a2] = 0
        if_ref[a2] = fill
        return carry
    lax.fori_loop(a, n_items, idle, 0)


def _plan(counts, s0, s1, n_slots, n_items):
    smem = pl.BlockSpec(memory_space=pltpu.SMEM)
    item = jax.ShapeDtypeStruct((n_items,), jnp.int32)
    return pl.pallas_call(
        _plan_kernel,
        in_specs=[smem, smem, smem],
        out_specs=[smem] * 5,
        out_shape=[jax.ShapeDtypeStruct((n_slots,), jnp.int32), item, item, item, item],
        scratch_shapes=[pltpu.VMEM((n_slots,), jnp.int32), pltpu.SemaphoreType.DMA(())],
        name="plan",
    )(counts, s0, s1)


def _unpack_pairs(u):
    lo = lax.bitcast_convert_type(u << 16, F32)
    hi = lax.bitcast_convert_type(u & jnp.uint32(0xFFFF0000), F32)
    return jnp.concatenate([lo, hi], axis=1)


def _store_row_tiles(dst, packed):
    r = packed.shape[0]
    for s in range(TILE_ROWS):
        dst[pl.ds(s, r, stride=TILE_ROWS), :] = packed[:, s * LANES:(s + 1) * LANES]


def _load_row_tiles(src, first_row, r):
    return jnp.concatenate(
        [src[pl.ds(first_row * TILE_ROWS + s, r, stride=TILE_ROWS), :] for s in range(TILE_ROWS)],
        axis=1)


def _pack_pairs(y):
    w = y.shape[1] // 2
    bits = lax.bitcast_convert_type(y.astype(BF16).astype(F32), jnp.uint32)
    return (bits[:, :w] >> 16) | (bits[:, w:] & jnp.uint32(0xFFFF0000))


def _moe_block_sizes():
    sizes, rows = [], MOE_BLOCK
    while rows >= MOE_PAD:
        sizes.append(rows)
        rows //= 2
    return sizes


def _moe_kernel(item_e, item_start, item_rows, item_fill, slot_tok,
                h2p_hbm, wg_ref, wu_ref, wd_ref, yb_hbm,
                xbuf, act, ystage, pf, gsem, osem):
    a = pl.program_id(0)
    t = pl.program_id(1)
    n_items = pl.num_programs(0)
    n = item_rows[a]
    start = item_start[a]
    half_w = wg_ref.shape[2]
    cur = a % 2
    nxt = 1 - cur
    a1 = jnp.minimum(a + 1, n_items - 1)
    n_next = jnp.where(a + 1 < n_items, item_rows[a1], 0)
    start_next = item_start[a1]

    def tile_rows(i, count=1):
        return pl.ds(pl.multiple_of(i * TILE_ROWS, TILE_ROWS), count * TILE_ROWS)

    def row_copy(tok, slot, i):
        return pltpu.make_async_copy(h2p_hbm.at[tile_rows(tok), :], xbuf.at[slot, tile_rows(i), :],
                                     gsem.at[slot])

    def start_row(tok, slot, i):
        row_copy(tok, slot, i).start(priority=1)

    @pl.when(jnp.logical_and(a == 0, t == 0))
    def _():
        pf[1] = 0
        pf[2] = 0

        def issue(i, carry):
            start_row(slot_tok[start + i], 0, i)
            return carry
        lax.fori_loop(0, n, issue, 0)

    @pl.when(t == 0)
    def _():
        pf[0] = 0

        def drain(bk, carry):
            r0 = pl.multiple_of(bk * MOE_PAD, MOE_PAD)
            pltpu.make_async_copy(h2p_hbm.at[tile_rows(0, MOE_PAD), :],
                                  xbuf.at[cur, tile_rows(r0, MOE_PAD), :], gsem.at[cur]).wait()
            return carry
        lax.fori_loop(0, n // MOE_PAD, drain, 0)

    def prefetch_group():
        base = pf[0]
        for q in range(MOE_GATHER_GROUP):
            start_row(slot_tok[start_next + base + q], nxt, base + q)
        pf[0] = base + MOE_GATHER_GROUP

    def run_blocks(compute):
        def one(b, r0, rows):
            r0 = pl.multiple_of(r0, MOE_PAD)
            more = pf[0] < n_next

            def gather():
                prefetch_group()
                for _ in range(max(rows // (2 * MOE_GATHER_GROUP), 1) - 1):
                    pl.when(pf[0] < n_next)(prefetch_group)

            @pl.when(more)
            def _():
                compute(b, r0, rows, gather)

            @pl.when(jnp.logical_not(more))
            def _():
                compute(b, r0, rows, lambda: None)

        n_big = n // MOE_BLOCK

        def body(b, carry):
            one(b, b * MOE_BLOCK, MOE_BLOCK)
            return carry
        lax.fori_loop(0, n_big, body, 0)

        b, r0, rows = n_big, n_big * MOE_BLOCK, MOE_BLOCK // 2
        while rows >= MOE_PAD:
            has = (n & rows) != 0
            pl.when(has)(functools.partial(one, b, r0, rows))
            b = b + has.astype(jnp.int32)
            r0 = r0 + jnp.where(has, rows, 0)
            rows //= 2

    def gate_up(half):
        def compute(b, r0, rows, mid):
            xs = _unpack_pairs(_load_row_tiles(xbuf.at[cur], r0, rows)).astype(BF16)
            g = _dot(xs, wg_ref[0].astype(BF16))
            u = _dot(xs, wu_ref[0].astype(BF16))
            mid()
            act[pl.ds(r0, rows), half * half_w:(half + 1) * half_w] = (
                g * _sigmoid(g) * u).astype(BF16)
        run_blocks(compute)

    def out_copy(slot, r0, rows):
        return pltpu.make_async_copy(
            ystage.at[slot, tile_rows(0, rows), :],
            yb_hbm.at[tile_rows(pl.multiple_of(start + r0, MOE_PAD), rows), :],
            osem.at[slot])

    def wait_stage(slot):
        for rows in _moe_block_sizes():
            @pl.when(pf[1 + slot] == rows)
            def _(rows=rows):
                out_copy(slot, 0, rows).wait()
                pf[1 + slot] = 0

    def down():
        def compute(b, r0, rows, mid):
            slot = b % 2
            wait_stage(slot)
            y = _dot(act[pl.ds(r0, rows), :], wd_ref[0].astype(BF16))
            mid()
            _store_row_tiles(ystage.at[slot], _pack_pairs(y))
            out_copy(slot, r0, rows).start()
            pf[1 + slot] = rows
        run_blocks(compute)

        def tail(i, carry):
            start_row(slot_tok[start_next + i], nxt, i)
            return carry
        lax.fori_loop(pf[0], n_next, tail, 0)

    pl.when(t == 0)(functools.partial(gate_up, 0))
    pl.when(t == 1)(functools.partial(gate_up, 1))
    pl.when(t == 2)(down)

    fill = item_fill[a]

    @pl.when(jnp.logical_and(t == 0, fill > 0))
    def _():
        wait_stage(0)
        ystage[0] = jnp.zeros(ystage.shape[1:], ystage.dtype)

        def zero_block(bk, carry):
            cp = out_copy(0, bk * MOE_PAD, MOE_PAD)
            cp.start()
            cp.wait()
            return carry
        lax.fori_loop(0, fill // MOE_PAD, zero_block, 0)

    @pl.when(jnp.logical_and(a == n_items - 1, t == pl.num_programs(1) - 1))
    def _():
        wait_stage(0)
        wait_stage(1)


def _moe_ffn(h2p, w_gate, w_up, w_down, item_e, item_start, item_rows, item_fill, slot_tok,
             n_slots):
    d = w_gate.shape[1]
    assert d == 2 * TILE_ROWS * LANES and h2p.shape[1] == LANES, "one packed token row per tile"
    n_items = item_e.shape[0]
    half_w = D_EXPERT // 2

    def wgu_map(a, t, ie, ist, ir, fl, st):
        return (ie[a], 0, jnp.where(ir[a] > 0, jnp.minimum(t, 1), 1))

    def wd_map(a, t, ie, ist, ir, fl, st):
        return (jnp.where(t >= 2, ie[a], ie[jnp.maximum(a - 1, 0)]), 0, 0)

    grid_spec = pltpu.PrefetchScalarGridSpec(
        num_scalar_prefetch=5,
        grid=(n_items, 3),
        in_specs=[
            pl.BlockSpec(memory_space=pl.ANY),
            pl.BlockSpec((1, d, half_w), wgu_map),
            pl.BlockSpec((1, d, half_w), wgu_map),
            pl.BlockSpec((1, D_EXPERT, d), wd_map),
        ],
        out_specs=pl.BlockSpec(memory_space=pl.ANY),
        scratch_shapes=[
            pltpu.VMEM((2, MOE_ITEM_ROWS * TILE_ROWS, LANES), jnp.uint32),
            pltpu.VMEM((MOE_ITEM_ROWS, D_EXPERT), BF16),
            pltpu.VMEM((2, MOE_BLOCK * TILE_ROWS, LANES), jnp.uint32),
            pltpu.SMEM((3,), jnp.int32),
            pltpu.SemaphoreType.DMA((2,)),
            pltpu.SemaphoreType.DMA((2,)),
        ],
    )
    return pl.pallas_call(
        _moe_kernel,
        grid_spec=grid_spec,
        out_shape=jax.ShapeDtypeStruct((n_slots * TILE_ROWS, LANES), jnp.uint32),
        compiler_params=_cparams(2, 56),
        name="moe",
    )(item_e, item_start, item_rows, item_fill, slot_tok, h2p, w_gate, w_up, w_down)


def _combine_kernel(s0, s1, yb_hbm, gates_ref, x1_ref, nw_ref, g2_ref, out_ref, rows, sem):
    i = pl.program_id(0)
    n_tiles = pl.num_programs(0)
    tm = x1_ref.shape[0]

    def tile_rows(i, count=1):
        return pl.ds(pl.multiple_of(i * TILE_ROWS, TILE_ROWS), count * TILE_ROWS)

    def gather(tile, buf):
        def issue(r8, carry):
            t0 = tile * tm + r8 * 8
            for dr in range(8):
                for k, s_k in enumerate((s0, s1)):
                    pltpu.make_async_copy(yb_hbm.at[tile_rows(s_k[t0 + dr]), :],
                                          rows.at[buf, k, tile_rows(r8 * 8 + dr), :],
                                          sem.at[buf]).start()
            return carry
        lax.fori_loop(0, tm // 8, issue, 0)

    @pl.when(i == 0)
    def _():
        gather(0, 0)

    @pl.when(i + 1 < n_tiles)
    def _():
        gather(i + 1, (i + 1) % 2)

    buf = i % 2
    for k in range(EXPERT_TOPK):
        pltpu.make_async_copy(yb_hbm.at[tile_rows(0, tm), :], rows.at[buf, k], sem.at[buf]).wait()

    g = gates_ref[...]
    y = (_unpack_pairs(_load_row_tiles(rows.at[buf, 0], 0, tm)) * g[:, 0:1]
         + _unpack_pairs(_load_row_tiles(rows.at[buf, 1], 0, tm)) * g[:, 1:2])
    out_ref[...] = x1_ref[...] + g2_ref[0:1, :] * (_rms(y) * nw_ref[3:4, :])


def _combine(s0, s1, yb, gates, x1, norm_w, mod):
    n, d = x1.shape
    tm = 256
    grid_spec = pltpu.PrefetchScalarGridSpec(
        num_scalar_prefetch=2,
        grid=(n // tm,),
        in_specs=[
            pl.BlockSpec(memory_space=pl.ANY),
            pl.BlockSpec((tm, EXPERT_TOPK), lambda i, *_: (i, 0)),
            pl.BlockSpec((tm, d), lambda i, *_: (i, 0)),
            pl.BlockSpec((4, d), lambda i, *_: (0, 0)),
            pl.BlockSpec((8, d), lambda i, *_: (0, 5)),
        ],
        out_specs=pl.BlockSpec((tm, d), lambda i, *_: (i, 0)),
        scratch_shapes=[
            pltpu.VMEM((2, EXPERT_TOPK, tm * TILE_ROWS, LANES), jnp.uint32),
            pltpu.SemaphoreType.DMA((2,)),
        ],
    )
    return pl.pallas_call(
        _combine_kernel,
        grid_spec=grid_spec,
        out_shape=jax.ShapeDtypeStruct((n, d), F32),
        compiler_params=_cparams(1, 40),
        name="combine",
    )(s0, s1, yb, gates, x1, norm_w, mod)


def _rope_tables(n):
    rows = n // GRID_W
    r = np.repeat(np.arange(rows, dtype=np.float32), GRID_W)
    cidx = np.tile(np.arange(GRID_W, dtype=np.float32), rows)
    n_freq = RET_DK // 4
    inv = np.float32(ROPE_BASE) ** (-np.arange(n_freq, dtype=np.float32) / np.float32(n_freq))
    ang = np.concatenate([r[:, None] * inv, cidx[:, None] * inv], axis=-1).astype(np.float32)
    cos, sin = np.cos(ang), np.sin(ang)
    cs = np.concatenate([cos, cos], axis=-1)
    sn = np.concatenate([-sin, sin], axis=-1)
    return jnp.asarray(cs, F32), jnp.asarray(sn, F32)


def kernel(x, c, ctx, c_ctx, w_mod, b_mod, norm_w, w_in, hg_lb_logits, hg_norm_w, ret_decay_logits,
           ret_norm_w, w_out, router_group_w, router_group_b, router_expert_w, router_expert_b,
           w_gate, w_up, w_down):
    bsz, n, d = x.shape
    l = ctx.shape[1]
    assert bsz == 1 and w_mod.shape[0] == 1, "single sample, single layer"
    x2d, ctx2d = x[0], ctx[0]

    cc = jnp.zeros((8, d), F32).at[0].set(c[0]).at[1].set(c_ctx)
    mod = _modulation(cc, w_mod[0], b_mod[0])

    lb = jnp.cumsum(jax.nn.softmax(hg_lb_logits.astype(F32), axis=0), axis=0)[0]
    rk0 = 5 * HG_WIDTH + RET_HEADS * RET_DK
    cols = jnp.stack([
        jnp.zeros((PROJ_WIDTH,), F32).at[HG_WIDTH:3 * HG_WIDTH].set(lb.reshape(-1)),
        jnp.ones((PROJ_WIDTH,), F32).at[rk0:rk0 + RET_HEADS * RET_DK].set(RET_DK ** -0.5),
    ])
    w_in_bf = w_in[0].astype(BF16)
    p, g = _projection(x2d, norm_w[0], mod, w_in_bf, cols, 0, 1024, "proj_x")
    pc, gc = _projection(ctx2d, norm_w[0], mod, w_in_bf, cols, 1, l, "proj_ctx")

    ret_lg = jax.nn.log_sigmoid(ret_decay_logits[0].astype(F32))
    cs_tab, sn_tab = _rope_tables(n)

    o_f, o_b = _hgrn_scan(p, g, pc, gc)
    r_f, r_b = _ret_scan(p, pc, ret_lg, cs_tab, sn_tab)

    wr = jnp.concatenate([router_group_w[0], router_expert_w[0]], axis=1)
    wr = jnp.pad(wr, ((0, 0), (0, ROUTER_LANES - wr.shape[1])))
    wr_hi = wr.astype(BF16)
    wr_parts = jnp.concatenate([wr_hi, (wr - wr_hi.astype(F32)).astype(BF16)], axis=1)
    br = jnp.concatenate([router_group_b[0], router_expert_b[0]]).astype(F32)
    br = jnp.pad(br, (0, ROUTER_LANES - br.shape[0])).reshape(1, ROUTER_LANES)
    x1, h2, logits = _out_projection(
        o_f, o_b, r_f, r_b, p, x2d, w_out[0].astype(BF16), hg_norm_w, ret_norm_w, norm_w[0], mod,
        wr_parts, br)

    n_assign = n * EXPERT_TOPK
    n_slots = n_assign + N_EXPERTS * MOE_PAD
    n_items = (n_slots + N_EXPERTS * (MOE_ITEM_ROWS - MOE_PAD)) // MOE_ITEM_ROWS
    gates, s0, s1, counts = _router(logits)
    s0, s1 = s0.reshape(-1), s1.reshape(-1)
    slot_tok, item_e, item_start, item_rows, item_fill = _plan(counts, s0, s1, n_slots, n_items)

    yb = _moe_ffn(h2, w_gate[0], w_up[0], w_down[0], item_e, item_start, item_rows, item_fill,
                  slot_tok, n_slots)
    out = _combine(s0, s1, yb, gates, x1, norm_w[0], mod)
    return out[None]
```

```python
import functools

import jax
import jax.numpy as jnp
import numpy as np
from jax import lax
from jax.experimental import pallas as pl
from jax.experimental.pallas import tpu as pltpu

F32 = jnp.float32
BF16 = jnp.bfloat16

D_MODEL = 2048
GRID_W = 64
HG_HEADS = 8
HG_DK = 128
HG_WIDTH = 1024
RET_HEADS = 4
RET_DK = 128
RET_DV = 256
RET_WIDTH = 1024
PROJ_WIDTH = 8192
N_GROUPS = 4
EXPERTS_PER_GROUP = 8
N_EXPERTS = 32
EXPERT_TOPK = 2
D_EXPERT = 1024
RMS_EPS = 1e-6
ROPE_BASE = 10000.0

COL_HQ, COL_HF_FWD, COL_HF_BWD, COL_HI, COL_HGATE = 0, 1, 2, 3, 4
COL_RQ_512, COL_RK_512 = 10, 11
COL_RV, COL_RGATE = 6, 7

HG_CHUNK = 256
HG_STEP_HEADS = 8
HG_DIAG = 16
HG_LOG_CLAMP = 10.0
RET_CHUNK = 256

MOE_PAD = 128
MOE_BLOCK = 256
MOE_ITEM_ROWS = 1024
MOE_GATHER_GROUP = 128
ROUTER_LANES = 128
LANES = 128
TILE_ROWS = 8


def _cparams(n_axes, vmem_mb, **flags):
    return pltpu.CompilerParams(
        dimension_semantics=("arbitrary",) * n_axes,
        vmem_limit_bytes=vmem_mb * 1024 * 1024,
        flags=flags or None,
    )


def _sigmoid(x):
    return 1.0 / (1.0 + jnp.exp(-x))


def _dot(a, b):
    return jnp.dot(a, b, preferred_element_type=F32)


def _dot_nt(a, b):
    return lax.dot_general(a, b, (((1,), (1,)), ((), ())), preferred_element_type=F32)


def _dot_tn(a, b):
    return lax.dot_general(a, b, (((0,), (0,)), ((), ())), preferred_element_type=F32)


def _mod_kernel(cc_ref, w_ref, b_ref, o_ref):
    s = cc_ref[...]
    s = s * _sigmoid(s)
    s16 = jnp.concatenate([s, s], axis=0)
    hi = s16.astype(BF16).astype(F32)
    row = lax.broadcasted_iota(jnp.int32, s16.shape, 0)
    lhs = jnp.where(row < 8, hi, s16 - hi).astype(BF16)
    r = _dot(lhs, w_ref[...].astype(BF16))
    o_ref[...] = r[:8] + r[8:] + b_ref[...]


def _modulation(cc, w_mod, b_mod):
    d, n = w_mod.shape
    tn = 1024
    return pl.pallas_call(
        _mod_kernel,
        grid=(n // tn,),
        in_specs=[
            pl.BlockSpec((8, d), lambda j: (0, 0)),
            pl.BlockSpec((d, tn), lambda j: (0, j)),
            pl.BlockSpec((1, tn), lambda j: (0, j)),
        ],
        out_specs=pl.BlockSpec((8, tn), lambda j: (0, j)),
        out_shape=jax.ShapeDtypeStruct((8, n), F32),
        compiler_params=_cparams(1, 40),
        name="mod",
    )(cc, w_mod, b_mod.reshape(1, n))


PROJ_TN = 1024
PROLOGUE_ROWS = 32


def _proj_kernel(x_ref, nw_ref, sh_ref, sc_ref, w_ref, cols_ref, o_ref, g_ref, *rest, mod_row):
    h_scr = rest[-1]
    if len(rest) == 2:
        rest[0][...] = w_ref[...].astype(BF16)
        w_ref = rest[0]
    i = pl.program_id(0)
    j = pl.program_id(1)
    last_j = pl.num_programs(1) - 1
    cur = i % 2

    def normed(x):
        scale = nw_ref[0:1, :] * (1.0 + sc_ref[mod_row:mod_row + 1, :])
        shift = sh_ref[mod_row:mod_row + 1, :]
        y = x * lax.rsqrt(jnp.mean(x * x, axis=-1, keepdims=True) + RMS_EPS)
        return (y * scale + shift).astype(BF16)

    @pl.when(jnp.logical_and(i == 0, j == 0))
    def _():
        def body(r, carry):
            r0 = pl.multiple_of(r * PROLOGUE_ROWS, PROLOGUE_ROWS)
            h_scr[0, pl.ds(r0, PROLOGUE_ROWS), :] = normed(x_ref[pl.ds(r0, PROLOGUE_ROWS), :])
            return carry
        lax.fori_loop(0, x_ref.shape[0] // PROLOGUE_ROWS, body, 0)

    is_gate = jnp.logical_or(j == 1, j == 2)
    is_silu = jnp.logical_or(j == 0, j == 4)
    is_lin = jnp.logical_not(jnp.logical_or(jnp.logical_or(is_gate, is_silu), j == last_j))

    @pl.when(is_lin)
    def _():
        o_ref[...] = (_dot(h_scr[cur], w_ref[...]) * cols_ref[1:2, :]).astype(BF16)

    @pl.when(is_silu)
    def _():
        acc = _dot(h_scr[cur], w_ref[...])
        o_ref[...] = (acc * _sigmoid(acc)).astype(BF16)

    @pl.when(is_gate)
    def _():
        acc = _dot(h_scr[cur], w_ref[...])
        lb = cols_ref[0:1, :]
        g = jnp.maximum(jnp.log(lb + (1.0 - lb) * _sigmoid(acc)), -HG_LOG_CLAMP)
        g_ref[...] = g
        o_ref[...] = g.astype(BF16)

    @pl.when(j == last_j)
    def _():
        acc = _dot(h_scr[cur], w_ref[...])
        o_ref[...] = (acc * _sigmoid(acc)).astype(BF16)
        h_scr[1 - cur] = normed(x_ref[...])


def _projection(x2d, norm_w, mod, w_in, cols, mod_row, tm, name):
    n, d = x2d.shape
    tn = PROJ_TN
    n_i, n_j = n // tm, PROJ_WIDTH // tn
    emit_w = w_in.dtype != BF16
    assert not emit_w or n_i == 1, "the weight copy is written once per column tile"

    def x_map(i, j):
        return (jnp.minimum(i + (j == n_j - 1).astype(jnp.int32), n_i - 1), 0)

    return pl.pallas_call(
        functools.partial(_proj_kernel, mod_row=mod_row),
        grid=(n_i, n_j),
        in_specs=[
            pl.BlockSpec((tm, d), x_map),
            pl.BlockSpec((4, d), lambda i, j: (0, 0)),
            pl.BlockSpec((8, d), lambda i, j: (0, 0)),
            pl.BlockSpec((8, d), lambda i, j: (0, 1)),
            pl.BlockSpec((d, tn), lambda i, j: (0, j)),
            pl.BlockSpec((2, tn), lambda i, j: (0, j)),
        ],
        out_specs=[
            pl.BlockSpec((tm, tn), lambda i, j: (i, j)),
            pl.BlockSpec((tm, tn), lambda i, j: (i, jnp.clip(j - 1, 0, 1))),
            *([pl.BlockSpec((d, tn), lambda i, j: (0, j))] if emit_w else []),
        ],
        out_shape=[
            jax.ShapeDtypeStruct((n, PROJ_WIDTH), BF16),
            jax.ShapeDtypeStruct((n, 2 * HG_WIDTH), F32),
            *([jax.ShapeDtypeStruct(w_in.shape, BF16)] if emit_w else []),
        ],
        scratch_shapes=[pltpu.VMEM((2, tm, d), BF16)],
        compiler_params=_cparams(2, 56),
        name=name,
    )(x2d, norm_w, mod, mod, w_in, cols)


def _tri_ones(c, rev):
    t = lax.broadcasted_iota(jnp.int32, (c, c), 0)
    s = lax.broadcasted_iota(jnp.int32, (c, c), 1)
    return jnp.where((s >= t) if rev else (s <= t), 1.0, 0.0).astype(BF16)


def _cumsum_rows(g, tri):
    hi = g.astype(BF16)
    lo = (g - hi.astype(F32)).astype(BF16)
    return _dot(tri, hi) + _dot(tri, lo)


def _row_refs(b, idxs, blk):
    parts = [jnp.broadcast_to(b[i:i + 1, :], (blk, b.shape[1])) for i in idxs]
    return parts[0] if len(parts) == 1 else jnp.concatenate(parts, axis=0)


def _hgrn_levels(c, rev):
    t = lax.broadcasted_iota(jnp.int32, (c, c), 0)
    s = lax.broadcasted_iota(jnp.int32, (c, c), 1)
    levels = []
    m = c // 2
    while m >= HG_DIAG:
        blk = 2 * m
        same = ((t ^ s) & ~(blk - 1)) == 0
        if rev:
            idxs = [b0 + m for b0 in range(0, c, blk)]
            cross = jnp.logical_and((t & m) == 0, (s & m) != 0)
        else:
            idxs = [b0 + m - 1 for b0 in range(0, c, blk)]
            cross = jnp.logical_and((t & m) != 0, (s & m) == 0)
        levels.append((idxs, blk, jnp.logical_and(same, cross)))
        m //= 2
    blk = HG_DIAG
    same = ((t ^ s) & ~(blk - 1)) == 0
    if rev:
        idxs = [b0 + blk // 2 for b0 in range(0, c, blk)]
        mask = jnp.logical_and(same, s >= t)
    else:
        idxs = [b0 + blk // 2 - 1 for b0 in range(0, c, blk)]
        mask = jnp.logical_and(same, s <= t)
    levels.append((idxs, blk, mask))
    return levels


def _hgrn_head(q_bf, g, b, v_bf, st, rev, levels):
    c = g.shape[0]
    kk = 1.0 - jnp.exp(g)

    o = None
    if levels is not None:
        q = q_bf.astype(F32)
        att = jnp.zeros((c, c), F32)
        for idxs, blk, mask in levels:
            ref = _row_refs(b, idxs, blk)
            a_l = _dot_nt((q * jnp.exp(b - ref)).astype(BF16),
                          (kk * jnp.exp(ref - b)).astype(BF16))
            att = att + jnp.where(mask, a_l, 0.0)
        o = _dot(att.astype(BF16), v_bf) + _dot_nt((q * jnp.exp(b)).astype(BF16), st.astype(BF16))

    b_end = b[0:1, :] if rev else b[c - 1:c, :]
    k_end = (kk * jnp.exp(b_end - b)).astype(BF16)
    st_new = st * jnp.exp(b_end) + _dot_tn(v_bf, k_end)
    return o, st_new


def _hgrn_ctx_kernel(gf_ref, vf_ref, gb_ref, vb_ref, st_ref):
    @pl.when(pl.program_id(0) == 0)
    def _():
        st_ref[...] = jnp.zeros_like(st_ref)

    c = gf_ref.shape[0]
    bf = _cumsum_rows(gf_ref[...], _tri_ones(c, False))
    bb = _cumsum_rows(gb_ref[...], _tri_ones(c, True))
    for h in range(HG_HEADS):
        sl = slice(h * HG_DK, (h + 1) * HG_DK)
        _, st_ref[h] = _hgrn_head(None, gf_ref[:, sl], bf[:, sl], vf_ref[:, sl], st_ref[h],
                                  False, None)
        hb = HG_HEADS + h
        _, st_ref[hb] = _hgrn_head(None, gb_ref[:, sl], bb[:, sl], vb_ref[:, sl], st_ref[hb],
                                   True, None)


def _hgrn_kernel(qf_ref, gf_ref, vf_ref, qb_ref, gb_ref, vb_ref, st0_ref, of_ref, ob_ref, st_ref):
    hg = pl.program_id(1)

    @pl.when(jnp.logical_and(pl.program_id(0) == 0, hg == 0))
    def _():
        st_ref[...] = st0_ref[...]

    c = gf_ref.shape[0]
    lev_f = _hgrn_levels(c, False)
    lev_b = _hgrn_levels(c, True)
    bf = _cumsum_rows(gf_ref[...], _tri_ones(c, False))
    bb = _cumsum_rows(gb_ref[...], _tri_ones(c, True))
    for h in range(HG_STEP_HEADS):
        sl = slice(h * HG_DK, (h + 1) * HG_DK)
        hf = hg * HG_STEP_HEADS + h
        of_ref[:, sl], st_ref[hf] = _hgrn_head(qf_ref[:, sl], gf_ref[:, sl], bf[:, sl],
                                               vf_ref[:, sl], st_ref[hf], False, lev_f)
        hb = HG_HEADS + hf
        ob_ref[:, sl], st_ref[hb] = _hgrn_head(qb_ref[:, sl], gb_ref[:, sl], bb[:, sl],
                                               vb_ref[:, sl], st_ref[hb], True, lev_b)


def _hgrn_scan(p, g, pc, gc):
    c = HG_CHUNK
    w = HG_WIDTH
    n, l = p.shape[0], pc.shape[0]
    c_ctx = min(c, l)
    n_x, n_ctx = n // c, l // c_ctx
    st_shape = (2 * HG_HEADS, HG_DK, HG_DK)

    def spec(chunk_fn, col):
        return pl.BlockSpec((c_ctx, w), lambda s: (chunk_fn(s), col))

    st0 = pl.pallas_call(
        _hgrn_ctx_kernel,
        grid=(n_ctx,),
        in_specs=[
            spec(lambda s: s, 0), spec(lambda s: s, COL_HI),
            spec(lambda s: n_ctx - 1 - s, 1), spec(lambda s: n_ctx - 1 - s, COL_HI),
        ],
        out_specs=pl.BlockSpec(st_shape, lambda s: (0, 0, 0)),
        out_shape=jax.ShapeDtypeStruct(st_shape, F32),
        compiler_params=_cparams(1, 32),
        name="hgrn_ctx",
    )(gc, pc, gc, pc)

    n_hg = HG_HEADS // HG_STEP_HEADS
    wg = HG_STEP_HEADS * HG_DK
    fwd = lambda s: s
    bwd = lambda s: n_x - 1 - s

    def gspec(chunk_fn, col):
        return pl.BlockSpec((c, wg), lambda s, hg: (chunk_fn(s), col * n_hg + hg))

    return pl.pallas_call(
        _hgrn_kernel,
        grid=(n_x, n_hg),
        in_specs=[
            gspec(fwd, COL_HQ), gspec(fwd, 0), gspec(fwd, COL_HI),
            gspec(bwd, COL_HQ), gspec(bwd, 1), gspec(bwd, COL_HI),
            pl.BlockSpec(st_shape, lambda s, hg: (0, 0, 0)),
        ],
        out_specs=[gspec(fwd, 0), gspec(bwd, 0)],
        out_shape=[jax.ShapeDtypeStruct((n, w), F32)] * 2,
        scratch_shapes=[pltpu.VMEM(st_shape, F32)],
        compiler_params=_cparams(2, 32),
        name="hgrn",
    )(p, g, p, p, g, p, st0)


def _rope(x_bf, cs, sn):
    x = x_bf.astype(F32)
    return x * cs + pltpu.roll(x, RET_DK // 2, axis=1) * sn


def _ret_decays(lg, c, rev):
    t = lax.broadcasted_iota(jnp.int32, (c, c), 0)
    s = lax.broadcasted_iota(jnp.int32, (c, c), 1)
    diff = (s - t) if rev else (t - s)
    pair = jnp.where(diff >= 0, jnp.exp(lg * jnp.maximum(diff, 0).astype(F32)), 0.0)
    pos = lax.broadcasted_iota(jnp.int32, (c, RET_DK), 0).astype(F32)
    q_dec = jnp.exp(lg * ((c - pos) if rev else (pos + 1.0)))
    k_dec = jnp.exp(lg * (pos if rev else (c - 1.0 - pos)))
    chunk = jnp.exp(jnp.full((8, RET_DK), lg * c, F32))
    return pair, q_dec, k_dec, chunk


def _ret_head(q, k, v_bf, decays, st):
    pair, q_dec, k_dec, chunk = decays
    o = None
    if q is not None:
        att = _dot_nt(q.astype(BF16), k.astype(BF16)) * pair
        o = _dot(att.astype(BF16), v_bf) + _dot_nt((q * q_dec).astype(BF16), st.astype(BF16))
    st_new = st * chunk[0:1, :] + _dot_tn(v_bf, (k * k_dec).astype(BF16))
    return o, st_new


def _ret_ctx_kernel(lg_ref, kf_ref, vf_ref, kb_ref, vb_ref, st_ref):
    @pl.when(pl.program_id(0) == 0)
    def _():
        st_ref[...] = jnp.zeros_like(st_ref)

    c = kf_ref.shape[0]
    for h in range(RET_HEADS):
        sk = slice(h * RET_DK, (h + 1) * RET_DK)
        sv = slice(h * RET_DV, (h + 1) * RET_DV)
        _, st_ref[h] = _ret_head(None, kf_ref[:, sk].astype(F32), vf_ref[:, sv],
                                 _ret_decays(lg_ref[0, h], c, False), st_ref[h])
        hb = RET_HEADS + h
        _, st_ref[hb] = _ret_head(None, kb_ref[:, sk].astype(F32), vb_ref[:, sv],
                                  _ret_decays(lg_ref[1, h], c, True), st_ref[hb])


def _ret_kernel(lg_ref, qf_ref, kf_ref, vf_ref, csf_ref, snf_ref,
                qb_ref, kb_ref, vb_ref, csb_ref, snb_ref, st0_ref, of_ref, ob_ref,
                st_ref, pair_ref, qd_ref, kd_ref, cd_ref):
    c = qf_ref.shape[0]

    @pl.when(pl.program_id(0) == 0)
    def _():
        st_ref[...] = st0_ref[...]
        for d in range(2):
            for h in range(RET_HEADS):
                i = d * RET_HEADS + h
                pair_ref[i], qd_ref[i], kd_ref[i], cd_ref[i] = _ret_decays(lg_ref[d, h], c, d == 1)

    csf, snf = csf_ref[...], snf_ref[...]
    csb, snb = csb_ref[...], snb_ref[...]
    for h in range(RET_HEADS):
        sk = slice(h * RET_DK, (h + 1) * RET_DK)
        sv = slice(h * RET_DV, (h + 1) * RET_DV)
        of_ref[:, sv], st_ref[h] = _ret_head(
            _rope(qf_ref[:, sk], csf, snf), _rope(kf_ref[:, sk], csf, snf), vf_ref[:, sv],
            (pair_ref[h], qd_ref[h], kd_ref[h], cd_ref[h]), st_ref[h])
        hb = RET_HEADS + h
        ob_ref[:, sv], st_ref[hb] = _ret_head(
            _rope(qb_ref[:, sk], csb, snb), _rope(kb_ref[:, sk], csb, snb), vb_ref[:, sv],
            (pair_ref[hb], qd_ref[hb], kd_ref[hb], cd_ref[hb]), st_ref[hb])


def _ret_scan(p, pc, ret_lg, cs_tab, sn_tab):
    c = RET_CHUNK
    n, l = p.shape[0], pc.shape[0]
    n_x, n_ctx = n // c, l // c
    qk_w = RET_HEADS * RET_DK
    st_shape = (2 * RET_HEADS, RET_DV, RET_DK)
    smem = pl.BlockSpec(memory_space=pltpu.SMEM)

    def kv_specs(chunk_fn):
        return [
            pl.BlockSpec((c, qk_w), lambda s: (chunk_fn(s), COL_RK_512)),
            pl.BlockSpec((c, RET_WIDTH), lambda s: (chunk_fn(s), COL_RV)),
        ]

    st0 = pl.pallas_call(
        _ret_ctx_kernel,
        grid=(n_ctx,),
        in_specs=[smem, *kv_specs(lambda s: s), *kv_specs(lambda s: n_ctx - 1 - s)],
        out_specs=pl.BlockSpec(st_shape, lambda s: (0, 0, 0)),
        out_shape=jax.ShapeDtypeStruct(st_shape, F32),
        compiler_params=_cparams(1, 32),
        name="ret_ctx",
    )(ret_lg, pc, pc, pc, pc)

    def specs(chunk_fn):
        return [
            pl.BlockSpec((c, qk_w), lambda s: (chunk_fn(s), COL_RQ_512)),
            *kv_specs(chunk_fn),
            pl.BlockSpec((c, RET_DK), lambda s: (chunk_fn(s), 0)),
            pl.BlockSpec((c, RET_DK), lambda s: (chunk_fn(s), 0)),
        ]

    fwd = lambda s: s
    bwd = lambda s: n_x - 1 - s
    return pl.pallas_call(
        _ret_kernel,
        grid=(n_x,),
        in_specs=[smem, *specs(fwd), *specs(bwd), pl.BlockSpec(st_shape, lambda s: (0, 0, 0))],
        out_specs=[
            pl.BlockSpec((c, RET_WIDTH), lambda s: (fwd(s), 0)),
            pl.BlockSpec((c, RET_WIDTH), lambda s: (bwd(s), 0)),
        ],
        out_shape=[jax.ShapeDtypeStruct((n, RET_WIDTH), F32)] * 2,
        scratch_shapes=[
            pltpu.VMEM(st_shape, F32),
            pltpu.VMEM((2 * RET_HEADS, c, c), F32),
            pltpu.VMEM((2 * RET_HEADS, c, RET_DK), F32),
            pltpu.VMEM((2 * RET_HEADS, c, RET_DK), F32),
            pltpu.VMEM((2 * RET_HEADS, 8, RET_DK), F32),
        ],
        compiler_params=_cparams(1, 32),
        name="ret",
    )(ret_lg, p, p, p, cs_tab, sn_tab, p, p, p, cs_tab, sn_tab, st0)


def _rms(y):
    return y * lax.rsqrt(jnp.mean(y * y, axis=-1, keepdims=True) + RMS_EPS)


def _outproj_kernel(of_ref, ob_ref, rf_ref, rb_ref, hgate_ref, rgate_ref, x_ref, wout_ref,
                    hgw_ref, rnw_ref, nw_ref, g1_ref, sh2_ref, sc2_ref, wr_ref, br_ref,
                    x1_ref, h2_ref, lg_ref, cat_scr):
    for h in range(HG_HEADS):
        sl = slice(h * HG_DK, (h + 1) * HG_DK)
        o = of_ref[:, sl] + ob_ref[:, sl]
        cat_scr[:, sl] = (_rms(o) * hgw_ref[0:1, sl] * hgate_ref[:, sl].astype(F32)).astype(BF16)
    for h in range(RET_HEADS):
        sl = slice(h * RET_DV, (h + 1) * RET_DV)
        r = rf_ref[:, sl] + rb_ref[:, sl]
        cat_scr[:, HG_WIDTH + h * RET_DV:HG_WIDTH + (h + 1) * RET_DV] = (
            _rms(r) * rnw_ref[0:1, sl] * rgate_ref[:, sl].astype(F32)).astype(BF16)

    y = _dot(cat_scr[...], wout_ref[...])
    x1 = x_ref[...] + g1_ref[0:1, :] * (_rms(y) * nw_ref[1:2, :])
    x1_ref[...] = x1
    h2 = _rms(x1) * nw_ref[2:3, :] * (1.0 + sc2_ref[0:1, :]) + sh2_ref[0:1, :]
    _store_row_tiles(h2_ref, _pack_pairs(h2))
    hi = h2.astype(BF16)
    lo = (h2 - hi.astype(F32)).astype(BF16)
    parts = _dot(jnp.concatenate([hi, lo], axis=0), wr_ref[...])
    half = hi.shape[0]
    lg_ref[...] = (parts[:half, :ROUTER_LANES] + parts[half:, :ROUTER_LANES]
                   + parts[:half, ROUTER_LANES:] + br_ref[...])


def _out_projection(o_f, o_b, r_f, r_b, p, x2d, w_out_bf, hg_norm_w, ret_norm_w, norm_w, mod,
                    wr_parts, br):
    n, d = x2d.shape
    tm = 256
    row = lambda w: pl.BlockSpec((tm, w), lambda i: (i, 0))
    full = lambda a: pl.BlockSpec(a.shape, lambda i: (0,) * a.ndim)
    modcol = lambda k: pl.BlockSpec((8, d), lambda i: (0, k))
    return pl.pallas_call(
        _outproj_kernel,
        grid=(n // tm,),
        in_specs=[
            row(HG_WIDTH), row(HG_WIDTH), row(RET_WIDTH), row(RET_WIDTH),
            pl.BlockSpec((tm, HG_WIDTH), lambda i: (i, COL_HGATE)),
            pl.BlockSpec((tm, RET_WIDTH), lambda i: (i, COL_RGATE)),
            row(d), full(w_out_bf), full(hg_norm_w), full(ret_norm_w), full(norm_w),
            modcol(2), modcol(3), modcol(4),
            full(wr_parts), full(br),
        ],
        out_specs=[row(d), pl.BlockSpec((tm * TILE_ROWS, LANES), lambda i: (i, 0)),
                   row(ROUTER_LANES)],
        out_shape=[
            jax.ShapeDtypeStruct((n, d), F32),
            jax.ShapeDtypeStruct((n * TILE_ROWS, LANES), jnp.uint32),
            jax.ShapeDtypeStruct((n, ROUTER_LANES), F32),
        ],
        scratch_shapes=[pltpu.VMEM((tm, d), BF16)],
        compiler_params=_cparams(1, 48),
        name="outproj",
    )(o_f, o_b, r_f, r_b, p, p, x2d, w_out_bf, hg_norm_w, ret_norm_w, norm_w, mod, mod, mod,
      wr_parts, br)


ROUTER_TM = 1024


def _lane_first(hit, lane_f):
    return jnp.min(jnp.where(hit, lane_f, 1e9), axis=1, keepdims=True)


def _router_kernel(lg_ref, gates_ref, s0_ref, s1_ref, cnt_ref, ltri, carry, pstart):
    ph = pl.program_id(0)
    i = pl.program_id(1)
    tm = lg_ref.shape[0]

    @pl.when(jnp.logical_and(ph == 0, i == 0))
    def _():
        t = lax.broadcasted_iota(jnp.int32, (tm, tm), 0)
        s = lax.broadcasted_iota(jnp.int32, (tm, tm), 1)
        ltri[...] = jnp.where(s < t, 1.0, 0.0).astype(BF16)

    @pl.when(jnp.logical_and(ph == 1, i == 0))
    def _():
        cnt = carry[...]
        padded = jnp.floor((cnt + (MOE_PAD - 1.0)) * (1.0 / MOE_PAD)) * MOE_PAD
        r = lax.broadcasted_iota(jnp.int32, (LANES, LANES), 0)
        c = lax.broadcasted_iota(jnp.int32, (LANES, LANES), 1)
        before = jnp.where(r < c, 1.0, 0.0).astype(BF16)
        pstart[...] = _dot(padded.astype(BF16), before)
        cnt_ref[...] = cnt.astype(jnp.int32)

    @pl.when(i == 0)
    def _():
        carry[...] = jnp.zeros_like(carry)

    lg = lg_ref[...]
    lane = lax.broadcasted_iota(jnp.int32, lg.shape, 1)
    lane_f = lane.astype(F32)
    neg = -jnp.inf

    gl = jnp.where(lane < N_GROUPS, lg, neg)
    gmax = jnp.max(gl, axis=1, keepdims=True)
    grp = _lane_first(gl == gmax, lane_f).astype(jnp.int32)
    p_sel = 1.0 / jnp.sum(jnp.exp(gl - gmax), axis=1, keepdims=True)

    in_grp = jnp.logical_and(
        jnp.logical_and(lane >= N_GROUPS, lane < N_GROUPS + N_EXPERTS),
        ((lane - N_GROUPS) >> 3) == grp)
    el = jnp.where(in_grp, lg, neg)
    v1 = jnp.max(el, axis=1, keepdims=True)
    o1 = lane_f == _lane_first(el == v1, lane_f)
    el2 = jnp.where(o1, neg, el)
    v2 = jnp.max(el2, axis=1, keepdims=True)
    o2 = lane_f == _lane_first(el2 == v2, lane_f)

    osum = (jnp.where(o1, 1.0, 0.0) + jnp.where(o2, 1.0, 0.0)).astype(BF16)
    earlier = _dot(ltri[...], osum) + carry[0:1, :]
    carry[...] = carry[...] + _dot(jnp.ones((8, tm), BF16), osum)

    @pl.when(ph == 1)
    def _():
        ez = jnp.exp(v2 - v1)
        g1 = p_sel / (1.0 + ez)
        gates_ref[...] = jnp.concatenate([g1, g1 * ez], axis=1)
        base = earlier + pstart[0:1, :]
        for o, s_ref in ((o1, s0_ref), (o2, s1_ref)):
            col = jnp.sum(jnp.where(o, base, 0.0), axis=1, keepdims=True)
            wide = jnp.broadcast_to(col, (tm, LANES))
            for b in range(tm // LANES):
                s_ref[b:b + 1, :] = wide[b * LANES:(b + 1) * LANES, :].T[0:1, :].astype(jnp.int32)


def _router(logits):
    t = logits.shape[0]
    tm = ROUTER_TM
    rows = tm // LANES
    return pl.pallas_call(
        _router_kernel,
        grid=(2, t // tm),
        in_specs=[pl.BlockSpec((tm, LANES), lambda ph, i: (i, 0))],
        out_specs=[
            pl.BlockSpec((tm, EXPERT_TOPK), lambda ph, i: (i * ph, 0)),
            pl.BlockSpec((rows, LANES), lambda ph, i: (i * ph, 0)),
            pl.BlockSpec((rows, LANES), lambda ph, i: (i * ph, 0)),
            pl.BlockSpec((8, LANES), lambda ph, i: (0, 0)),
        ],
        out_shape=[
            jax.ShapeDtypeStruct((t, EXPERT_TOPK), F32),
            jax.ShapeDtypeStruct((t // LANES, LANES), jnp.int32),
            jax.ShapeDtypeStruct((t // LANES, LANES), jnp.int32),
            jax.ShapeDtypeStruct((8, LANES), jnp.int32),
        ],
        scratch_shapes=[
            pltpu.VMEM((tm, tm), BF16),
            pltpu.VMEM((8, LANES), F32),
            pltpu.VMEM((8, LANES), F32),
        ],
        compiler_params=_cparams(2, 32),
        name="router",
    )(logits)


def _round_up_pow2(x, m):
    assert m & (m - 1) == 0
    return (x + (m - 1)) & ~(m - 1)


def _plan_kernel(cnt_ref, s0_ref, s1_ref, tok_ref, ie_ref, ist_ref, ir_ref, if_ref, zeros, sem):
    n_tok = s0_ref.shape[0]
    n_slots = tok_ref.shape[0]
    n_items = ie_ref.shape[0]
    r = MOE_ITEM_ROWS

    zeros[...] = jnp.zeros_like(zeros)
    clear = pltpu.make_async_copy(zeros, tok_ref, sem)
    clear.start()
    clear.wait()

    def scatter(t16, carry):
        for d in range(16):
            t = t16 * 16 + d
            tok_ref[s0_ref[t]] = t
            tok_ref[s1_ref[t]] = t
        return carry
    lax.fori_loop(0, n_tok // 16, scatter, 0)

    a = jnp.int32(0)
    used = jnp.int32(0)
    e_last = jnp.int32(0)
    for e in range(N_EXPERTS):
        pad = _round_up_pow2(cnt_ref[0, N_GROUPS + e], MOE_PAD)

        def item(sub, a, e=e, pad=pad, used=used):
            ie_ref[a] = e
            ist_ref[a] = used + sub * r
            ir_ref[a] = jnp.minimum(pad - sub * r, r)
            if_ref[a] = 0
            return a + 1
        a = lax.fori_loop(0, _round_up_pow2(pad, r) >> (r.bit_length() - 1), item, a)
        e_last = jnp.where(pad > 0, e, e_last)
        used = used + pad

    def idle(a2, carry):
        fill_start = used + (a2 - a) * r
        fill = jnp.clip(n_slots - fill_start, 0, r)
        ie_ref[a2] = e_last
        ist_ref[a2] = jnp.where(fill > 0, fill_start, 0)
        ir_ref[a2] = 0
        if_ref[a2] = fill
        return carry
    lax.fori_loop(a, n_items, idle, 0)


def _plan(counts, s0, s1, n_slots, n_items):
    smem = pl.BlockSpec(memory_space=pltpu.SMEM)
    item = jax.ShapeDtypeStruct((n_items,), jnp.int32)
    return pl.pallas_call(
        _plan_kernel,
        in_specs=[smem, smem, smem],
        out_specs=[smem] * 5,
        out_shape=[jax.ShapeDtypeStruct((n_slots,), jnp.int32), item, item, item, item],
        scratch_shapes=[pltpu.VMEM((n_slots,), jnp.int32), pltpu.SemaphoreType.DMA(())],
        name="plan",
    )(counts, s0, s1)


def _unpack_pairs(u):
    lo = lax.bitcast_convert_type(u << 16, F32)
    hi = lax.bitcast_convert_type(u & jnp.uint32(0xFFFF0000), F32)
    return jnp.concatenate([lo, hi], axis=1)


def _store_row_tiles(dst, packed):
    r = packed.shape[0]
    for s in range(TILE_ROWS):
        dst[pl.ds(s, r, stride=TILE_ROWS), :] = packed[:, s * LANES:(s + 1) * LANES]


def _load_row_tiles(src, first_row, r):
    return jnp.concatenate(
        [src[pl.ds(first_row * TILE_ROWS + s, r, stride=TILE_ROWS), :] for s in range(TILE_ROWS)],
        axis=1)


def _pack_pairs(y):
    w = y.shape[1] // 2
    bits = lax.bitcast_convert_type(y.astype(BF16).astype(F32), jnp.uint32)
    return (bits[:, :w] >> 16) | (bits[:, w:] & jnp.uint32(0xFFFF0000))


def _moe_block_sizes():
    sizes, rows = [], MOE_BLOCK
    while rows >= MOE_PAD:
        sizes.append(rows)
        rows //= 2
    return sizes


def _moe_kernel(item_e, item_start, item_rows, item_fill, slot_tok,
                h2p_hbm, wg_ref, wu_ref, wd_ref, yb_hbm,
                xbuf, act, ystage, pf, gsem, osem):
    a = pl.program_id(0)
    t = pl.program_id(1)
    n_items = pl.num_programs(0)
    n = item_rows[a]
    start = item_start[a]
    half_w = wg_ref.shape[2]
    cur = a % 2
    nxt = 1 - cur
    a1 = jnp.minimum(a + 1, n_items - 1)
    n_next = jnp.where(a + 1 < n_items, item_rows[a1], 0)
    start_next = item_start[a1]

    def tile_rows(i, count=1):
        return pl.ds(pl.multiple_of(i * TILE_ROWS, TILE_ROWS), count * TILE_ROWS)

    def row_copy(tok, slot, i):
        return pltpu.make_async_copy(h2p_hbm.at[tile_rows(tok), :], xbuf.at[slot, tile_rows(i), :],
                                     gsem.at[slot])

    def start_row(tok, slot, i):
        row_copy(tok, slot, i).start(priority=1)

    @pl.when(jnp.logical_and(a == 0, t == 0))
    def _():
        pf[1] = 0
        pf[2] = 0

        def issue(i, carry):
            start_row(slot_tok[start + i], 0, i)
            return carry
        lax.fori_loop(0, n, issue, 0)

    @pl.when(t == 0)
    def _():
        pf[0] = 0

        def drain(bk, carry):
            r0 = pl.multiple_of(bk * MOE_PAD, MOE_PAD)
            pltpu.make_async_copy(h2p_hbm.at[tile_rows(0, MOE_PAD), :],
                                  xbuf.at[cur, tile_rows(r0, MOE_PAD), :], gsem.at[cur]).wait()
            return carry
        lax.fori_loop(0, n // MOE_PAD, drain, 0)

    def prefetch_group():
        base = pf[0]
        for q in range(MOE_GATHER_GROUP):
            start_row(slot_tok[start_next + base + q], nxt, base + q)
        pf[0] = base + MOE_GATHER_GROUP

    def run_blocks(compute):
        def one(b, r0, rows):
            r0 = pl.multiple_of(r0, MOE_PAD)
            more = pf[0] < n_next

            def gather():
                prefetch_group()
                for _ in range(max(rows // (2 * MOE_GATHER_GROUP), 1) - 1):
                    pl.when(pf[0] < n_next)(prefetch_group)

            @pl.when(more)
            def _():
                compute(b, r0, rows, gather)

            @pl.when(jnp.logical_not(more))
            def _():
                compute(b, r0, rows, lambda: None)

        n_big = n // MOE_BLOCK

        def body(b, carry):
            one(b, b * MOE_BLOCK, MOE_BLOCK)
            return carry
        lax.fori_loop(0, n_big, body, 0)

        b, r0, rows = n_big, n_big * MOE_BLOCK, MOE_BLOCK // 2
        while rows >= MOE_PAD:
            has = (n & rows) != 0
            pl.when(has)(functools.partial(one, b, r0, rows))
            b = b + has.astype(jnp.int32)
            r0 = r0 + jnp.where(has, rows, 0)
            rows //= 2

    def gate_up(half):
        def compute(b, r0, rows, mid):
            xs = _unpack_pairs(_load_row_tiles(xbuf.at[cur], r0, rows)).astype(BF16)
            g = _dot(xs, wg_ref[0].astype(BF16))
            u = _dot(xs, wu_ref[0].astype(BF16))
            mid()
            act[pl.ds(r0, rows), half * half_w:(half + 1) * half_w] = (
                g * _sigmoid(g) * u).astype(BF16)
        run_blocks(compute)

    def out_copy(slot, r0, rows):
        return pltpu.make_async_copy(
            ystage.at[slot, tile_rows(0, rows), :],
            yb_hbm.at[tile_rows(pl.multiple_of(start + r0, MOE_PAD), rows), :],
            osem.at[slot])

    def wait_stage(slot):
        for rows in _moe_block_sizes():
            @pl.when(pf[1 + slot] == rows)
            def _(rows=rows):
                out_copy(slot, 0, rows).wait()
                pf[1 + slot] = 0

    def down():
        def compute(b, r0, rows, mid):
            slot = b % 2
            wait_stage(slot)
            y = _dot(act[pl.ds(r0, rows), :], wd_ref[0].astype(BF16))
            mid()
            _store_row_tiles(ystage.at[slot], _pack_pairs(y))
            out_copy(slot, r0, rows).start()
            pf[1 + slot] = rows
        run_blocks(compute)

        def tail(i, carry):
            start_row(slot_tok[start_next + i], nxt, i)
            return carry
        lax.fori_loop(pf[0], n_next, tail, 0)

    pl.when(t == 0)(functools.partial(gate_up, 0))
    pl.when(t == 1)(functools.partial(gate_up, 1))
    pl.when(t == 2)(down)

    fill = item_fill[a]

    @pl.when(jnp.logical_and(t == 0, fill > 0))
    def _():
        wait_stage(0)
        ystage[0] = jnp.zeros(ystage.shape[1:], ystage.dtype)

        def zero_block(bk, carry):
            cp = out_copy(0, bk * MOE_PAD, MOE_PAD)
            cp.start()
            cp.wait()
            return carry
        lax.fori_loop(0, fill // MOE_PAD, zero_block, 0)

    @pl.when(jnp.logical_and(a == n_items - 1, t == pl.num_programs(1) - 1))
    def _():
        wait_stage(0)
        wait_stage(1)


def _moe_ffn(h2p, w_gate, w_up, w_down, item_e, item_start, item_rows, item_fill, slot_tok,
             n_slots):
    d = w_gate.shape[1]
    assert d == 2 * TILE_ROWS * LANES and h2p.shape[1] == LANES, "one packed token row per tile"
    n_items = item_e.shape[0]
    half_w = D_EXPERT // 2

    def wgu_map(a, t, ie, ist, ir, fl, st):
        return (ie[a], 0, jnp.where(ir[a] > 0, jnp.minimum(t, 1), 1))

    def wd_map(a, t, ie, ist, ir, fl, st):
        return (jnp.where(t >= 2, ie[a], ie[jnp.maximum(a - 1, 0)]), 0, 0)

    grid_spec = pltpu.PrefetchScalarGridSpec(
        num_scalar_prefetch=5,
        grid=(n_items, 3),
        in_specs=[
            pl.BlockSpec(memory_space=pl.ANY),
            pl.BlockSpec((1, d, half_w), wgu_map),
            pl.BlockSpec((1, d, half_w), wgu_map),
            pl.BlockSpec((1, D_EXPERT, d), wd_map),
        ],
        out_specs=pl.BlockSpec(memory_space=pl.ANY),
        scratch_shapes=[
            pltpu.VMEM((2, MOE_ITEM_ROWS * TILE_ROWS, LANES), jnp.uint32),
            pltpu.VMEM((MOE_ITEM_ROWS, D_EXPERT), BF16),
            pltpu.VMEM((2, MOE_BLOCK * TILE_ROWS, LANES), jnp.uint32),
            pltpu.SMEM((3,), jnp.int32),
            pltpu.SemaphoreType.DMA((2,)),
            pltpu.SemaphoreType.DMA((2,)),
        ],
    )
    return pl.pallas_call(
        _moe_kernel,
        grid_spec=grid_spec,
        out_shape=jax.ShapeDtypeStruct((n_slots * TILE_ROWS, LANES), jnp.uint32),
        compiler_params=_cparams(2, 56),
        name="moe",
    )(item_e, item_start, item_rows, item_fill, slot_tok, h2p, w_gate, w_up, w_down)


def _combine_kernel(s0, s1, yb_hbm, gates_ref, x1_ref, nw_ref, g2_ref, out_ref, rows, sem):
    i = pl.program_id(0)
    n_tiles = pl.num_programs(0)
    tm = x1_ref.shape[0]

    def tile_rows(i, count=1):
        return pl.ds(pl.multiple_of(i * TILE_ROWS, TILE_ROWS), count * TILE_ROWS)

    def gather(tile, buf):
        def issue(r8, carry):
            t0 = tile * tm + r8 * 8
            for dr in range(8):
                for k, s_k in enumerate((s0, s1)):
                    pltpu.make_async_copy(yb_hbm.at[tile_rows(s_k[t0 + dr]), :],
                                          rows.at[buf, k, tile_rows(r8 * 8 + dr), :],
                                          sem.at[buf]).start()
            return carry
        lax.fori_loop(0, tm // 8, issue, 0)

    @pl.when(i == 0)
    def _():
        gather(0, 0)

    @pl.when(i + 1 < n_tiles)
    def _():
        gather(i + 1, (i + 1) % 2)

    buf = i % 2
    for k in range(EXPERT_TOPK):
        pltpu.make_async_copy(yb_hbm.at[tile_rows(0, tm), :], rows.at[buf, k], sem.at[buf]).wait()

    g = gates_ref[...]
    y = (_unpack_pairs(_load_row_tiles(rows.at[buf, 0], 0, tm)) * g[:, 0:1]
         + _unpack_pairs(_load_row_tiles(rows.at[buf, 1], 0, tm)) * g[:, 1:2])
    out_ref[...] = x1_ref[...] + g2_ref[0:1, :] * (_rms(y) * nw_ref[3:4, :])


def _combine(s0, s1, yb, gates, x1, norm_w, mod):
    n, d = x1.shape
    tm = 256
    grid_spec = pltpu.PrefetchScalarGridSpec(
        num_scalar_prefetch=2,
        grid=(n // tm,),
        in_specs=[
            pl.BlockSpec(memory_space=pl.ANY),
            pl.BlockSpec((tm, EXPERT_TOPK), lambda i, *_: (i, 0)),
            pl.BlockSpec((tm, d), lambda i, *_: (i, 0)),
            pl.BlockSpec((4, d), lambda i, *_: (0, 0)),
            pl.BlockSpec((8, d), lambda i, *_: (0, 5)),
        ],
        out_specs=pl.BlockSpec((tm, d), lambda i, *_: (i, 0)),
        scratch_shapes=[
            pltpu.VMEM((2, EXPERT_TOPK, tm * TILE_ROWS, LANES), jnp.uint32),
            pltpu.SemaphoreType.DMA((2,)),
        ],
    )
    return pl.pallas_call(
        _combine_kernel,
        grid_spec=grid_spec,
        out_shape=jax.ShapeDtypeStruct((n, d), F32),
        compiler_params=_cparams(1, 40),
        name="combine",
    )(s0, s1, yb, gates, x1, norm_w, mod)


def _rope_tables(n):
    rows = n // GRID_W
    r = np.repeat(np.arange(rows, dtype=np.float32), GRID_W)
    cidx = np.tile(np.arange(GRID_W, dtype=np.float32), rows)
    n_freq = RET_DK // 4
    inv = np.float32(ROPE_BASE) ** (-np.arange(n_freq, dtype=np.float32) / np.float32(n_freq))
    ang = np.concatenate([r[:, None] * inv, cidx[:, None] * inv], axis=-1).astype(np.float32)
    cos, sin = np.cos(ang), np.sin(ang)
    cs = np.concatenate([cos, cos], axis=-1)
    sn = np.concatenate([-sin, sin], axis=-1)
    return jnp.asarray(cs, F32), jnp.asarray(sn, F32)


def kernel(x, c, ctx, c_ctx, w_mod, b_mod, norm_w, w_in, hg_lb_logits, hg_norm_w, ret_decay_logits,
           ret_norm_w, w_out, router_group_w, router_group_b, router_expert_w, router_expert_b,
           w_gate, w_up, w_down):
    bsz, n, d = x.shape
    l = ctx.shape[1]
    assert bsz == 1 and w_mod.shape[0] == 1, "single sample, single layer"
    x2d, ctx2d = x[0], ctx[0]

    cc = jnp.zeros((8, d), F32).at[0].set(c[0]).at[1].set(c_ctx)
    mod = _modulation(cc, w_mod[0], b_mod[0])

    lb = jnp.cumsum(jax.nn.softmax(hg_lb_logits.astype(F32), axis=0), axis=0)[0]
    rk0 = 5 * HG_WIDTH + RET_HEADS * RET_DK
    cols = jnp.stack([
        jnp.zeros((PROJ_WIDTH,), F32).at[HG_WIDTH:3 * HG_WIDTH].set(lb.reshape(-1)),
        jnp.ones((PROJ_WIDTH,), F32).at[rk0:rk0 + RET_HEADS * RET_DK].set(RET_DK ** -0.5),
    ])
    pc, gc, w_in_bf = _projection(ctx2d, norm_w[0], mod, w_in[0], cols, 1, l, "proj_ctx")
    p, g = _projection(x2d, norm_w[0], mod, w_in_bf, cols, 0, 1024, "proj_x")

    ret_lg = jax.nn.log_sigmoid(ret_decay_logits[0].astype(F32))
    cs_tab, sn_tab = _rope_tables(n)

    o_f, o_b = _hgrn_scan(p, g, pc, gc)
    r_f, r_b = _ret_scan(p, pc, ret_lg, cs_tab, sn_tab)

    wr = jnp.concatenate([router_group_w[0], router_expert_w[0]], axis=1)
    wr = jnp.pad(wr, ((0, 0), (0, ROUTER_LANES - wr.shape[1])))
    wr_hi = wr.astype(BF16)
    wr_parts = jnp.concatenate([wr_hi, (wr - wr_hi.astype(F32)).astype(BF16)], axis=1)
    br = jnp.concatenate([router_group_b[0], router_expert_b[0]]).astype(F32)
    br = jnp.pad(br, (0, ROUTER_LANES - br.shape[0])).reshape(1, ROUTER_LANES)
    x1, h2, logits = _out_projection(
        o_f, o_b, r_f, r_b, p, x2d, w_out[0].astype(BF16), hg_norm_w, ret_norm_w, norm_w[0], mod,
        wr_parts, br)

    n_assign = n * EXPERT_TOPK
    n_slots = n_assign + N_EXPERTS * MOE_PAD
    n_items = (n_slots + N_EXPERTS * (MOE_ITEM_ROWS - MOE_PAD)) // MOE_ITEM_ROWS
    gates, s0, s1, counts = _router(logits)
    s0, s1 = s0.reshape(-1), s1.reshape(-1)
    slot_tok, item_e, item_start, item_rows, item_fill = _plan(counts, s0, s1, n_slots, n_items)

    yb = _moe_ffn(h2, w_gate[0], w_up[0], w_down[0], item_e, item_start, item_rows, item_fill,
                  slot_tok, n_slots)
    out = _combine(s0, s1, yb, gates, x1, norm_w[0], mod)
    return out[None]
```

```python
import functools

import jax
import jax.numpy as jnp
import numpy as np
from jax import lax
from jax.experimental import pallas as pl
from jax.experimental.pallas import tpu as pltpu

F32 = jnp.float32
BF16 = jnp.bfloat16

D_MODEL = 2048
GRID_W = 64
HG_HEADS = 8
HG_DK = 128
HG_WIDTH = 1024
RET_HEADS = 4
RET_DK = 128
RET_DV = 256
RET_WIDTH = 1024
PROJ_WIDTH = 8192
N_GROUPS = 4
EXPERTS_PER_GROUP = 8
N_EXPERTS = 32
EXPERT_TOPK = 2
D_EXPERT = 1024
RMS_EPS = 1e-6
ROPE_BASE = 10000.0

COL_HQ, COL_HF_FWD, COL_HF_BWD, COL_HI, COL_HGATE = 0, 1, 2, 3, 4
COL_RQ_512, COL_RK_512 = 10, 11
COL_RV, COL_RGATE = 6, 7

HG_CHUNK = 256
HG_STEP_HEADS = 8
HG_DIAG = 16
HG_LOG_CLAMP = 10.0
RET_CHUNK = 256

MOE_PAD = 128
MOE_BLOCK = 256
MOE_ITEM_ROWS = 1024
MOE_GATHER_GROUP = 128
ROUTER_LANES = 128
LANES = 128
TILE_ROWS = 8


def _cparams(n_axes, vmem_mb, **flags):
    return pltpu.CompilerParams(
        dimension_semantics=("arbitrary",) * n_axes,
        vmem_limit_bytes=vmem_mb * 1024 * 1024,
        flags=flags or None,
    )


def _sigmoid(x):
    return 1.0 / (1.0 + jnp.exp(-x))


def _dot(a, b):
    return jnp.dot(a, b, preferred_element_type=F32)


def _dot_nt(a, b):
    return lax.dot_general(a, b, (((1,), (1,)), ((), ())), preferred_element_type=F32)


def _dot_tn(a, b):
    return lax.dot_general(a, b, (((0,), (0,)), ((), ())), preferred_element_type=F32)


def _mod_kernel(cc_ref, w_ref, b_ref, o_ref):
    s = cc_ref[...]
    s = s * _sigmoid(s)
    s16 = jnp.concatenate([s, s], axis=0)
    hi = s16.astype(BF16).astype(F32)
    row = lax.broadcasted_iota(jnp.int32, s16.shape, 0)
    lhs = jnp.where(row < 8, hi, s16 - hi).astype(BF16)
    r = _dot(lhs, w_ref[...].astype(BF16))
    o_ref[...] = r[:8] + r[8:] + b_ref[...]


def _modulation(cc, w_mod, b_mod):
    d, n = w_mod.shape
    tn = 1024
    return pl.pallas_call(
        _mod_kernel,
        grid=(n // tn,),
        in_specs=[
            pl.BlockSpec((8, d), lambda j: (0, 0)),
            pl.BlockSpec((d, tn), lambda j: (0, j)),
            pl.BlockSpec((1, tn), lambda j: (0, j)),
        ],
        out_specs=pl.BlockSpec((8, tn), lambda j: (0, j)),
        out_shape=jax.ShapeDtypeStruct((8, n), F32),
        compiler_params=_cparams(1, 40),
        name="mod",
    )(cc, w_mod, b_mod.reshape(1, n))


PROJ_TN = 1024
PROLOGUE_ROWS = 32


def _proj_kernel(x_ref, nw_ref, sh_ref, sc_ref, w_ref, cols_ref, o_ref, g_ref, *rest, mod_row):
    h_scr = rest[-1]
    if len(rest) == 2:
        rest[0][...] = w_ref[...].astype(BF16)
        w_ref = rest[0]
    i = pl.program_id(0)
    j = pl.program_id(1)
    last_j = pl.num_programs(1) - 1
    cur = i % 2

    def normed(x):
        scale = nw_ref[0:1, :] * (1.0 + sc_ref[mod_row:mod_row + 1, :])
        shift = sh_ref[mod_row:mod_row + 1, :]
        y = x * lax.rsqrt(jnp.mean(x * x, axis=-1, keepdims=True) + RMS_EPS)
        return (y * scale + shift).astype(BF16)

    @pl.when(jnp.logical_and(i == 0, j == 0))
    def _():
        def body(r, carry):
            r0 = pl.multiple_of(r * PROLOGUE_ROWS, PROLOGUE_ROWS)
            h_scr[0, pl.ds(r0, PROLOGUE_ROWS), :] = normed(x_ref[pl.ds(r0, PROLOGUE_ROWS), :])
            return carry
        lax.fori_loop(0, x_ref.shape[0] // PROLOGUE_ROWS, body, 0)

    is_gate = jnp.logical_or(j == 1, j == 2)
    is_silu = jnp.logical_or(j == 0, j == 4)
    is_lin = jnp.logical_not(jnp.logical_or(jnp.logical_or(is_gate, is_silu), j == last_j))

    @pl.when(is_lin)
    def _():
        o_ref[...] = (_dot(h_scr[cur], w_ref[...]) * cols_ref[1:2, :]).astype(BF16)

    @pl.when(is_silu)
    def _():
        acc = _dot(h_scr[cur], w_ref[...])
        o_ref[...] = (acc * _sigmoid(acc)).astype(BF16)

    @pl.when(is_gate)
    def _():
        acc = _dot(h_scr[cur], w_ref[...])
        lb = cols_ref[0:1, :]
        g = jnp.maximum(jnp.log(lb + (1.0 - lb) * _sigmoid(acc)), -HG_LOG_CLAMP)
        g_ref[...] = g
        o_ref[...] = g.astype(BF16)

    @pl.when(j == last_j)
    def _():
        acc = _dot(h_scr[cur], w_ref[...])
        o_ref[...] = (acc * _sigmoid(acc)).astype(BF16)
        h_scr[1 - cur] = normed(x_ref[...])


def _projection(x2d, norm_w, mod, w_in, cols, mod_row, tm, name):
    n, d = x2d.shape
    tn = PROJ_TN
    n_i, n_j = n // tm, PROJ_WIDTH // tn
    emit_w = w_in.dtype != BF16
    assert not emit_w or n_i == 1, "the weight copy is written once per column tile"

    def x_map(i, j):
        return (jnp.minimum(i + (j == n_j - 1).astype(jnp.int32), n_i - 1), 0)

    return pl.pallas_call(
        functools.partial(_proj_kernel, mod_row=mod_row),
        grid=(n_i, n_j),
        in_specs=[
            pl.BlockSpec((tm, d), x_map),
            pl.BlockSpec((4, d), lambda i, j: (0, 0)),
            pl.BlockSpec((8, d), lambda i, j: (0, 0)),
            pl.BlockSpec((8, d), lambda i, j: (0, 1)),
            pl.BlockSpec((d, tn), lambda i, j: (0, j)),
            pl.BlockSpec((2, tn), lambda i, j: (0, j)),
        ],
        out_specs=[
            pl.BlockSpec((tm, tn), lambda i, j: (i, j)),
            pl.BlockSpec((tm, tn), lambda i, j: (i, jnp.clip(j - 1, 0, 1))),
            *([pl.BlockSpec((d, tn), lambda i, j: (0, j))] if emit_w else []),
        ],
        out_shape=[
            jax.ShapeDtypeStruct((n, PROJ_WIDTH), BF16),
            jax.ShapeDtypeStruct((n, 2 * HG_WIDTH), F32),
            *([jax.ShapeDtypeStruct(w_in.shape, BF16)] if emit_w else []),
        ],
        scratch_shapes=[pltpu.VMEM((2, tm, d), BF16)],
        compiler_params=_cparams(2, 56),
        name=name,
    )(x2d, norm_w, mod, mod, w_in, cols)


def _tri_ones(c, rev):
    t = lax.broadcasted_iota(jnp.int32, (c, c), 0)
    s = lax.broadcasted_iota(jnp.int32, (c, c), 1)
    return jnp.where((s >= t) if rev else (s <= t), 1.0, 0.0).astype(BF16)


def _cumsum_rows(g, tri):
    hi = g.astype(BF16)
    lo = (g - hi.astype(F32)).astype(BF16)
    return _dot(tri, hi) + _dot(tri, lo)


def _row_refs(b, idxs, blk):
    parts = [jnp.broadcast_to(b[i:i + 1, :], (blk, b.shape[1])) for i in idxs]
    return parts[0] if len(parts) == 1 else jnp.concatenate(parts, axis=0)


def _hgrn_levels(c, rev):
    t = lax.broadcasted_iota(jnp.int32, (c, c), 0)
    s = lax.broadcasted_iota(jnp.int32, (c, c), 1)
    levels = []
    m = c // 2
    while m >= HG_DIAG:
        blk = 2 * m
        same = ((t ^ s) & ~(blk - 1)) == 0
        if rev:
            idxs = [b0 + m for b0 in range(0, c, blk)]
            cross = jnp.logical_and((t & m) == 0, (s & m) != 0)
        else:
            idxs = [b0 + m - 1 for b0 in range(0, c, blk)]
            cross = jnp.logical_and((t & m) != 0, (s & m) == 0)
        levels.append((idxs, blk, jnp.logical_and(same, cross)))
        m //= 2
    blk = HG_DIAG
    same = ((t ^ s) & ~(blk - 1)) == 0
    if rev:
        idxs = [b0 + blk // 2 for b0 in range(0, c, blk)]
        mask = jnp.logical_and(same, s >= t)
    else:
        idxs = [b0 + blk // 2 - 1 for b0 in range(0, c, blk)]
        mask = jnp.logical_and(same, s <= t)
    levels.append((idxs, blk, mask))
    return levels


def _hgrn_head(q_bf, g, b, v_bf, st, rev, levels):
    c = g.shape[0]
    kk = 1.0 - jnp.exp(g)

    o = None
    if levels is not None:
        q = q_bf.astype(F32)
        att = jnp.zeros((c, c), F32)
        for idxs, blk, mask in levels:
            ref = _row_refs(b, idxs, blk)
            a_l = _dot_nt((q * jnp.exp(b - ref)).astype(BF16),
                          (kk * jnp.exp(ref - b)).astype(BF16))
            att = att + jnp.where(mask, a_l, 0.0)
        o = (_dot(att.astype(BF16), v_bf)
             + _dot_nt((q * jnp.exp(b)).astype(BF16), st.astype(BF16))).astype(BF16)

    b_end = b[0:1, :] if rev else b[c - 1:c, :]
    k_end = (kk * jnp.exp(b_end - b)).astype(BF16)
    st_new = st * jnp.exp(b_end) + _dot_tn(v_bf, k_end)
    return o, st_new


def _hgrn_ctx_kernel(gf_ref, vf_ref, gb_ref, vb_ref, st_ref):
    @pl.when(pl.program_id(0) == 0)
    def _():
        st_ref[...] = jnp.zeros_like(st_ref)

    c = gf_ref.shape[0]
    bf = _cumsum_rows(gf_ref[...], _tri_ones(c, False))
    bb = _cumsum_rows(gb_ref[...], _tri_ones(c, True))
    for h in range(HG_HEADS):
        sl = slice(h * HG_DK, (h + 1) * HG_DK)
        _, st_ref[h] = _hgrn_head(None, gf_ref[:, sl], bf[:, sl], vf_ref[:, sl], st_ref[h],
                                  False, None)
        hb = HG_HEADS + h
        _, st_ref[hb] = _hgrn_head(None, gb_ref[:, sl], bb[:, sl], vb_ref[:, sl], st_ref[hb],
                                   True, None)


def _hgrn_kernel(qf_ref, gf_ref, vf_ref, qb_ref, gb_ref, vb_ref, st0_ref, of_ref, ob_ref, st_ref):
    hg = pl.program_id(1)

    @pl.when(jnp.logical_and(pl.program_id(0) == 0, hg == 0))
    def _():
        st_ref[...] = st0_ref[...]

    c = gf_ref.shape[0]
    lev_f = _hgrn_levels(c, False)
    lev_b = _hgrn_levels(c, True)
    bf = _cumsum_rows(gf_ref[...], _tri_ones(c, False))
    bb = _cumsum_rows(gb_ref[...], _tri_ones(c, True))
    for h in range(HG_STEP_HEADS):
        sl = slice(h * HG_DK, (h + 1) * HG_DK)
        hf = hg * HG_STEP_HEADS + h
        of_ref[:, sl], st_ref[hf] = _hgrn_head(qf_ref[:, sl], gf_ref[:, sl], bf[:, sl],
                                               vf_ref[:, sl], st_ref[hf], False, lev_f)
        hb = HG_HEADS + hf
        ob_ref[:, sl], st_ref[hb] = _hgrn_head(qb_ref[:, sl], gb_ref[:, sl], bb[:, sl],
                                               vb_ref[:, sl], st_ref[hb], True, lev_b)


def _hgrn_scan(p, g, pc, gc):
    c = HG_CHUNK
    w = HG_WIDTH
    n, l = p.shape[0], pc.shape[0]
    c_ctx = min(c, l)
    n_x, n_ctx = n // c, l // c_ctx
    st_shape = (2 * HG_HEADS, HG_DK, HG_DK)

    def spec(chunk_fn, col):
        return pl.BlockSpec((c_ctx, w), lambda s: (chunk_fn(s), col))

    st0 = pl.pallas_call(
        _hgrn_ctx_kernel,
        grid=(n_ctx,),
        in_specs=[
            spec(lambda s: s, 0), spec(lambda s: s, COL_HI),
            spec(lambda s: n_ctx - 1 - s, 1), spec(lambda s: n_ctx - 1 - s, COL_HI),
        ],
        out_specs=pl.BlockSpec(st_shape, lambda s: (0, 0, 0)),
        out_shape=jax.ShapeDtypeStruct(st_shape, F32),
        compiler_params=_cparams(1, 32),
        name="hgrn_ctx",
    )(gc, pc, gc, pc)

    n_hg = HG_HEADS // HG_STEP_HEADS
    wg = HG_STEP_HEADS * HG_DK
    fwd = lambda s: s
    bwd = lambda s: n_x - 1 - s

    def gspec(chunk_fn, col):
        return pl.BlockSpec((c, wg), lambda s, hg: (chunk_fn(s), col * n_hg + hg))

    return pl.pallas_call(
        _hgrn_kernel,
        grid=(n_x, n_hg),
        in_specs=[
            gspec(fwd, COL_HQ), gspec(fwd, 0), gspec(fwd, COL_HI),
            gspec(bwd, COL_HQ), gspec(bwd, 1), gspec(bwd, COL_HI),
            pl.BlockSpec(st_shape, lambda s, hg: (0, 0, 0)),
        ],
        out_specs=[gspec(fwd, 0), gspec(bwd, 0)],
        out_shape=[jax.ShapeDtypeStruct((n, w), BF16)] * 2,
        scratch_shapes=[pltpu.VMEM(st_shape, F32)],
        compiler_params=_cparams(2, 32),
        name="hgrn",
    )(p, g, p, p, g, p, st0)


def _rope(x_bf, cs, sn):
    x = x_bf.astype(F32)
    return x * cs + pltpu.roll(x, RET_DK // 2, axis=1) * sn


def _ret_decays(lg, c, rev):
    t = lax.broadcasted_iota(jnp.int32, (c, c), 0)
    s = lax.broadcasted_iota(jnp.int32, (c, c), 1)
    diff = (s - t) if rev else (t - s)
    pair = jnp.where(diff >= 0, jnp.exp(lg * jnp.maximum(diff, 0).astype(F32)), 0.0)
    pos = lax.broadcasted_iota(jnp.int32, (c, RET_DK), 0).astype(F32)
    q_dec = jnp.exp(lg * ((c - pos) if rev else (pos + 1.0)))
    k_dec = jnp.exp(lg * (pos if rev else (c - 1.0 - pos)))
    chunk = jnp.exp(jnp.full((8, RET_DK), lg * c, F32))
    return pair, q_dec, k_dec, chunk


def _ret_head(q, k, v_bf, decays, st):
    pair, q_dec, k_dec, chunk = decays
    o = None
    if q is not None:
        att = _dot_nt(q.astype(BF16), k.astype(BF16)) * pair
        o = (_dot(att.astype(BF16), v_bf)
             + _dot_nt((q * q_dec).astype(BF16), st.astype(BF16))).astype(BF16)
    st_new = st * chunk[0:1, :] + _dot_tn(v_bf, (k * k_dec).astype(BF16))
    return o, st_new


def _ret_ctx_kernel(lg_ref, kf_ref, vf_ref, kb_ref, vb_ref, st_ref):
    @pl.when(pl.program_id(0) == 0)
    def _():
        st_ref[...] = jnp.zeros_like(st_ref)

    c = kf_ref.shape[0]
    for h in range(RET_HEADS):
        sk = slice(h * RET_DK, (h + 1) * RET_DK)
        sv = slice(h * RET_DV, (h + 1) * RET_DV)
        _, st_ref[h] = _ret_head(None, kf_ref[:, sk].astype(F32), vf_ref[:, sv],
                                 _ret_decays(lg_ref[0, h], c, False), st_ref[h])
        hb = RET_HEADS + h
        _, st_ref[hb] = _ret_head(None, kb_ref[:, sk].astype(F32), vb_ref[:, sv],
                                  _ret_decays(lg_ref[1, h], c, True), st_ref[hb])


def _ret_kernel(lg_ref, qf_ref, kf_ref, vf_ref, csf_ref, snf_ref,
                qb_ref, kb_ref, vb_ref, csb_ref, snb_ref, st0_ref, of_ref, ob_ref,
                st_ref, pair_ref, qd_ref, kd_ref, cd_ref):
    c = qf_ref.shape[0]

    @pl.when(pl.program_id(0) == 0)
    def _():
        st_ref[...] = st0_ref[...]
        for d in range(2):
            for h in range(RET_HEADS):
                i = d * RET_HEADS + h
                pair_ref[i], qd_ref[i], kd_ref[i], cd_ref[i] = _ret_decays(lg_ref[d, h], c, d == 1)

    csf, snf = csf_ref[...], snf_ref[...]
    csb, snb = csb_ref[...], snb_ref[...]
    for h in range(RET_HEADS):
        sk = slice(h * RET_DK, (h + 1) * RET_DK)
        sv = slice(h * RET_DV, (h + 1) * RET_DV)
        of_ref[:, sv], st_ref[h] = _ret_head(
            _rope(qf_ref[:, sk], csf, snf), _rope(kf_ref[:, sk], csf, snf), vf_ref[:, sv],
            (pair_ref[h], qd_ref[h], kd_ref[h], cd_ref[h]), st_ref[h])
        hb = RET_HEADS + h
        ob_ref[:, sv], st_ref[hb] = _ret_head(
            _rope(qb_ref[:, sk], csb, snb), _rope(kb_ref[:, sk], csb, snb), vb_ref[:, sv],
            (pair_ref[hb], qd_ref[hb], kd_ref[hb], cd_ref[hb]), st_ref[hb])


def _ret_scan(p, pc, ret_lg, cs_tab, sn_tab):
    c = RET_CHUNK
    n, l = p.shape[0], pc.shape[0]
    n_x, n_ctx = n // c, l // c
    qk_w = RET_HEADS * RET_DK
    st_shape = (2 * RET_HEADS, RET_DV, RET_DK)
    smem = pl.BlockSpec(memory_space=pltpu.SMEM)

    def kv_specs(chunk_fn):
        return [
            pl.BlockSpec((c, qk_w), lambda s: (chunk_fn(s), COL_RK_512)),
            pl.BlockSpec((c, RET_WIDTH), lambda s: (chunk_fn(s), COL_RV)),
        ]

    st0 = pl.pallas_call(
        _ret_ctx_kernel,
        grid=(n_ctx,),
        in_specs=[smem, *kv_specs(lambda s: s), *kv_specs(lambda s: n_ctx - 1 - s)],
        out_specs=pl.BlockSpec(st_shape, lambda s: (0, 0, 0)),
        out_shape=jax.ShapeDtypeStruct(st_shape, F32),
        compiler_params=_cparams(1, 32),
        name="ret_ctx",
    )(ret_lg, pc, pc, pc, pc)

    def specs(chunk_fn):
        return [
            pl.BlockSpec((c, qk_w), lambda s: (chunk_fn(s), COL_RQ_512)),
            *kv_specs(chunk_fn),
            pl.BlockSpec((c, RET_DK), lambda s: (chunk_fn(s), 0)),
            pl.BlockSpec((c, RET_DK), lambda s: (chunk_fn(s), 0)),
        ]

    fwd = lambda s: s
    bwd = lambda s: n_x - 1 - s
    return pl.pallas_call(
        _ret_kernel,
        grid=(n_x,),
        in_specs=[smem, *specs(fwd), *specs(bwd), pl.BlockSpec(st_shape, lambda s: (0, 0, 0))],
        out_specs=[
            pl.BlockSpec((c, RET_WIDTH), lambda s: (fwd(s), 0)),
            pl.BlockSpec((c, RET_WIDTH), lambda s: (bwd(s), 0)),
        ],
        out_shape=[jax.ShapeDtypeStruct((n, RET_WIDTH), BF16)] * 2,
        scratch_shapes=[
            pltpu.VMEM(st_shape, F32),
            pltpu.VMEM((2 * RET_HEADS, c, c), F32),
            pltpu.VMEM((2 * RET_HEADS, c, RET_DK), F32),
            pltpu.VMEM((2 * RET_HEADS, c, RET_DK), F32),
            pltpu.VMEM((2 * RET_HEADS, 8, RET_DK), F32),
        ],
        compiler_params=_cparams(1, 32),
        name="ret",
    )(ret_lg, p, p, p, cs_tab, sn_tab, p, p, p, cs_tab, sn_tab, st0)


def _rms(y):
    return y * lax.rsqrt(jnp.mean(y * y, axis=-1, keepdims=True) + RMS_EPS)


def _outproj_kernel(of_ref, ob_ref, rf_ref, rb_ref, hgate_ref, rgate_ref, x_ref, wout_ref,
                    hgw_ref, rnw_ref, nw_ref, g1_ref, sh2_ref, sc2_ref, wr_ref, br_ref,
                    x1_ref, h2_ref, lg_ref, cat_scr):
    for h in range(HG_HEADS):
        sl = slice(h * HG_DK, (h + 1) * HG_DK)
        o = of_ref[:, sl].astype(F32) + ob_ref[:, sl].astype(F32)
        cat_scr[:, sl] = (_rms(o) * hgw_ref[0:1, sl] * hgate_ref[:, sl].astype(F32)).astype(BF16)
    for h in range(RET_HEADS):
        sl = slice(h * RET_DV, (h + 1) * RET_DV)
        r = rf_ref[:, sl].astype(F32) + rb_ref[:, sl].astype(F32)
        cat_scr[:, HG_WIDTH + h * RET_DV:HG_WIDTH + (h + 1) * RET_DV] = (
            _rms(r) * rnw_ref[0:1, sl] * rgate_ref[:, sl].astype(F32)).astype(BF16)

    y = _dot(cat_scr[...], wout_ref[...])
    x1 = x_ref[...] + g1_ref[0:1, :] * (_rms(y) * nw_ref[1:2, :])
    x1_ref[...] = x1
    h2 = _rms(x1) * nw_ref[2:3, :] * (1.0 + sc2_ref[0:1, :]) + sh2_ref[0:1, :]
    _store_row_tiles(h2_ref, _pack_pairs(h2))
    hi = h2.astype(BF16)
    lo = (h2 - hi.astype(F32)).astype(BF16)
    parts = _dot(jnp.concatenate([hi, lo], axis=0), wr_ref[...])
    half = hi.shape[0]
    lg_ref[...] = (parts[:half, :ROUTER_LANES] + parts[half:, :ROUTER_LANES]
                   + parts[:half, ROUTER_LANES:] + br_ref[...])


def _out_projection(o_f, o_b, r_f, r_b, p, x2d, w_out_bf, hg_norm_w, ret_norm_w, norm_w, mod,
                    wr_parts, br):
    n, d = x2d.shape
    tm = 256
    row = lambda w: pl.BlockSpec((tm, w), lambda i: (i, 0))
    full = lambda a: pl.BlockSpec(a.shape, lambda i: (0,) * a.ndim)
    modcol = lambda k: pl.BlockSpec((8, d), lambda i: (0, k))
    return pl.pallas_call(
        _outproj_kernel,
        grid=(n // tm,),
        in_specs=[
            row(HG_WIDTH), row(HG_WIDTH), row(RET_WIDTH), row(RET_WIDTH),
            pl.BlockSpec((tm, HG_WIDTH), lambda i: (i, COL_HGATE)),
            pl.BlockSpec((tm, RET_WIDTH), lambda i: (i, COL_RGATE)),
            row(d), full(w_out_bf), full(hg_norm_w), full(ret_norm_w), full(norm_w),
            modcol(2), modcol(3), modcol(4),
            full(wr_parts), full(br),
        ],
        out_specs=[row(d), pl.BlockSpec((tm * TILE_ROWS, LANES), lambda i: (i, 0)),
                   row(ROUTER_LANES)],
        out_shape=[
            jax.ShapeDtypeStruct((n, d), F32),
            jax.ShapeDtypeStruct((n * TILE_ROWS, LANES), jnp.uint32),
            jax.ShapeDtypeStruct((n, ROUTER_LANES), F32),
        ],
        scratch_shapes=[pltpu.VMEM((tm, d), BF16)],
        compiler_params=_cparams(1, 48),
        name="outproj",
    )(o_f, o_b, r_f, r_b, p, p, x2d, w_out_bf, hg_norm_w, ret_norm_w, norm_w, mod, mod, mod,
      wr_parts, br)


ROUTER_TM = 1024


def _lane_first(hit, lane_f):
    return jnp.min(jnp.where(hit, lane_f, 1e9), axis=1, keepdims=True)


def _router_kernel(lg_ref, gates_ref, s0_ref, s1_ref, cnt_ref, ltri, carry, pstart):
    ph = pl.program_id(0)
    i = pl.program_id(1)
    tm = lg_ref.shape[0]

    @pl.when(jnp.logical_and(ph == 0, i == 0))
    def _():
        t = lax.broadcasted_iota(jnp.int32, (tm, tm), 0)
        s = lax.broadcasted_iota(jnp.int32, (tm, tm), 1)
        ltri[...] = jnp.where(s < t, 1.0, 0.0).astype(BF16)

    @pl.when(jnp.logical_and(ph == 1, i == 0))
    def _():
        cnt = carry[...]
        padded = jnp.floor((cnt + (MOE_PAD - 1.0)) * (1.0 / MOE_PAD)) * MOE_PAD
        r = lax.broadcasted_iota(jnp.int32, (LANES, LANES), 0)
        c = lax.broadcasted_iota(jnp.int32, (LANES, LANES), 1)
        before = jnp.where(r < c, 1.0, 0.0).astype(BF16)
        pstart[...] = _dot(padded.astype(BF16), before)
        cnt_ref[...] = cnt.astype(jnp.int32)

    @pl.when(i == 0)
    def _():
        carry[...] = jnp.zeros_like(carry)

    lg = lg_ref[...]
    lane = lax.broadcasted_iota(jnp.int32, lg.shape, 1)
    lane_f = lane.astype(F32)
    neg = -jnp.inf

    gl = jnp.where(lane < N_GROUPS, lg, neg)
    gmax = jnp.max(gl, axis=1, keepdims=True)
    grp = _lane_first(gl == gmax, lane_f).astype(jnp.int32)
    p_sel = 1.0 / jnp.sum(jnp.exp(gl - gmax), axis=1, keepdims=True)

    in_grp = jnp.logical_and(
        jnp.logical_and(lane >= N_GROUPS, lane < N_GROUPS + N_EXPERTS),
        ((lane - N_GROUPS) >> 3) == grp)
    el = jnp.where(in_grp, lg, neg)
    v1 = jnp.max(el, axis=1, keepdims=True)
    o1 = lane_f == _lane_first(el == v1, lane_f)
    el2 = jnp.where(o1, neg, el)
    v2 = jnp.max(el2, axis=1, keepdims=True)
    o2 = lane_f == _lane_first(el2 == v2, lane_f)

    osum = (jnp.where(o1, 1.0, 0.0) + jnp.where(o2, 1.0, 0.0)).astype(BF16)
    earlier = _dot(ltri[...], osum) + carry[0:1, :]
    carry[...] = carry[...] + _dot(jnp.ones((8, tm), BF16), osum)

    @pl.when(ph == 1)
    def _():
        ez = jnp.exp(v2 - v1)
        g1 = p_sel / (1.0 + ez)
        gates_ref[...] = jnp.concatenate([g1, g1 * ez], axis=1)
        base = earlier + pstart[0:1, :]
        for o, s_ref in ((o1, s0_ref), (o2, s1_ref)):
            col = jnp.sum(jnp.where(o, base, 0.0), axis=1, keepdims=True)
            wide = jnp.broadcast_to(col, (tm, LANES))
            for b in range(tm // LANES):
                s_ref[b:b + 1, :] = wide[b * LANES:(b + 1) * LANES, :].T[0:1, :].astype(jnp.int32)


def _router(logits):
    t = logits.shape[0]
    tm = ROUTER_TM
    rows = tm // LANES
    return pl.pallas_call(
        _router_kernel,
        grid=(2, t // tm),
        in_specs=[pl.BlockSpec((tm, LANES), lambda ph, i: (i, 0))],
        out_specs=[
            pl.BlockSpec((tm, EXPERT_TOPK), lambda ph, i: (i * ph, 0)),
            pl.BlockSpec((rows, LANES), lambda ph, i: (i * ph, 0)),
            pl.BlockSpec((rows, LANES), lambda ph, i: (i * ph, 0)),
            pl.BlockSpec((8, LANES), lambda ph, i: (0, 0)),
        ],
        out_shape=[
            jax.ShapeDtypeStruct((t, EXPERT_TOPK), F32),
            jax.ShapeDtypeStruct((t // LANES, LANES), jnp.int32),
            jax.ShapeDtypeStruct((t // LANES, LANES), jnp.int32),
            jax.ShapeDtypeStruct((8, LANES), jnp.int32),
        ],
        scratch_shapes=[
            pltpu.VMEM((tm, tm), BF16),
            pltpu.VMEM((8, LANES), F32),
            pltpu.VMEM((8, LANES), F32),
        ],
        compiler_params=_cparams(2, 32),
        name="router",
    )(logits)


def _round_up_pow2(x, m):
    assert m & (m - 1) == 0
    return (x + (m - 1)) & ~(m - 1)


def _plan_kernel(cnt_ref, s0_ref, s1_ref, tok_ref, ie_ref, ist_ref, ir_ref, if_ref, zeros, sem):
    n_tok = s0_ref.shape[0]
    n_slots = tok_ref.shape[0]
    n_items = ie_ref.shape[0]
    r = MOE_ITEM_ROWS

    zeros[...] = jnp.zeros_like(zeros)
    clear = pltpu.make_async_copy(zeros, tok_ref, sem)
    clear.start()
    clear.wait()

    def scatter(t16, carry):
        for d in range(16):
            t = t16 * 16 + d
            tok_ref[s0_ref[t]] = t
            tok_ref[s1_ref[t]] = t
        return carry
    lax.fori_loop(0, n_tok // 16, scatter, 0)

    a = jnp.int32(0)
    used = jnp.int32(0)
    e_last = jnp.int32(0)
    for e in range(N_EXPERTS):
        pad = _round_up_pow2(cnt_ref[0, N_GROUPS + e], MOE_PAD)

        def item(sub, a, e=e, pad=pad, used=used):
            ie_ref[a] = e
            ist_ref[a] = used + sub * r
            ir_ref[a] = jnp.minimum(pad - sub * r, r)
            if_ref[a] = 0
            return a + 1
        a = lax.fori_loop(0, _round_up_pow2(pad, r) >> (r.bit_length() - 1), item, a)
        e_last = jnp.where(pad > 0, e, e_last)
        used = used + pad

    def idle(a2, carry):
        fill_start = used + (a2 - a) * r
        fill = jnp.clip(n_slots - fill_start, 0, r)
        ie_ref[a2] = e_last
        ist_ref[a2] = jnp.where(fill > 0, fill_start, 0)
        ir_ref[a2] = 0
        if_ref[a2] = fill
        return carry
    lax.fori_loop(a, n_items, idle, 0)


def _plan(counts, s0, s1, n_slots, n_items):
    smem = pl.BlockSpec(memory_space=pltpu.SMEM)
    item = jax.ShapeDtypeStruct((n_items,), jnp.int32)
    return pl.pallas_call(
        _plan_kernel,
        in_specs=[smem, smem, smem],
        out_specs=[smem] * 5,
        out_shape=[jax.ShapeDtypeStruct((n_slots,), jnp.int32), item, item, item, item],
        scratch_shapes=[pltpu.VMEM((n_slots,), jnp.int32), pltpu.SemaphoreType.DMA(())],
        name="plan",
    )(counts, s0, s1)


def _unpack_pairs(u):
    lo = lax.bitcast_convert_type(u << 16, F32)
    hi = lax.bitcast_convert_type(u & jnp.uint32(0xFFFF0000), F32)
    return jnp.concatenate([lo, hi], axis=1)


def _store_row_tiles(dst, packed):
    r = packed.shape[0]
    for s in range(TILE_ROWS):
        dst[pl.ds(s, r, stride=TILE_ROWS), :] = packed[:, s * LANES:(s + 1) * LANES]


def _load_row_tiles(src, first_row, r):
    return jnp.concatenate(
        [src[pl.ds(first_row * TILE_ROWS + s, r, stride=TILE_ROWS), :] for s in range(TILE_ROWS)],
        axis=1)


def _pack_pairs(y):
    w = y.shape[1] // 2
    bits = lax.bitcast_convert_type(y.astype(BF16).astype(F32), jnp.uint32)
    return (bits[:, :w] >> 16) | (bits[:, w:] & jnp.uint32(0xFFFF0000))


def _moe_block_sizes():
    sizes, rows = [], MOE_BLOCK
    while rows >= MOE_PAD:
        sizes.append(rows)
        rows //= 2
    return sizes


def _moe_kernel(item_e, item_start, item_rows, item_fill, slot_tok,
                h2p_hbm, wg_ref, wu_ref, wd_ref, yb_hbm,
                xbuf, act, ystage, pf, gsem, osem):
    a = pl.program_id(0)
    t = pl.program_id(1)
    n_items = pl.num_programs(0)
    n = item_rows[a]
    start = item_start[a]
    half_w = wg_ref.shape[2]
    cur = a % 2
    nxt = 1 - cur
    a1 = jnp.minimum(a + 1, n_items - 1)
    n_next = jnp.where(a + 1 < n_items, item_rows[a1], 0)
    start_next = item_start[a1]

    def tile_rows(i, count=1):
        return pl.ds(pl.multiple_of(i * TILE_ROWS, TILE_ROWS), count * TILE_ROWS)

    def row_copy(tok, slot, i):
        return pltpu.make_async_copy(h2p_hbm.at[tile_rows(tok), :], xbuf.at[slot, tile_rows(i), :],
                                     gsem.at[slot])

    def start_row(tok, slot, i):
        row_copy(tok, slot, i).start(priority=1)

    @pl.when(jnp.logical_and(a == 0, t == 0))
    def _():
        pf[1] = 0
        pf[2] = 0

        def issue(i, carry):
            start_row(slot_tok[start + i], 0, i)
            return carry
        lax.fori_loop(0, n, issue, 0)

    @pl.when(t == 0)
    def _():
        pf[0] = 0

        def drain(bk, carry):
            r0 = pl.multiple_of(bk * MOE_PAD, MOE_PAD)
            pltpu.make_async_copy(h2p_hbm.at[tile_rows(0, MOE_PAD), :],
                                  xbuf.at[cur, tile_rows(r0, MOE_PAD), :], gsem.at[cur]).wait()
            return carry
        lax.fori_loop(0, n // MOE_PAD, drain, 0)

    def prefetch_group():
        base = pf[0]
        for q in range(MOE_GATHER_GROUP):
            start_row(slot_tok[start_next + base + q], nxt, base + q)
        pf[0] = base + MOE_GATHER_GROUP

    def run_blocks(compute):
        def one(b, r0, rows):
            r0 = pl.multiple_of(r0, MOE_PAD)
            more = pf[0] < n_next

            def gather():
                prefetch_group()
                for _ in range(max(rows // (2 * MOE_GATHER_GROUP), 1) - 1):
                    pl.when(pf[0] < n_next)(prefetch_group)

            @pl.when(more)
            def _():
                compute(b, r0, rows, gather)

            @pl.when(jnp.logical_not(more))
            def _():
                compute(b, r0, rows, lambda: None)

        n_big = n // MOE_BLOCK

        def body(b, carry):
            one(b, b * MOE_BLOCK, MOE_BLOCK)
            return carry
        lax.fori_loop(0, n_big, body, 0)

        b, r0, rows = n_big, n_big * MOE_BLOCK, MOE_BLOCK // 2
        while rows >= MOE_PAD:
            has = (n & rows) != 0
            pl.when(has)(functools.partial(one, b, r0, rows))
            b = b + has.astype(jnp.int32)
            r0 = r0 + jnp.where(has, rows, 0)
            rows //= 2

    def gate_up(half):
        def compute(b, r0, rows, mid):
            xs = _unpack_pairs(_load_row_tiles(xbuf.at[cur], r0, rows)).astype(BF16)
            g = _dot(xs, wg_ref[0].astype(BF16))
            u = _dot(xs, wu_ref[0].astype(BF16))
            mid()
            act[pl.ds(r0, rows), half * half_w:(half + 1) * half_w] = (
                g * _sigmoid(g) * u).astype(BF16)
        run_blocks(compute)

    def out_copy(slot, r0, rows):
        return pltpu.make_async_copy(
            ystage.at[slot, tile_rows(0, rows), :],
            yb_hbm.at[tile_rows(pl.multiple_of(start + r0, MOE_PAD), rows), :],
            osem.at[slot])

    def wait_stage(slot):
        for rows in _moe_block_sizes():
            @pl.when(pf[1 + slot] == rows)
            def _(rows=rows):
                out_copy(slot, 0, rows).wait()
                pf[1 + slot] = 0

    def down():
        def compute(b, r0, rows, mid):
            slot = b % 2
            wait_stage(slot)
            y = _dot(act[pl.ds(r0, rows), :], wd_ref[0].astype(BF16))
            mid()
            _store_row_tiles(ystage.at[slot], _pack_pairs(y))
            out_copy(slot, r0, rows).start()
            pf[1 + slot] = rows
        run_blocks(compute)

        def tail(i, carry):
            start_row(slot_tok[start_next + i], nxt, i)
            return carry
        lax.fori_loop(pf[0], n_next, tail, 0)

    pl.when(t == 0)(functools.partial(gate_up, 0))
    pl.when(t == 1)(functools.partial(gate_up, 1))
    pl.when(t == 2)(down)

    fill = item_fill[a]

    @pl.when(jnp.logical_and(t == 0, fill > 0))
    def _():
        wait_stage(0)
        ystage[0] = jnp.zeros(ystage.shape[1:], ystage.dtype)

        def zero_block(bk, carry):
            cp = out_copy(0, bk * MOE_PAD, MOE_PAD)
            cp.start()
            cp.wait()
            return carry
        lax.fori_loop(0, fill // MOE_PAD, zero_block, 0)

    @pl.when(jnp.logical_and(a == n_items - 1, t == pl.num_programs(1) - 1))
    def _():
        wait_stage(0)
        wait_stage(1)


def _moe_ffn(h2p, w_gate, w_up, w_down, item_e, item_start, item_rows, item_fill, slot_tok,
             n_slots):
    d = w_gate.shape[1]
    assert d == 2 * TILE_ROWS * LANES and h2p.shape[1] == LANES, "one packed token row per tile"
    n_items = item_e.shape[0]
    half_w = D_EXPERT // 2

    def wgu_map(a, t, ie, ist, ir, fl, st):
        return (ie[a], 0, jnp.where(ir[a] > 0, jnp.minimum(t, 1), 1))

    def wd_map(a, t, ie, ist, ir, fl, st):
        return (jnp.where(t >= 2, ie[a], ie[jnp.maximum(a - 1, 0)]), 0, 0)

    grid_spec = pltpu.PrefetchScalarGridSpec(
        num_scalar_prefetch=5,
        grid=(n_items, 3),
        in_specs=[
            pl.BlockSpec(memory_space=pl.ANY),
            pl.BlockSpec((1, d, half_w), wgu_map),
            pl.BlockSpec((1, d, half_w), wgu_map),
            pl.BlockSpec((1, D_EXPERT, d), wd_map),
        ],
        out_specs=pl.BlockSpec(memory_space=pl.ANY),
        scratch_shapes=[
            pltpu.VMEM((2, MOE_ITEM_ROWS * TILE_ROWS, LANES), jnp.uint32),
            pltpu.VMEM((MOE_ITEM_ROWS, D_EXPERT), BF16),
            pltpu.VMEM((2, MOE_BLOCK * TILE_ROWS, LANES), jnp.uint32),
            pltpu.SMEM((3,), jnp.int32),
            pltpu.SemaphoreType.DMA((2,)),
            pltpu.SemaphoreType.DMA((2,)),
        ],
    )
    return pl.pallas_call(
        _moe_kernel,
        grid_spec=grid_spec,
        out_shape=jax.ShapeDtypeStruct((n_slots * TILE_ROWS, LANES), jnp.uint32),
        compiler_params=_cparams(2, 56),
        name="moe",
    )(item_e, item_start, item_rows, item_fill, slot_tok, h2p, w_gate, w_up, w_down)


def _combine_kernel(s0, s1, yb_hbm, gates_ref, x1_ref, nw_ref, g2_ref, out_ref, rows, sem):
    i = pl.program_id(0)
    n_tiles = pl.num_programs(0)
    tm = x1_ref.shape[0]

    def tile_rows(i, count=1):
        return pl.ds(pl.multiple_of(i * TILE_ROWS, TILE_ROWS), count * TILE_ROWS)

    def gather(tile, buf):
        def issue(r8, carry):
            t0 = tile * tm + r8 * 8
            for dr in range(8):
                for k, s_k in enumerate((s0, s1)):
                    pltpu.make_async_copy(yb_hbm.at[tile_rows(s_k[t0 + dr]), :],
                                          rows.at[buf, k, tile_rows(r8 * 8 + dr), :],
                                          sem.at[buf]).start()
            return carry
        lax.fori_loop(0, tm // 8, issue, 0)

    @pl.when(i == 0)
    def _():
        gather(0, 0)

    @pl.when(i + 1 < n_tiles)
    def _():
        gather(i + 1, (i + 1) % 2)

    buf = i % 2
    for k in range(EXPERT_TOPK):
        pltpu.make_async_copy(yb_hbm.at[tile_rows(0, tm), :], rows.at[buf, k], sem.at[buf]).wait()

    g = gates_ref[...]
    y = (_unpack_pairs(_load_row_tiles(rows.at[buf, 0], 0, tm)) * g[:, 0:1]
         + _unpack_pairs(_load_row_tiles(rows.at[buf, 1], 0, tm)) * g[:, 1:2])
    out_ref[...] = x1_ref[...] + g2_ref[0:1, :] * (_rms(y) * nw_ref[3:4, :])


def _combine(s0, s1, yb, gates, x1, norm_w, mod):
    n, d = x1.shape
    tm = 256
    grid_spec = pltpu.PrefetchScalarGridSpec(
        num_scalar_prefetch=2,
        grid=(n // tm,),
        in_specs=[
            pl.BlockSpec(memory_space=pl.ANY),
            pl.BlockSpec((tm, EXPERT_TOPK), lambda i, *_: (i, 0)),
            pl.BlockSpec((tm, d), lambda i, *_: (i, 0)),
            pl.BlockSpec((4, d), lambda i, *_: (0, 0)),
            pl.BlockSpec((8, d), lambda i, *_: (0, 5)),
        ],
        out_specs=pl.BlockSpec((tm, d), lambda i, *_: (i, 0)),
        scratch_shapes=[
            pltpu.VMEM((2, EXPERT_TOPK, tm * TILE_ROWS, LANES), jnp.uint32),
            pltpu.SemaphoreType.DMA((2,)),
        ],
    )
    return pl.pallas_call(
        _combine_kernel,
        grid_spec=grid_spec,
        out_shape=jax.ShapeDtypeStruct((n, d), F32),
        compiler_params=_cparams(1, 40),
        name="combine",
    )(s0, s1, yb, gates, x1, norm_w, mod)


def _rope_tables(n):
    rows = n // GRID_W
    r = np.repeat(np.arange(rows, dtype=np.float32), GRID_W)
    cidx = np.tile(np.arange(GRID_W, dtype=np.float32), rows)
    n_freq = RET_DK // 4
    inv = np.float32(ROPE_BASE) ** (-np.arange(n_freq, dtype=np.float32) / np.float32(n_freq))
    ang = np.concatenate([r[:, None] * inv, cidx[:, None] * inv], axis=-1).astype(np.float32)
    cos, sin = np.cos(ang), np.sin(ang)
    cs = np.concatenate([cos, cos], axis=-1)
    sn = np.concatenate([-sin, sin], axis=-1)
    return jnp.asarray(cs, F32), jnp.asarray(sn, F32)


def kernel(x, c, ctx, c_ctx, w_mod, b_mod, norm_w, w_in, hg_lb_logits, hg_norm_w, ret_decay_logits,
           ret_norm_w, w_out, router_group_w, router_group_b, router_expert_w, router_expert_b,
           w_gate, w_up, w_down):
    bsz, n, d = x.shape
    l = ctx.shape[1]
    assert bsz == 1 and w_mod.shape[0] == 1, "single sample, single layer"
    x2d, ctx2d = x[0], ctx[0]

    cc = jnp.zeros((8, d), F32).at[0].set(c[0]).at[1].set(c_ctx)
    mod = _modulation(cc, w_mod[0], b_mod[0])

    lb = jnp.cumsum(jax.nn.softmax(hg_lb_logits.astype(F32), axis=0), axis=0)[0]
    rk0 = 5 * HG_WIDTH + RET_HEADS * RET_DK
    cols = jnp.stack([
        jnp.zeros((PROJ_WIDTH,), F32).at[HG_WIDTH:3 * HG_WIDTH].set(lb.reshape(-1)),
        jnp.ones((PROJ_WIDTH,), F32).at[rk0:rk0 + RET_HEADS * RET_DK].set(RET_DK ** -0.5),
    ])
    pc, gc, w_in_bf = _projection(ctx2d, norm_w[0], mod, w_in[0], cols, 1, l, "proj_ctx")
    p, g = _projection(x2d, norm_w[0], mod, w_in_bf, cols, 0, 1024, "proj_x")

    ret_lg = jax.nn.log_sigmoid(ret_decay_logits[0].astype(F32))
    cs_tab, sn_tab = _rope_tables(n)

    o_f, o_b = _hgrn_scan(p, g, pc, gc)
    r_f, r_b = _ret_scan(p, pc, ret_lg, cs_tab, sn_tab)

    wr = jnp.concatenate([router_group_w[0], router_expert_w[0]], axis=1)
    wr = jnp.pad(wr, ((0, 0), (0, ROUTER_LANES - wr.shape[1])))
    wr_hi = wr.astype(BF16)
    wr_parts = jnp.concatenate([wr_hi, (wr - wr_hi.astype(F32)).astype(BF16)], axis=1)
    br = jnp.concatenate([router_group_b[0], router_expert_b[0]]).astype(F32)
    br = jnp.pad(br, (0, ROUTER_LANES - br.shape[0])).reshape(1, ROUTER_LANES)
    x1, h2, logits = _out_projection(
        o_f, o_b, r_f, r_b, p, x2d, w_out[0].astype(BF16), hg_norm_w, ret_norm_w, norm_w[0], mod,
        wr_parts, br)

    n_assign = n * EXPERT_TOPK
    n_slots = n_assign + N_EXPERTS * MOE_PAD
    n_items = (n_slots + N_EXPERTS * (MOE_ITEM_ROWS - MOE_PAD)) // MOE_ITEM_ROWS
    gates, s0, s1, counts = _router(logits)
    s0, s1 = s0.reshape(-1), s1.reshape(-1)
    slot_tok, item_e, item_start, item_rows, item_fill = _plan(counts, s0, s1, n_slots, n_items)

    yb = _moe_ffn(h2, w_gate[0], w_up[0], w_down[0], item_e, item_start, item_rows, item_fill,
                  slot_tok, n_slots)
    out = _combine(s0, s1, yb, gates, x1, norm_w[0], mod)
    return out[None]
```

```python
import functools

import jax
import jax.numpy as jnp
import numpy as np
from jax import lax
from jax.experimental import pallas as pl
from jax.experimental.pallas import tpu as pltpu

F32 = jnp.float32
BF16 = jnp.bfloat16

GRID_W = 64
HG_HEADS = 8
HG_DK = 128
HG_WIDTH = 1024
RET_HEADS = 4
RET_DK = 128
RET_DV = 256
RET_WIDTH = 1024
PROJ_WIDTH = 8192
N_GROUPS = 4
EXPERTS_PER_GROUP = 8
N_EXPERTS = 32
EXPERT_TOPK = 2
D_EXPERT = 1024
RMS_EPS = 1e-6
ROPE_BASE = 10000.0

COL_HQ, COL_HI, COL_HGATE = 0, 3, 4
COL_RQ_512, COL_RK_512 = 10, 11
COL_RV, COL_RGATE = 6, 7

HG_CHUNK = 256
HG_STEP_HEADS = 8
HG_DIAG = 16
HG_LOG_CLAMP = 10.0
RET_CHUNK = 256

MOE_PAD = 128
MOE_BLOCK = 256
MOE_ITEM_ROWS = 1024
MOE_GATHER_GROUP = 128
ROUTER_LANES = 128
LANES = 128
TILE_ROWS = 8


def _cparams(n_axes, vmem_mb, **flags):
    return pltpu.CompilerParams(
        dimension_semantics=("arbitrary",) * n_axes,
        vmem_limit_bytes=vmem_mb * 1024 * 1024,
        flags=flags or None,
    )


def _sigmoid(x):
    return 1.0 / (1.0 + jnp.exp(-x))


def _dot(a, b):
    return jnp.dot(a, b, preferred_element_type=F32)


def _dot_nt(a, b):
    return lax.dot_general(a, b, (((1,), (1,)), ((), ())), preferred_element_type=F32)


def _dot_tn(a, b):
    return lax.dot_general(a, b, (((0,), (0,)), ((), ())), preferred_element_type=F32)


def _mod_kernel(cc_ref, w_ref, b_ref, o_ref):
    s = cc_ref[...]
    s = s * _sigmoid(s)
    s16 = jnp.concatenate([s, s], axis=0)
    hi = s16.astype(BF16).astype(F32)
    row = lax.broadcasted_iota(jnp.int32, s16.shape, 0)
    lhs = jnp.where(row < 8, hi, s16 - hi).astype(BF16)
    r = _dot(lhs, w_ref[...].astype(BF16))
    o_ref[...] = r[:8] + r[8:] + b_ref[...]


def _modulation(cc, w_mod, b_mod):
    d, n = w_mod.shape
    tn = 1024
    return pl.pallas_call(
        _mod_kernel,
        grid=(n // tn,),
        in_specs=[
            pl.BlockSpec((8, d), lambda j: (0, 0)),
            pl.BlockSpec((d, tn), lambda j: (0, j)),
            pl.BlockSpec((1, tn), lambda j: (0, j)),
        ],
        out_specs=pl.BlockSpec((8, tn), lambda j: (0, j)),
        out_shape=jax.ShapeDtypeStruct((8, n), F32),
        compiler_params=_cparams(1, 40),
        name="mod",
    )(cc, w_mod, b_mod.reshape(1, n))


PROJ_TN = 1024
PROLOGUE_ROWS = 32


def _proj_kernel(x_ref, nw_ref, sh_ref, sc_ref, w_ref, cols_ref, o_ref, g_ref, *rest, mod_row):
    h_scr = rest[-1]
    if len(rest) == 2:
        rest[0][...] = w_ref[...].astype(BF16)
        w_ref = rest[0]
    i = pl.program_id(0)
    j = pl.program_id(1)
    last_j = pl.num_programs(1) - 1
    cur = i % 2

    def normed(x):
        scale = nw_ref[0:1, :] * (1.0 + sc_ref[mod_row:mod_row + 1, :])
        shift = sh_ref[mod_row:mod_row + 1, :]
        y = x * lax.rsqrt(jnp.mean(x * x, axis=-1, keepdims=True) + RMS_EPS)
        return (y * scale + shift).astype(BF16)

    @pl.when(jnp.logical_and(i == 0, j == 0))
    def _():
        def body(r, carry):
            r0 = pl.multiple_of(r * PROLOGUE_ROWS, PROLOGUE_ROWS)
            h_scr[0, pl.ds(r0, PROLOGUE_ROWS), :] = normed(x_ref[pl.ds(r0, PROLOGUE_ROWS), :])
            return carry
        lax.fori_loop(0, x_ref.shape[0] // PROLOGUE_ROWS, body, 0)

    is_gate = jnp.logical_or(j == 1, j == 2)
    is_silu = jnp.logical_or(j == 0, j == 4)
    is_lin = jnp.logical_not(jnp.logical_or(jnp.logical_or(is_gate, is_silu), j == last_j))

    @pl.when(is_lin)
    def _():
        o_ref[...] = (_dot(h_scr[cur], w_ref[...]) * cols_ref[1:2, :]).astype(BF16)

    @pl.when(is_silu)
    def _():
        acc = _dot(h_scr[cur], w_ref[...])
        o_ref[...] = (acc * _sigmoid(acc)).astype(BF16)

    @pl.when(is_gate)
    def _():
        acc = _dot(h_scr[cur], w_ref[...])
        lb = cols_ref[0:1, :]
        g = jnp.maximum(jnp.log(lb + (1.0 - lb) * _sigmoid(acc)), -HG_LOG_CLAMP)
        g_ref[...] = g
        o_ref[...] = g.astype(BF16)

    @pl.when(j == last_j)
    def _():
        acc = _dot(h_scr[cur], w_ref[...])
        o_ref[...] = (acc * _sigmoid(acc)).astype(BF16)
        h_scr[1 - cur] = normed(x_ref[...])


def _projection(x2d, norm_w, mod, w_in, cols, mod_row, tm, name):
    n, d = x2d.shape
    tn = PROJ_TN
    n_i, n_j = n // tm, PROJ_WIDTH // tn
    emit_w = w_in.dtype != BF16
    assert not emit_w or n_i == 1, "the weight copy is written once per column tile"

    def x_map(i, j):
        return (jnp.minimum(i + (j == n_j - 1).astype(jnp.int32), n_i - 1), 0)

    return pl.pallas_call(
        functools.partial(_proj_kernel, mod_row=mod_row),
        grid=(n_i, n_j),
        in_specs=[
            pl.BlockSpec((tm, d), x_map),
            pl.BlockSpec((4, d), lambda i, j: (0, 0)),
            pl.BlockSpec((8, d), lambda i, j: (0, 0)),
            pl.BlockSpec((8, d), lambda i, j: (0, 1)),
            pl.BlockSpec((d, tn), lambda i, j: (0, j)),
            pl.BlockSpec((2, tn), lambda i, j: (0, j)),
        ],
        out_specs=[
            pl.BlockSpec((tm, tn), lambda i, j: (i, j)),
            pl.BlockSpec((tm, tn), lambda i, j: (i, jnp.clip(j - 1, 0, 1))),
            *([pl.BlockSpec((d, tn), lambda i, j: (0, j))] if emit_w else []),
        ],
        out_shape=[
            jax.ShapeDtypeStruct((n, PROJ_WIDTH), BF16),
            jax.ShapeDtypeStruct((n, 2 * HG_WIDTH), F32),
            *([jax.ShapeDtypeStruct(w_in.shape, BF16)] if emit_w else []),
        ],
        scratch_shapes=[pltpu.VMEM((2, tm, d), BF16)],
        compiler_params=_cparams(2, 56),
        name=name,
    )(x2d, norm_w, mod, mod, w_in, cols)


def _tri_ones(c, rev):
    t = lax.broadcasted_iota(jnp.int32, (c, c), 0)
    s = lax.broadcasted_iota(jnp.int32, (c, c), 1)
    return jnp.where((s >= t) if rev else (s <= t), 1.0, 0.0).astype(BF16)


def _cumsum_rows(g, tri):
    hi = g.astype(BF16)
    lo = (g - hi.astype(F32)).astype(BF16)
    return _dot(tri, hi) + _dot(tri, lo)


def _row_refs(b, idxs, blk):
    parts = [jnp.broadcast_to(b[i:i + 1, :], (blk, b.shape[1])) for i in idxs]
    return parts[0] if len(parts) == 1 else jnp.concatenate(parts, axis=0)


def _hgrn_levels(c, rev):
    t = lax.broadcasted_iota(jnp.int32, (c, c), 0)
    s = lax.broadcasted_iota(jnp.int32, (c, c), 1)
    levels = []
    m = c // 2
    while m >= HG_DIAG:
        blk = 2 * m
        same = ((t ^ s) & ~(blk - 1)) == 0
        if rev:
            idxs = [b0 + m for b0 in range(0, c, blk)]
            cross = jnp.logical_and((t & m) == 0, (s & m) != 0)
        else:
            idxs = [b0 + m - 1 for b0 in range(0, c, blk)]
            cross = jnp.logical_and((t & m) != 0, (s & m) == 0)
        levels.append((idxs, blk, jnp.logical_and(same, cross)))
        m //= 2
    blk = HG_DIAG
    same = ((t ^ s) & ~(blk - 1)) == 0
    if rev:
        idxs = [b0 + blk // 2 for b0 in range(0, c, blk)]
        mask = jnp.logical_and(same, s >= t)
    else:
        idxs = [b0 + blk // 2 - 1 for b0 in range(0, c, blk)]
        mask = jnp.logical_and(same, s <= t)
    levels.append((idxs, blk, mask))
    return levels


def _hgrn_head(q_bf, g, b, v_bf, st, rev, levels):
    c = g.shape[0]
    kk = 1.0 - jnp.exp(g)

    o = None
    if levels is not None:
        q = q_bf.astype(F32)
        att = jnp.zeros((c, c), F32)
        for idxs, blk, mask in levels:
            ref = _row_refs(b, idxs, blk)
            a_l = _dot_nt((q * jnp.exp(b - ref)).astype(BF16),
                          (kk * jnp.exp(ref - b)).astype(BF16))
            att = att + jnp.where(mask, a_l, 0.0)
        o = _dot(att.astype(BF16), v_bf) + _dot_nt((q * jnp.exp(b)).astype(BF16), st.astype(BF16))

    b_end = b[0:1, :] if rev else b[c - 1:c, :]
    k_end = (kk * jnp.exp(b_end - b)).astype(BF16)
    st_new = st * jnp.exp(b_end) + _dot_tn(v_bf, k_end)
    return o, st_new


def _hgrn_ctx_kernel(gf_ref, vf_ref, gb_ref, vb_ref, st_ref):
    @pl.when(pl.program_id(0) == 0)
    def _():
        st_ref[...] = jnp.zeros_like(st_ref)

    c = gf_ref.shape[0]
    bf = _cumsum_rows(gf_ref[...], _tri_ones(c, False))
    bb = _cumsum_rows(gb_ref[...], _tri_ones(c, True))
    for h in range(HG_HEADS):
        sl = slice(h * HG_DK, (h + 1) * HG_DK)
        _, st_ref[h] = _hgrn_head(None, gf_ref[:, sl], bf[:, sl], vf_ref[:, sl], st_ref[h],
                                  False, None)
        hb = HG_HEADS + h
        _, st_ref[hb] = _hgrn_head(None, gb_ref[:, sl], bb[:, sl], vb_ref[:, sl], st_ref[hb],
                                   True, None)


def _hgrn_kernel(qf_ref, gf_ref, vf_ref, qb_ref, gb_ref, vb_ref, st0_ref, of_ref, ob_ref, st_ref):
    hg = pl.program_id(1)

    @pl.when(jnp.logical_and(pl.program_id(0) == 0, hg == 0))
    def _():
        st_ref[...] = st0_ref[...]

    c = gf_ref.shape[0]
    lev_f = _hgrn_levels(c, False)
    lev_b = _hgrn_levels(c, True)
    bf = _cumsum_rows(gf_ref[...], _tri_ones(c, False))
    bb = _cumsum_rows(gb_ref[...], _tri_ones(c, True))
    for h in range(HG_STEP_HEADS):
        sl = slice(h * HG_DK, (h + 1) * HG_DK)
        hf = hg * HG_STEP_HEADS + h
        of_ref[:, sl], st_ref[hf] = _hgrn_head(qf_ref[:, sl], gf_ref[:, sl], bf[:, sl],
                                               vf_ref[:, sl], st_ref[hf], False, lev_f)
        hb = HG_HEADS + hf
        ob_ref[:, sl], st_ref[hb] = _hgrn_head(qb_ref[:, sl], gb_ref[:, sl], bb[:, sl],
                                               vb_ref[:, sl], st_ref[hb], True, lev_b)


def _hgrn_scan(p, g, pc, gc):
    c = HG_CHUNK
    w = HG_WIDTH
    n, l = p.shape[0], pc.shape[0]
    c_ctx = min(c, l)
    n_x, n_ctx = n // c, l // c_ctx
    st_shape = (2 * HG_HEADS, HG_DK, HG_DK)

    def spec(chunk_fn, col):
        return pl.BlockSpec((c_ctx, w), lambda s: (chunk_fn(s), col))

    st0 = pl.pallas_call(
        _hgrn_ctx_kernel,
        grid=(n_ctx,),
        in_specs=[
            spec(lambda s: s, 0), spec(lambda s: s, COL_HI),
            spec(lambda s: n_ctx - 1 - s, 1), spec(lambda s: n_ctx - 1 - s, COL_HI),
        ],
        out_specs=pl.BlockSpec(st_shape, lambda s: (0, 0, 0)),
        out_shape=jax.ShapeDtypeStruct(st_shape, F32),
        compiler_params=_cparams(1, 32),
        name="hgrn_ctx",
    )(gc, pc, gc, pc)

    n_hg = HG_HEADS // HG_STEP_HEADS
    wg = HG_STEP_HEADS * HG_DK
    fwd = lambda s: s
    bwd = lambda s: n_x - 1 - s

    def gspec(chunk_fn, col):
        return pl.BlockSpec((c, wg), lambda s, hg: (chunk_fn(s), col * n_hg + hg))

    return pl.pallas_call(
        _hgrn_kernel,
        grid=(n_x, n_hg),
        in_specs=[
            gspec(fwd, COL_HQ), gspec(fwd, 0), gspec(fwd, COL_HI),
            gspec(bwd, COL_HQ), gspec(bwd, 1), gspec(bwd, COL_HI),
            pl.BlockSpec(st_shape, lambda s, hg: (0, 0, 0)),
        ],
        out_specs=[gspec(fwd, 0), gspec(bwd, 0)],
        out_shape=[jax.ShapeDtypeStruct((n, w), F32)] * 2,
        scratch_shapes=[pltpu.VMEM(st_shape, F32)],
        compiler_params=_cparams(2, 32),
        name="hgrn",
    )(p, g, p, p, g, p, st0)


def _rope(x_bf, cs, sn):
    x = x_bf.astype(F32)
    return x * cs + pltpu.roll(x, RET_DK // 2, axis=1) * sn


def _ret_decays(lg, c, rev):
    t = lax.broadcasted_iota(jnp.int32, (c, c), 0)
    s = lax.broadcasted_iota(jnp.int32, (c, c), 1)
    diff = (s - t) if rev else (t - s)
    pair = jnp.where(diff >= 0, jnp.exp(lg * jnp.maximum(diff, 0).astype(F32)), 0.0)
    pos = lax.broadcasted_iota(jnp.int32, (c, RET_DK), 0).astype(F32)
    q_dec = jnp.exp(lg * ((c - pos) if rev else (pos + 1.0)))
    k_dec = jnp.exp(lg * (pos if rev else (c - 1.0 - pos)))
    chunk = jnp.exp(jnp.full((8, RET_DK), lg * c, F32))
    return pair, q_dec, k_dec, chunk


def _ret_head(q, k, v_bf, decays, st):
    pair, q_dec, k_dec, chunk = decays
    o = None
    if q is not None:
        att = _dot_nt(q.astype(BF16), k.astype(BF16)) * pair
        o = _dot(att.astype(BF16), v_bf) + _dot_nt((q * q_dec).astype(BF16), st.astype(BF16))
    st_new = st * chunk[0:1, :] + _dot_tn(v_bf, (k * k_dec).astype(BF16))
    return o, st_new


def _ret_ctx_kernel(lg_ref, kf_ref, vf_ref, kb_ref, vb_ref, st_ref):
    @pl.when(pl.program_id(0) == 0)
    def _():
        st_ref[...] = jnp.zeros_like(st_ref)

    c = kf_ref.shape[0]
    for h in range(RET_HEADS):
        sk = slice(h * RET_DK, (h + 1) * RET_DK)
        sv = slice(h * RET_DV, (h + 1) * RET_DV)
        _, st_ref[h] = _ret_head(None, kf_ref[:, sk].astype(F32), vf_ref[:, sv],
                                 _ret_decays(lg_ref[0, h], c, False), st_ref[h])
        hb = RET_HEADS + h
        _, st_ref[hb] = _ret_head(None, kb_ref[:, sk].astype(F32), vb_ref[:, sv],
                                  _ret_decays(lg_ref[1, h], c, True), st_ref[hb])


def _ret_kernel(lg_ref, qf_ref, kf_ref, vf_ref, csf_ref, snf_ref,
                qb_ref, kb_ref, vb_ref, csb_ref, snb_ref, st0_ref, of_ref, ob_ref,
                st_ref, pair_ref, qd_ref, kd_ref, cd_ref):
    c = qf_ref.shape[0]

    @pl.when(pl.program_id(0) == 0)
    def _():
        st_ref[...] = st0_ref[...]
        for d in range(2):
            for h in range(RET_HEADS):
                i = d * RET_HEADS + h
                pair_ref[i], qd_ref[i], kd_ref[i], cd_ref[i] = _ret_decays(lg_ref[d, h], c, d == 1)

    csf, snf = csf_ref[...], snf_ref[...]
    csb, snb = csb_ref[...], snb_ref[...]
    for h in range(RET_HEADS):
        sk = slice(h * RET_DK, (h + 1) * RET_DK)
        sv = slice(h * RET_DV, (h + 1) * RET_DV)
        of_ref[:, sv], st_ref[h] = _ret_head(
            _rope(qf_ref[:, sk], csf, snf), _rope(kf_ref[:, sk], csf, snf), vf_ref[:, sv],
            (pair_ref[h], qd_ref[h], kd_ref[h], cd_ref[h]), st_ref[h])
        hb = RET_HEADS + h
        ob_ref[:, sv], st_ref[hb] = _ret_head(
            _rope(qb_ref[:, sk], csb, snb), _rope(kb_ref[:, sk], csb, snb), vb_ref[:, sv],
            (pair_ref[hb], qd_ref[hb], kd_ref[hb], cd_ref[hb]), st_ref[hb])


def _ret_scan(p, pc, ret_lg, cs_tab, sn_tab):
    c = RET_CHUNK
    n, l = p.shape[0], pc.shape[0]
    n_x, n_ctx = n // c, l // c
    qk_w = RET_HEADS * RET_DK
    st_shape = (2 * RET_HEADS, RET_DV, RET_DK)
    smem = pl.BlockSpec(memory_space=pltpu.SMEM)

    def kv_specs(chunk_fn):
        return [
            pl.BlockSpec((c, qk_w), lambda s: (chunk_fn(s), COL_RK_512)),
            pl.BlockSpec((c, RET_WIDTH), lambda s: (chunk_fn(s), COL_RV)),
        ]

    st0 = pl.pallas_call(
        _ret_ctx_kernel,
        grid=(n_ctx,),
        in_specs=[smem, *kv_specs(lambda s: s), *kv_specs(lambda s: n_ctx - 1 - s)],
        out_specs=pl.BlockSpec(st_shape, lambda s: (0, 0, 0)),
        out_shape=jax.ShapeDtypeStruct(st_shape, F32),
        compiler_params=_cparams(1, 32),
        name="ret_ctx",
    )(ret_lg, pc, pc, pc, pc)

    def specs(chunk_fn):
        return [
            pl.BlockSpec((c, qk_w), lambda s: (chunk_fn(s), COL_RQ_512)),
            *kv_specs(chunk_fn),
            pl.BlockSpec((c, RET_DK), lambda s: (chunk_fn(s), 0)),
            pl.BlockSpec((c, RET_DK), lambda s: (chunk_fn(s), 0)),
        ]

    fwd = lambda s: s
    bwd = lambda s: n_x - 1 - s
    return pl.pallas_call(
        _ret_kernel,
        grid=(n_x,),
        in_specs=[smem, *specs(fwd), *specs(bwd), pl.BlockSpec(st_shape, lambda s: (0, 0, 0))],
        out_specs=[
            pl.BlockSpec((c, RET_WIDTH), lambda s: (fwd(s), 0)),
            pl.BlockSpec((c, RET_WIDTH), lambda s: (bwd(s), 0)),
        ],
        out_shape=[jax.ShapeDtypeStruct((n, RET_WIDTH), F32)] * 2,
        scratch_shapes=[
            pltpu.VMEM(st_shape, F32),
            pltpu.VMEM((2 * RET_HEADS, c, c), F32),
            pltpu.VMEM((2 * RET_HEADS, c, RET_DK), F32),
            pltpu.VMEM((2 * RET_HEADS, c, RET_DK), F32),
            pltpu.VMEM((2 * RET_HEADS, 8, RET_DK), F32),
        ],
        compiler_params=_cparams(1, 32),
        name="ret",
    )(ret_lg, p, p, p, cs_tab, sn_tab, p, p, p, cs_tab, sn_tab, st0)


def _rms(y):
    return y * lax.rsqrt(jnp.mean(y * y, axis=-1, keepdims=True) + RMS_EPS)


def _outproj_kernel(of_ref, ob_ref, rf_ref, rb_ref, hgate_ref, rgate_ref, x_ref, wout_ref,
                    hgw_ref, rnw_ref, nw_ref, g1_ref, sh2_ref, sc2_ref, wr_ref, br_ref,
                    x1_ref, h2_ref, lg_ref, cat_scr):
    for h in range(HG_HEADS):
        sl = slice(h * HG_DK, (h + 1) * HG_DK)
        o = of_ref[:, sl] + ob_ref[:, sl]
        cat_scr[:, sl] = (_rms(o) * hgw_ref[0:1, sl] * hgate_ref[:, sl].astype(F32)).astype(BF16)
    for h in range(RET_HEADS):
        sl = slice(h * RET_DV, (h + 1) * RET_DV)
        r = rf_ref[:, sl] + rb_ref[:, sl]
        cat_scr[:, HG_WIDTH + h * RET_DV:HG_WIDTH + (h + 1) * RET_DV] = (
            _rms(r) * rnw_ref[0:1, sl] * rgate_ref[:, sl].astype(F32)).astype(BF16)

    y = _dot(cat_scr[...], wout_ref[...])
    x1 = x_ref[...] + g1_ref[0:1, :] * (_rms(y) * nw_ref[1:2, :])
    x1_ref[...] = x1
    h2 = _rms(x1) * nw_ref[2:3, :] * (1.0 + sc2_ref[0:1, :]) + sh2_ref[0:1, :]
    _store_row_tiles(h2_ref, _pack_pairs(h2))
    hi = h2.astype(BF16)
    lo = (h2 - hi.astype(F32)).astype(BF16)
    parts = _dot(jnp.concatenate([hi, lo], axis=0), wr_ref[...])
    half = hi.shape[0]
    lg_ref[...] = (parts[:half, :ROUTER_LANES] + parts[half:, :ROUTER_LANES]
                   + parts[:half, ROUTER_LANES:] + br_ref[...])


def _out_projection(o_f, o_b, r_f, r_b, p, x2d, w_out_bf, hg_norm_w, ret_norm_w, norm_w, mod,
                    wr_parts, br):
    n, d = x2d.shape
    tm = 256
    row = lambda w: pl.BlockSpec((tm, w), lambda i: (i, 0))
    full = lambda a: pl.BlockSpec(a.shape, lambda i: (0,) * a.ndim)
    modcol = lambda k: pl.BlockSpec((8, d), lambda i: (0, k))
    return pl.pallas_call(
        _outproj_kernel,
        grid=(n // tm,),
        in_specs=[
            row(HG_WIDTH), row(HG_WIDTH), row(RET_WIDTH), row(RET_WIDTH),
            pl.BlockSpec((tm, HG_WIDTH), lambda i: (i, COL_HGATE)),
            pl.BlockSpec((tm, RET_WIDTH), lambda i: (i, COL_RGATE)),
            row(d), full(w_out_bf), full(hg_norm_w), full(ret_norm_w), full(norm_w),
            modcol(2), modcol(3), modcol(4),
            full(wr_parts), full(br),
        ],
        out_specs=[row(d), pl.BlockSpec((tm * TILE_ROWS, LANES), lambda i: (i, 0)),
                   row(ROUTER_LANES)],
        out_shape=[
            jax.ShapeDtypeStruct((n, d), F32),
            jax.ShapeDtypeStruct((n * TILE_ROWS, LANES), jnp.uint32),
            jax.ShapeDtypeStruct((n, ROUTER_LANES), F32),
        ],
        scratch_shapes=[pltpu.VMEM((tm, d), BF16)],
        compiler_params=_cparams(1, 48),
        name="outproj",
    )(o_f, o_b, r_f, r_b, p, p, x2d, w_out_bf, hg_norm_w, ret_norm_w, norm_w, mod, mod, mod,
      wr_parts, br)


ROUTER_TM = 1024


def _lane_first(hit, lane_f):
    return jnp.min(jnp.where(hit, lane_f, 1e9), axis=1, keepdims=True)


def _router_kernel(lg_ref, gates_ref, s0_ref, s1_ref, cnt_ref, ltri, carry, pstart):
    ph = pl.program_id(0)
    i = pl.program_id(1)
    tm = lg_ref.shape[0]

    @pl.when(jnp.logical_and(ph == 0, i == 0))
    def _():
        t = lax.broadcasted_iota(jnp.int32, (tm, tm), 0)
        s = lax.broadcasted_iota(jnp.int32, (tm, tm), 1)
        ltri[...] = jnp.where(s < t, 1.0, 0.0).astype(BF16)

    @pl.when(jnp.logical_and(ph == 1, i == 0))
    def _():
        cnt = carry[...]
        padded = jnp.floor((cnt + (MOE_PAD - 1.0)) * (1.0 / MOE_PAD)) * MOE_PAD
        r = lax.broadcasted_iota(jnp.int32, (LANES, LANES), 0)
        c = lax.broadcasted_iota(jnp.int32, (LANES, LANES), 1)
        before = jnp.where(r < c, 1.0, 0.0).astype(BF16)
        pstart[...] = _dot(padded.astype(BF16), before)
        cnt_ref[...] = cnt.astype(jnp.int32)

    @pl.when(i == 0)
    def _():
        carry[...] = jnp.zeros_like(carry)

    lg = lg_ref[...]
    lane = lax.broadcasted_iota(jnp.int32, lg.shape, 1)
    lane_f = lane.astype(F32)
    neg = -jnp.inf

    gl = jnp.where(lane < N_GROUPS, lg, neg)
    gmax = jnp.max(gl, axis=1, keepdims=True)
    grp = _lane_first(gl == gmax, lane_f).astype(jnp.int32)
    p_sel = 1.0 / jnp.sum(jnp.exp(gl - gmax), axis=1, keepdims=True)

    in_grp = jnp.logical_and(
        jnp.logical_and(lane >= N_GROUPS, lane < N_GROUPS + N_EXPERTS),
        ((lane - N_GROUPS) >> (EXPERTS_PER_GROUP.bit_length() - 1)) == grp)
    el = jnp.where(in_grp, lg, neg)
    v1 = jnp.max(el, axis=1, keepdims=True)
    o1 = lane_f == _lane_first(el == v1, lane_f)
    el2 = jnp.where(o1, neg, el)
    v2 = jnp.max(el2, axis=1, keepdims=True)
    o2 = lane_f == _lane_first(el2 == v2, lane_f)

    osum = (jnp.where(o1, 1.0, 0.0) + jnp.where(o2, 1.0, 0.0)).astype(BF16)
    earlier = _dot(ltri[...], osum) + carry[0:1, :]
    carry[...] = carry[...] + _dot(jnp.ones((8, tm), BF16), osum)

    @pl.when(ph == 1)
    def _():
        ez = jnp.exp(v2 - v1)
        g1 = p_sel / (1.0 + ez)
        gates_ref[...] = jnp.concatenate([g1, g1 * ez], axis=1)
        base = earlier + pstart[0:1, :]
        for o, s_ref in ((o1, s0_ref), (o2, s1_ref)):
            col = jnp.sum(jnp.where(o, base, 0.0), axis=1, keepdims=True)
            wide = jnp.broadcast_to(col, (tm, LANES))
            for b in range(tm // LANES):
                s_ref[b:b + 1, :] = wide[b * LANES:(b + 1) * LANES, :].T[0:1, :].astype(jnp.int32)


def _router(logits):
    t = logits.shape[0]
    tm = ROUTER_TM
    rows = tm // LANES
    return pl.pallas_call(
        _router_kernel,
        grid=(2, t // tm),
        in_specs=[pl.BlockSpec((tm, LANES), lambda ph, i: (i, 0))],
        out_specs=[
            pl.BlockSpec((tm, EXPERT_TOPK), lambda ph, i: (i * ph, 0)),
            pl.BlockSpec((rows, LANES), lambda ph, i: (i * ph, 0)),
            pl.BlockSpec((rows, LANES), lambda ph, i: (i * ph, 0)),
            pl.BlockSpec((8, LANES), lambda ph, i: (0, 0)),
        ],
        out_shape=[
            jax.ShapeDtypeStruct((t, EXPERT_TOPK), F32),
            jax.ShapeDtypeStruct((t // LANES, LANES), jnp.int32),
            jax.ShapeDtypeStruct((t // LANES, LANES), jnp.int32),
            jax.ShapeDtypeStruct((8, LANES), jnp.int32),
        ],
        scratch_shapes=[
            pltpu.VMEM((tm, tm), BF16),
            pltpu.VMEM((8, LANES), F32),
            pltpu.VMEM((8, LANES), F32),
        ],
        compiler_params=_cparams(2, 32),
        name="router",
    )(logits)


def _round_up_pow2(x, m):
    assert m & (m - 1) == 0
    return (x + (m - 1)) & ~(m - 1)


def _plan_kernel(cnt_ref, s0_ref, s1_ref, tok_ref, ie_ref, ist_ref, ir_ref, if_ref, zeros, sem):
    n_tok = s0_ref.shape[0]
    n_slots = tok_ref.shape[0]
    n_items = ie_ref.shape[0]
    r = MOE_ITEM_ROWS

    zeros[...] = jnp.zeros_like(zeros)
    clear = pltpu.make_async_copy(zeros, tok_ref, sem)
    clear.start()
    clear.wait()

    def scatter(t16, carry):
        for d in range(16):
            t = t16 * 16 + d
            tok_ref[s0_ref[t]] = t
            tok_ref[s1_ref[t]] = t
        return carry
    lax.fori_loop(0, n_tok // 16, scatter, 0)

    a = jnp.int32(0)
    used = jnp.int32(0)
    e_last = jnp.int32(0)
    for e in range(N_EXPERTS):
        pad = _round_up_pow2(cnt_ref[0, N_GROUPS + e], MOE_PAD)

        def item(sub, a, e=e, pad=pad, used=used):
            ie_ref[a] = e
            ist_ref[a] = used + sub * r
            ir_ref[a] = jnp.minimum(pad - sub * r, r)
            if_ref[a] = 0
            return a + 1
        a = lax.fori_loop(0, _round_up_pow2(pad, r) >> (r.bit_length() - 1), item, a)
        e_last = jnp.where(pad > 0, e, e_last)
        used = used + pad

    def idle(a2, carry):
        fill_start = used + (a2 - a) * r
        fill = jnp.clip(n_slots - fill_start, 0, r)
        ie_ref[a2] = e_last
        ist_ref[a2] = jnp.where(fill > 0, fill_start, 0)
        ir_ref[a2] = 0
        if_ref[a2] = fill
        return carry
    lax.fori_loop(a, n_items, idle, 0)


def _plan(counts, s0, s1, n_slots, n_items):
    smem = pl.BlockSpec(memory_space=pltpu.SMEM)
    item = jax.ShapeDtypeStruct((n_items,), jnp.int32)
    return pl.pallas_call(
        _plan_kernel,
        in_specs=[smem, smem, smem],
        out_specs=[smem] * 5,
        out_shape=[jax.ShapeDtypeStruct((n_slots,), jnp.int32), item, item, item, item],
        scratch_shapes=[pltpu.VMEM((n_slots,), jnp.int32), pltpu.SemaphoreType.DMA(())],
        name="plan",
    )(counts, s0, s1)


def _unpack_pairs(u):
    lo = lax.bitcast_convert_type(u << 16, F32)
    hi = lax.bitcast_convert_type(u & jnp.uint32(0xFFFF0000), F32)
    return jnp.concatenate([lo, hi], axis=1)


def _store_row_tiles(dst, packed):
    r = packed.shape[0]
    for s in range(TILE_ROWS):
        dst[pl.ds(s, r, stride=TILE_ROWS), :] = packed[:, s * LANES:(s + 1) * LANES]


def _load_row_tiles(src, first_row, r):
    return jnp.concatenate(
        [src[pl.ds(first_row * TILE_ROWS + s, r, stride=TILE_ROWS), :] for s in range(TILE_ROWS)],
        axis=1)


def _pack_pairs(y):
    w = y.shape[1] // 2
    bits = lax.bitcast_convert_type(y.astype(BF16).astype(F32), jnp.uint32)
    return (bits[:, :w] >> 16) | (bits[:, w:] & jnp.uint32(0xFFFF0000))


def _moe_block_sizes():
    sizes, rows = [], MOE_BLOCK
    while rows >= MOE_PAD:
        sizes.append(rows)
        rows //= 2
    return sizes


def _moe_kernel(item_e, item_start, item_rows, item_fill, slot_tok,
                h2p_hbm, wg_ref, wu_ref, wd_ref, yb_hbm,
                xbuf, act, ystage, pf, gsem, osem):
    a = pl.program_id(0)
    t = pl.program_id(1)
    n_items = pl.num_programs(0)
    n = item_rows[a]
    start = item_start[a]
    half_w = wg_ref.shape[2]
    cur = a % 2
    nxt = 1 - cur
    a1 = jnp.minimum(a + 1, n_items - 1)
    n_next = jnp.where(a + 1 < n_items, item_rows[a1], 0)
    start_next = item_start[a1]

    def tile_rows(i, count=1):
        return pl.ds(pl.multiple_of(i * TILE_ROWS, TILE_ROWS), count * TILE_ROWS)

    def row_copy(tok, slot, i):
        return pltpu.make_async_copy(h2p_hbm.at[tile_rows(tok), :], xbuf.at[slot, tile_rows(i), :],
                                     gsem.at[slot])

    def start_row(tok, slot, i):
        row_copy(tok, slot, i).start()

    @pl.when(jnp.logical_and(a == 0, t == 0))
    def _():
        pf[1] = 0
        pf[2] = 0

        def issue(i, carry):
            start_row(slot_tok[start + i], 0, i)
            return carry
        lax.fori_loop(0, n, issue, 0)

    @pl.when(t == 0)
    def _():
        pf[0] = 0

        def drain(bk, carry):
            r0 = pl.multiple_of(bk * MOE_PAD, MOE_PAD)
            pltpu.make_async_copy(h2p_hbm.at[tile_rows(0, MOE_PAD), :],
                                  xbuf.at[cur, tile_rows(r0, MOE_PAD), :], gsem.at[cur]).wait()
            return carry
        lax.fori_loop(0, n // MOE_PAD, drain, 0)

    def prefetch_group():
        base = pf[0]
        for q in range(MOE_GATHER_GROUP):
            start_row(slot_tok[start_next + base + q], nxt, base + q)
        pf[0] = base + MOE_GATHER_GROUP

    def run_blocks(compute):
        def one(b, r0, rows):
            r0 = pl.multiple_of(r0, MOE_PAD)
            more = pf[0] < n_next

            def gather():
                prefetch_group()
                for _ in range(max(rows // (2 * MOE_GATHER_GROUP), 1) - 1):
                    pl.when(pf[0] < n_next)(prefetch_group)

            @pl.when(more)
            def _():
                compute(b, r0, rows, gather)

            @pl.when(jnp.logical_not(more))
            def _():
                compute(b, r0, rows, lambda: None)

        n_big = n // MOE_BLOCK

        def body(b, carry):
            one(b, b * MOE_BLOCK, MOE_BLOCK)
            return carry
        lax.fori_loop(0, n_big, body, 0)

        b, r0, rows = n_big, n_big * MOE_BLOCK, MOE_BLOCK // 2
        while rows >= MOE_PAD:
            has = (n & rows) != 0
            pl.when(has)(functools.partial(one, b, r0, rows))
            b = b + has.astype(jnp.int32)
            r0 = r0 + jnp.where(has, rows, 0)
            rows //= 2

    def gate_up(half):
        def compute(b, r0, rows, mid):
            xs = _unpack_pairs(_load_row_tiles(xbuf.at[cur], r0, rows)).astype(BF16)
            g = _dot(xs, wg_ref[0].astype(BF16))
            u = _dot(xs, wu_ref[0].astype(BF16))
            mid()
            act[pl.ds(r0, rows), half * half_w:(half + 1) * half_w] = (
                g * _sigmoid(g) * u).astype(BF16)
        run_blocks(compute)

    def out_copy(slot, r0, rows):
        return pltpu.make_async_copy(
            ystage.at[slot, tile_rows(0, rows), :],
            yb_hbm.at[tile_rows(pl.multiple_of(start + r0, MOE_PAD), rows), :],
            osem.at[slot])

    def wait_stage(slot):
        for rows in _moe_block_sizes():
            @pl.when(pf[1 + slot] == rows)
            def _(rows=rows):
                out_copy(slot, 0, rows).wait()
                pf[1 + slot] = 0

    def down():
        def compute(b, r0, rows, mid):
            slot = b % 2
            wait_stage(slot)
            y = _dot(act[pl.ds(r0, rows), :], wd_ref[0].astype(BF16))
            mid()
            _store_row_tiles(ystage.at[slot], _pack_pairs(y))
            out_copy(slot, r0, rows).start()
            pf[1 + slot] = rows
        run_blocks(compute)

        def tail(i, carry):
            start_row(slot_tok[start_next + i], nxt, i)
            return carry
        lax.fori_loop(pf[0], n_next, tail, 0)

    pl.when(t == 0)(functools.partial(gate_up, 0))
    pl.when(t == 1)(functools.partial(gate_up, 1))
    pl.when(t == 2)(down)

    fill = item_fill[a]

    @pl.when(jnp.logical_and(t == 0, fill > 0))
    def _():
        wait_stage(0)
        ystage[0] = jnp.zeros(ystage.shape[1:], ystage.dtype)

        def zero_block(bk, carry):
            cp = out_copy(0, bk * MOE_PAD, MOE_PAD)
            cp.start()
            cp.wait()
            return carry
        lax.fori_loop(0, fill // MOE_PAD, zero_block, 0)

    @pl.when(jnp.logical_and(a == n_items - 1, t == pl.num_programs(1) - 1))
    def _():
        wait_stage(0)
        wait_stage(1)


def _moe_ffn(h2p, w_gate, w_up, w_down, item_e, item_start, item_rows, item_fill, slot_tok,
             n_slots):
    d = w_gate.shape[1]
    assert d == 2 * TILE_ROWS * LANES and h2p.shape[1] == LANES, "one packed token row per tile"
    n_items = item_e.shape[0]
    half_w = D_EXPERT // 2

    def wgu_map(a, t, ie, ist, ir, fl, st):
        return (ie[a], 0, jnp.where(ir[a] > 0, jnp.minimum(t, 1), 1))

    def wd_map(a, t, ie, ist, ir, fl, st):
        return (jnp.where(t >= 2, ie[a], ie[jnp.maximum(a - 1, 0)]), 0, 0)

    grid_spec = pltpu.PrefetchScalarGridSpec(
        num_scalar_prefetch=5,
        grid=(n_items, 3),
        in_specs=[
            pl.BlockSpec(memory_space=pl.ANY),
            pl.BlockSpec((1, d, half_w), wgu_map),
            pl.BlockSpec((1, d, half_w), wgu_map),
            pl.BlockSpec((1, D_EXPERT, d), wd_map),
        ],
        out_specs=pl.BlockSpec(memory_space=pl.ANY),
        scratch_shapes=[
            pltpu.VMEM((2, MOE_ITEM_ROWS * TILE_ROWS, LANES), jnp.uint32),
            pltpu.VMEM((MOE_ITEM_ROWS, D_EXPERT), BF16),
            pltpu.VMEM((2, MOE_BLOCK * TILE_ROWS, LANES), jnp.uint32),
            pltpu.SMEM((3,), jnp.int32),
            pltpu.SemaphoreType.DMA((2,)),
            pltpu.SemaphoreType.DMA((2,)),
        ],
    )
    return pl.pallas_call(
        _moe_kernel,
        grid_spec=grid_spec,
        out_shape=jax.ShapeDtypeStruct((n_slots * TILE_ROWS, LANES), jnp.uint32),
        compiler_params=_cparams(2, 56),
        name="moe",
    )(item_e, item_start, item_rows, item_fill, slot_tok, h2p, w_gate, w_up, w_down)


def _combine_kernel(s0, s1, yb_hbm, gates_ref, x1_ref, nw_ref, g2_ref, out_ref, rows, sem):
    i = pl.program_id(0)
    n_tiles = pl.num_programs(0)
    tm = x1_ref.shape[0]

    def tile_rows(i, count=1):
        return pl.ds(pl.multiple_of(i * TILE_ROWS, TILE_ROWS), count * TILE_ROWS)

    def gather(tile, buf):
        def issue(r8, carry):
            t0 = tile * tm + r8 * 8
            for dr in range(8):
                for k, s_k in enumerate((s0, s1)):
                    pltpu.make_async_copy(yb_hbm.at[tile_rows(s_k[t0 + dr]), :],
                                          rows.at[buf, k, tile_rows(r8 * 8 + dr), :],
                                          sem.at[buf]).start()
            return carry
        lax.fori_loop(0, tm // 8, issue, 0)

    @pl.when(i == 0)
    def _():
        gather(0, 0)

    @pl.when(i + 1 < n_tiles)
    def _():
        gather(i + 1, (i + 1) % 2)

    buf = i % 2
    for k in range(EXPERT_TOPK):
        pltpu.make_async_copy(yb_hbm.at[tile_rows(0, tm), :], rows.at[buf, k], sem.at[buf]).wait()

    g = gates_ref[...]
    y = (_unpack_pairs(_load_row_tiles(rows.at[buf, 0], 0, tm)) * g[:, 0:1]
         + _unpack_pairs(_load_row_tiles(rows.at[buf, 1], 0, tm)) * g[:, 1:2])
    out_ref[...] = x1_ref[...] + g2_ref[0:1, :] * (_rms(y) * nw_ref[3:4, :])


def _combine(s0, s1, yb, gates, x1, norm_w, mod):
    n, d = x1.shape
    tm = 256
    grid_spec = pltpu.PrefetchScalarGridSpec(
        num_scalar_prefetch=2,
        grid=(n // tm,),
        in_specs=[
            pl.BlockSpec(memory_space=pl.ANY),
            pl.BlockSpec((tm, EXPERT_TOPK), lambda i, *_: (i, 0)),
            pl.BlockSpec((tm, d), lambda i, *_: (i, 0)),
            pl.BlockSpec((4, d), lambda i, *_: (0, 0)),
            pl.BlockSpec((8, d), lambda i, *_: (0, 5)),
        ],
        out_specs=pl.BlockSpec((tm, d), lambda i, *_: (i, 0)),
        scratch_shapes=[
            pltpu.VMEM((2, EXPERT_TOPK, tm * TILE_ROWS, LANES), jnp.uint32),
            pltpu.SemaphoreType.DMA((2,)),
        ],
    )
    return pl.pallas_call(
        _combine_kernel,
        grid_spec=grid_spec,
        out_shape=jax.ShapeDtypeStruct((n, d), F32),
        compiler_params=_cparams(1, 40),
        name="combine",
    )(s0, s1, yb, gates, x1, norm_w, mod)


def _rope_tables(n):
    rows = n // GRID_W
    r = np.repeat(np.arange(rows, dtype=np.float32), GRID_W)
    cidx = np.tile(np.arange(GRID_W, dtype=np.float32), rows)
    n_freq = RET_DK // 4
    inv = np.float32(ROPE_BASE) ** (-np.arange(n_freq, dtype=np.float32) / np.float32(n_freq))
    ang = np.concatenate([r[:, None] * inv, cidx[:, None] * inv], axis=-1).astype(np.float32)
    cos, sin = np.cos(ang), np.sin(ang)
    cs = np.concatenate([cos, cos], axis=-1)
    sn = np.concatenate([-sin, sin], axis=-1)
    return jnp.asarray(cs, F32), jnp.asarray(sn, F32)


def kernel(x, c, ctx, c_ctx, w_mod, b_mod, norm_w, w_in, hg_lb_logits, hg_norm_w, ret_decay_logits,
           ret_norm_w, w_out, router_group_w, router_group_b, router_expert_w, router_expert_b,
           w_gate, w_up, w_down):
    bsz, n, d = x.shape
    l = ctx.shape[1]
    assert bsz == 1 and w_mod.shape[0] == 1, "single sample, single layer"
    x2d, ctx2d = x[0], ctx[0]

    cc = jnp.zeros((8, d), F32).at[0].set(c[0]).at[1].set(c_ctx)
    mod = _modulation(cc, w_mod[0], b_mod[0])

    lb = jnp.cumsum(jax.nn.softmax(hg_lb_logits.astype(F32), axis=0), axis=0)[0]
    rk0 = 5 * HG_WIDTH + RET_HEADS * RET_DK
    cols = jnp.stack([
        jnp.zeros((PROJ_WIDTH,), F32).at[HG_WIDTH:3 * HG_WIDTH].set(lb.reshape(-1)),
        jnp.ones((PROJ_WIDTH,), F32).at[rk0:rk0 + RET_HEADS * RET_DK].set(RET_DK ** -0.5),
    ])
    pc, gc, w_in_bf = _projection(ctx2d, norm_w[0], mod, w_in[0], cols, 1, l, "proj_ctx")
    p, g = _projection(x2d, norm_w[0], mod, w_in_bf, cols, 0, 1024, "proj_x")

    ret_lg = jax.nn.log_sigmoid(ret_decay_logits[0].astype(F32))
    cs_tab, sn_tab = _rope_tables(n)

    o_f, o_b = _hgrn_scan(p, g, pc, gc)
    r_f, r_b = _ret_scan(p, pc, ret_lg, cs_tab, sn_tab)

    wr = jnp.concatenate([router_group_w[0], router_expert_w[0]], axis=1)
    wr = jnp.pad(wr, ((0, 0), (0, ROUTER_LANES - wr.shape[1])))
    wr_hi = wr.astype(BF16)
    wr_parts = jnp.concatenate([wr_hi, (wr - wr_hi.astype(F32)).astype(BF16)], axis=1)
    br = jnp.concatenate([router_group_b[0], router_expert_b[0]]).astype(F32)
    br = jnp.pad(br, (0, ROUTER_LANES - br.shape[0])).reshape(1, ROUTER_LANES)
    x1, h2, logits = _out_projection(
        o_f, o_b, r_f, r_b, p, x2d, w_out[0].astype(BF16), hg_norm_w, ret_norm_w, norm_w[0], mod,
        wr_parts, br)

    n_assign = n * EXPERT_TOPK
    n_slots = n_assign + N_EXPERTS * MOE_PAD
    n_items = (n_slots + N_EXPERTS * (MOE_ITEM_ROWS - MOE_PAD)) // MOE_ITEM_ROWS
    gates, s0, s1, counts = _router(logits)
    s0, s1 = s0.reshape(-1), s1.reshape(-1)
    slot_tok, item_e, item_start, item_rows, item_fill = _plan(counts, s0, s1, n_slots, n_items)

    yb = _moe_ffn(h2, w_gate[0], w_up[0], w_down[0], item_e, item_start, item_rows, item_fill,
                  slot_tok, n_slots)
    out = _combine(s0, s1, yb, gates, x1, norm_w[0], mod)
    return out[None]
```

```python
import functools

import jax
import jax.numpy as jnp
import numpy as np
from jax import lax
from jax.experimental import pallas as pl
from jax.experimental.pallas import tpu as pltpu

F32 = jnp.float32
BF16 = jnp.bfloat16

GRID_W = 64
HG_HEADS = 8
HG_DK = 128
HG_WIDTH = 1024
RET_HEADS = 4
RET_DK = 128
RET_DV = 256
RET_WIDTH = 1024
PROJ_WIDTH = 8192
N_GROUPS = 4
EXPERTS_PER_GROUP = 8
N_EXPERTS = 32
EXPERT_TOPK = 2
D_EXPERT = 1024
RMS_EPS = 1e-6
ROPE_BASE = 10000.0

COL_HQ, COL_HI, COL_HGATE = 0, 3, 4
COL_RQ_512, COL_RK_512 = 10, 11
COL_RV, COL_RGATE = 6, 7

HG_CHUNK = 256
HG_STEP_HEADS = 8
HG_DIAG = 16
HG_LOG_CLAMP = 10.0
RET_CHUNK = 256

MOE_PAD = 128
MOE_BLOCK = 256
MOE_ITEM_ROWS = 1024
MOE_GATHER_GROUP = 128
ROUTER_LANES = 128
LANES = 128
TILE_ROWS = 8


def _cparams(n_axes, vmem_mb, **flags):
    return pltpu.CompilerParams(
        dimension_semantics=("arbitrary",) * n_axes,
        vmem_limit_bytes=vmem_mb * 1024 * 1024,
        flags=flags or None,
    )


def _sigmoid(x):
    return 1.0 / (1.0 + jnp.exp(-x))


def _dot(a, b):
    return jnp.dot(a, b, preferred_element_type=F32)


def _dot_nt(a, b):
    return lax.dot_general(a, b, (((1,), (1,)), ((), ())), preferred_element_type=F32)


def _dot_tn(a, b):
    return lax.dot_general(a, b, (((0,), (0,)), ((), ())), preferred_element_type=F32)


def _mod_kernel(cc_ref, w_ref, b_ref, o_ref):
    s = cc_ref[...]
    s = s * _sigmoid(s)
    s16 = jnp.concatenate([s, s], axis=0)
    hi = s16.astype(BF16).astype(F32)
    row = lax.broadcasted_iota(jnp.int32, s16.shape, 0)
    lhs = jnp.where(row < 8, hi, s16 - hi).astype(BF16)
    r = _dot(lhs, w_ref[...].astype(BF16))
    o_ref[...] = r[:8] + r[8:] + b_ref[...]


def _modulation(cc, w_mod, b_mod):
    d, n = w_mod.shape
    tn = 1024
    return pl.pallas_call(
        _mod_kernel,
        grid=(n // tn,),
        in_specs=[
            pl.BlockSpec((8, d), lambda j: (0, 0)),
            pl.BlockSpec((d, tn), lambda j: (0, j)),
            pl.BlockSpec((1, tn), lambda j: (0, j)),
        ],
        out_specs=pl.BlockSpec((8, tn), lambda j: (0, j)),
        out_shape=jax.ShapeDtypeStruct((8, n), F32),
        compiler_params=_cparams(1, 40),
        name="mod",
    )(cc, w_mod, b_mod.reshape(1, n))


PROJ_TN = 1024
PROLOGUE_ROWS = 32


def _proj_kernel(x_ref, nw_ref, sh_ref, sc_ref, w_ref, cols_ref, o_ref, g_ref, *rest, mod_row):
    h_scr = rest[-1]
    if len(rest) == 2:
        rest[0][...] = w_ref[...].astype(BF16)
        w_ref = rest[0]
    i = pl.program_id(0)
    j = pl.program_id(1)
    last_j = pl.num_programs(1) - 1
    cur = i % 2

    def normed(x):
        scale = nw_ref[0:1, :] * (1.0 + sc_ref[mod_row:mod_row + 1, :])
        shift = sh_ref[mod_row:mod_row + 1, :]
        y = x * lax.rsqrt(jnp.mean(x * x, axis=-1, keepdims=True) + RMS_EPS)
        return (y * scale + shift).astype(BF16)

    @pl.when(jnp.logical_and(i == 0, j == 0))
    def _():
        def body(r, carry):
            r0 = pl.multiple_of(r * PROLOGUE_ROWS, PROLOGUE_ROWS)
            h_scr[0, pl.ds(r0, PROLOGUE_ROWS), :] = normed(x_ref[pl.ds(r0, PROLOGUE_ROWS), :])
            return carry
        lax.fori_loop(0, x_ref.shape[0] // PROLOGUE_ROWS, body, 0)

    is_gate = jnp.logical_or(j == 1, j == 2)
    is_silu = jnp.logical_or(j == 0, j == 4)
    is_lin = jnp.logical_not(jnp.logical_or(jnp.logical_or(is_gate, is_silu), j == last_j))

    @pl.when(is_lin)
    def _():
        o_ref[...] = (_dot(h_scr[cur], w_ref[...]) * cols_ref[1:2, :]).astype(BF16)

    @pl.when(is_silu)
    def _():
        acc = _dot(h_scr[cur], w_ref[...])
        o_ref[...] = (acc * _sigmoid(acc)).astype(BF16)

    @pl.when(is_gate)
    def _():
        acc = _dot(h_scr[cur], w_ref[...])
        lb = cols_ref[0:1, :]
        g = jnp.maximum(jnp.log(lb + (1.0 - lb) * _sigmoid(acc)), -HG_LOG_CLAMP)
        g_ref[...] = g
        o_ref[...] = g.astype(BF16)

    @pl.when(j == last_j)
    def _():
        acc = _dot(h_scr[cur], w_ref[...])
        o_ref[...] = (acc * _sigmoid(acc)).astype(BF16)
        h_scr[1 - cur] = normed(x_ref[...])


def _projection(x2d, norm_w, mod, w_in, cols, mod_row, tm, name):
    n, d = x2d.shape
    tn = PROJ_TN
    n_i, n_j = n // tm, PROJ_WIDTH // tn
    emit_w = w_in.dtype != BF16
    assert not emit_w or n_i == 1, "the weight copy is written once per column tile"

    def x_map(i, j):
        return (jnp.minimum(i + (j == n_j - 1).astype(jnp.int32), n_i - 1), 0)

    return pl.pallas_call(
        functools.partial(_proj_kernel, mod_row=mod_row),
        grid=(n_i, n_j),
        in_specs=[
            pl.BlockSpec((tm, d), x_map),
            pl.BlockSpec((4, d), lambda i, j: (0, 0)),
            pl.BlockSpec((8, d), lambda i, j: (0, 0)),
            pl.BlockSpec((8, d), lambda i, j: (0, 1)),
            pl.BlockSpec((d, tn), lambda i, j: (0, j)),
            pl.BlockSpec((2, tn), lambda i, j: (0, j)),
        ],
        out_specs=[
            pl.BlockSpec((tm, tn), lambda i, j: (i, j)),
            pl.BlockSpec((tm, tn), lambda i, j: (i, jnp.clip(j - 1, 0, 1))),
            *([pl.BlockSpec((d, tn), lambda i, j: (0, j))] if emit_w else []),
        ],
        out_shape=[
            jax.ShapeDtypeStruct((n, PROJ_WIDTH), BF16),
            jax.ShapeDtypeStruct((n, 2 * HG_WIDTH), F32),
            *([jax.ShapeDtypeStruct(w_in.shape, BF16)] if emit_w else []),
        ],
        scratch_shapes=[pltpu.VMEM((2, tm, d), BF16)],
        compiler_params=_cparams(2, 56),
        name=name,
    )(x2d, norm_w, mod, mod, w_in, cols)


def _tri_ones(c, rev):
    t = lax.broadcasted_iota(jnp.int32, (c, c), 0)
    s = lax.broadcasted_iota(jnp.int32, (c, c), 1)
    return jnp.where((s >= t) if rev else (s <= t), 1.0, 0.0).astype(BF16)


def _cumsum_rows(g, tri):
    hi = g.astype(BF16)
    lo = (g - hi.astype(F32)).astype(BF16)
    return _dot(tri, hi) + _dot(tri, lo)


def _row_refs(b, idxs, blk):
    parts = [jnp.broadcast_to(b[i:i + 1, :], (blk, b.shape[1])) for i in idxs]
    return parts[0] if len(parts) == 1 else jnp.concatenate(parts, axis=0)


def _hgrn_levels(c, rev):
    t = lax.broadcasted_iota(jnp.int32, (c, c), 0)
    s = lax.broadcasted_iota(jnp.int32, (c, c), 1)
    levels = []
    m = c // 2
    while m >= HG_DIAG:
        blk = 2 * m
        same = ((t ^ s) & ~(blk - 1)) == 0
        if rev:
            idxs = [b0 + m for b0 in range(0, c, blk)]
            cross = jnp.logical_and((t & m) == 0, (s & m) != 0)
        else:
            idxs = [b0 + m - 1 for b0 in range(0, c, blk)]
            cross = jnp.logical_and((t & m) != 0, (s & m) == 0)
        levels.append((idxs, blk, jnp.logical_and(same, cross)))
        m //= 2
    blk = HG_DIAG
    same = ((t ^ s) & ~(blk - 1)) == 0
    if rev:
        idxs = [b0 + blk // 2 for b0 in range(0, c, blk)]
        mask = jnp.logical_and(same, s >= t)
    else:
        idxs = [b0 + blk // 2 - 1 for b0 in range(0, c, blk)]
        mask = jnp.logical_and(same, s <= t)
    levels.append((idxs, blk, mask))
    return levels


def _hgrn_head(q_bf, g, b, v_bf, st, rev, levels):
    c = g.shape[0]
    kk = 1.0 - jnp.exp(g)

    o = None
    if levels is not None:
        q = q_bf.astype(F32)
        att = jnp.zeros((c, c), F32)
        for idxs, blk, mask in levels:
            ref = _row_refs(b, idxs, blk)
            a_l = _dot_nt((q * jnp.exp(b - ref)).astype(BF16),
                          (kk * jnp.exp(ref - b)).astype(BF16))
            att = att + jnp.where(mask, a_l, 0.0)
        o = _dot(att.astype(BF16), v_bf) + _dot_nt((q * jnp.exp(b)).astype(BF16), st.astype(BF16))

    b_end = b[0:1, :] if rev else b[c - 1:c, :]
    k_end = (kk * jnp.exp(b_end - b)).astype(BF16)
    st_new = st * jnp.exp(b_end) + _dot_tn(v_bf, k_end)
    return o, st_new


def _hgrn_ctx_kernel(gf_ref, vf_ref, gb_ref, vb_ref, st_ref):
    @pl.when(pl.program_id(0) == 0)
    def _():
        st_ref[...] = jnp.zeros_like(st_ref)

    c = gf_ref.shape[0]
    bf = _cumsum_rows(gf_ref[...], _tri_ones(c, False))
    bb = _cumsum_rows(gb_ref[...], _tri_ones(c, True))
    for h in range(HG_HEADS):
        sl = slice(h * HG_DK, (h + 1) * HG_DK)
        _, st_ref[h] = _hgrn_head(None, gf_ref[:, sl], bf[:, sl], vf_ref[:, sl], st_ref[h],
                                  False, None)
        hb = HG_HEADS + h
        _, st_ref[hb] = _hgrn_head(None, gb_ref[:, sl], bb[:, sl], vb_ref[:, sl], st_ref[hb],
                                   True, None)


def _hgrn_kernel(qf_ref, gf_ref, vf_ref, qb_ref, gb_ref, vb_ref, st0_ref, of_ref, ob_ref, st_ref):
    hg = pl.program_id(1)

    @pl.when(jnp.logical_and(pl.program_id(0) == 0, hg == 0))
    def _():
        st_ref[...] = st0_ref[...]

    c = gf_ref.shape[0]
    lev_f = _hgrn_levels(c, False)
    lev_b = _hgrn_levels(c, True)
    bf = _cumsum_rows(gf_ref[...], _tri_ones(c, False))
    bb = _cumsum_rows(gb_ref[...], _tri_ones(c, True))
    for h in range(HG_STEP_HEADS):
        sl = slice(h * HG_DK, (h + 1) * HG_DK)
        hf = hg * HG_STEP_HEADS + h
        of_ref[:, sl], st_ref[hf] = _hgrn_head(qf_ref[:, sl], gf_ref[:, sl], bf[:, sl],
                                               vf_ref[:, sl], st_ref[hf], False, lev_f)
        hb = HG_HEADS + hf
        ob_ref[:, sl], st_ref[hb] = _hgrn_head(qb_ref[:, sl], gb_ref[:, sl], bb[:, sl],
                                               vb_ref[:, sl], st_ref[hb], True, lev_b)


def _hgrn_scan(p, g, pc, gc):
    c = HG_CHUNK
    w = HG_WIDTH
    n, l = p.shape[0], pc.shape[0]
    c_ctx = min(c, l)
    n_x, n_ctx = n // c, l // c_ctx
    st_shape = (2 * HG_HEADS, HG_DK, HG_DK)

    def spec(chunk_fn, col):
        return pl.BlockSpec((c_ctx, w), lambda s: (chunk_fn(s), col))

    st0 = pl.pallas_call(
        _hgrn_ctx_kernel,
        grid=(n_ctx,),
        in_specs=[
            spec(lambda s: s, 0), spec(lambda s: s, COL_HI),
            spec(lambda s: n_ctx - 1 - s, 1), spec(lambda s: n_ctx - 1 - s, COL_HI),
        ],
        out_specs=pl.BlockSpec(st_shape, lambda s: (0, 0, 0)),
        out_shape=jax.ShapeDtypeStruct(st_shape, F32),
        compiler_params=_cparams(1, 32),
        name="hgrn_ctx",
    )(gc, pc, gc, pc)

    n_hg = HG_HEADS // HG_STEP_HEADS
    wg = HG_STEP_HEADS * HG_DK
    fwd = lambda s: s
    bwd = lambda s: n_x - 1 - s

    def gspec(chunk_fn, col):
        return pl.BlockSpec((c, wg), lambda s, hg: (chunk_fn(s), col * n_hg + hg))

    assert n_hg == 1, "the fused latent scan walks all heads per step"
    return dict(
        kernel=_hgrn_kernel,
        grid=(n_x, n_hg),
        in_specs=[
            gspec(fwd, COL_HQ), gspec(fwd, 0), gspec(fwd, COL_HI),
            gspec(bwd, COL_HQ), gspec(bwd, 1), gspec(bwd, COL_HI),
            pl.BlockSpec(st_shape, lambda s, hg: (0, 0, 0)),
        ],
        out_specs=[gspec(fwd, 0), gspec(bwd, 0)],
        out_shape=[jax.ShapeDtypeStruct((n, w), F32)] * 2,
        scratch_shapes=[pltpu.VMEM(st_shape, F32)],
        args=(p, g, p, p, g, p, st0),
    )


def _rope(x_bf, cs, sn):
    x = x_bf.astype(F32)
    return x * cs + pltpu.roll(x, RET_DK // 2, axis=1) * sn


def _ret_decays(lg, c, rev):
    t = lax.broadcasted_iota(jnp.int32, (c, c), 0)
    s = lax.broadcasted_iota(jnp.int32, (c, c), 1)
    diff = (s - t) if rev else (t - s)
    pair = jnp.where(diff >= 0, jnp.exp(lg * jnp.maximum(diff, 0).astype(F32)), 0.0)
    pos = lax.broadcasted_iota(jnp.int32, (c, RET_DK), 0).astype(F32)
    q_dec = jnp.exp(lg * ((c - pos) if rev else (pos + 1.0)))
    k_dec = jnp.exp(lg * (pos if rev else (c - 1.0 - pos)))
    chunk = jnp.exp(jnp.full((8, RET_DK), lg * c, F32))
    return pair, q_dec, k_dec, chunk


def _ret_head(q, k, v_bf, decays, st):
    pair, q_dec, k_dec, chunk = decays
    o = None
    if q is not None:
        att = _dot_nt(q.astype(BF16), k.astype(BF16)) * pair
        o = _dot(att.astype(BF16), v_bf) + _dot_nt((q * q_dec).astype(BF16), st.astype(BF16))
    st_new = st * chunk[0:1, :] + _dot_tn(v_bf, (k * k_dec).astype(BF16))
    return o, st_new


def _ret_ctx_kernel(lg_ref, kf_ref, vf_ref, kb_ref, vb_ref, st_ref):
    @pl.when(pl.program_id(0) == 0)
    def _():
        st_ref[...] = jnp.zeros_like(st_ref)

    c = kf_ref.shape[0]
    for h in range(RET_HEADS):
        sk = slice(h * RET_DK, (h + 1) * RET_DK)
        sv = slice(h * RET_DV, (h + 1) * RET_DV)
        _, st_ref[h] = _ret_head(None, kf_ref[:, sk].astype(F32), vf_ref[:, sv],
                                 _ret_decays(lg_ref[0, h], c, False), st_ref[h])
        hb = RET_HEADS + h
        _, st_ref[hb] = _ret_head(None, kb_ref[:, sk].astype(F32), vb_ref[:, sv],
                                  _ret_decays(lg_ref[1, h], c, True), st_ref[hb])


def _ret_kernel(lg_ref, qf_ref, kf_ref, vf_ref, csf_ref, snf_ref,
                qb_ref, kb_ref, vb_ref, csb_ref, snb_ref, st0_ref, of_ref, ob_ref,
                st_ref, pair_ref, qd_ref, kd_ref, cd_ref):
    c = qf_ref.shape[0]

    @pl.when(pl.program_id(0) == 0)
    def _():
        st_ref[...] = st0_ref[...]
        for d in range(2):
            for h in range(RET_HEADS):
                i = d * RET_HEADS + h
                pair_ref[i], qd_ref[i], kd_ref[i], cd_ref[i] = _ret_decays(lg_ref[d, h], c, d == 1)

    csf, snf = csf_ref[...], snf_ref[...]
    csb, snb = csb_ref[...], snb_ref[...]
    for h in range(RET_HEADS):
        sk = slice(h * RET_DK, (h + 1) * RET_DK)
        sv = slice(h * RET_DV, (h + 1) * RET_DV)
        of_ref[:, sv], st_ref[h] = _ret_head(
            _rope(qf_ref[:, sk], csf, snf), _rope(kf_ref[:, sk], csf, snf), vf_ref[:, sv],
            (pair_ref[h], qd_ref[h], kd_ref[h], cd_ref[h]), st_ref[h])
        hb = RET_HEADS + h
        ob_ref[:, sv], st_ref[hb] = _ret_head(
            _rope(qb_ref[:, sk], csb, snb), _rope(kb_ref[:, sk], csb, snb), vb_ref[:, sv],
            (pair_ref[hb], qd_ref[hb], kd_ref[hb], cd_ref[hb]), st_ref[hb])


def _ret_scan(p, pc, ret_lg, cs_tab, sn_tab):
    c = RET_CHUNK
    n, l = p.shape[0], pc.shape[0]
    n_x, n_ctx = n // c, l // c
    qk_w = RET_HEADS * RET_DK
    st_shape = (2 * RET_HEADS, RET_DV, RET_DK)
    smem = pl.BlockSpec(memory_space=pltpu.SMEM)

    def kv_specs(chunk_fn):
        return [
            pl.BlockSpec((c, qk_w), lambda s: (chunk_fn(s), COL_RK_512)),
            pl.BlockSpec((c, RET_WIDTH), lambda s: (chunk_fn(s), COL_RV)),
        ]

    st0 = pl.pallas_call(
        _ret_ctx_kernel,
        grid=(n_ctx,),
        in_specs=[smem, *kv_specs(lambda s: s), *kv_specs(lambda s: n_ctx - 1 - s)],
        out_specs=pl.BlockSpec(st_shape, lambda s: (0, 0, 0)),
        out_shape=jax.ShapeDtypeStruct(st_shape, F32),
        compiler_params=_cparams(1, 32),
        name="ret_ctx",
    )(ret_lg, pc, pc, pc, pc)

    def specs(chunk_fn):
        return [
            pl.BlockSpec((c, qk_w), lambda s, hg: (chunk_fn(s), COL_RQ_512)),
            pl.BlockSpec((c, qk_w), lambda s, hg: (chunk_fn(s), COL_RK_512)),
            pl.BlockSpec((c, RET_WIDTH), lambda s, hg: (chunk_fn(s), COL_RV)),
            pl.BlockSpec((c, RET_DK), lambda s, hg: (chunk_fn(s), 0)),
            pl.BlockSpec((c, RET_DK), lambda s, hg: (chunk_fn(s), 0)),
        ]

    fwd = lambda s: s
    bwd = lambda s: n_x - 1 - s
    return dict(
        kernel=_ret_kernel,
        grid=(n_x, 1),
        in_specs=[smem, *specs(fwd), *specs(bwd),
                  pl.BlockSpec(st_shape, lambda s, hg: (0, 0, 0))],
        out_specs=[
            pl.BlockSpec((c, RET_WIDTH), lambda s, hg: (fwd(s), 0)),
            pl.BlockSpec((c, RET_WIDTH), lambda s, hg: (bwd(s), 0)),
        ],
        out_shape=[jax.ShapeDtypeStruct((n, RET_WIDTH), F32)] * 2,
        scratch_shapes=[
            pltpu.VMEM(st_shape, F32),
            pltpu.VMEM((2 * RET_HEADS, c, c), F32),
            pltpu.VMEM((2 * RET_HEADS, c, RET_DK), F32),
            pltpu.VMEM((2 * RET_HEADS, c, RET_DK), F32),
            pltpu.VMEM((2 * RET_HEADS, 8, RET_DK), F32),
        ],
        args=(ret_lg, p, p, p, cs_tab, sn_tab, p, p, p, cs_tab, sn_tab, st0),
    )


def _fused_scan_kernel(*refs, n_in, n_scr):
    ins, outs, scr = refs[:sum(n_in)], refs[sum(n_in):sum(n_in) + 4], refs[sum(n_in) + 4:]
    _hgrn_kernel(*ins[:n_in[0]], *outs[:2], *scr[:n_scr[0]])
    _ret_kernel(*ins[n_in[0]:], *outs[2:], *scr[n_scr[0]:])


def _latent_scans(hg, rt):
    assert hg["grid"] == rt["grid"], "both scans must walk the same chunks"
    return pl.pallas_call(
        functools.partial(_fused_scan_kernel,
                          n_in=(len(hg["in_specs"]), len(rt["in_specs"])),
                          n_scr=(len(hg["scratch_shapes"]), len(rt["scratch_shapes"]))),
        grid=hg["grid"],
        in_specs=[*hg["in_specs"], *rt["in_specs"]],
        out_specs=[*hg["out_specs"], *rt["out_specs"]],
        out_shape=[*hg["out_shape"], *rt["out_shape"]],
        scratch_shapes=[*hg["scratch_shapes"], *rt["scratch_shapes"]],
        compiler_params=_cparams(2, 48),
        name="scans",
    )(*hg["args"], *rt["args"])


def _rms(y):
    return y * lax.rsqrt(jnp.mean(y * y, axis=-1, keepdims=True) + RMS_EPS)


def _outproj_kernel(of_ref, ob_ref, rf_ref, rb_ref, hgate_ref, rgate_ref, x_ref, wout_ref,
                    hgw_ref, rnw_ref, nw_ref, g1_ref, sh2_ref, sc2_ref, wr_ref, br_ref,
                    x1_ref, h2_ref, lg_ref, cat_scr):
    for h in range(HG_HEADS):
        sl = slice(h * HG_DK, (h + 1) * HG_DK)
        o = of_ref[:, sl] + ob_ref[:, sl]
        cat_scr[:, sl] = (_rms(o) * hgw_ref[0:1, sl] * hgate_ref[:, sl].astype(F32)).astype(BF16)
    for h in range(RET_HEADS):
        sl = slice(h * RET_DV, (h + 1) * RET_DV)
        r = rf_ref[:, sl] + rb_ref[:, sl]
        cat_scr[:, HG_WIDTH + h * RET_DV:HG_WIDTH + (h + 1) * RET_DV] = (
            _rms(r) * rnw_ref[0:1, sl] * rgate_ref[:, sl].astype(F32)).astype(BF16)

    y = _dot(cat_scr[...], wout_ref[...])
    x1 = x_ref[...] + g1_ref[0:1, :] * (_rms(y) * nw_ref[1:2, :])
    x1_ref[...] = x1
    h2 = _rms(x1) * nw_ref[2:3, :] * (1.0 + sc2_ref[0:1, :]) + sh2_ref[0:1, :]
    _store_row_tiles(h2_ref, _pack_pairs(h2))
    hi = h2.astype(BF16)
    lo = (h2 - hi.astype(F32)).astype(BF16)
    parts = _dot(jnp.concatenate([hi, lo], axis=0), wr_ref[...])
    half = hi.shape[0]
    lg_ref[...] = (parts[:half, :ROUTER_LANES] + parts[half:, :ROUTER_LANES]
                   + parts[:half, ROUTER_LANES:] + br_ref[...])


def _out_projection(o_f, o_b, r_f, r_b, p, x2d, w_out_bf, hg_norm_w, ret_norm_w, norm_w, mod,
                    wr_parts, br):
    n, d = x2d.shape
    tm = 256
    row = lambda w: pl.BlockSpec((tm, w), lambda i: (i, 0))
    full = lambda a: pl.BlockSpec(a.shape, lambda i: (0,) * a.ndim)
    modcol = lambda k: pl.BlockSpec((8, d), lambda i: (0, k))
    return pl.pallas_call(
        _outproj_kernel,
        grid=(n // tm,),
        in_specs=[
            row(HG_WIDTH), row(HG_WIDTH), row(RET_WIDTH), row(RET_WIDTH),
            pl.BlockSpec((tm, HG_WIDTH), lambda i: (i, COL_HGATE)),
            pl.BlockSpec((tm, RET_WIDTH), lambda i: (i, COL_RGATE)),
            row(d), full(w_out_bf), full(hg_norm_w), full(ret_norm_w), full(norm_w),
            modcol(2), modcol(3), modcol(4),
            full(wr_parts), full(br),
        ],
        out_specs=[row(d), pl.BlockSpec((tm * TILE_ROWS, LANES), lambda i: (i, 0)),
                   row(ROUTER_LANES)],
        out_shape=[
            jax.ShapeDtypeStruct((n, d), F32),
            jax.ShapeDtypeStruct((n * TILE_ROWS, LANES), jnp.uint32),
            jax.ShapeDtypeStruct((n, ROUTER_LANES), F32),
        ],
        scratch_shapes=[pltpu.VMEM((tm, d), BF16)],
        compiler_params=_cparams(1, 48),
        name="outproj",
    )(o_f, o_b, r_f, r_b, p, p, x2d, w_out_bf, hg_norm_w, ret_norm_w, norm_w, mod, mod, mod,
      wr_parts, br)


ROUTER_TM = 1024


def _lane_first(hit, lane_f):
    return jnp.min(jnp.where(hit, lane_f, 1e9), axis=1, keepdims=True)


def _router_kernel(lg_ref, gates_ref, s0_ref, s1_ref, cnt_ref, ltri, carry, pstart):
    ph = pl.program_id(0)
    i = pl.program_id(1)
    tm = lg_ref.shape[0]

    @pl.when(jnp.logical_and(ph == 0, i == 0))
    def _():
        t = lax.broadcasted_iota(jnp.int32, (tm, tm), 0)
        s = lax.broadcasted_iota(jnp.int32, (tm, tm), 1)
        ltri[...] = jnp.where(s < t, 1.0, 0.0).astype(BF16)

    @pl.when(jnp.logical_and(ph == 1, i == 0))
    def _():
        cnt = carry[...]
        padded = jnp.floor((cnt + (MOE_PAD - 1.0)) * (1.0 / MOE_PAD)) * MOE_PAD
        r = lax.broadcasted_iota(jnp.int32, (LANES, LANES), 0)
        c = lax.broadcasted_iota(jnp.int32, (LANES, LANES), 1)
        before = jnp.where(r < c, 1.0, 0.0).astype(BF16)
        pstart[...] = _dot(padded.astype(BF16), before)
        cnt_ref[...] = cnt.astype(jnp.int32)

    @pl.when(i == 0)
    def _():
        carry[...] = jnp.zeros_like(carry)

    lg = lg_ref[...]
    lane = lax.broadcasted_iota(jnp.int32, lg.shape, 1)
    lane_f = lane.astype(F32)
    neg = -jnp.inf

    gl = jnp.where(lane < N_GROUPS, lg, neg)
    gmax = jnp.max(gl, axis=1, keepdims=True)
    grp = _lane_first(gl == gmax, lane_f).astype(jnp.int32)
    p_sel = 1.0 / jnp.sum(jnp.exp(gl - gmax), axis=1, keepdims=True)

    in_grp = jnp.logical_and(
        jnp.logical_and(lane >= N_GROUPS, lane < N_GROUPS + N_EXPERTS),
        ((lane - N_GROUPS) >> (EXPERTS_PER_GROUP.bit_length() - 1)) == grp)
    el = jnp.where(in_grp, lg, neg)
    v1 = jnp.max(el, axis=1, keepdims=True)
    o1 = lane_f == _lane_first(el == v1, lane_f)
    el2 = jnp.where(o1, neg, el)
    v2 = jnp.max(el2, axis=1, keepdims=True)
    o2 = lane_f == _lane_first(el2 == v2, lane_f)

    osum = (jnp.where(o1, 1.0, 0.0) + jnp.where(o2, 1.0, 0.0)).astype(BF16)
    earlier = _dot(ltri[...], osum) + carry[0:1, :]
    carry[...] = carry[...] + _dot(jnp.ones((8, tm), BF16), osum)

    @pl.when(ph == 1)
    def _():
        ez = jnp.exp(v2 - v1)
        g1 = p_sel / (1.0 + ez)
        gates_ref[...] = jnp.concatenate([g1, g1 * ez], axis=1)
        base = earlier + pstart[0:1, :]
        for o, s_ref in ((o1, s0_ref), (o2, s1_ref)):
            col = jnp.sum(jnp.where(o, base, 0.0), axis=1, keepdims=True)
            wide = jnp.broadcast_to(col, (tm, LANES))
            for b in range(tm // LANES):
                s_ref[b:b + 1, :] = wide[b * LANES:(b + 1) * LANES, :].T[0:1, :].astype(jnp.int32)


def _router(logits):
    t = logits.shape[0]
    tm = ROUTER_TM
    rows = tm // LANES
    return pl.pallas_call(
        _router_kernel,
        grid=(2, t // tm),
        in_specs=[pl.BlockSpec((tm, LANES), lambda ph, i: (i, 0))],
        out_specs=[
            pl.BlockSpec((tm, EXPERT_TOPK), lambda ph, i: (i * ph, 0)),
            pl.BlockSpec((rows, LANES), lambda ph, i: (i * ph, 0)),
            pl.BlockSpec((rows, LANES), lambda ph, i: (i * ph, 0)),
            pl.BlockSpec((8, LANES), lambda ph, i: (0, 0)),
        ],
        out_shape=[
            jax.ShapeDtypeStruct((t, EXPERT_TOPK), F32),
            jax.ShapeDtypeStruct((t // LANES, LANES), jnp.int32),
            jax.ShapeDtypeStruct((t // LANES, LANES), jnp.int32),
            jax.ShapeDtypeStruct((8, LANES), jnp.int32),
        ],
        scratch_shapes=[
            pltpu.VMEM((tm, tm), BF16),
            pltpu.VMEM((8, LANES), F32),
            pltpu.VMEM((8, LANES), F32),
        ],
        compiler_params=_cparams(2, 32),
        name="router",
    )(logits)


def _round_up_pow2(x, m):
    assert m & (m - 1) == 0
    return (x + (m - 1)) & ~(m - 1)


def _plan_kernel(cnt_ref, s0_ref, s1_ref, tok_ref, ie_ref, ist_ref, ir_ref, if_ref, zeros, sem):
    n_tok = s0_ref.shape[0]
    n_slots = tok_ref.shape[0]
    n_items = ie_ref.shape[0]
    r = MOE_ITEM_ROWS

    zeros[...] = jnp.zeros_like(zeros)
    clear = pltpu.make_async_copy(zeros, tok_ref, sem)
    clear.start()
    clear.wait()

    def scatter(t16, carry):
        for d in range(16):
            t = t16 * 16 + d
            tok_ref[s0_ref[t]] = t
            tok_ref[s1_ref[t]] = t
        return carry
    lax.fori_loop(0, n_tok // 16, scatter, 0)

    a = jnp.int32(0)
    used = jnp.int32(0)
    e_last = jnp.int32(0)
    for e in range(N_EXPERTS):
        pad = _round_up_pow2(cnt_ref[0, N_GROUPS + e], MOE_PAD)

        def item(sub, a, e=e, pad=pad, used=used):
            ie_ref[a] = e
            ist_ref[a] = used + sub * r
            ir_ref[a] = jnp.minimum(pad - sub * r, r)
            if_ref[a] = 0
            return a + 1
        a = lax.fori_loop(0, _round_up_pow2(pad, r) >> (r.bit_length() - 1), item, a)
        e_last = jnp.where(pad > 0, e, e_last)
        used = used + pad

    def idle(a2, carry):
        fill_start = used + (a2 - a) * r
        fill = jnp.clip(n_slots - fill_start, 0, r)
        ie_ref[a2] = e_last
        ist_ref[a2] = jnp.where(fill > 0, fill_start, 0)
        ir_ref[a2] = 0
        if_ref[a2] = fill
        return carry
    lax.fori_loop(a, n_items, idle, 0)


def _plan(counts, s0, s1, n_slots, n_items):
    smem = pl.BlockSpec(memory_space=pltpu.SMEM)
    item = jax.ShapeDtypeStruct((n_items,), jnp.int32)
    return pl.pallas_call(
        _plan_kernel,
        in_specs=[smem, smem, smem],
        out_specs=[smem] * 5,
        out_shape=[jax.ShapeDtypeStruct((n_slots,), jnp.int32), item, item, item, item],
        scratch_shapes=[pltpu.VMEM((n_slots,), jnp.int32), pltpu.SemaphoreType.DMA(())],
        name="plan",
    )(counts, s0, s1)


def _unpack_pairs(u):
    lo = lax.bitcast_convert_type(u << 16, F32)
    hi = lax.bitcast_convert_type(u & jnp.uint32(0xFFFF0000), F32)
    return jnp.concatenate([lo, hi], axis=1)


def _store_row_tiles(dst, packed):
    r = packed.shape[0]
    for s in range(TILE_ROWS):
        dst[pl.ds(s, r, stride=TILE_ROWS), :] = packed[:, s * LANES:(s + 1) * LANES]


def _load_row_tiles(src, first_row, r):
    return jnp.concatenate(
        [src[pl.ds(first_row * TILE_ROWS + s, r, stride=TILE_ROWS), :] for s in range(TILE_ROWS)],
        axis=1)


def _pack_pairs(y):
    w = y.shape[1] // 2
    bits = lax.bitcast_convert_type(y.astype(BF16).astype(F32), jnp.uint32)
    return (bits[:, :w] >> 16) | (bits[:, w:] & jnp.uint32(0xFFFF0000))


def _moe_block_sizes():
    sizes, rows = [], MOE_BLOCK
    while rows >= MOE_PAD:
        sizes.append(rows)
        rows //= 2
    return sizes


def _moe_kernel(item_e, item_start, item_rows, item_fill, slot_tok,
                h2p_hbm, wg_ref, wu_ref, wd_ref, yb_hbm,
                xbuf, act, ystage, pf, gsem, osem):
    a = pl.program_id(0)
    t = pl.program_id(1)
    n_items = pl.num_programs(0)
    n = item_rows[a]
    start = item_start[a]
    half_w = wg_ref.shape[2]
    cur = a % 2
    nxt = 1 - cur
    a1 = jnp.minimum(a + 1, n_items - 1)
    n_next = jnp.where(a + 1 < n_items, item_rows[a1], 0)
    start_next = item_start[a1]

    def tile_rows(i, count=1):
        return pl.ds(pl.multiple_of(i * TILE_ROWS, TILE_ROWS), count * TILE_ROWS)

    def row_copy(tok, slot, i):
        return pltpu.make_async_copy(h2p_hbm.at[tile_rows(tok), :], xbuf.at[slot, tile_rows(i), :],
                                     gsem.at[slot])

    def start_row(tok, slot, i):
        row_copy(tok, slot, i).start()

    @pl.when(jnp.logical_and(a == 0, t == 0))
    def _():
        pf[1] = 0
        pf[2] = 0

        def issue(i, carry):
            start_row(slot_tok[start + i], 0, i)
            return carry
        lax.fori_loop(0, n, issue, 0)

    @pl.when(t == 0)
    def _():
        pf[0] = 0

        def drain(bk, carry):
            r0 = pl.multiple_of(bk * MOE_PAD, MOE_PAD)
            pltpu.make_async_copy(h2p_hbm.at[tile_rows(0, MOE_PAD), :],
                                  xbuf.at[cur, tile_rows(r0, MOE_PAD), :], gsem.at[cur]).wait()
            return carry
        lax.fori_loop(0, n // MOE_PAD, drain, 0)

    def prefetch_group():
        base = pf[0]
        for q in range(MOE_GATHER_GROUP):
            start_row(slot_tok[start_next + base + q], nxt, base + q)
        pf[0] = base + MOE_GATHER_GROUP

    def run_blocks(compute):
        def one(b, r0, rows):
            r0 = pl.multiple_of(r0, MOE_PAD)
            more = pf[0] < n_next

            def gather():
                prefetch_group()
                for _ in range(max(rows // (2 * MOE_GATHER_GROUP), 1) - 1):
                    pl.when(pf[0] < n_next)(prefetch_group)

            @pl.when(more)
            def _():
                compute(b, r0, rows, gather)

            @pl.when(jnp.logical_not(more))
            def _():
                compute(b, r0, rows, lambda: None)

        n_big = n // MOE_BLOCK

        def body(b, carry):
            one(b, b * MOE_BLOCK, MOE_BLOCK)
            return carry
        lax.fori_loop(0, n_big, body, 0)

        b, r0, rows = n_big, n_big * MOE_BLOCK, MOE_BLOCK // 2
        while rows >= MOE_PAD:
            has = (n & rows) != 0
            pl.when(has)(functools.partial(one, b, r0, rows))
            b = b + has.astype(jnp.int32)
            r0 = r0 + jnp.where(has, rows, 0)
            rows //= 2

    def gate_up(half):
        def compute(b, r0, rows, mid):
            xs = _unpack_pairs(_load_row_tiles(xbuf.at[cur], r0, rows)).astype(BF16)
            g = _dot(xs, wg_ref[0].astype(BF16))
            u = _dot(xs, wu_ref[0].astype(BF16))
            mid()
            act[pl.ds(r0, rows), half * half_w:(half + 1) * half_w] = (
                g * _sigmoid(g) * u).astype(BF16)
        run_blocks(compute)

    def out_copy(slot, r0, rows):
        return pltpu.make_async_copy(
            ystage.at[slot, tile_rows(0, rows), :],
            yb_hbm.at[tile_rows(pl.multiple_of(start + r0, MOE_PAD), rows), :],
            osem.at[slot])

    def wait_stage(slot):
        for rows in _moe_block_sizes():
            @pl.when(pf[1 + slot] == rows)
            def _(rows=rows):
                out_copy(slot, 0, rows).wait()
                pf[1 + slot] = 0

    def down():
        def compute(b, r0, rows, mid):
            slot = b % 2
            wait_stage(slot)
            y = _dot(act[pl.ds(r0, rows), :], wd_ref[0].astype(BF16))
            mid()
            _store_row_tiles(ystage.at[slot], _pack_pairs(y))
            out_copy(slot, r0, rows).start()
            pf[1 + slot] = rows
        run_blocks(compute)

        def tail(i, carry):
            start_row(slot_tok[start_next + i], nxt, i)
            return carry
        lax.fori_loop(pf[0], n_next, tail, 0)

    pl.when(t == 0)(functools.partial(gate_up, 0))
    pl.when(t == 1)(functools.partial(gate_up, 1))
    pl.when(t == 2)(down)

    fill = item_fill[a]

    @pl.when(jnp.logical_and(t == 0, fill > 0))
    def _():
        wait_stage(0)
        ystage[0] = jnp.zeros(ystage.shape[1:], ystage.dtype)

        def zero_block(bk, carry):
            cp = out_copy(0, bk * MOE_PAD, MOE_PAD)
            cp.start()
            cp.wait()
            return carry
        lax.fori_loop(0, fill // MOE_PAD, zero_block, 0)

    @pl.when(jnp.logical_and(a == n_items - 1, t == pl.num_programs(1) - 1))
    def _():
        wait_stage(0)
        wait_stage(1)


def _moe_ffn(h2p, w_gate, w_up, w_down, item_e, item_start, item_rows, item_fill, slot_tok,
             n_slots):
    d = w_gate.shape[1]
    assert d == 2 * TILE_ROWS * LANES and h2p.shape[1] == LANES, "one packed token row per tile"
    n_items = item_e.shape[0]
    half_w = D_EXPERT // 2

    def wgu_map(a, t, ie, ist, ir, fl, st):
        return (ie[a], 0, jnp.where(ir[a] > 0, jnp.minimum(t, 1), 1))

    def wd_map(a, t, ie, ist, ir, fl, st):
        return (jnp.where(t >= 2, ie[a], ie[jnp.maximum(a - 1, 0)]), 0, 0)

    grid_spec = pltpu.PrefetchScalarGridSpec(
        num_scalar_prefetch=5,
        grid=(n_items, 3),
        in_specs=[
            pl.BlockSpec(memory_space=pl.ANY),
            pl.BlockSpec((1, d, half_w), wgu_map),
            pl.BlockSpec((1, d, half_w), wgu_map),
            pl.BlockSpec((1, D_EXPERT, d), wd_map),
        ],
        out_specs=pl.BlockSpec(memory_space=pl.ANY),
        scratch_shapes=[
            pltpu.VMEM((2, MOE_ITEM_ROWS * TILE_ROWS, LANES), jnp.uint32),
            pltpu.VMEM((MOE_ITEM_ROWS, D_EXPERT), BF16),
            pltpu.VMEM((2, MOE_BLOCK * TILE_ROWS, LANES), jnp.uint32),
            pltpu.SMEM((3,), jnp.int32),
            pltpu.SemaphoreType.DMA((2,)),
            pltpu.SemaphoreType.DMA((2,)),
        ],
    )
    return pl.pallas_call(
        _moe_kernel,
        grid_spec=grid_spec,
        out_shape=jax.ShapeDtypeStruct((n_slots * TILE_ROWS, LANES), jnp.uint32),
        compiler_params=_cparams(2, 56),
        name="moe",
    )(item_e, item_start, item_rows, item_fill, slot_tok, h2p, w_gate, w_up, w_down)


def _combine_kernel(s0, s1, yb_hbm, gates_ref, x1_ref, nw_ref, g2_ref, out_ref, rows, sem):
    i = pl.program_id(0)
    n_tiles = pl.num_programs(0)
    tm = x1_ref.shape[0]

    def tile_rows(i, count=1):
        return pl.ds(pl.multiple_of(i * TILE_ROWS, TILE_ROWS), count * TILE_ROWS)

    def gather(tile, buf):
        def issue(r8, carry):
            t0 = tile * tm + r8 * 8
            for dr in range(8):
                for k, s_k in enumerate((s0, s1)):
                    pltpu.make_async_copy(yb_hbm.at[tile_rows(s_k[t0 + dr]), :],
                                          rows.at[buf, k, tile_rows(r8 * 8 + dr), :],
                                          sem.at[buf]).start()
            return carry
        lax.fori_loop(0, tm // 8, issue, 0)

    @pl.when(i == 0)
    def _():
        gather(0, 0)

    @pl.when(i + 1 < n_tiles)
    def _():
        gather(i + 1, (i + 1) % 2)

    buf = i % 2
    for k in range(EXPERT_TOPK):
        pltpu.make_async_copy(yb_hbm.at[tile_rows(0, tm), :], rows.at[buf, k], sem.at[buf]).wait()

    g = gates_ref[...]
    y = (_unpack_pairs(_load_row_tiles(rows.at[buf, 0], 0, tm)) * g[:, 0:1]
         + _unpack_pairs(_load_row_tiles(rows.at[buf, 1], 0, tm)) * g[:, 1:2])
    out_ref[...] = x1_ref[...] + g2_ref[0:1, :] * (_rms(y) * nw_ref[3:4, :])


def _combine(s0, s1, yb, gates, x1, norm_w, mod):
    n, d = x1.shape
    tm = 256
    grid_spec = pltpu.PrefetchScalarGridSpec(
        num_scalar_prefetch=2,
        grid=(n // tm,),
        in_specs=[
            pl.BlockSpec(memory_space=pl.ANY),
            pl.BlockSpec((tm, EXPERT_TOPK), lambda i, *_: (i, 0)),
            pl.BlockSpec((tm, d), lambda i, *_: (i, 0)),
            pl.BlockSpec((4, d), lambda i, *_: (0, 0)),
            pl.BlockSpec((8, d), lambda i, *_: (0, 5)),
        ],
        out_specs=pl.BlockSpec((tm, d), lambda i, *_: (i, 0)),
        scratch_shapes=[
            pltpu.VMEM((2, EXPERT_TOPK, tm * TILE_ROWS, LANES), jnp.uint32),
            pltpu.SemaphoreType.DMA((2,)),
        ],
    )
    return pl.pallas_call(
        _combine_kernel,
        grid_spec=grid_spec,
        out_shape=jax.ShapeDtypeStruct((n, d), F32),
        compiler_params=_cparams(1, 40),
        name="combine",
    )(s0, s1, yb, gates, x1, norm_w, mod)


def _rope_tables(n):
    rows = n // GRID_W
    r = np.repeat(np.arange(rows, dtype=np.float32), GRID_W)
    cidx = np.tile(np.arange(GRID_W, dtype=np.float32), rows)
    n_freq = RET_DK // 4
    inv = np.float32(ROPE_BASE) ** (-np.arange(n_freq, dtype=np.float32) / np.float32(n_freq))
    ang = np.concatenate([r[:, None] * inv, cidx[:, None] * inv], axis=-1).astype(np.float32)
    cos, sin = np.cos(ang), np.sin(ang)
    cs = np.concatenate([cos, cos], axis=-1)
    sn = np.concatenate([-sin, sin], axis=-1)
    return jnp.asarray(cs, F32), jnp.asarray(sn, F32)


def kernel(x, c, ctx, c_ctx, w_mod, b_mod, norm_w, w_in, hg_lb_logits, hg_norm_w, ret_decay_logits,
           ret_norm_w, w_out, router_group_w, router_group_b, router_expert_w, router_expert_b,
           w_gate, w_up, w_down):
    bsz, n, d = x.shape
    l = ctx.shape[1]
    assert bsz == 1 and w_mod.shape[0] == 1, "single sample, single layer"
    x2d, ctx2d = x[0], ctx[0]

    cc = jnp.zeros((8, d), F32).at[0].set(c[0]).at[1].set(c_ctx)
    mod = _modulation(cc, w_mod[0], b_mod[0])

    lb = jnp.cumsum(jax.nn.softmax(hg_lb_logits.astype(F32), axis=0), axis=0)[0]
    rk0 = 5 * HG_WIDTH + RET_HEADS * RET_DK
    cols = jnp.stack([
        jnp.zeros((PROJ_WIDTH,), F32).at[HG_WIDTH:3 * HG_WIDTH].set(lb.reshape(-1)),
        jnp.ones((PROJ_WIDTH,), F32).at[rk0:rk0 + RET_HEADS * RET_DK].set(RET_DK ** -0.5),
    ])
    pc, gc, w_in_bf = _projection(ctx2d, norm_w[0], mod, w_in[0], cols, 1, l, "proj_ctx")
    p, g = _projection(x2d, norm_w[0], mod, w_in_bf, cols, 0, 1024, "proj_x")

    ret_lg = jax.nn.log_sigmoid(ret_decay_logits[0].astype(F32))
    cs_tab, sn_tab = _rope_tables(n)

    o_f, o_b, r_f, r_b = _latent_scans(_hgrn_scan(p, g, pc, gc),
                                       _ret_scan(p, pc, ret_lg, cs_tab, sn_tab))

    wr = jnp.concatenate([router_group_w[0], router_expert_w[0]], axis=1)
    wr = jnp.pad(wr, ((0, 0), (0, ROUTER_LANES - wr.shape[1])))
    wr_hi = wr.astype(BF16)
    wr_parts = jnp.concatenate([wr_hi, (wr - wr_hi.astype(F32)).astype(BF16)], axis=1)
    br = jnp.concatenate([router_group_b[0], router_expert_b[0]]).astype(F32)
    br = jnp.pad(br, (0, ROUTER_LANES - br.shape[0])).reshape(1, ROUTER_LANES)
    x1, h2, logits = _out_projection(
        o_f, o_b, r_f, r_b, p, x2d, w_out[0].astype(BF16), hg_norm_w, ret_norm_w, norm_w[0], mod,
        wr_parts, br)

    n_assign = n * EXPERT_TOPK
    n_slots = n_assign + N_EXPERTS * MOE_PAD
    n_items = (n_slots + N_EXPERTS * (MOE_ITEM_ROWS - MOE_PAD)) // MOE_ITEM_ROWS
    gates, s0, s1, counts = _router(logits)
    s0, s1 = s0.reshape(-1), s1.reshape(-1)
    slot_tok, item_e, item_start, item_rows, item_fill = _plan(counts, s0, s1, n_slots, n_items)

    yb = _moe_ffn(h2, w_gate[0], w_up[0], w_down[0], item_e, item_start, item_rows, item_fill,
                  slot_tok, n_slots)
    out = _combine(s0, s1, yb, gates, x1, norm_w[0], mod)
    return out[None]
```

```python
import functools

import jax
import jax.numpy as jnp
import numpy as np
from jax import lax
from jax.experimental import pallas as pl
from jax.experimental.pallas import tpu as pltpu

F32 = jnp.float32
BF16 = jnp.bfloat16

GRID_W = 64
HG_HEADS = 8
HG_DK = 128
HG_WIDTH = 1024
RET_HEADS = 4
RET_DK = 128
RET_DV = 256
RET_WIDTH = 1024
PROJ_WIDTH = 8192
N_GROUPS = 4
EXPERTS_PER_GROUP = 8
N_EXPERTS = 32
EXPERT_TOPK = 2
D_EXPERT = 1024
RMS_EPS = 1e-6
ROPE_BASE = 10000.0

COL_HQ, COL_HI, COL_HGATE = 0, 3, 4
COL_RQ_512, COL_RK_512 = 10, 11
COL_RV, COL_RGATE = 6, 7

HG_CHUNK = 256
HG_STEP_HEADS = 8
HG_DIAG = 16
HG_LOG_CLAMP = 10.0
RET_CHUNK = 256

MOE_PAD = 128
MOE_BLOCK = 256
MOE_ITEM_ROWS = 1024
MOE_GATHER_GROUP = 128
ROUTER_LANES = 128
LANES = 128
TILE_ROWS = 8


def _cparams(n_axes, vmem_mb, **flags):
    return pltpu.CompilerParams(
        dimension_semantics=("arbitrary",) * n_axes,
        vmem_limit_bytes=vmem_mb * 1024 * 1024,
        flags=flags or None,
    )


def _sigmoid(x):
    return 1.0 / (1.0 + jnp.exp(-x))


def _dot(a, b):
    return jnp.dot(a, b, preferred_element_type=F32)


def _dot_nt(a, b):
    return lax.dot_general(a, b, (((1,), (1,)), ((), ())), preferred_element_type=F32)


def _dot_tn(a, b):
    return lax.dot_general(a, b, (((0,), (0,)), ((), ())), preferred_element_type=F32)


def _mod_kernel(cc_ref, w_ref, b_ref, o_ref):
    s = cc_ref[...]
    s = s * _sigmoid(s)
    s16 = jnp.concatenate([s, s], axis=0)
    hi = s16.astype(BF16).astype(F32)
    row = lax.broadcasted_iota(jnp.int32, s16.shape, 0)
    lhs = jnp.where(row < 8, hi, s16 - hi).astype(BF16)
    r = _dot(lhs, w_ref[...].astype(BF16))
    o_ref[...] = r[:8] + r[8:] + b_ref[...]


def _modulation(cc, w_mod, b_mod):
    d, n = w_mod.shape
    tn = 1024
    return pl.pallas_call(
        _mod_kernel,
        grid=(n // tn,),
        in_specs=[
            pl.BlockSpec((8, d), lambda j: (0, 0)),
            pl.BlockSpec((d, tn), lambda j: (0, j)),
            pl.BlockSpec((1, tn), lambda j: (0, j)),
        ],
        out_specs=pl.BlockSpec((8, tn), lambda j: (0, j)),
        out_shape=jax.ShapeDtypeStruct((8, n), F32),
        compiler_params=_cparams(1, 40),
        name="mod",
    )(cc, w_mod, b_mod.reshape(1, n))


PROJ_TN = 1024
PROLOGUE_ROWS = 32


def _proj_kernel(x_ref, nw_ref, sh_ref, sc_ref, w_ref, cols_ref, o_ref, g_ref, *rest, mod_row):
    h_scr = rest[-1]
    if len(rest) == 2:
        rest[0][...] = w_ref[...].astype(BF16)
        w_ref = rest[0]
    i = pl.program_id(0)
    j = pl.program_id(1)
    last_j = pl.num_programs(1) - 1
    cur = i % 2

    def normed(x):
        scale = nw_ref[0:1, :] * (1.0 + sc_ref[mod_row:mod_row + 1, :])
        shift = sh_ref[mod_row:mod_row + 1, :]
        y = x * lax.rsqrt(jnp.mean(x * x, axis=-1, keepdims=True) + RMS_EPS)
        return (y * scale + shift).astype(BF16)

    @pl.when(jnp.logical_and(i == 0, j == 0))
    def _():
        def body(r, carry):
            r0 = pl.multiple_of(r * PROLOGUE_ROWS, PROLOGUE_ROWS)
            h_scr[0, pl.ds(r0, PROLOGUE_ROWS), :] = normed(x_ref[pl.ds(r0, PROLOGUE_ROWS), :])
            return carry
        lax.fori_loop(0, x_ref.shape[0] // PROLOGUE_ROWS, body, 0)

    is_gate = jnp.logical_or(j == 1, j == 2)
    is_silu = jnp.logical_or(j == 0, j == 4)
    is_lin = jnp.logical_not(jnp.logical_or(jnp.logical_or(is_gate, is_silu), j == last_j))

    @pl.when(is_lin)
    def _():
        o_ref[...] = (_dot(h_scr[cur], w_ref[...]) * cols_ref[1:2, :]).astype(BF16)

    @pl.when(is_silu)
    def _():
        acc = _dot(h_scr[cur], w_ref[...])
        o_ref[...] = (acc * _sigmoid(acc)).astype(BF16)

    @pl.when(is_gate)
    def _():
        acc = _dot(h_scr[cur], w_ref[...])
        lb = cols_ref[0:1, :]
        g = jnp.maximum(jnp.log(lb + (1.0 - lb) * _sigmoid(acc)), -HG_LOG_CLAMP)
        g_ref[...] = g
        o_ref[...] = g.astype(BF16)

    @pl.when(j == last_j)
    def _():
        acc = _dot(h_scr[cur], w_ref[...])
        o_ref[...] = (acc * _sigmoid(acc)).astype(BF16)
        h_scr[1 - cur] = normed(x_ref[...])


def _projection(x2d, norm_w, mod, w_in, cols, mod_row, tm, name):
    n, d = x2d.shape
    tn = PROJ_TN
    n_i, n_j = n // tm, PROJ_WIDTH // tn
    emit_w = w_in.dtype != BF16
    assert not emit_w or n_i == 1, "the weight copy is written once per column tile"

    def x_map(i, j):
        return (jnp.minimum(i + (j == n_j - 1).astype(jnp.int32), n_i - 1), 0)

    return pl.pallas_call(
        functools.partial(_proj_kernel, mod_row=mod_row),
        grid=(n_i, n_j),
        in_specs=[
            pl.BlockSpec((tm, d), x_map),
            pl.BlockSpec((4, d), lambda i, j: (0, 0)),
            pl.BlockSpec((8, d), lambda i, j: (0, 0)),
            pl.BlockSpec((8, d), lambda i, j: (0, 1)),
            pl.BlockSpec((d, tn), lambda i, j: (0, j)),
            pl.BlockSpec((2, tn), lambda i, j: (0, j)),
        ],
        out_specs=[
            pl.BlockSpec((tm, tn), lambda i, j: (i, j)),
            pl.BlockSpec((tm, tn), lambda i, j: (i, jnp.clip(j - 1, 0, 1))),
            *([pl.BlockSpec((d, tn), lambda i, j: (0, j))] if emit_w else []),
        ],
        out_shape=[
            jax.ShapeDtypeStruct((n, PROJ_WIDTH), BF16),
            jax.ShapeDtypeStruct((n, 2 * HG_WIDTH), F32),
            *([jax.ShapeDtypeStruct(w_in.shape, BF16)] if emit_w else []),
        ],
        scratch_shapes=[pltpu.VMEM((2, tm, d), BF16)],
        compiler_params=_cparams(2, 56),
        name=name,
    )(x2d, norm_w, mod, mod, w_in, cols)


def _tri_ones(c, rev):
    t = lax.broadcasted_iota(jnp.int32, (c, c), 0)
    s = lax.broadcasted_iota(jnp.int32, (c, c), 1)
    return jnp.where((s >= t) if rev else (s <= t), 1.0, 0.0).astype(BF16)


def _cumsum_rows(g, tri):
    hi = g.astype(BF16)
    lo = (g - hi.astype(F32)).astype(BF16)
    return _dot(tri, hi) + _dot(tri, lo)


def _row_refs(b, idxs, blk):
    parts = [jnp.broadcast_to(b[i:i + 1, :], (blk, b.shape[1])) for i in idxs]
    return parts[0] if len(parts) == 1 else jnp.concatenate(parts, axis=0)


def _hgrn_levels(c, rev):
    t = lax.broadcasted_iota(jnp.int32, (c, c), 0)
    s = lax.broadcasted_iota(jnp.int32, (c, c), 1)
    levels = []
    m = c // 2
    while m >= HG_DIAG:
        blk = 2 * m
        same = ((t ^ s) & ~(blk - 1)) == 0
        if rev:
            idxs = [b0 + m for b0 in range(0, c, blk)]
            cross = jnp.logical_and((t & m) == 0, (s & m) != 0)
        else:
            idxs = [b0 + m - 1 for b0 in range(0, c, blk)]
            cross = jnp.logical_and((t & m) != 0, (s & m) == 0)
        levels.append((idxs, blk, jnp.logical_and(same, cross)))
        m //= 2
    blk = HG_DIAG
    same = ((t ^ s) & ~(blk - 1)) == 0
    if rev:
        idxs = [b0 + blk // 2 for b0 in range(0, c, blk)]
        mask = jnp.logical_and(same, s >= t)
    else:
        idxs = [b0 + blk // 2 - 1 for b0 in range(0, c, blk)]
        mask = jnp.logical_and(same, s <= t)
    levels.append((idxs, blk, mask))
    return levels


def _hgrn_head(q_bf, g, b, v_bf, st, rev, levels):
    c = g.shape[0]
    kk = 1.0 - jnp.exp(g)

    o = None
    if levels is not None:
        q = q_bf.astype(F32)
        att = jnp.zeros((c, c), F32)
        for idxs, blk, mask in levels:
            ref = _row_refs(b, idxs, blk)
            a_l = _dot_nt((q * jnp.exp(b - ref)).astype(BF16),
                          (kk * jnp.exp(ref - b)).astype(BF16))
            att = att + jnp.where(mask, a_l, 0.0)
        o = _dot(att.astype(BF16), v_bf) + _dot_nt((q * jnp.exp(b)).astype(BF16), st.astype(BF16))

    b_end = b[0:1, :] if rev else b[c - 1:c, :]
    k_end = (kk * jnp.exp(b_end - b)).astype(BF16)
    st_new = st * jnp.exp(b_end) + _dot_tn(v_bf, k_end)
    return o, st_new


def _hgrn_ctx_kernel(gf_ref, vf_ref, gb_ref, vb_ref, st_ref):
    @pl.when(pl.program_id(0) == 0)
    def _():
        st_ref[...] = jnp.zeros_like(st_ref)

    c = gf_ref.shape[0]
    bf = _cumsum_rows(gf_ref[...], _tri_ones(c, False))
    bb = _cumsum_rows(gb_ref[...], _tri_ones(c, True))
    for h in range(HG_HEADS):
        sl = slice(h * HG_DK, (h + 1) * HG_DK)
        _, st_ref[h] = _hgrn_head(None, gf_ref[:, sl], bf[:, sl], vf_ref[:, sl], st_ref[h],
                                  False, None)
        hb = HG_HEADS + h
        _, st_ref[hb] = _hgrn_head(None, gb_ref[:, sl], bb[:, sl], vb_ref[:, sl], st_ref[hb],
                                   True, None)


def _hgrn_kernel(qf_ref, gf_ref, vf_ref, qb_ref, gb_ref, vb_ref, st0_ref, of_ref, ob_ref, st_ref):
    hg = pl.program_id(1)

    @pl.when(jnp.logical_and(pl.program_id(0) == 0, hg == 0))
    def _():
        st_ref[...] = st0_ref[...]

    c = gf_ref.shape[0]
    lev_f = _hgrn_levels(c, False)
    lev_b = _hgrn_levels(c, True)
    bf = _cumsum_rows(gf_ref[...], _tri_ones(c, False))
    bb = _cumsum_rows(gb_ref[...], _tri_ones(c, True))
    for h in range(HG_STEP_HEADS):
        sl = slice(h * HG_DK, (h + 1) * HG_DK)
        hf = hg * HG_STEP_HEADS + h
        of_ref[:, sl], st_ref[hf] = _hgrn_head(qf_ref[:, sl], gf_ref[:, sl], bf[:, sl],
                                               vf_ref[:, sl], st_ref[hf], False, lev_f)
        hb = HG_HEADS + hf
        ob_ref[:, sl], st_ref[hb] = _hgrn_head(qb_ref[:, sl], gb_ref[:, sl], bb[:, sl],
                                               vb_ref[:, sl], st_ref[hb], True, lev_b)


def _hgrn_scan(p, g, pc, gc):
    c = HG_CHUNK
    w = HG_WIDTH
    n, l = p.shape[0], pc.shape[0]
    c_ctx = min(c, l)
    n_x, n_ctx = n // c, l // c_ctx
    st_shape = (2 * HG_HEADS, HG_DK, HG_DK)

    def spec(chunk_fn, col):
        return pl.BlockSpec((c_ctx, w), lambda s: (chunk_fn(s), col))

    st0 = pl.pallas_call(
        _hgrn_ctx_kernel,
        grid=(n_ctx,),
        in_specs=[
            spec(lambda s: s, 0), spec(lambda s: s, COL_HI),
            spec(lambda s: n_ctx - 1 - s, 1), spec(lambda s: n_ctx - 1 - s, COL_HI),
        ],
        out_specs=pl.BlockSpec(st_shape, lambda s: (0, 0, 0)),
        out_shape=jax.ShapeDtypeStruct(st_shape, F32),
        compiler_params=_cparams(1, 32),
        name="hgrn_ctx",
    )(gc, pc, gc, pc)

    n_hg = HG_HEADS // HG_STEP_HEADS
    wg = HG_STEP_HEADS * HG_DK
    fwd = lambda s: s
    bwd = lambda s: n_x - 1 - s

    def gspec(chunk_fn, col):
        return pl.BlockSpec((c, wg), lambda s, hg: (chunk_fn(s), col * n_hg + hg))

    assert n_hg == 1, "the fused latent scan walks all heads per step"
    return dict(
        kernel=_hgrn_kernel,
        grid=(n_x, n_hg),
        in_specs=[
            gspec(fwd, COL_HQ), gspec(fwd, 0), gspec(fwd, COL_HI),
            gspec(bwd, COL_HQ), gspec(bwd, 1), gspec(bwd, COL_HI),
            pl.BlockSpec(st_shape, lambda s, hg: (0, 0, 0)),
        ],
        out_specs=[gspec(fwd, 0), gspec(bwd, 0)],
        out_shape=[jax.ShapeDtypeStruct((n, w), F32)] * 2,
        scratch_shapes=[pltpu.VMEM(st_shape, F32)],
        args=(p, g, p, p, g, p, st0),
    )


def _rope(x_bf, cs, sn):
    x = x_bf.astype(F32)
    return x * cs + pltpu.roll(x, RET_DK // 2, axis=1) * sn


def _ret_decays(lg, c, rev):
    t = lax.broadcasted_iota(jnp.int32, (c, c), 0)
    s = lax.broadcasted_iota(jnp.int32, (c, c), 1)
    diff = (s - t) if rev else (t - s)
    pair = jnp.where(diff >= 0, jnp.exp(lg * jnp.maximum(diff, 0).astype(F32)), 0.0)
    pos = lax.broadcasted_iota(jnp.int32, (c, RET_DK), 0).astype(F32)
    q_dec = jnp.exp(lg * ((c - pos) if rev else (pos + 1.0)))
    k_dec = jnp.exp(lg * (pos if rev else (c - 1.0 - pos)))
    chunk = jnp.exp(jnp.full((8, RET_DK), lg * c, F32))
    return pair, q_dec, k_dec, chunk


def _ret_head(q, k, v_bf, decays, st):
    pair, q_dec, k_dec, chunk = decays
    o = None
    if q is not None:
        att = _dot_nt(q.astype(BF16), k.astype(BF16)) * pair
        o = _dot(att.astype(BF16), v_bf) + _dot_nt((q * q_dec).astype(BF16), st.astype(BF16))
    st_new = st * chunk[0:1, :] + _dot_tn(v_bf, (k * k_dec).astype(BF16))
    return o, st_new


def _ret_ctx_kernel(lg_ref, kf_ref, vf_ref, kb_ref, vb_ref, st_ref):
    @pl.when(pl.program_id(0) == 0)
    def _():
        st_ref[...] = jnp.zeros_like(st_ref)

    c = kf_ref.shape[0]
    for h in range(RET_HEADS):
        sk = slice(h * RET_DK, (h + 1) * RET_DK)
        sv = slice(h * RET_DV, (h + 1) * RET_DV)
        _, st_ref[h] = _ret_head(None, kf_ref[:, sk].astype(F32), vf_ref[:, sv],
                                 _ret_decays(lg_ref[0, h], c, False), st_ref[h])
        hb = RET_HEADS + h
        _, st_ref[hb] = _ret_head(None, kb_ref[:, sk].astype(F32), vb_ref[:, sv],
                                  _ret_decays(lg_ref[1, h], c, True), st_ref[hb])


def _ret_kernel(lg_ref, qf_ref, kf_ref, vf_ref, csf_ref, snf_ref,
                qb_ref, kb_ref, vb_ref, csb_ref, snb_ref, st0_ref, of_ref, ob_ref,
                st_ref, pair_ref, qd_ref, kd_ref, cd_ref):
    c = qf_ref.shape[0]

    @pl.when(pl.program_id(0) == 0)
    def _():
        st_ref[...] = st0_ref[...]
        for d in range(2):
            for h in range(RET_HEADS):
                i = d * RET_HEADS + h
                pair_ref[i], qd_ref[i], kd_ref[i], cd_ref[i] = _ret_decays(lg_ref[d, h], c, d == 1)

    csf, snf = csf_ref[...], snf_ref[...]
    csb, snb = csb_ref[...], snb_ref[...]
    for h in range(RET_HEADS):
        sk = slice(h * RET_DK, (h + 1) * RET_DK)
        sv = slice(h * RET_DV, (h + 1) * RET_DV)
        of_ref[:, sv], st_ref[h] = _ret_head(
            _rope(qf_ref[:, sk], csf, snf), _rope(kf_ref[:, sk], csf, snf), vf_ref[:, sv],
            (pair_ref[h], qd_ref[h], kd_ref[h], cd_ref[h]), st_ref[h])
        hb = RET_HEADS + h
        ob_ref[:, sv], st_ref[hb] = _ret_head(
            _rope(qb_ref[:, sk], csb, snb), _rope(kb_ref[:, sk], csb, snb), vb_ref[:, sv],
            (pair_ref[hb], qd_ref[hb], kd_ref[hb], cd_ref[hb]), st_ref[hb])


def _ret_scan(p, pc, ret_lg, cs_tab, sn_tab):
    c = RET_CHUNK
    n, l = p.shape[0], pc.shape[0]
    n_x, n_ctx = n // c, l // c
    qk_w = RET_HEADS * RET_DK
    st_shape = (2 * RET_HEADS, RET_DV, RET_DK)
    smem = pl.BlockSpec(memory_space=pltpu.SMEM)

    def kv_specs(chunk_fn):
        return [
            pl.BlockSpec((c, qk_w), lambda s: (chunk_fn(s), COL_RK_512)),
            pl.BlockSpec((c, RET_WIDTH), lambda s: (chunk_fn(s), COL_RV)),
        ]

    st0 = pl.pallas_call(
        _ret_ctx_kernel,
        grid=(n_ctx,),
        in_specs=[smem, *kv_specs(lambda s: s), *kv_specs(lambda s: n_ctx - 1 - s)],
        out_specs=pl.BlockSpec(st_shape, lambda s: (0, 0, 0)),
        out_shape=jax.ShapeDtypeStruct(st_shape, F32),
        compiler_params=_cparams(1, 32),
        name="ret_ctx",
    )(ret_lg, pc, pc, pc, pc)

    def specs(chunk_fn):
        return [
            pl.BlockSpec((c, qk_w), lambda s, hg: (chunk_fn(s), COL_RQ_512)),
            pl.BlockSpec((c, qk_w), lambda s, hg: (chunk_fn(s), COL_RK_512)),
            pl.BlockSpec((c, RET_WIDTH), lambda s, hg: (chunk_fn(s), COL_RV)),
            pl.BlockSpec((c, RET_DK), lambda s, hg: (chunk_fn(s), 0)),
            pl.BlockSpec((c, RET_DK), lambda s, hg: (chunk_fn(s), 0)),
        ]

    fwd = lambda s: s
    bwd = lambda s: n_x - 1 - s
    return dict(
        kernel=_ret_kernel,
        grid=(n_x, 1),
        in_specs=[smem, *specs(fwd), *specs(bwd),
                  pl.BlockSpec(st_shape, lambda s, hg: (0, 0, 0))],
        out_specs=[
            pl.BlockSpec((c, RET_WIDTH), lambda s, hg: (fwd(s), 0)),
            pl.BlockSpec((c, RET_WIDTH), lambda s, hg: (bwd(s), 0)),
        ],
        out_shape=[jax.ShapeDtypeStruct((n, RET_WIDTH), F32)] * 2,
        scratch_shapes=[
            pltpu.VMEM(st_shape, F32),
            pltpu.VMEM((2 * RET_HEADS, c, c), F32),
            pltpu.VMEM((2 * RET_HEADS, c, RET_DK), F32),
            pltpu.VMEM((2 * RET_HEADS, c, RET_DK), F32),
            pltpu.VMEM((2 * RET_HEADS, 8, RET_DK), F32),
        ],
        args=(ret_lg, p, p, p, cs_tab, sn_tab, p, p, p, cs_tab, sn_tab, st0),
    )


def _fused_scan_kernel(*refs, n_in, n_scr):
    ins, outs, scr = refs[:sum(n_in)], refs[sum(n_in):sum(n_in) + 4], refs[sum(n_in) + 4:]
    _hgrn_kernel(*ins[:n_in[0]], *outs[:2], *scr[:n_scr[0]])
    _ret_kernel(*ins[n_in[0]:], *outs[2:], *scr[n_scr[0]:])


def _latent_scans(hg, rt):
    assert hg["grid"] == rt["grid"], "both scans must walk the same chunks"
    return pl.pallas_call(
        functools.partial(_fused_scan_kernel,
                          n_in=(len(hg["in_specs"]), len(rt["in_specs"])),
                          n_scr=(len(hg["scratch_shapes"]), len(rt["scratch_shapes"]))),
        grid=hg["grid"],
        in_specs=[*hg["in_specs"], *rt["in_specs"]],
        out_specs=[*hg["out_specs"], *rt["out_specs"]],
        out_shape=[*hg["out_shape"], *rt["out_shape"]],
        scratch_shapes=[*hg["scratch_shapes"], *rt["scratch_shapes"]],
        compiler_params=_cparams(2, 48),
        name="scans",
    )(*hg["args"], *rt["args"])


def _rms(y):
    return y * lax.rsqrt(jnp.mean(y * y, axis=-1, keepdims=True) + RMS_EPS)


def _outproj_kernel(of_ref, ob_ref, rf_ref, rb_ref, hgate_ref, rgate_ref, x_ref, wout_ref,
                    hgw_ref, rnw_ref, nw_ref, g1_ref, sh2_ref, sc2_ref, wr_ref, br_ref,
                    x1_ref, h2_ref, lg_ref, cat_scr):
    for h in range(HG_HEADS):
        sl = slice(h * HG_DK, (h + 1) * HG_DK)
        o = of_ref[:, sl] + ob_ref[:, sl]
        cat_scr[:, sl] = (_rms(o) * hgw_ref[0:1, sl] * hgate_ref[:, sl].astype(F32)).astype(BF16)
    for h in range(RET_HEADS):
        sl = slice(h * RET_DV, (h + 1) * RET_DV)
        r = rf_ref[:, sl] + rb_ref[:, sl]
        cat_scr[:, HG_WIDTH + h * RET_DV:HG_WIDTH + (h + 1) * RET_DV] = (
            _rms(r) * rnw_ref[0:1, sl] * rgate_ref[:, sl].astype(F32)).astype(BF16)

    y = _dot(cat_scr[...], wout_ref[...])
    x1 = x_ref[...] + g1_ref[0:1, :] * (_rms(y) * nw_ref[1:2, :])
    x1_ref[...] = x1
    h2 = _rms(x1) * nw_ref[2:3, :] * (1.0 + sc2_ref[0:1, :]) + sh2_ref[0:1, :]
    _store_row_tiles(h2_ref, _pack_pairs(h2))
    hi = h2.astype(BF16)
    lo = (h2 - hi.astype(F32)).astype(BF16)
    parts = _dot(jnp.concatenate([hi, lo], axis=0), wr_ref[...])
    half = hi.shape[0]
    lg_ref[...] = (parts[:half, :ROUTER_LANES] + parts[half:, :ROUTER_LANES]
                   + parts[:half, ROUTER_LANES:] + br_ref[...])


def _out_projection(o_f, o_b, r_f, r_b, p, x2d, w_out_bf, hg_norm_w, ret_norm_w, norm_w, mod,
                    wr_parts, br):
    n, d = x2d.shape
    tm = 512
    row = lambda w: pl.BlockSpec((tm, w), lambda i: (i, 0))
    full = lambda a: pl.BlockSpec(a.shape, lambda i: (0,) * a.ndim, pipeline_mode=pl.Buffered(1))
    modcol = lambda k: pl.BlockSpec((8, d), lambda i: (0, k))
    return pl.pallas_call(
        _outproj_kernel,
        grid=(n // tm,),
        in_specs=[
            row(HG_WIDTH), row(HG_WIDTH), row(RET_WIDTH), row(RET_WIDTH),
            pl.BlockSpec((tm, HG_WIDTH), lambda i: (i, COL_HGATE)),
            pl.BlockSpec((tm, RET_WIDTH), lambda i: (i, COL_RGATE)),
            row(d), full(w_out_bf), full(hg_norm_w), full(ret_norm_w), full(norm_w),
            modcol(2), modcol(3), modcol(4),
            full(wr_parts), full(br),
        ],
        out_specs=[row(d), pl.BlockSpec((tm * TILE_ROWS, LANES), lambda i: (i, 0)),
                   row(ROUTER_LANES)],
        out_shape=[
            jax.ShapeDtypeStruct((n, d), F32),
            jax.ShapeDtypeStruct((n * TILE_ROWS, LANES), jnp.uint32),
            jax.ShapeDtypeStruct((n, ROUTER_LANES), F32),
        ],
        scratch_shapes=[pltpu.VMEM((tm, d), BF16)],
        compiler_params=_cparams(1, 58),
        name="outproj",
    )(o_f, o_b, r_f, r_b, p, p, x2d, w_out_bf, hg_norm_w, ret_norm_w, norm_w, mod, mod, mod,
      wr_parts, br)


ROUTER_TM = 1024


def _lane_first(hit, lane_f):
    return jnp.min(jnp.where(hit, lane_f, 1e9), axis=1, keepdims=True)


def _router_kernel(lg_ref, gates_ref, s0_ref, s1_ref, cnt_ref, ltri, carry, pstart):
    ph = pl.program_id(0)
    i = pl.program_id(1)
    tm = lg_ref.shape[0]

    @pl.when(jnp.logical_and(ph == 0, i == 0))
    def _():
        t = lax.broadcasted_iota(jnp.int32, (tm, tm), 0)
        s = lax.broadcasted_iota(jnp.int32, (tm, tm), 1)
        ltri[...] = jnp.where(s < t, 1.0, 0.0).astype(BF16)

    @pl.when(jnp.logical_and(ph == 1, i == 0))
    def _():
        cnt = carry[...]
        padded = jnp.floor((cnt + (MOE_PAD - 1.0)) * (1.0 / MOE_PAD)) * MOE_PAD
        r = lax.broadcasted_iota(jnp.int32, (LANES, LANES), 0)
        c = lax.broadcasted_iota(jnp.int32, (LANES, LANES), 1)
        before = jnp.where(r < c, 1.0, 0.0).astype(BF16)
        pstart[...] = _dot(padded.astype(BF16), before)
        cnt_ref[...] = cnt.astype(jnp.int32)

    @pl.when(i == 0)
    def _():
        carry[...] = jnp.zeros_like(carry)

    lg = lg_ref[...]
    lane = lax.broadcasted_iota(jnp.int32, lg.shape, 1)
    lane_f = lane.astype(F32)
    neg = -jnp.inf

    gl = jnp.where(lane < N_GROUPS, lg, neg)
    gmax = jnp.max(gl, axis=1, keepdims=True)
    grp = _lane_first(gl == gmax, lane_f).astype(jnp.int32)
    p_sel = 1.0 / jnp.sum(jnp.exp(gl - gmax), axis=1, keepdims=True)

    in_grp = jnp.logical_and(
        jnp.logical_and(lane >= N_GROUPS, lane < N_GROUPS + N_EXPERTS),
        ((lane - N_GROUPS) >> (EXPERTS_PER_GROUP.bit_length() - 1)) == grp)
    el = jnp.where(in_grp, lg, neg)
    v1 = jnp.max(el, axis=1, keepdims=True)
    o1 = lane_f == _lane_first(el == v1, lane_f)
    el2 = jnp.where(o1, neg, el)
    v2 = jnp.max(el2, axis=1, keepdims=True)
    o2 = lane_f == _lane_first(el2 == v2, lane_f)

    osum = (jnp.where(o1, 1.0, 0.0) + jnp.where(o2, 1.0, 0.0)).astype(BF16)
    earlier = _dot(ltri[...], osum) + carry[0:1, :]
    carry[...] = carry[...] + _dot(jnp.ones((8, tm), BF16), osum)

    @pl.when(ph == 1)
    def _():
        ez = jnp.exp(v2 - v1)
        g1 = p_sel / (1.0 + ez)
        gates_ref[...] = jnp.concatenate([g1, g1 * ez], axis=1)
        base = earlier + pstart[0:1, :]
        for o, s_ref in ((o1, s0_ref), (o2, s1_ref)):
            col = jnp.sum(jnp.where(o, base, 0.0), axis=1, keepdims=True)
            wide = jnp.broadcast_to(col, (tm, LANES))
            for b in range(tm // LANES):
                s_ref[b:b + 1, :] = wide[b * LANES:(b + 1) * LANES, :].T[0:1, :].astype(jnp.int32)


def _router(logits):
    t = logits.shape[0]
    tm = ROUTER_TM
    rows = tm // LANES
    return pl.pallas_call(
        _router_kernel,
        grid=(2, t // tm),
        in_specs=[pl.BlockSpec((tm, LANES), lambda ph, i: (i, 0))],
        out_specs=[
            pl.BlockSpec((tm, EXPERT_TOPK), lambda ph, i: (i * ph, 0)),
            pl.BlockSpec((rows, LANES), lambda ph, i: (i * ph, 0)),
            pl.BlockSpec((rows, LANES), lambda ph, i: (i * ph, 0)),
            pl.BlockSpec((8, LANES), lambda ph, i: (0, 0)),
        ],
        out_shape=[
            jax.ShapeDtypeStruct((t, EXPERT_TOPK), F32),
            jax.ShapeDtypeStruct((t // LANES, LANES), jnp.int32),
            jax.ShapeDtypeStruct((t // LANES, LANES), jnp.int32),
            jax.ShapeDtypeStruct((8, LANES), jnp.int32),
        ],
        scratch_shapes=[
            pltpu.VMEM((tm, tm), BF16),
            pltpu.VMEM((8, LANES), F32),
            pltpu.VMEM((8, LANES), F32),
        ],
        compiler_params=_cparams(2, 32),
        name="router",
    )(logits)


def _round_up_pow2(x, m):
    assert m & (m - 1) == 0
    return (x + (m - 1)) & ~(m - 1)


def _plan_kernel(cnt_ref, s0_ref, s1_ref, tok_ref, ie_ref, ist_ref, ir_ref, if_ref, zeros, sem):
    n_tok = s0_ref.shape[0]
    n_slots = tok_ref.shape[0]
    n_items = ie_ref.shape[0]
    r = MOE_ITEM_ROWS

    zeros[...] = jnp.zeros_like(zeros)
    clear = pltpu.make_async_copy(zeros, tok_ref, sem)
    clear.start()
    clear.wait()

    def scatter(t16, carry):
        for d in range(16):
            t = t16 * 16 + d
            tok_ref[s0_ref[t]] = t
            tok_ref[s1_ref[t]] = t
        return carry
    lax.fori_loop(0, n_tok // 16, scatter, 0)

    a = jnp.int32(0)
    used = jnp.int32(0)
    e_last = jnp.int32(0)
    for e in range(N_EXPERTS):
        pad = _round_up_pow2(cnt_ref[0, N_GROUPS + e], MOE_PAD)

        def item(sub, a, e=e, pad=pad, used=used):
            ie_ref[a] = e
            ist_ref[a] = used + sub * r
            ir_ref[a] = jnp.minimum(pad - sub * r, r)
            if_ref[a] = 0
            return a + 1
        a = lax.fori_loop(0, _round_up_pow2(pad, r) >> (r.bit_length() - 1), item, a)
        e_last = jnp.where(pad > 0, e, e_last)
        used = used + pad

    def idle(a2, carry):
        fill_start = used + (a2 - a) * r
        fill = jnp.clip(n_slots - fill_start, 0, r)
        ie_ref[a2] = e_last
        ist_ref[a2] = jnp.where(fill > 0, fill_start, 0)
        ir_ref[a2] = 0
        if_ref[a2] = fill
        return carry
    lax.fori_loop(a, n_items, idle, 0)


def _plan(counts, s0, s1, n_slots, n_items):
    smem = pl.BlockSpec(memory_space=pltpu.SMEM)
    item = jax.ShapeDtypeStruct((n_items,), jnp.int32)
    return pl.pallas_call(
        _plan_kernel,
        in_specs=[smem, smem, smem],
        out_specs=[smem] * 5,
        out_shape=[jax.ShapeDtypeStruct((n_slots,), jnp.int32), item, item, item, item],
        scratch_shapes=[pltpu.VMEM((n_slots,), jnp.int32), pltpu.SemaphoreType.DMA(())],
        name="plan",
    )(counts, s0, s1)


def _unpack_pairs(u):
    lo = lax.bitcast_convert_type(u << 16, F32)
    hi = lax.bitcast_convert_type(u & jnp.uint32(0xFFFF0000), F32)
    return jnp.concatenate([lo, hi], axis=1)


def _store_row_tiles(dst, packed):
    r = packed.shape[0]
    for s in range(TILE_ROWS):
        dst[pl.ds(s, r, stride=TILE_ROWS), :] = packed[:, s * LANES:(s + 1) * LANES]


def _load_row_tiles(src, first_row, r):
    return jnp.concatenate(
        [src[pl.ds(first_row * TILE_ROWS + s, r, stride=TILE_ROWS), :] for s in range(TILE_ROWS)],
        axis=1)


def _pack_pairs(y):
    w = y.shape[1] // 2
    bits = lax.bitcast_convert_type(y.astype(BF16).astype(F32), jnp.uint32)
    return (bits[:, :w] >> 16) | (bits[:, w:] & jnp.uint32(0xFFFF0000))


def _moe_block_sizes():
    sizes, rows = [], MOE_BLOCK
    while rows >= MOE_PAD:
        sizes.append(rows)
        rows //= 2
    return sizes


def _moe_kernel(item_e, item_start, item_rows, item_fill, slot_tok,
                h2p_hbm, wg_ref, wu_ref, wd_ref, yb_hbm,
                xbuf, act, ystage, pf, gsem, osem):
    a = pl.program_id(0)
    t = pl.program_id(1)
    n_items = pl.num_programs(0)
    n = item_rows[a]
    start = item_start[a]
    half_w = wg_ref.shape[2]
    cur = a % 2
    nxt = 1 - cur
    a1 = jnp.minimum(a + 1, n_items - 1)
    n_next = jnp.where(a + 1 < n_items, item_rows[a1], 0)
    start_next = item_start[a1]

    def tile_rows(i, count=1):
        return pl.ds(pl.multiple_of(i * TILE_ROWS, TILE_ROWS), count * TILE_ROWS)

    def row_copy(tok, slot, i):
        return pltpu.make_async_copy(h2p_hbm.at[tile_rows(tok), :], xbuf.at[slot, tile_rows(i), :],
                                     gsem.at[slot])

    def start_row(tok, slot, i):
        row_copy(tok, slot, i).start()

    @pl.when(jnp.logical_and(a == 0, t == 0))
    def _():
        pf[1] = 0
        pf[2] = 0

        def issue(i, carry):
            start_row(slot_tok[start + i], 0, i)
            return carry
        lax.fori_loop(0, n, issue, 0)

    @pl.when(t == 0)
    def _():
        pf[0] = 0

        def drain(bk, carry):
            r0 = pl.multiple_of(bk * MOE_PAD, MOE_PAD)
            pltpu.make_async_copy(h2p_hbm.at[tile_rows(0, MOE_PAD), :],
                                  xbuf.at[cur, tile_rows(r0, MOE_PAD), :], gsem.at[cur]).wait()
            return carry
        lax.fori_loop(0, n // MOE_PAD, drain, 0)

    def prefetch_group():
        base = pf[0]
        for q in range(MOE_GATHER_GROUP):
            start_row(slot_tok[start_next + base + q], nxt, base + q)
        pf[0] = base + MOE_GATHER_GROUP

    def run_blocks(compute):
        def one(b, r0, rows):
            r0 = pl.multiple_of(r0, MOE_PAD)
            more = pf[0] < n_next

            def gather():
                prefetch_group()
                for _ in range(max(rows // (2 * MOE_GATHER_GROUP), 1) - 1):
                    pl.when(pf[0] < n_next)(prefetch_group)

            @pl.when(more)
            def _():
                compute(b, r0, rows, gather)

            @pl.when(jnp.logical_not(more))
            def _():
                compute(b, r0, rows, lambda: None)

        n_big = n // MOE_BLOCK

        def body(b, carry):
            one(b, b * MOE_BLOCK, MOE_BLOCK)
            return carry
        lax.fori_loop(0, n_big, body, 0)

        b, r0, rows = n_big, n_big * MOE_BLOCK, MOE_BLOCK // 2
        while rows >= MOE_PAD:
            has = (n & rows) != 0
            pl.when(has)(functools.partial(one, b, r0, rows))
            b = b + has.astype(jnp.int32)
            r0 = r0 + jnp.where(has, rows, 0)
            rows //= 2

    def gate_up(half):
        def compute(b, r0, rows, mid):
            xs = _unpack_pairs(_load_row_tiles(xbuf.at[cur], r0, rows)).astype(BF16)
            g = _dot(xs, wg_ref[0].astype(BF16))
            u = _dot(xs, wu_ref[0].astype(BF16))
            mid()
            act[pl.ds(r0, rows), half * half_w:(half + 1) * half_w] = (
                g * _sigmoid(g) * u).astype(BF16)
        run_blocks(compute)

    def out_copy(slot, r0, rows):
        return pltpu.make_async_copy(
            ystage.at[slot, tile_rows(0, rows), :],
            yb_hbm.at[tile_rows(pl.multiple_of(start + r0, MOE_PAD), rows), :],
            osem.at[slot])

    def wait_stage(slot):
        for rows in _moe_block_sizes():
            @pl.when(pf[1 + slot] == rows)
            def _(rows=rows):
                out_copy(slot, 0, rows).wait()
                pf[1 + slot] = 0

    def down():
        def compute(b, r0, rows, mid):
            slot = b % 2
            wait_stage(slot)
            y = _dot(act[pl.ds(r0, rows), :], wd_ref[0].astype(BF16))
            mid()
            _store_row_tiles(ystage.at[slot], _pack_pairs(y))
            out_copy(slot, r0, rows).start()
            pf[1 + slot] = rows
        run_blocks(compute)

        def tail(i, carry):
            start_row(slot_tok[start_next + i], nxt, i)
            return carry
        lax.fori_loop(pf[0], n_next, tail, 0)

    pl.when(t == 0)(functools.partial(gate_up, 0))
    pl.when(t == 1)(functools.partial(gate_up, 1))
    pl.when(t == 2)(down)

    fill = item_fill[a]

    @pl.when(jnp.logical_and(t == 0, fill > 0))
    def _():
        wait_stage(0)
        ystage[0] = jnp.zeros(ystage.shape[1:], ystage.dtype)

        def zero_block(bk, carry):
            cp = out_copy(0, bk * MOE_PAD, MOE_PAD)
            cp.start()
            cp.wait()
            return carry
        lax.fori_loop(0, fill // MOE_PAD, zero_block, 0)

    @pl.when(jnp.logical_and(a == n_items - 1, t == pl.num_programs(1) - 1))
    def _():
        wait_stage(0)
        wait_stage(1)


def _moe_ffn(h2p, w_gate, w_up, w_down, item_e, item_start, item_rows, item_fill, slot_tok,
             n_slots):
    d = w_gate.shape[1]
    assert d == 2 * TILE_ROWS * LANES and h2p.shape[1] == LANES, "one packed token row per tile"
    n_items = item_e.shape[0]
    half_w = D_EXPERT // 2

    def wgu_map(a, t, ie, ist, ir, fl, st):
        return (ie[a], 0, jnp.where(ir[a] > 0, jnp.minimum(t, 1), 1))

    def wd_map(a, t, ie, ist, ir, fl, st):
        return (jnp.where(t >= 2, ie[a], ie[jnp.maximum(a - 1, 0)]), 0, 0)

    grid_spec = pltpu.PrefetchScalarGridSpec(
        num_scalar_prefetch=5,
        grid=(n_items, 3),
        in_specs=[
            pl.BlockSpec(memory_space=pl.ANY),
            pl.BlockSpec((1, d, half_w), wgu_map),
            pl.BlockSpec((1, d, half_w), wgu_map),
            pl.BlockSpec((1, D_EXPERT, d), wd_map),
        ],
        out_specs=pl.BlockSpec(memory_space=pl.ANY),
        scratch_shapes=[
            pltpu.VMEM((2, MOE_ITEM_ROWS * TILE_ROWS, LANES), jnp.uint32),
            pltpu.VMEM((MOE_ITEM_ROWS, D_EXPERT), BF16),
            pltpu.VMEM((2, MOE_BLOCK * TILE_ROWS, LANES), jnp.uint32),
            pltpu.SMEM((3,), jnp.int32),
            pltpu.SemaphoreType.DMA((2,)),
            pltpu.SemaphoreType.DMA((2,)),
        ],
    )
    return pl.pallas_call(
        _moe_kernel,
        grid_spec=grid_spec,
        out_shape=jax.ShapeDtypeStruct((n_slots * TILE_ROWS, LANES), jnp.uint32),
        compiler_params=_cparams(2, 56),
        name="moe",
    )(item_e, item_start, item_rows, item_fill, slot_tok, h2p, w_gate, w_up, w_down)


def _combine_kernel(s0, s1, yb_hbm, gates_ref, x1_ref, nw_ref, g2_ref, out_ref, rows, sem):
    i = pl.program_id(0)
    n_tiles = pl.num_programs(0)
    tm = x1_ref.shape[0]

    def tile_rows(i, count=1):
        return pl.ds(pl.multiple_of(i * TILE_ROWS, TILE_ROWS), count * TILE_ROWS)

    def gather(tile, buf):
        def issue(r8, carry):
            t0 = tile * tm + r8 * 8
            for dr in range(8):
                for k, s_k in enumerate((s0, s1)):
                    pltpu.make_async_copy(yb_hbm.at[tile_rows(s_k[t0 + dr]), :],
                                          rows.at[buf, k, tile_rows(r8 * 8 + dr), :],
                                          sem.at[buf]).start()
            return carry
        lax.fori_loop(0, tm // 8, issue, 0)

    @pl.when(i == 0)
    def _():
        gather(0, 0)

    @pl.when(i + 1 < n_tiles)
    def _():
        gather(i + 1, (i + 1) % 2)

    buf = i % 2
    for k in range(EXPERT_TOPK):
        pltpu.make_async_copy(yb_hbm.at[tile_rows(0, tm), :], rows.at[buf, k], sem.at[buf]).wait()

    g = gates_ref[...]
    y = (_unpack_pairs(_load_row_tiles(rows.at[buf, 0], 0, tm)) * g[:, 0:1]
         + _unpack_pairs(_load_row_tiles(rows.at[buf, 1], 0, tm)) * g[:, 1:2])
    out_ref[...] = x1_ref[...] + g2_ref[0:1, :] * (_rms(y) * nw_ref[3:4, :])


def _combine(s0, s1, yb, gates, x1, norm_w, mod):
    n, d = x1.shape
    tm = 256
    grid_spec = pltpu.PrefetchScalarGridSpec(
        num_scalar_prefetch=2,
        grid=(n // tm,),
        in_specs=[
            pl.BlockSpec(memory_space=pl.ANY),
            pl.BlockSpec((tm, EXPERT_TOPK), lambda i, *_: (i, 0)),
            pl.BlockSpec((tm, d), lambda i, *_: (i, 0)),
            pl.BlockSpec((4, d), lambda i, *_: (0, 0)),
            pl.BlockSpec((8, d), lambda i, *_: (0, 5)),
        ],
        out_specs=pl.BlockSpec((tm, d), lambda i, *_: (i, 0)),
        scratch_shapes=[
            pltpu.VMEM((2, EXPERT_TOPK, tm * TILE_ROWS, LANES), jnp.uint32),
            pltpu.SemaphoreType.DMA((2,)),
        ],
    )
    return pl.pallas_call(
        _combine_kernel,
        grid_spec=grid_spec,
        out_shape=jax.ShapeDtypeStruct((n, d), F32),
        compiler_params=_cparams(1, 40),
        name="combine",
    )(s0, s1, yb, gates, x1, norm_w, mod)


def _rope_tables(n):
    rows = n // GRID_W
    r = np.repeat(np.arange(rows, dtype=np.float32), GRID_W)
    cidx = np.tile(np.arange(GRID_W, dtype=np.float32), rows)
    n_freq = RET_DK // 4
    inv = np.float32(ROPE_BASE) ** (-np.arange(n_freq, dtype=np.float32) / np.float32(n_freq))
    ang = np.concatenate([r[:, None] * inv, cidx[:, None] * inv], axis=-1).astype(np.float32)
    cos, sin = np.cos(ang), np.sin(ang)
    cs = np.concatenate([cos, cos], axis=-1)
    sn = np.concatenate([-sin, sin], axis=-1)
    return jnp.asarray(cs, F32), jnp.asarray(sn, F32)


def kernel(x, c, ctx, c_ctx, w_mod, b_mod, norm_w, w_in, hg_lb_logits, hg_norm_w, ret_decay_logits,
           ret_norm_w, w_out, router_group_w, router_group_b, router_expert_w, router_expert_b,
           w_gate, w_up, w_down):
    bsz, n, d = x.shape
    l = ctx.shape[1]
    assert bsz == 1 and w_mod.shape[0] == 1, "single sample, single layer"
    x2d, ctx2d = x[0], ctx[0]

    cc = jnp.zeros((8, d), F32).at[0].set(c[0]).at[1].set(c_ctx)
    mod = _modulation(cc, w_mod[0], b_mod[0])

    lb = jnp.cumsum(jax.nn.softmax(hg_lb_logits.astype(F32), axis=0), axis=0)[0]
    rk0 = 5 * HG_WIDTH + RET_HEADS * RET_DK
    cols = jnp.stack([
        jnp.zeros((PROJ_WIDTH,), F32).at[HG_WIDTH:3 * HG_WIDTH].set(lb.reshape(-1)),
        jnp.ones((PROJ_WIDTH,), F32).at[rk0:rk0 + RET_HEADS * RET_DK].set(RET_DK ** -0.5),
    ])
    pc, gc, w_in_bf = _projection(ctx2d, norm_w[0], mod, w_in[0], cols, 1, l, "proj_ctx")
    p, g = _projection(x2d, norm_w[0], mod, w_in_bf, cols, 0, 1024, "proj_x")

    ret_lg = jax.nn.log_sigmoid(ret_decay_logits[0].astype(F32))
    cs_tab, sn_tab = _rope_tables(n)

    o_f, o_b, r_f, r_b = _latent_scans(_hgrn_scan(p, g, pc, gc),
                                       _ret_scan(p, pc, ret_lg, cs_tab, sn_tab))

    wr = jnp.concatenate([router_group_w[0], router_expert_w[0]], axis=1)
    wr = jnp.pad(wr, ((0, 0), (0, ROUTER_LANES - wr.shape[1])))
    wr_hi = wr.astype(BF16)
    wr_parts = jnp.concatenate([wr_hi, (wr - wr_hi.astype(F32)).astype(BF16)], axis=1)
    br = jnp.concatenate([router_group_b[0], router_expert_b[0]]).astype(F32)
    br = jnp.pad(br, (0, ROUTER_LANES - br.shape[0])).reshape(1, ROUTER_LANES)
    x1, h2, logits = _out_projection(
        o_f, o_b, r_f, r_b, p, x2d, w_out[0].astype(BF16), hg_norm_w, ret_norm_w, norm_w[0], mod,
        wr_parts, br)

    n_assign = n * EXPERT_TOPK
    n_slots = n_assign + N_EXPERTS * MOE_PAD
    n_items = (n_slots + N_EXPERTS * (MOE_ITEM_ROWS - MOE_PAD)) // MOE_ITEM_ROWS
    gates, s0, s1, counts = _router(logits)
    s0, s1 = s0.reshape(-1), s1.reshape(-1)
    slot_tok, item_e, item_start, item_rows, item_fill = _plan(counts, s0, s1, n_slots, n_items)

    yb = _moe_ffn(h2, w_gate[0], w_up[0], w_down[0], item_e, item_start, item_rows, item_fill,
                  slot_tok, n_slots)
    out = _combine(s0, s1, yb, gates, x1, norm_w[0], mod)
    return out[None]
```
